```python
import math
import jax
import jax.numpy as jnp
from jax import lax
import numpy as np

D_MODEL = 1024
BATCH = 16
SEQ = 2048
DEPTH = 2
DEC_BATCH = 8
DEC_SEQ = 32
PAST_LEN = 4096

CHUNK = 64
HEAD_DIM = 64
A_HEADS = D_MODEL // 256
A_V_DIM = 2 * HEAD_DIM
A_WIDTH = A_HEADS * A_V_DIM
R_HEADS = D_MODEL // 256
R_KEY_DIM = HEAD_DIM
R_V_DIM = HEAD_DIM
R_WIDTH = R_HEADS * R_V_DIM
C_WIDTH = D_MODEL // 4
CONV_WIDTH = 31
MIX_WIDTH = A_WIDTH + R_WIDTH + C_WIDTH
_IN_COLS = [A_HEADS * 2 * HEAD_DIM, A_HEADS * 2 * HEAD_DIM, A_WIDTH,
            R_HEADS * R_KEY_DIM, R_HEADS * R_KEY_DIM, R_WIDTH, R_WIDTH, 2 * C_WIDTH]
IN_WIDTH = sum(_IN_COLS)
IN_SPLITS = [sum(_IN_COLS[:i + 1]) for i in range(len(_IN_COLS) - 1)]
N_BUCKETS = 32
MAX_DISTANCE = 128
Q_BLOCK = 128
ROPE_BASE = 10000.0
N_EXPERTS = 16
N_GROUPS = 4
EXPERTS_PER_GROUP = N_EXPERTS // N_GROUPS
TOP_K = 2
D_EXPERT = 512
PLE_DIM = 256
ALPHA = (2 * DEPTH) ** 0.25
BETA = (8 * DEPTH) ** -0.25
LN_EPS = 1e-5
NEG_INF = -1e30

kernel_name = "hybrid_streaming_encoder_step"


def layer_norm(x, g, b):
    xf = x.astype(jnp.float32)
    mu = jnp.mean(xf, axis=-1, keepdims=True)
    var = jnp.mean(jnp.square(xf - mu), axis=-1, keepdims=True)
    y = (xf - mu) * lax.rsqrt(var + LN_EPS) * g.astype(jnp.float32) + b.astype(jnp.float32)
    return y.astype(x.dtype)


def rms_norm(x):
    xf = x.astype(jnp.float32)
    return xf * lax.rsqrt(jnp.mean(jnp.square(xf), axis=-1, keepdims=True) + LN_EPS)


def t5_bucket(rel):
    nb = N_BUCKETS // 2
    max_exact = nb // 2
    n = jnp.abs(rel)
    nf = jnp.maximum(n, 1).astype(jnp.float32)
    large = max_exact + (jnp.log(nf / max_exact) / math.log(MAX_DISTANCE / max_exact)
                         * (nb - max_exact)).astype(jnp.int32)
    large = jnp.minimum(large, nb - 1)
    return jnp.where(rel > 0, nb, 0) + jnp.where(n < max_exact, n, large)


def diff_attention(q, k, v, q_pos, k_pos, lam, rel_bias):
    s = jnp.einsum('bqhmd,bkhmd->bhmqk', q, k).astype(jnp.float32) * (HEAD_DIM ** -0.5)
    bias = jnp.moveaxis(rel_bias.astype(jnp.float32)[t5_bucket(k_pos[None, :] - q_pos[:, None])], -1, 0)
    visible = (k_pos[None, :] // CHUNK) <= (q_pos[:, None] // CHUNK)
    s = jnp.where(visible, s + bias[None, :, None], NEG_INF)
    pr = jax.nn.softmax(s, axis=-1)
    pr = pr[:, :, 0] - lam * pr[:, :, 1]
    return jnp.einsum('bhqk,bkhd->bqhd', pr, v.astype(jnp.float32))


def rotary(x, pos):
    half = x.shape[-1] // 2
    inv_freq = 1.0 / (ROPE_BASE ** jnp.linspace(0.0, 1.0, half, dtype=jnp.float32))
    ang = pos.astype(jnp.float32)[:, None] * inv_freq[None, :]
    cos = jnp.cos(ang)[None, :, None, :]
    sin = jnp.sin(ang)[None, :, None, :]
    x1, x2 = x[..., :half], x[..., half:]
    return jnp.concatenate([x1 * cos - x2 * sin, x1 * sin + x2 * cos], axis=-1)


def retention_chunkwise(q, k, v, state0, chunk):
    B, T, H, dk = q.shape
    dv = v.shape[-1]
    n = T // chunk
    log_g = jnp.log1p(-jnp.exp2(-5.0 - jnp.arange(H, dtype=jnp.float32)))
    qc = q.reshape(B, n, chunk, H, dk)
    kc = k.reshape(B, n, chunk, H, dk)
    vc = v.reshape(B, n, chunk, H, dv)
    i = jnp.arange(chunk, dtype=jnp.float32)
    diff = i[:, None] - i[None, :]
    dmask = jnp.where(diff >= 0, jnp.exp(jnp.maximum(diff, 0.0)[None] * log_g[:, None, None]), 0.0)
    inner = jnp.einsum('bnihd,bnjhd->bnhij', qc, kc) * dmask
    inner = jnp.einsum('bnhij,bnjhe->bnihe', inner, vc)
    k_dec = jnp.exp((chunk - 1 - i)[None, :] * log_g[:, None])
    kv = jnp.einsum('bnjhd,hj,bnjhe->nbhde', kc, k_dec, vc)
    chunk_dec = jnp.exp(chunk * log_g)[None, :, None, None]

    def step(r, kv_n):
        return chunk_dec * r + kv_n, r

    r_final, r_prev = lax.scan(step, state0, kv)
    q_dec = jnp.exp((i + 1.0)[:, None] * log_g[None, :])
    cross = jnp.einsum('bnihd,nbhde->bnihe', qc * q_dec[None, None, :, :, None], r_prev)
    return (inner + cross).reshape(B, T, H, dv), r_final


def conformer_conv(a, hist, dw_w, dw_b, g, b):
    u = a[..., :C_WIDTH] * jax.nn.sigmoid(a[..., C_WIDTH:])
    upad = jnp.concatenate([hist.astype(u.dtype), u], axis=1)
    y = lax.conv_general_dilated(upad, dw_w[:, None, :].astype(upad.dtype), window_strides=(1,),
                                 padding='VALID', dimension_numbers=('NWC', 'WIO', 'NWC'),
                                 feature_group_count=C_WIDTH)
    y = jax.nn.silu(layer_norm(y + dw_b.astype(y.dtype), g, b))
    return y, upad[:, -(CONV_WIDTH - 1):]


def token_mixers(x, q_pos, k_hist, v_hist, ret_state0, conv_hist, lam, lam_init,
                 rel_bias, w_in, subln_g, dw_w, dw_b, conv_ln_g, conv_ln_b, w_out):
    B, T, _ = x.shape
    a_q, a_k, a_v, r_q, r_k, r_v, r_g, c_in = jnp.split(x @ w_in, IN_SPLITS, axis=-1)

    q = a_q.reshape(B, T, A_HEADS, 2, HEAD_DIM)
    k_new = a_k.reshape(B, T, A_HEADS, 2 * HEAD_DIM)
    v_new = a_v.reshape(B, T, A_HEADS, A_V_DIM)
    if k_hist is None:
        k_all = k_new.reshape(B, T, A_HEADS, 2, HEAD_DIM)

        def q_block(start):
            qb = lax.dynamic_slice_in_dim(q, start, Q_BLOCK, axis=1)
            pb = lax.dynamic_slice_in_dim(q_pos, start, Q_BLOCK)
            return diff_attention(qb, k_all, v_new, pb, q_pos, lam, rel_bias)

        o = lax.map(q_block, jnp.arange(0, T, Q_BLOCK, dtype=jnp.int32))
        o = jnp.moveaxis(o, 0, 1).reshape(B, T, A_HEADS, A_V_DIM)
        ret_chunk = CHUNK
    else:
        k_all = jnp.concatenate([k_hist.astype(k_new.dtype), k_new], axis=1)
        v_all = jnp.concatenate([v_hist.astype(v_new.dtype), v_new], axis=1)
        k_pos = jnp.arange(k_all.shape[1], dtype=jnp.int32)
        o = diff_attention(q, k_all.reshape(B, -1, A_HEADS, 2, HEAD_DIM), v_all, q_pos, k_pos, lam, rel_bias)
        ret_chunk = T
    o = rms_norm(o) * subln_g.astype(jnp.float32) * (1.0 - lam_init)
    attn_out = o.reshape(B, T, A_WIDTH).astype(x.dtype)

    rq = rotary(r_q.reshape(B, T, R_HEADS, R_KEY_DIM).astype(jnp.float32), q_pos)
    rk = rotary(r_k.reshape(B, T, R_HEADS, R_KEY_DIM).astype(jnp.float32), q_pos) * (R_KEY_DIM ** -0.5)
    rv = r_v.reshape(B, T, R_HEADS, R_V_DIM).astype(jnp.float32)
    ro, ret_state = retention_chunkwise(rq, rk, rv, ret_state0.astype(jnp.float32), ret_chunk)
    ret_out = (rms_norm(ro).reshape(B, T, R_WIDTH) * jax.nn.silu(r_g.astype(jnp.float32))).astype(x.dtype)

    conv_out, conv_tail = conformer_conv(c_in, conv_hist, dw_w, dw_b, conv_ln_g, conv_ln_b)

    mix = jnp.concatenate([attn_out, ret_out, conv_out.astype(x.dtype)], axis=-1) @ w_out
    return mix, k_new, v_new, ret_state, conv_tail


def moe(x, w_router, b_router, w_exp_gate, w_exp_up, w_exp_down):
    N = x.shape[0]
    logits = (x @ w_router).astype(jnp.float32) + b_router.astype(jnp.float32)
    scores = jax.nn.softmax(logits, axis=-1).reshape(N, N_GROUPS, EXPERTS_PER_GROUP)
    top_v, top_i = lax.top_k(scores, TOP_K)
    grp = jnp.argmax(jnp.sum(top_v, axis=-1), axis=-1)
    sel_v = jnp.take_along_axis(top_v, grp[:, None, None], axis=1)[:, 0]
    sel_i = jnp.take_along_axis(top_i, grp[:, None, None], axis=1)[:, 0] + grp[:, None] * EXPERTS_PER_GROUP
    w = sel_v / jnp.sum(sel_v, axis=-1, keepdims=True)
    gates = jnp.sum(jax.nn.one_hot(sel_i, N_EXPERTS, dtype=jnp.float32) * w[..., None], axis=1)
    gates = gates.astype(x.dtype)
    y = jnp.zeros_like(x)
    for e in range(N_EXPERTS):
        h = jax.nn.silu(x @ w_exp_gate[e]) * (x @ w_exp_up[e])
        y = y + gates[:, e:e + 1] * (h @ w_exp_down[e])
    return y


def channel_block(x, mix, p_l, ln1_g, ln1_b, w_router, b_router, w_exp_gate, w_exp_up, w_exp_down,
                  ln2_g, ln2_b, w_ple_gate, b_ple_gate, w_ple_proj):
    B, T, D = x.shape
    x = layer_norm(ALPHA * x + mix, ln1_g, ln1_b)
    f = moe(x.reshape(B * T, D), w_router, b_router, w_exp_gate, w_exp_up, w_exp_down).reshape(B, T, D)
    x = layer_norm(ALPHA * x + f, ln2_g, ln2_b)
    return x + jax.nn.sigmoid(x @ w_ple_gate + b_ple_gate) * (p_l @ w_ple_proj)


def setup_inputs(seed: int = 0) -> dict:
    key = jax.random.key(seed)
    ks = jax.random.split(key, 34)
    f32 = jnp.float32

    def nrm(k, shape, scale):
        return jax.random.normal(k, shape, f32) * scale

    def gain(k, shape):
        return 1.0 + nrm(k, shape, 0.01)

    return {
        'x_prompt': nrm(ks[0], (BATCH, SEQ, D_MODEL), 1.0),
        'x_sample': nrm(ks[1], (DEC_BATCH, DEC_SEQ, D_MODEL), 1.0),
        'p_prompt': nrm(ks[2], (DEPTH, BATCH, SEQ, PLE_DIM), 1.0),
        'p_sample': nrm(ks[3], (DEPTH, DEC_BATCH, DEC_SEQ, PLE_DIM), 1.0),
        'cache_k': nrm(ks[4], (DEPTH, DEC_BATCH, PAST_LEN, A_HEADS, 2 * HEAD_DIM), 1.0),
        'cache_v': nrm(ks[5], (DEPTH, DEC_BATCH, PAST_LEN, A_HEADS, A_V_DIM), 1.0),
        'state_ret': nrm(ks[6], (DEPTH, DEC_BATCH, R_HEADS, R_KEY_DIM, R_V_DIM), 0.5),
        'state_conv': nrm(ks[7], (DEPTH, DEC_BATCH, CONV_WIDTH - 1, C_WIDTH), 0.5),
        'ln_emb_g': gain(ks[8], (D_MODEL,)),
        'ln_emb_b': nrm(ks[9], (D_MODEL,), 0.01),
        'rel_bias': nrm(ks[10], (N_BUCKETS, A_HEADS), 0.2),
        'w_router': nrm(ks[11], (D_MODEL, N_EXPERTS), D_MODEL ** -0.5),
        'b_router': nrm(ks[12], (N_EXPERTS,), 0.01),
        'w_in': nrm(ks[13], (DEPTH, D_MODEL, IN_WIDTH), D_MODEL ** -0.5),
        'lam_q1': nrm(ks[14], (DEPTH, HEAD_DIM), 0.1),
        'lam_k1': nrm(ks[15], (DEPTH, HEAD_DIM), 0.1),
        'lam_q2': nrm(ks[16], (DEPTH, HEAD_DIM), 0.1),
        'lam_k2': nrm(ks[17], (DEPTH, HEAD_DIM), 0.1),
        'subln_g': gain(ks[18], (DEPTH, A_V_DIM)),
        'dw_w': nrm(ks[19], (DEPTH, CONV_WIDTH, C_WIDTH), CONV_WIDTH ** -0.5),
        'dw_b': nrm(ks[20], (DEPTH, C_WIDTH), 0.01),
        'conv_ln_g': gain(ks[21], (DEPTH, C_WIDTH)),
        'conv_ln_b': nrm(ks[22], (DEPTH, C_WIDTH), 0.01),
        'w_out': nrm(ks[23], (DEPTH, MIX_WIDTH, D_MODEL), BETA * MIX_WIDTH ** -0.5),
        'ln1_g': gain(ks[24], (DEPTH, D_MODEL)),
        'ln1_b': nrm(ks[25], (DEPTH, D_MODEL), 0.01),
        'w_exp_gate': nrm(ks[26], (DEPTH, N_EXPERTS, D_MODEL, D_EXPERT), D_MODEL ** -0.5),
        'w_exp_up': nrm(ks[27], (DEPTH, N_EXPERTS, D_MODEL, D_EXPERT), D_MODEL ** -0.5),
        'w_exp_down': nrm(ks[28], (DEPTH, N_EXPERTS, D_EXPERT, D_MODEL), BETA * D_EXPERT ** -0.5),
        'ln2_g': gain(ks[29], (DEPTH, D_MODEL)),
        'ln2_b': nrm(ks[30], (DEPTH, D_MODEL), 0.01),
        'w_ple_gate': nrm(ks[31], (DEPTH, D_MODEL, D_MODEL), D_MODEL ** -0.5),
        'b_ple_gate': nrm(ks[32], (DEPTH, D_MODEL), 0.01),
        'w_ple_proj': nrm(ks[33], (DEPTH, PLE_DIM, D_MODEL), PLE_DIM ** -0.5),
    }


def reference(x_prompt, x_sample, p_prompt, p_sample, cache_k, cache_v, state_ret, state_conv,
              ln_emb_g, ln_emb_b, rel_bias, w_router, b_router, w_in, lam_q1, lam_k1, lam_q2, lam_k2,
              subln_g, dw_w, dw_b, conv_ln_g, conv_ln_b, w_out, ln1_g, ln1_b,
              w_exp_gate, w_exp_up, w_exp_down, ln2_g, ln2_b, w_ple_gate, b_ple_gate, w_ple_proj):
    f32 = jnp.float32
    B, T_p, _ = x_prompt.shape
    T_s = x_sample.shape[1]
    past = cache_k.shape[2]
    pos_p = jnp.arange(T_p, dtype=jnp.int32)
    pos_s = past + jnp.arange(T_s, dtype=jnp.int32)
    xp = layer_norm(x_prompt, ln_emb_g, ln_emb_b)
    xs = layer_norm(x_sample, ln_emb_g, ln_emb_b)
    ret0_p = jnp.zeros((B, R_HEADS, R_KEY_DIM, R_V_DIM), f32)
    conv0_p = jnp.zeros((B, CONV_WIDTH - 1, C_WIDTH), xp.dtype)
    kps, vps, rps, cps, kss, vss, rss, css = [], [], [], [], [], [], [], []
    for l in range(DEPTH):
        lam_init = 0.8 - 0.6 * math.exp(-0.3 * l)
        lam = (jnp.exp(jnp.sum(lam_q1[l].astype(f32) * lam_k1[l].astype(f32)))
               - jnp.exp(jnp.sum(lam_q2[l].astype(f32) * lam_k2[l].astype(f32))) + lam_init)
        mix_p, kp, vp, rp, cp = token_mixers(xp, pos_p, None, None, ret0_p, conv0_p, lam, lam_init,
                                             rel_bias, w_in[l], subln_g[l], dw_w[l], dw_b[l],
                                             conv_ln_g[l], conv_ln_b[l], w_out[l])
        xp = channel_block(xp, mix_p, p_prompt[l], ln1_g[l], ln1_b[l], w_router, b_router,
                           w_exp_gate[l], w_exp_up[l], w_exp_down[l], ln2_g[l], ln2_b[l],
                           w_ple_gate[l], b_ple_gate[l], w_ple_proj[l])
        mix_s, ksn, vsn, rsn, csn = token_mixers(xs, pos_s, cache_k[l], cache_v[l], state_ret[l], state_conv[l],
                                                 lam, lam_init, rel_bias, w_in[l], subln_g[l], dw_w[l], dw_b[l],
                                                 conv_ln_g[l], conv_ln_b[l], w_out[l])
        xs = channel_block(xs, mix_s, p_sample[l], ln1_g[l], ln1_b[l], w_router, b_router,
                           w_exp_gate[l], w_exp_up[l], w_exp_down[l], ln2_g[l], ln2_b[l],
                           w_ple_gate[l], b_ple_gate[l], w_ple_proj[l])
        kps.append(kp); vps.append(vp); rps.append(rp); cps.append(cp)
        kss.append(ksn); vss.append(vsn); rss.append(rsn); css.append(csn)
    new_k_prompt = jnp.stack(kps)
    new_v_prompt = jnp.stack(vps)
    new_ret_prompt = jnp.stack(rps)
    new_conv_prompt = jnp.stack(cps)
    new_k_sample = jnp.stack(kss)
    new_v_sample = jnp.stack(vss)
    new_ret_sample = jnp.stack(rss)
    new_conv_sample = jnp.stack(css)
    return (xp, xs, new_k_prompt, new_v_prompt, new_ret_prompt, new_conv_prompt,
            new_k_sample, new_v_sample, new_ret_sample, new_conv_sample)
```

```python
import functools
import math

import jax
import jax.numpy as jnp
from jax import lax
from jax.experimental import pallas as pl
from jax.experimental.pallas import tpu as pltpu

F32 = jnp.float32
BF16 = jnp.bfloat16

D_MODEL = 1024
CHUNK = 64
HEAD_DIM = 64
A_HEADS = 4
A_V_DIM = 128
A_WIDTH = 512
R_HEADS = 4
R_KEY_DIM = 64
R_V_DIM = 64
R_WIDTH = 256
C_WIDTH = 256
CONV_WIDTH = 31
IN_WIDTH = 3072
N_BUCKETS = 32
MAX_DISTANCE = 128
ROPE_BASE = 10000.0
N_EXPERTS = 16
N_GROUPS = 4
EXPERTS_PER_GROUP = 4
D_EXPERT = 512
PLE_DIM = 256
LN_EPS = 1e-5
NEG_INF = -1e30

LANES = 128
_LOG2_CHUNK = 6
assert CHUNK == HEAD_DIM == R_KEY_DIM == R_V_DIM == 1 << _LOG2_CHUNK
MIB = 1024 * 1024

_OFF_Q, _OFF_K, _OFF_V = 0, 512, 1024
_OFF_RQ, _OFF_RK, _OFF_RV, _OFF_RG, _OFF_C = 1536, 1792, 2048, 2304, 2560


def _params(sem, vmem_mib):
    return pltpu.CompilerParams(dimension_semantics=sem, vmem_limit_bytes=vmem_mib * MIB)


def _layer_norm(x, g, b):
    mu = jnp.mean(x, axis=-1, keepdims=True)
    xc = x - mu
    var = jnp.mean(xc * xc, axis=-1, keepdims=True)
    return xc * lax.rsqrt(var + LN_EPS) * g + b


def _sigmoid(x):
    return 1.0 / (1.0 + jnp.exp(-x))


def _dot(a, b):
    return jnp.dot(a, b, preferred_element_type=F32)


def _dot_nt(a, b):
    return lax.dot_general(a, b, (((1,), (1,)), ((), ())), preferred_element_type=F32)


def _dot_tn(a, b):
    return lax.dot_general(a, b, (((0,), (0,)), ((), ())), preferred_element_type=F32)


def _bias_kernel(relb_ref, idx_ref, out_ref):
    idx = idx_ref[...]
    for h in range(A_HEADS):
        acc = jnp.zeros(idx.shape, F32)
        for b in range(N_BUCKETS):
            acc = jnp.where(idx == b, relb_ref[b, h], acc)
        out_ref[h] = acc


def _bias_table(rel_bias, idx):
    r, c = idx.shape
    return pl.pallas_call(
        _bias_kernel,
        out_shape=jax.ShapeDtypeStruct((A_HEADS, r, c), F32),
        in_specs=[pl.BlockSpec(memory_space=pltpu.SMEM),
                  pl.BlockSpec(memory_space=pltpu.VMEM)],
        out_specs=pl.BlockSpec(memory_space=pltpu.VMEM),
        name="bias_table",
    )(rel_bias, idx)


def _t5_bucket(rel):
    nb = N_BUCKETS // 2
    max_exact = nb // 2
    n = jnp.abs(rel)
    nf = jnp.maximum(n, 1).astype(jnp.float32)
    large = max_exact + (jnp.log(nf / max_exact) / math.log(MAX_DISTANCE / max_exact)
                         * (nb - max_exact)).astype(jnp.int32)
    large = jnp.minimum(large, nb - 1)
    return jnp.where(rel > 0, nb, 0) + jnp.where(n < max_exact, n, large)


def _rotary128(x, cos, sin_signed, lo32):
    partner = jnp.where(lo32, pltpu.roll(x, 96, 1), pltpu.roll(x, 32, 1))
    return x * cos + partner * sin_signed


def _inproj_kernel(*refs, apply_ln):
    if apply_ln:
        x_ref, g_ref, b_ref, w_ref, cos_ref, sin_ref = refs[:6]
        xn_ref = refs[6]
        outs = refs[7:]
    else:
        x_ref, w_ref, cos_ref, sin_ref = refs[:4]
        outs = refs[4:]
    q_ref, k_ref, kb_ref, v_ref, vb_ref, rq_ref, rk_ref, rv_ref, rg_ref, u_ref = outs

    x = x_ref[...]
    if apply_ln:
        x = _layer_norm(x, g_ref[...], b_ref[...])
        xn_ref[...] = x
    xb = x.astype(BF16)

    def mm(c0, c1):
        return _dot(xb, w_ref[:, c0:c1])

    q_ref[...] = (mm(_OFF_Q, _OFF_K) * (HEAD_DIM ** -0.5)).astype(BF16)
    a = mm(_OFF_K, _OFF_V)
    k_ref[...] = a
    kb_ref[...] = a.astype(BF16)
    a = mm(_OFF_V, _OFF_RQ)
    v_ref[...] = a
    vb_ref[...] = a.astype(BF16)

    cos = cos_ref[...]
    sin = sin_ref[...]
    lane = lax.broadcasted_iota(jnp.int32, cos.shape, 1)
    lo32 = (lane & 63) < 32

    def rot(a, scale):
        parts = [_rotary128(a[:, c * LANES:(c + 1) * LANES], cos, sin, lo32) for c in range(2)]
        r = jnp.concatenate(parts, axis=1)
        if scale != 1.0:
            r = r * scale
        return r.astype(BF16)

    rq_ref[...] = rot(mm(_OFF_RQ, _OFF_RK), 1.0)
    rk_ref[...] = rot(mm(_OFF_RK, _OFF_RV), R_KEY_DIM ** -0.5)
    rv_ref[...] = mm(_OFF_RV, _OFF_RG).astype(BF16)
    a = mm(_OFF_RG, _OFF_C)
    rg_ref[...] = a * _sigmoid(a)
    a = mm(_OFF_C, IN_WIDTH)
    u_ref[...] = a[:, :C_WIDTH] * _sigmoid(a[:, C_WIDTH:])


def _inproj(x, w_bf, cos_t, sin_t, ln, tm):
    n = x.shape[0]
    nblk = cos_t.shape[0] // tm
    row = lambda i: (i, 0)
    const = lambda i: (0, 0)
    tab = lambda i: (i % nblk, 0)
    in_specs = [pl.BlockSpec((tm, D_MODEL), row)]
    args = [x]
    if ln is not None:
        in_specs += [pl.BlockSpec((1, D_MODEL), const)] * 2
        args += [ln[0].reshape(1, D_MODEL), ln[1].reshape(1, D_MODEL)]
    in_specs += [pl.BlockSpec((D_MODEL, IN_WIDTH), const),
                 pl.BlockSpec((tm, LANES), tab), pl.BlockSpec((tm, LANES), tab)]
    args += [w_bf, cos_t, sin_t]

    def o(width, dt):
        return jax.ShapeDtypeStruct((n, width), dt), pl.BlockSpec((tm, width), row)

    outs = []
    if ln is not None:
        outs.append(o(D_MODEL, F32))
    outs += [o(512, BF16), o(512, F32), o(512, BF16), o(512, F32), o(512, BF16),
             o(256, BF16), o(256, BF16), o(256, BF16), o(256, F32), o(256, F32)]
    res = pl.pallas_call(
        functools.partial(_inproj_kernel, apply_ln=ln is not None),
        grid=(n // tm,),
        in_specs=in_specs,
        out_specs=[s for _, s in outs],
        out_shape=[s for s, _ in outs],
        compiler_params=_params(("arbitrary",), 52),
        name="inproj",
    )(*args)
    if ln is None:
        res = [x] + list(res)
    return res


def _lambda(lam_ref, lam_init):
    lv = lam_ref[...]
    s1 = jnp.sum(lv[0:1] * lv[1:2], axis=-1, keepdims=True)
    s2 = jnp.sum(lv[2:3] * lv[3:4], axis=-1, keepdims=True)
    return jnp.exp(s1) - jnp.exp(s2) + lam_init


def _stack_maps(q):
    lane = lax.broadcasted_iota(jnp.int32, q.shape, 1)
    lo = lane < HEAD_DIM
    z = jnp.zeros_like(q)
    return jnp.concatenate([jnp.where(lo, q, z), jnp.where(lo, z, q)], axis=0)


def _diff_finish(o2, t, lam, g, lam_init):
    o = o2[:t] - lam * o2[t:]
    ms = jnp.mean(o * o, axis=-1, keepdims=True)
    return (o * lax.rsqrt(ms + LN_EPS) * g * (1.0 - lam_init)).astype(BF16)


def _attn_prompt_kernel(lam_ref, g_ref, bias_ref, q_ref, k_ref, v_ref, o_ref,
                        s_scr, m_scr, l_scr, acc_scr, bp_scr, bd_scr, *, T, TB, lam_init):
    lam = _lambda(lam_ref, lam_init)
    g = g_ref[...]
    bt = bias_ref[...]
    bt = bt - bt[0:1, 0:1]
    bp_scr[...] = bt[:, :TB]
    bd_scr[...] = bt[:, TB:]
    half = TB // 2

    def update_max(s):
        m_scr[...] = jnp.maximum(m_scr[...], jnp.maximum(s[:, :half], s[:, half:]))

    def q_body(qi, carry):
        r0 = pl.multiple_of(qi * TB, TB)
        q2 = _stack_maps(q_ref[pl.ds(r0, TB), :])
        m_scr[...] = jnp.full(m_scr.shape, NEG_INF, F32)

        def scores(ki):
            k = k_ref[pl.ds(pl.multiple_of(ki * TB, TB), TB), :]
            return _dot_nt(q2, k)

        def far(ki, c):
            s = scores(ki)
            s_scr[ki] = s
            update_max(s)
            return c

        lax.fori_loop(0, qi - 1, far, 0)

        @pl.when(qi >= 1)
        def _():
            s = scores(qi - 1).reshape(2, TB, TB) + bp_scr[...][None]
            s = s.reshape(2 * TB, TB)
            s_scr[qi - 1] = s
            update_max(s)

        row = lax.broadcasted_iota(jnp.int32, (TB, TB), 0)
        col = lax.broadcasted_iota(jnp.int32, (TB, TB), 1)
        vis = (col >> _LOG2_CHUNK) <= (row >> _LOG2_CHUNK)
        s = scores(qi).reshape(2, TB, TB) + bd_scr[...][None]
        s = jnp.where(vis[None], s, NEG_INF).reshape(2 * TB, TB)
        s_scr[qi] = s
        update_max(s)

        m_row = jnp.max(m_scr[...], axis=1, keepdims=True)
        m_scr[...] = jnp.broadcast_to(m_row, m_scr.shape)
        l_scr[...] = jnp.zeros(l_scr.shape, F32)
        acc_scr[...] = jnp.zeros(acc_scr.shape, F32)

        def pv(ki, c):
            mb = m_scr[...]
            p = jnp.exp(s_scr[ki] - jnp.concatenate([mb, mb], axis=1))
            l_scr[...] += p[:, :half] + p[:, half:]
            v = v_ref[pl.ds(pl.multiple_of(ki * TB, TB), TB), :]
            acc_scr[...] += _dot(p.astype(BF16), v)
            return c

        lax.fori_loop(0, qi + 1, pv, 0)
        l_row = jnp.sum(l_scr[...], axis=1, keepdims=True)
        o_ref[pl.ds(r0, TB), :] = _diff_finish(acc_scr[...] / l_row, TB, lam, g, lam_init)
        return carry

    lax.fori_loop(0, T // TB, q_body, 0)


def _attn_prompt(lam4, g, bias, q, k, v, B, T, lam_init):
    TB = 2 * LANES
    assert T % TB == 0 and TB % CHUNK == 0
    n = B * T
    blk = pl.BlockSpec((T, A_V_DIM), lambda b, h: (b, h))
    return pl.pallas_call(
        functools.partial(_attn_prompt_kernel, T=T, TB=TB, lam_init=lam_init),
        grid=(B, A_HEADS),
        in_specs=[pl.BlockSpec((4, HEAD_DIM), lambda b, h: (0, 0)),
                  pl.BlockSpec((1, A_V_DIM), lambda b, h: (0, 0)),
                  pl.BlockSpec((None, TB, 2 * TB), lambda b, h: (h, 0, 0)),
                  blk, blk, blk],
        out_specs=blk,
        out_shape=jax.ShapeDtypeStruct((n, A_WIDTH), BF16),
        scratch_shapes=[pltpu.VMEM((T // TB, 2 * TB, TB), F32),
                        pltpu.VMEM((2 * TB, LANES), F32),
                        pltpu.VMEM((2 * TB, LANES), F32),
                        pltpu.VMEM((2 * TB, A_V_DIM), F32),
                        pltpu.VMEM((TB, TB), F32),
                        pltpu.VMEM((TB, TB), F32)],
        compiler_params=_params(("arbitrary", "arbitrary"), 32),
        name="attn_prompt",
    )(lam4, g, bias, q, k, v)


def _attn_decode_kernel(lam_ref, g_ref, bias_ref, q_ref, kc_ref, vc_ref, kn_ref, vn_ref, o_ref,
                        *, Ts, past, lam_init):
    lam = _lambda(lam_ref, lam_init)
    q2 = _stack_maps(q_ref[...])
    bias = bias_ref[...]
    s_p = _dot_nt(q2, kc_ref[...].astype(BF16)).reshape(2, Ts, past) + bias[:, :past][None]
    s_n = _dot_nt(q2, kn_ref[...]).reshape(2, Ts, Ts) + bias[:, past:past + Ts][None]
    s_p = s_p.reshape(2 * Ts, past)
    s_n = s_n.reshape(2 * Ts, Ts)
    m = jnp.maximum(jnp.max(s_p, axis=1, keepdims=True), jnp.max(s_n, axis=1, keepdims=True))
    p_p = jnp.exp(s_p - m)
    p_n = jnp.exp(s_n - m)
    l = jnp.sum(p_p, axis=1, keepdims=True) + jnp.sum(p_n, axis=1, keepdims=True)
    acc = _dot(p_p.astype(BF16), vc_ref[...].astype(BF16)) + _dot(p_n.astype(BF16), vn_ref[...])
    o_ref[...] = _diff_finish(acc / l, Ts, lam, g_ref[...], lam_init)


def _attn_decode(lam4, g, bias, q, cache_k, cache_v, kn, vn, layer, Bs, Ts, lam_init):
    past = cache_k.shape[2]
    assert past % CHUNK == 0 and Ts <= CHUNK
    padk = bias.shape[2]
    new = pl.BlockSpec((Ts, A_V_DIM), lambda b, h: (b, h))
    cache = pl.BlockSpec((None, None, past, A_V_DIM), lambda b, h: (layer, b, 0, h))
    return pl.pallas_call(
        functools.partial(_attn_decode_kernel, Ts=Ts, past=past, lam_init=lam_init),
        grid=(Bs, A_HEADS),
        in_specs=[pl.BlockSpec((4, HEAD_DIM), lambda b, h: (0, 0)),
                  pl.BlockSpec((1, A_V_DIM), lambda b, h: (0, 0)),
                  pl.BlockSpec((None, Ts, padk), lambda b, h: (h, 0, 0)),
                  new, cache, cache, new, new],
        out_specs=new,
        out_shape=jax.ShapeDtypeStruct((Bs * Ts, A_WIDTH), BF16),
        compiler_params=_params(("arbitrary", "arbitrary"), 40),
        name="attn_decode",
    )(lam4, g, bias, q, cache_k, cache_v, kn, vn)


def _ret_kernel(q_ref, k_ref, v_ref, g_ref, s0_ref, dm_ref, qd_ref, kd_ref, cd_ref,
                o_ref, sn_ref, *, T, C):
    z = jnp.zeros((R_KEY_DIM, R_V_DIM), F32)
    state = jnp.concatenate([jnp.concatenate([s0_ref[0], z], axis=1),
                             jnp.concatenate([z, s0_ref[1]], axis=1)], axis=0)
    r = lax.broadcasted_iota(jnp.int32, (LANES, LANES), 0)
    c = lax.broadcasted_iota(jnp.int32, (LANES, LANES), 1)
    same_head = (r >> _LOG2_CHUNK) == (c >> _LOG2_CHUNK)
    ones_bd = jnp.where(same_head, 1.0, 0.0).astype(BF16)
    lo = lax.broadcasted_iota(jnp.int32, (C, LANES), 1) < R_V_DIM
    cd = cd_ref[...]

    def chunk(n, state):
        r0 = pl.multiple_of(n * C, C)
        q = q_ref[pl.ds(r0, C), :]
        k = k_ref[pl.ds(r0, C), :]
        v = v_ref[pl.ds(r0, C), :]
        p = (_dot_nt(_stack_maps(q), k) * dm_ref[...]).astype(BF16)
        o2 = _dot(p, v)
        inner = jnp.where(lo, o2[:C], o2[C:])
        qd = (q.astype(F32) * qd_ref[...]).astype(BF16)
        o = inner + _dot(qd, state.astype(BF16))
        vk = (v.astype(F32) * kd_ref[...]).astype(BF16)
        new_state = cd * state + jnp.where(same_head, _dot_tn(k, vk), 0.0)
        oo = o * o
        hi = oo.astype(BF16)
        lo_part = (oo - hi.astype(F32)).astype(BF16)
        ss = _dot(hi, ones_bd) + _dot(lo_part, ones_bd)
        out = o * lax.rsqrt(ss * (1.0 / R_V_DIM) + LN_EPS) * g_ref[pl.ds(r0, C), :]
        o_ref[pl.ds(r0, C), :] = out.astype(BF16)
        return new_state

    state = lax.fori_loop(0, T // C, chunk, state)
    sn_ref[0] = state[:R_KEY_DIM, :R_V_DIM]
    sn_ref[1] = state[R_KEY_DIM:, R_V_DIM:]


def _retention(rq, rk, rv, rg, state0, tabs, B, T, C):
    dm, qd, kd, cd = tabs
    blk = pl.BlockSpec((T, LANES), lambda b, hp: (b, hp))
    st = pl.BlockSpec((None, 2, R_KEY_DIM, R_V_DIM), lambda b, hp: (b, hp, 0, 0))
    return pl.pallas_call(
        functools.partial(_ret_kernel, T=T, C=C),
        grid=(B, 2),
        in_specs=[blk, blk, blk, blk, st,
                  pl.BlockSpec((None, 2 * C, C), lambda b, hp: (hp, 0, 0)),
                  pl.BlockSpec((None, C, LANES), lambda b, hp: (hp, 0, 0)),
                  pl.BlockSpec((None, C, LANES), lambda b, hp: (hp, 0, 0)),
                  pl.BlockSpec((None, 1, LANES), lambda b, hp: (hp, 0, 0))],
        out_specs=[blk, st],
        out_shape=[jax.ShapeDtypeStruct((B * T, R_WIDTH), BF16),
                   jax.ShapeDtypeStruct((B, R_HEADS, R_KEY_DIM, R_V_DIM), F32)],
        compiler_params=_params(("arbitrary", "arbitrary"), 32),
        name="retention",
    )(rq, rk, rv, rg, state0, dm, qd, kd, cd)


def _retention_tables(C):
    h = jnp.arange(R_HEADS, dtype=F32)
    log_g = jnp.log1p(-jnp.exp2(-5.0 - h))
    i = jnp.arange(C, dtype=F32)
    diff = i[:, None] - i[None, :]
    dmask = jnp.where(diff >= 0, jnp.exp(jnp.maximum(diff, 0.0)[None] * log_g[:, None, None]), 0.0)
    dm = dmask.reshape(2, 2 * C, C)
    q_dec = jnp.exp((i + 1.0)[None, :] * log_g[:, None])
    k_dec = jnp.exp((C - 1 - i)[None, :] * log_g[:, None])
    c_dec = jnp.exp(C * log_g)

    def lanes(t):
        t = jnp.repeat(t[:, :, None], R_V_DIM, axis=2).reshape(2, 2, C, R_V_DIM)
        return jnp.concatenate([t[:, 0], t[:, 1]], axis=-1)

    cd = jnp.repeat(c_dec[:, None], R_V_DIM, axis=1).reshape(2, 1, LANES)
    return dm, lanes(q_dec), lanes(k_dec), cd


_CONV_PAD = 32


def _conv_kernel(u_ref, h_ref, w_ref, b_ref, g_ref, be_ref, o_ref, t_ref, up_scr, *, T, RT):
    hist = CONV_WIDTH - 1
    off = _CONV_PAD - hist
    up_scr[off:_CONV_PAD, :] = h_ref[...]
    up_scr[_CONV_PAD:_CONV_PAD + T, :] = u_ref[...]
    bias = b_ref[...]
    g = g_ref[...]
    be = be_ref[...]
    for t0 in range(0, T, RT):
        acc = jnp.zeros((RT, C_WIDTH), F32)
        for j in range(CONV_WIDTH):
            acc = acc + up_scr[t0 + j + off:t0 + j + off + RT, :] * w_ref[j:j + 1, :]
        y = _layer_norm(acc + bias, g, be)
        o_ref[t0:t0 + RT, :] = (y * _sigmoid(y)).astype(BF16)
    t_ref[...] = up_scr[T + off:T + _CONV_PAD, :]


def _conv(u, hist, dw_w, dw_b, ln_g, ln_b, B, T):
    RT = min(T, LANES)
    assert T % RT == 0
    hl = CONV_WIDTH - 1
    vec = pl.BlockSpec((1, C_WIDTH), lambda b: (0, 0))
    hb = pl.BlockSpec((None, hl, C_WIDTH), lambda b: (b, 0, 0))
    return pl.pallas_call(
        functools.partial(_conv_kernel, T=T, RT=RT),
        grid=(B,),
        in_specs=[pl.BlockSpec((T, C_WIDTH), lambda b: (b, 0)), hb,
                  pl.BlockSpec((CONV_WIDTH, C_WIDTH), lambda b: (0, 0)), vec, vec, vec],
        out_specs=[pl.BlockSpec((T, C_WIDTH), lambda b: (b, 0)), hb],
        out_shape=[jax.ShapeDtypeStruct((B * T, C_WIDTH), BF16),
                   jax.ShapeDtypeStruct((B, hl, C_WIDTH), F32)],
        scratch_shapes=[pltpu.VMEM((T + _CONV_PAD, C_WIDTH), F32)],
        compiler_params=_params(("arbitrary",), 32),
        name="conv",
    )(u, hist, dw_w, dw_b.reshape(1, C_WIDTH), ln_g.reshape(1, C_WIDTH), ln_b.reshape(1, C_WIDTH))


def _first_index(vals, target):
    idx = jnp.full(target.shape, len(vals) - 1, jnp.int32)
    for j in range(len(vals) - 2, -1, -1):
        idx = jnp.where(vals[j] == target, j, idx)
    return idx


def _select(idx, vals):
    out = vals[-1]
    for j in range(len(vals) - 2, -1, -1):
        out = jnp.where(idx == j, vals[j], out)
    return out


def _route(logits_t):
    rows = [logits_t[e:e + 1, :] for e in range(N_EXPERTS)]
    m = functools.reduce(jnp.maximum, rows)
    ex = [jnp.exp(r - m) for r in rows]
    z = functools.reduce(jnp.add, ex)
    sc = [e / z for e in ex]
    v1s, v2s, i1s, i2s, gss = [], [], [], [], []
    for g in range(N_GROUPS):
        a = sc[g * EXPERTS_PER_GROUP:(g + 1) * EXPERTS_PER_GROUP]
        v1 = functools.reduce(jnp.maximum, a)
        i1 = _first_index(a, v1)
        rest = [jnp.where(i1 == j, -1.0, a[j]) for j in range(EXPERTS_PER_GROUP)]
        v2 = functools.reduce(jnp.maximum, rest)
        i2 = _first_index(rest, v2)
        v1s.append(v1); v2s.append(v2); i1s.append(i1); i2s.append(i2); gss.append(v1 + v2)
    grp = _first_index(gss, functools.reduce(jnp.maximum, gss))
    v1 = _select(grp, v1s)
    v2 = _select(grp, v2s)
    e1 = _select(grp, i1s) + grp * EXPERTS_PER_GROUP
    e2 = _select(grp, i2s) + grp * EXPERTS_PER_GROUP
    den = v1 + v2
    w1 = v1 / den
    w2 = v2 / den
    gates = [jnp.where(e1 == e, w1, 0.0) + jnp.where(e2 == e, w2, 0.0) for e in range(N_EXPERTS)]
    return jnp.concatenate(gates, axis=0)


def _outproj_kernel(a_ref, r_ref, c_ref, x_ref, w_ref, g_ref, b_ref, wr_ref, br_ref,
                    x1_ref, x1b_ref, gates_ref, *, alpha):
    mix = (_dot(a_ref[...], w_ref[0:A_WIDTH, :])
           + _dot(r_ref[...], w_ref[A_WIDTH:A_WIDTH + R_WIDTH, :])
           + _dot(c_ref[...], w_ref[A_WIDTH + R_WIDTH:, :]))
    x1 = _layer_norm(alpha * x_ref[...] + mix, g_ref[...], b_ref[...])
    x1_ref[...] = x1
    x1b_ref[...] = x1.astype(BF16)
    logits_t = lax.dot_general(wr_ref[...], x1, (((1,), (1,)), ((), ())),
                               precision=lax.Precision.HIGHEST,
                               preferred_element_type=F32) + br_ref[...]
    gates_t = _route(logits_t)
    tm = gates_t.shape[1]
    padded = jnp.concatenate([gates_t, jnp.zeros((LANES - N_EXPERTS, tm), F32)], axis=0)
    gates_ref[...] = padded.T


def _outproj(attn, ret, conv, x, w_bf, ln_g, ln_b, wr_t, br, alpha, tm):
    n = x.shape[0]
    row = lambda i: (i, 0)
    const = lambda i: (0, 0)
    return pl.pallas_call(
        functools.partial(_outproj_kernel, alpha=alpha),
        grid=(n // tm,),
        in_specs=[pl.BlockSpec((tm, A_WIDTH), row), pl.BlockSpec((tm, R_WIDTH), row),
                  pl.BlockSpec((tm, C_WIDTH), row), pl.BlockSpec((tm, D_MODEL), row),
                  pl.BlockSpec((D_MODEL, D_MODEL), const),
                  pl.BlockSpec((1, D_MODEL), const), pl.BlockSpec((1, D_MODEL), const),
                  pl.BlockSpec((N_EXPERTS, D_MODEL), const), pl.BlockSpec((N_EXPERTS, 1), const)],
        out_specs=[pl.BlockSpec((tm, D_MODEL), row), pl.BlockSpec((tm, D_MODEL), row),
                   pl.BlockSpec((tm, LANES), row)],
        out_shape=[jax.ShapeDtypeStruct((n, D_MODEL), F32), jax.ShapeDtypeStruct((n, D_MODEL), BF16),
                   jax.ShapeDtypeStruct((n, LANES), F32)],
        compiler_params=_params(("arbitrary",), 40),
        name="outproj",
    )(attn, ret, conv, x, w_bf, ln_g.reshape(1, D_MODEL), ln_b.reshape(1, D_MODEL), wr_t, br)


def _moe_kernel(x1b_ref, x1_ref, gates_ref, p_ref, wg_ref, wu_ref, wd_ref, g_ref, b_ref,
                wpg_ref, bpg_ref, wpp_ref, out_ref, acc_scr, *, alpha):
    e = pl.program_id(1)

    @pl.when(e == 0)
    def _():
        acc_scr[...] = jnp.zeros(acc_scr.shape, F32)

    xb = x1b_ref[...]
    hg = _dot(xb, wg_ref[...])
    hu = _dot(xb, wu_ref[...])
    gates = gates_ref[...]
    lane = lax.broadcasted_iota(jnp.int32, gates.shape, 1)
    ge = jnp.sum(jnp.where(lane == e, gates, 0.0), axis=1, keepdims=True)
    h = (hg * _sigmoid(hg) * hu * ge).astype(BF16)
    acc_scr[...] += _dot(h, wd_ref[...])

    @pl.when(e == N_EXPERTS - 1)
    def _():
        x2 = _layer_norm(alpha * x1_ref[...] + acc_scr[...], g_ref[...], b_ref[...])
        gate = _sigmoid(_dot(x2.astype(BF16), wpg_ref[...]) + bpg_ref[...])
        out_ref[...] = x2 + gate * _dot(p_ref[...].astype(BF16), wpp_ref[...])


def _moe(x1b, x1, gates, p, wg, wu, wd, ln_g, ln_b, wpg, bpg, wpp, alpha, tm):
    n = x1.shape[0]
    row = lambda i, e: (i, 0)
    const = lambda i, e: (0, 0)
    return pl.pallas_call(
        functools.partial(_moe_kernel, alpha=alpha),
        grid=(n // tm, N_EXPERTS),
        in_specs=[pl.BlockSpec((tm, D_MODEL), row), pl.BlockSpec((tm, D_MODEL), row),
                  pl.BlockSpec((tm, LANES), row), pl.BlockSpec((tm, PLE_DIM), row),
                  pl.BlockSpec((None, D_MODEL, D_EXPERT), lambda i, e: (e, 0, 0)),
                  pl.BlockSpec((None, D_MODEL, D_EXPERT), lambda i, e: (e, 0, 0)),
                  pl.BlockSpec((None, D_EXPERT, D_MODEL), lambda i, e: (e, 0, 0)),
                  pl.BlockSpec((1, D_MODEL), const), pl.BlockSpec((1, D_MODEL), const),
                  pl.BlockSpec((D_MODEL, D_MODEL), const), pl.BlockSpec((1, D_MODEL), const),
                  pl.BlockSpec((PLE_DIM, D_MODEL), const)],
        out_specs=pl.BlockSpec((tm, D_MODEL), row),
        out_shape=jax.ShapeDtypeStruct((n, D_MODEL), F32),
        scratch_shapes=[pltpu.VMEM((tm, D_MODEL), F32)],
        compiler_params=_params(("arbitrary", "arbitrary"), 48),
        name="moe",
    )(x1b, x1, gates, p, wg, wu, wd, ln_g.reshape(1, D_MODEL), ln_b.reshape(1, D_MODEL),
      wpg, bpg.reshape(1, D_MODEL), wpp)


def _rope_tables(pos, rows):
    half = R_KEY_DIM // 2
    inv_freq = 1.0 / (ROPE_BASE ** jnp.linspace(0.0, 1.0, half, dtype=jnp.float32))
    ang = pos.astype(jnp.float32)[:, None] * inv_freq[None, :]
    cos = jnp.cos(ang)
    sin = jnp.sin(ang)
    cos_t = jnp.tile(cos, (rows // pos.shape[0], 4))
    sin_t = jnp.tile(jnp.concatenate([-sin, sin], axis=1), (rows // pos.shape[0], 2))
    return cos_t, sin_t


def kernel(x_prompt, x_sample, p_prompt, p_sample, cache_k, cache_v, state_ret, state_conv, ln_emb_g, ln_emb_b, rel_bias, w_router, b_router, w_in, lam_q1, lam_k1, lam_q2, lam_k2, subln_g, dw_w, dw_b, conv_ln_g, conv_ln_b, w_out, ln1_g, ln1_b, w_exp_gate, w_exp_up, w_exp_down, ln2_g, ln2_b, w_ple_gate, b_ple_gate, w_ple_proj):
    B, T, D = x_prompt.shape
    Bs, Ts, _ = x_sample.shape
    depth = w_in.shape[0]
    past = cache_k.shape[2]
    n_p, n_s = B * T, Bs * Ts
    alpha = (2 * depth) ** 0.25

    tm_p = 512 if n_p % 512 == 0 else n_p
    tm_s = n_s
    TB = 2 * LANES
    c_p = min(T, 2 * LANES)

    pos_p = jnp.arange(T, dtype=jnp.int32)
    pos_s = past + jnp.arange(Ts, dtype=jnp.int32)
    rope_p = _rope_tables(pos_p, max(T, tm_p))
    rope_s = _rope_tables(pos_s, max(Ts, tm_s))
    rel_p = (jnp.arange(2 * TB, dtype=jnp.int32)[None, :] - TB) - jnp.arange(TB, dtype=jnp.int32)[:, None]
    padk = -(-(past + Ts) // LANES) * LANES
    rel_s = jnp.arange(padk, dtype=jnp.int32)[None, :] - pos_s[:, None]
    bias_p = _bias_table(rel_bias, _t5_bucket(rel_p))
    bias_s = _bias_table(rel_bias, _t5_bucket(rel_s))
    tabs_p = _retention_tables(c_p)
    tabs_s = _retention_tables(Ts)

    wr_t = w_router.T
    br = b_router.reshape(N_EXPERTS, 1)
    ck = cache_k.reshape(depth, Bs, past, A_WIDTH)
    cv = cache_v.reshape(depth, Bs, past, A_WIDTH)
    ret0_p = jnp.zeros((B, R_HEADS, R_KEY_DIM, R_V_DIM), F32)
    conv0_p = jnp.zeros((B, CONV_WIDTH - 1, C_WIDTH), F32)

    xp = x_prompt.reshape(n_p, D)
    xs = x_sample.reshape(n_s, D)
    outs = {k: [] for k in ("kp", "vp", "rp", "cp", "ks", "vs", "rs", "cs")}
    for l in range(depth):
        lam_init = 0.8 - 0.6 * math.exp(-0.3 * l)
        lam4 = jnp.stack([lam_q1[l], lam_k1[l], lam_q2[l], lam_k2[l]])
        g_sub = subln_g[l].reshape(1, A_V_DIM)
        w_in_b = w_in[l].astype(BF16)
        w_out_b = w_out[l].astype(BF16)
        wg, wu, wd = (w_exp_gate[l].astype(BF16), w_exp_up[l].astype(BF16), w_exp_down[l].astype(BF16))
        wpg, wpp = w_ple_gate[l].astype(BF16), w_ple_proj[l].astype(BF16)
        ln = (ln_emb_g, ln_emb_b) if l == 0 else None

        def channel(x, attn, ret, conv, p_l, tm):
            x1, x1b, gates = _outproj(attn, ret, conv, x, w_out_b, ln1_g[l], ln1_b[l], wr_t, br, alpha, tm)
            return _moe(x1b, x1, gates, p_l, wg, wu, wd, ln2_g[l], ln2_b[l], wpg, b_ple_gate[l], wpp, alpha, tm)

        xp, q, k, kb, v, vb, rq, rk, rv, rg, u = _inproj(xp, w_in_b, rope_p[0], rope_p[1], ln, tm_p)
        attn = _attn_prompt(lam4, g_sub, bias_p, q, kb, vb, B, T, lam_init)
        ret, rstate = _retention(rq, rk, rv, rg, ret0_p, tabs_p, B, T, c_p)
        conv, ctail = _conv(u, conv0_p, dw_w[l], dw_b[l], conv_ln_g[l], conv_ln_b[l], B, T)
        xp = channel(xp, attn, ret, conv, p_prompt[l].reshape(n_p, PLE_DIM), tm_p)
        outs["kp"].append(k.reshape(B, T, A_HEADS, 2 * HEAD_DIM))
        outs["vp"].append(v.reshape(B, T, A_HEADS, A_V_DIM))
        outs["rp"].append(rstate)
        outs["cp"].append(ctail)

        xs, q, k, kb, v, vb, rq, rk, rv, rg, u = _inproj(xs, w_in_b, rope_s[0], rope_s[1], ln, tm_s)
        attn = _attn_decode(lam4, g_sub, bias_s, q, ck, cv, kb, vb, l, Bs, Ts, lam_init)
        ret, rstate = _retention(rq, rk, rv, rg, state_ret[l], tabs_s, Bs, Ts, Ts)
        conv, ctail = _conv(u, state_conv[l], dw_w[l], dw_b[l], conv_ln_g[l], conv_ln_b[l], Bs, Ts)
        xs = channel(xs, attn, ret, conv, p_sample[l].reshape(n_s, PLE_DIM), tm_s)
        outs["ks"].append(k.reshape(Bs, Ts, A_HEADS, 2 * HEAD_DIM))
        outs["vs"].append(v.reshape(Bs, Ts, A_HEADS, A_V_DIM))
        outs["rs"].append(rstate)
        outs["cs"].append(ctail)

    return (xp.reshape(B, T, D), xs.reshape(Bs, Ts, D),
            jnp.stack(outs["kp"]), jnp.stack(outs["vp"]), jnp.stack(outs["rp"]), jnp.stack(outs["cp"]),
            jnp.stack(outs["ks"]), jnp.stack(outs["vs"]), jnp.stack(outs["rs"]), jnp.stack(outs["cs"]))
```

```python
import functools
import math

import jax
import jax.numpy as jnp
from jax import lax
from jax.experimental import pallas as pl
from jax.experimental.pallas import tpu as pltpu

F32 = jnp.float32
BF16 = jnp.bfloat16

D_MODEL = 1024
CHUNK = 64
HEAD_DIM = 64
A_HEADS = 4
A_V_DIM = 128
A_WIDTH = 512
R_HEADS = 4
R_KEY_DIM = 64
R_V_DIM = 64
R_WIDTH = 256
C_WIDTH = 256
CONV_WIDTH = 31
IN_WIDTH = 3072
N_BUCKETS = 32
MAX_DISTANCE = 128
ROPE_BASE = 10000.0
N_EXPERTS = 16
N_GROUPS = 4
EXPERTS_PER_GROUP = 4
D_EXPERT = 512
PLE_DIM = 256
LN_EPS = 1e-5
NEG_INF = -1e30
LOG2E = 1.4426950408889634

LANES = 128
_LOG2_CHUNK = 6
assert CHUNK == HEAD_DIM == R_KEY_DIM == R_V_DIM == 1 << _LOG2_CHUNK
MIB = 1024 * 1024

_OFF_Q, _OFF_K, _OFF_V = 0, 512, 1024
_OFF_RQ, _OFF_RK, _OFF_RV, _OFF_RG, _OFF_C = 1536, 1792, 2048, 2304, 2560


def _params(sem, vmem_mib):
    return pltpu.CompilerParams(dimension_semantics=sem, vmem_limit_bytes=vmem_mib * MIB)


def _layer_norm(x, g, b):
    mu = jnp.mean(x, axis=-1, keepdims=True)
    xc = x - mu
    var = jnp.mean(xc * xc, axis=-1, keepdims=True)
    return xc * lax.rsqrt(var + LN_EPS) * g + b


def _sigmoid(x):
    return 1.0 / (1.0 + jnp.exp(-x))


def _dot(a, b):
    return jnp.dot(a, b, preferred_element_type=F32)


def _dot_nt(a, b):
    return lax.dot_general(a, b, (((1,), (1,)), ((), ())), preferred_element_type=F32)


def _dot_tn(a, b):
    return lax.dot_general(a, b, (((0,), (0,)), ((), ())), preferred_element_type=F32)


def _bias_kernel(relb_ref, idx_ref, out_ref):
    idx = idx_ref[...]
    for h in range(A_HEADS):
        acc = jnp.zeros(idx.shape, F32)
        for b in range(N_BUCKETS):
            acc = jnp.where(idx == b, relb_ref[b, h], acc)
        out_ref[h] = acc


def _bias_table(rel_bias, idx):
    r, c = idx.shape
    return pl.pallas_call(
        _bias_kernel,
        out_shape=jax.ShapeDtypeStruct((A_HEADS, r, c), F32),
        in_specs=[pl.BlockSpec(memory_space=pltpu.SMEM),
                  pl.BlockSpec(memory_space=pltpu.VMEM)],
        out_specs=pl.BlockSpec(memory_space=pltpu.VMEM),
        name="bias_table",
    )(rel_bias, idx)


def _t5_bucket(rel):
    nb = N_BUCKETS // 2
    max_exact = nb // 2
    n = jnp.abs(rel)
    nf = jnp.maximum(n, 1).astype(jnp.float32)
    large = max_exact + (jnp.log(nf / max_exact) / math.log(MAX_DISTANCE / max_exact)
                         * (nb - max_exact)).astype(jnp.int32)
    large = jnp.minimum(large, nb - 1)
    return jnp.where(rel > 0, nb, 0) + jnp.where(n < max_exact, n, large)


def _rotary128(x, cos, sin_signed, lo32):
    partner = jnp.where(lo32, pltpu.roll(x, 96, 1), pltpu.roll(x, 32, 1))
    return x * cos + partner * sin_signed


def _inproj_kernel(*refs, apply_ln):
    if apply_ln:
        x_ref, g_ref, b_ref, w_ref, cos_ref, sin_ref = refs[:6]
        xn_ref = refs[6]
        outs = refs[7:]
    else:
        x_ref, w_ref, cos_ref, sin_ref = refs[:4]
        outs = refs[4:]
    q_ref, k_ref, kb_ref, v_ref, vb_ref, rq_ref, rk_ref, rv_ref, rg_ref, u_ref = outs

    x = x_ref[...]
    if apply_ln:
        x = _layer_norm(x, g_ref[...], b_ref[...])
        xn_ref[...] = x
    xb = x.astype(BF16)

    def mm(c0, c1):
        return _dot(xb, w_ref[:, c0:c1])

    q_ref[...] = (mm(_OFF_Q, _OFF_K) * (LOG2E * HEAD_DIM ** -0.5)).astype(BF16)
    a = mm(_OFF_K, _OFF_V)
    k_ref[...] = a
    kb_ref[...] = a.astype(BF16)
    a = mm(_OFF_V, _OFF_RQ)
    v_ref[...] = a
    vb_ref[...] = a.astype(BF16)

    cos = cos_ref[...]
    sin = sin_ref[...]
    lane = lax.broadcasted_iota(jnp.int32, cos.shape, 1)
    lo32 = (lane & 63) < 32

    def rot(a, scale):
        parts = [_rotary128(a[:, c * LANES:(c + 1) * LANES], cos, sin, lo32) for c in range(2)]
        r = jnp.concatenate(parts, axis=1)
        if scale != 1.0:
            r = r * scale
        return r.astype(BF16)

    rq_ref[...] = rot(mm(_OFF_RQ, _OFF_RK), 1.0)
    rk_ref[...] = rot(mm(_OFF_RK, _OFF_RV), R_KEY_DIM ** -0.5)
    rv_ref[...] = mm(_OFF_RV, _OFF_RG).astype(BF16)
    a = mm(_OFF_RG, _OFF_C)
    rg_ref[...] = a * _sigmoid(a)
    a = mm(_OFF_C, IN_WIDTH)
    u_ref[...] = a[:, :C_WIDTH] * _sigmoid(a[:, C_WIDTH:])


def _inproj(x, w_bf, cos_t, sin_t, ln, tm):
    n = x.shape[0]
    nblk = cos_t.shape[0] // tm
    row = lambda i: (i, 0)
    const = lambda i: (0, 0)
    tab = lambda i: (i % nblk, 0)
    in_specs = [pl.BlockSpec((tm, D_MODEL), row)]
    args = [x]
    if ln is not None:
        in_specs += [pl.BlockSpec((1, D_MODEL), const)] * 2
        args += [ln[0].reshape(1, D_MODEL), ln[1].reshape(1, D_MODEL)]
    in_specs += [pl.BlockSpec((D_MODEL, IN_WIDTH), const),
                 pl.BlockSpec((tm, LANES), tab), pl.BlockSpec((tm, LANES), tab)]
    args += [w_bf, cos_t, sin_t]

    def o(width, dt):
        return jax.ShapeDtypeStruct((n, width), dt), pl.BlockSpec((tm, width), row)

    outs = []
    if ln is not None:
        outs.append(o(D_MODEL, F32))
    outs += [o(512, BF16), o(512, F32), o(512, BF16), o(512, F32), o(512, BF16),
             o(256, BF16), o(256, BF16), o(256, BF16), o(256, F32), o(256, F32)]
    res = pl.pallas_call(
        functools.partial(_inproj_kernel, apply_ln=ln is not None),
        grid=(n // tm,),
        in_specs=in_specs,
        out_specs=[s for _, s in outs],
        out_shape=[s for s, _ in outs],
        compiler_params=_params(("arbitrary",), 52),
        name="inproj",
    )(*args)
    if ln is None:
        res = [x] + list(res)
    return res


def _lambda(lam_ref, lam_init):
    lv = lam_ref[...]
    s1 = jnp.sum(lv[0:1] * lv[1:2], axis=-1, keepdims=True)
    s2 = jnp.sum(lv[2:3] * lv[3:4], axis=-1, keepdims=True)
    return jnp.exp(s1) - jnp.exp(s2) + lam_init


def _stack_maps(q):
    lane = lax.broadcasted_iota(jnp.int32, q.shape, 1)
    lo = lane < HEAD_DIM
    z = jnp.zeros_like(q)
    return jnp.concatenate([jnp.where(lo, q, z), jnp.where(lo, z, q)], axis=0)


def _diff_finish(o2, t, lam, g, lam_init):
    o = o2[:t] - lam * o2[t:]
    ms = jnp.mean(o * o, axis=-1, keepdims=True)
    return (o * lax.rsqrt(ms + LN_EPS) * g * (1.0 - lam_init)).astype(BF16)


def _attn_prompt_kernel(lam_ref, g_ref, bias_ref, q_ref, k_ref, v_ref, o_ref,
                        s_scr, m_scr, bp_scr, bd_scr, *, T, TB, lam_init):
    lam = _lambda(lam_ref, lam_init)
    g = g_ref[...]
    bt = bias_ref[...]
    bt = (bt - bt[0:1, 0:1]) * LOG2E
    bp_scr[...] = bt[:, :TB]
    bd_scr[...] = bt[:, TB:]
    half = TB // 2

    for qi in range(T // TB):
        r0 = qi * TB
        q2 = _stack_maps(q_ref[r0:r0 + TB, :])
        m = None
        for ki in range(qi + 1):
            s = _dot_nt(q2, k_ref[ki * TB:(ki + 1) * TB, :])
            if ki == qi - 1:
                s = (s.reshape(2, TB, TB) + bp_scr[...][None]).reshape(2 * TB, TB)
            elif ki == qi:
                row = lax.broadcasted_iota(jnp.int32, (TB, TB), 0)
                col = lax.broadcasted_iota(jnp.int32, (TB, TB), 1)
                vis = (col >> _LOG2_CHUNK) <= (row >> _LOG2_CHUNK)
                s = s.reshape(2, TB, TB) + bd_scr[...][None]
                s = jnp.where(vis[None], s, NEG_INF).reshape(2 * TB, TB)
            s_scr[ki] = s
            mt = jnp.maximum(s[:, :half], s[:, half:])
            m = mt if m is None else jnp.maximum(m, mt)
        m_scr[...] = jnp.broadcast_to(jnp.max(m, axis=1, keepdims=True), m_scr.shape)
        l = None
        acc = None
        for ki in range(qi + 1):
            mb = m_scr[...]
            p = jnp.exp2(s_scr[ki] - jnp.concatenate([mb, mb], axis=1))
            lt = p[:, :half] + p[:, half:]
            pv = _dot(p.astype(BF16), v_ref[ki * TB:(ki + 1) * TB, :])
            l = lt if l is None else l + lt
            acc = pv if acc is None else acc + pv
        l_row = jnp.sum(l, axis=1, keepdims=True)
        o_ref[r0:r0 + TB, :] = _diff_finish(acc / l_row, TB, lam, g, lam_init)


def _attn_prompt(lam4, g, bias, q, k, v, B, T, lam_init):
    TB = 2 * LANES
    assert T % TB == 0 and TB % CHUNK == 0
    n = B * T
    blk = pl.BlockSpec((T, A_V_DIM), lambda b, h: (b, h))
    return pl.pallas_call(
        functools.partial(_attn_prompt_kernel, T=T, TB=TB, lam_init=lam_init),
        grid=(B, A_HEADS),
        in_specs=[pl.BlockSpec((4, HEAD_DIM), lambda b, h: (0, 0)),
                  pl.BlockSpec((1, A_V_DIM), lambda b, h: (0, 0)),
                  pl.BlockSpec((None, TB, 2 * TB), lambda b, h: (h, 0, 0)),
                  blk, blk, blk],
        out_specs=blk,
        out_shape=jax.ShapeDtypeStruct((n, A_WIDTH), BF16),
        scratch_shapes=[pltpu.VMEM((T // TB, 2 * TB, TB), F32),
                        pltpu.VMEM((2 * TB, LANES), F32),
                        pltpu.VMEM((TB, TB), F32),
                        pltpu.VMEM((TB, TB), F32)],
        compiler_params=_params(("arbitrary", "arbitrary"), 32),
        name="attn_prompt",
    )(lam4, g, bias, q, k, v)


def _attn_decode_kernel(lam_ref, g_ref, bias_ref, q_ref, kc_ref, vc_ref, kn_ref, vn_ref, o_ref,
                        *, Ts, past, lam_init):
    lam = _lambda(lam_ref, lam_init)
    q2 = _stack_maps(q_ref[...])
    bias = bias_ref[...] * LOG2E
    s_p = _dot_nt(q2, kc_ref[...].astype(BF16)).reshape(2, Ts, past) + bias[:, :past][None]
    s_n = _dot_nt(q2, kn_ref[...]).reshape(2, Ts, Ts) + bias[:, past:past + Ts][None]
    s_p = s_p.reshape(2 * Ts, past)
    s_n = s_n.reshape(2 * Ts, Ts)
    m = jnp.maximum(jnp.max(s_p, axis=1, keepdims=True), jnp.max(s_n, axis=1, keepdims=True))
    p_p = jnp.exp2(s_p - m)
    p_n = jnp.exp2(s_n - m)
    l = jnp.sum(p_p, axis=1, keepdims=True) + jnp.sum(p_n, axis=1, keepdims=True)
    acc = _dot(p_p.astype(BF16), vc_ref[...].astype(BF16)) + _dot(p_n.astype(BF16), vn_ref[...])
    o_ref[...] = _diff_finish(acc / l, Ts, lam, g_ref[...], lam_init)


def _attn_decode(lam4, g, bias, q, cache_k, cache_v, kn, vn, layer, Bs, Ts, lam_init):
    past = cache_k.shape[2]
    assert past % CHUNK == 0 and Ts <= CHUNK
    padk = bias.shape[2]
    new = pl.BlockSpec((Ts, A_V_DIM), lambda b, h: (b, h))
    cache = pl.BlockSpec((None, None, past, A_V_DIM), lambda b, h: (layer, b, 0, h))
    return pl.pallas_call(
        functools.partial(_attn_decode_kernel, Ts=Ts, past=past, lam_init=lam_init),
        grid=(Bs, A_HEADS),
        in_specs=[pl.BlockSpec((4, HEAD_DIM), lambda b, h: (0, 0)),
                  pl.BlockSpec((1, A_V_DIM), lambda b, h: (0, 0)),
                  pl.BlockSpec((None, Ts, padk), lambda b, h: (h, 0, 0)),
                  new, cache, cache, new, new],
        out_specs=new,
        out_shape=jax.ShapeDtypeStruct((Bs * Ts, A_WIDTH), BF16),
        compiler_params=_params(("arbitrary", "arbitrary"), 40),
        name="attn_decode",
    )(lam4, g, bias, q, cache_k, cache_v, kn, vn)


def _ret_kernel(q_ref, k_ref, v_ref, g_ref, s0_ref, dm_ref, qd_ref, kd_ref, cd_ref,
                o_ref, sn_ref, *, T, C):
    z = jnp.zeros((R_KEY_DIM, R_V_DIM), F32)
    state = jnp.concatenate([jnp.concatenate([s0_ref[0], z], axis=1),
                             jnp.concatenate([z, s0_ref[1]], axis=1)], axis=0)
    r = lax.broadcasted_iota(jnp.int32, (LANES, LANES), 0)
    c = lax.broadcasted_iota(jnp.int32, (LANES, LANES), 1)
    same_head = (r >> _LOG2_CHUNK) == (c >> _LOG2_CHUNK)
    ones_bd = jnp.where(same_head, 1.0, 0.0).astype(BF16)
    lo = lax.broadcasted_iota(jnp.int32, (C, LANES), 1) < R_V_DIM
    cd = cd_ref[...]

    def chunk(n, state):
        r0 = pl.multiple_of(n * C, C)
        q = q_ref[pl.ds(r0, C), :]
        k = k_ref[pl.ds(r0, C), :]
        v = v_ref[pl.ds(r0, C), :]
        p = (_dot_nt(_stack_maps(q), k) * dm_ref[...]).astype(BF16)
        o2 = _dot(p, v)
        inner = jnp.where(lo, o2[:C], o2[C:])
        qd = (q.astype(F32) * qd_ref[...]).astype(BF16)
        o = inner + _dot(qd, state.astype(BF16))
        vk = (v.astype(F32) * kd_ref[...]).astype(BF16)
        new_state = cd * state + jnp.where(same_head, _dot_tn(k, vk), 0.0)
        oo = o * o
        hi = oo.astype(BF16)
        lo_part = (oo - hi.astype(F32)).astype(BF16)
        ss = _dot(hi, ones_bd) + _dot(lo_part, ones_bd)
        out = o * lax.rsqrt(ss * (1.0 / R_V_DIM) + LN_EPS) * g_ref[pl.ds(r0, C), :]
        o_ref[pl.ds(r0, C), :] = out.astype(BF16)
        return new_state

    state = lax.fori_loop(0, T // C, chunk, state)
    sn_ref[0] = state[:R_KEY_DIM, :R_V_DIM]
    sn_ref[1] = state[R_KEY_DIM:, R_V_DIM:]


def _retention(rq, rk, rv, rg, state0, tabs, B, T, C):
    dm, qd, kd, cd = tabs
    blk = pl.BlockSpec((T, LANES), lambda b, hp: (b, hp))
    st = pl.BlockSpec((None, 2, R_KEY_DIM, R_V_DIM), lambda b, hp: (b, hp, 0, 0))
    return pl.pallas_call(
        functools.partial(_ret_kernel, T=T, C=C),
        grid=(B, 2),
        in_specs=[blk, blk, blk, blk, st,
                  pl.BlockSpec((None, 2 * C, C), lambda b, hp: (hp, 0, 0)),
                  pl.BlockSpec((None, C, LANES), lambda b, hp: (hp, 0, 0)),
                  pl.BlockSpec((None, C, LANES), lambda b, hp: (hp, 0, 0)),
                  pl.BlockSpec((None, 1, LANES), lambda b, hp: (hp, 0, 0))],
        out_specs=[blk, st],
        out_shape=[jax.ShapeDtypeStruct((B * T, R_WIDTH), BF16),
                   jax.ShapeDtypeStruct((B, R_HEADS, R_KEY_DIM, R_V_DIM), F32)],
        compiler_params=_params(("arbitrary", "arbitrary"), 32),
        name="retention",
    )(rq, rk, rv, rg, state0, dm, qd, kd, cd)


def _retention_tables(C):
    h = jnp.arange(R_HEADS, dtype=F32)
    log_g = jnp.log1p(-jnp.exp2(-5.0 - h))
    i = jnp.arange(C, dtype=F32)
    diff = i[:, None] - i[None, :]
    dmask = jnp.where(diff >= 0, jnp.exp(jnp.maximum(diff, 0.0)[None] * log_g[:, None, None]), 0.0)
    dm = dmask.reshape(2, 2 * C, C)
    q_dec = jnp.exp((i + 1.0)[None, :] * log_g[:, None])
    k_dec = jnp.exp((C - 1 - i)[None, :] * log_g[:, None])
    c_dec = jnp.exp(C * log_g)

    def lanes(t):
        t = jnp.repeat(t[:, :, None], R_V_DIM, axis=2).reshape(2, 2, C, R_V_DIM)
        return jnp.concatenate([t[:, 0], t[:, 1]], axis=-1)

    cd = jnp.repeat(c_dec[:, None], R_V_DIM, axis=1).reshape(2, 1, LANES)
    return dm, lanes(q_dec), lanes(k_dec), cd


_CONV_PAD = 32


def _conv_kernel(u_ref, h_ref, w_ref, b_ref, g_ref, be_ref, o_ref, t_ref, up_scr, *, T, RT):
    hist = CONV_WIDTH - 1
    off = _CONV_PAD - hist
    up_scr[off:_CONV_PAD, :] = h_ref[...]
    up_scr[_CONV_PAD:_CONV_PAD + T, :] = u_ref[...]
    bias = b_ref[...]
    g = g_ref[...]
    be = be_ref[...]
    for t0 in range(0, T, RT):
        acc = jnp.zeros((RT, C_WIDTH), F32)
        for j in range(CONV_WIDTH):
            acc = acc + up_scr[t0 + j + off:t0 + j + off + RT, :] * w_ref[j:j + 1, :]
        y = _layer_norm(acc + bias, g, be)
        o_ref[t0:t0 + RT, :] = (y * _sigmoid(y)).astype(BF16)
    t_ref[...] = up_scr[T + off:T + _CONV_PAD, :]


def _conv(u, hist, dw_w, dw_b, ln_g, ln_b, B, T):
    RT = min(T, LANES)
    assert T % RT == 0
    hl = CONV_WIDTH - 1
    vec = pl.BlockSpec((1, C_WIDTH), lambda b: (0, 0))
    hb = pl.BlockSpec((None, hl, C_WIDTH), lambda b: (b, 0, 0))
    return pl.pallas_call(
        functools.partial(_conv_kernel, T=T, RT=RT),
        grid=(B,),
        in_specs=[pl.BlockSpec((T, C_WIDTH), lambda b: (b, 0)), hb,
                  pl.BlockSpec((CONV_WIDTH, C_WIDTH), lambda b: (0, 0)), vec, vec, vec],
        out_specs=[pl.BlockSpec((T, C_WIDTH), lambda b: (b, 0)), hb],
        out_shape=[jax.ShapeDtypeStruct((B * T, C_WIDTH), BF16),
                   jax.ShapeDtypeStruct((B, hl, C_WIDTH), F32)],
        scratch_shapes=[pltpu.VMEM((T + _CONV_PAD, C_WIDTH), F32)],
        compiler_params=_params(("arbitrary",), 32),
        name="conv",
    )(u, hist, dw_w, dw_b.reshape(1, C_WIDTH), ln_g.reshape(1, C_WIDTH), ln_b.reshape(1, C_WIDTH))


def _first_index(vals, target):
    idx = jnp.full(target.shape, len(vals) - 1, jnp.int32)
    for j in range(len(vals) - 2, -1, -1):
        idx = jnp.where(vals[j] == target, j, idx)
    return idx


def _select(idx, vals):
    out = vals[-1]
    for j in range(len(vals) - 2, -1, -1):
        out = jnp.where(idx == j, vals[j], out)
    return out


def _route(logits_t):
    rows = [logits_t[e:e + 1, :] for e in range(N_EXPERTS)]
    m = functools.reduce(jnp.maximum, rows)
    ex = [jnp.exp(r - m) for r in rows]
    z = functools.reduce(jnp.add, ex)
    sc = [e / z for e in ex]
    v1s, v2s, i1s, i2s, gss = [], [], [], [], []
    for g in range(N_GROUPS):
        a = sc[g * EXPERTS_PER_GROUP:(g + 1) * EXPERTS_PER_GROUP]
        v1 = functools.reduce(jnp.maximum, a)
        i1 = _first_index(a, v1)
        rest = [jnp.where(i1 == j, -1.0, a[j]) for j in range(EXPERTS_PER_GROUP)]
        v2 = functools.reduce(jnp.maximum, rest)
        i2 = _first_index(rest, v2)
        v1s.append(v1); v2s.append(v2); i1s.append(i1); i2s.append(i2); gss.append(v1 + v2)
    grp = _first_index(gss, functools.reduce(jnp.maximum, gss))
    v1 = _select(grp, v1s)
    v2 = _select(grp, v2s)
    e1 = _select(grp, i1s) + grp * EXPERTS_PER_GROUP
    e2 = _select(grp, i2s) + grp * EXPERTS_PER_GROUP
    den = v1 + v2
    w1 = v1 / den
    w2 = v2 / den
    gates = [jnp.where(e1 == e, w1, 0.0) + jnp.where(e2 == e, w2, 0.0) for e in range(N_EXPERTS)]
    return jnp.concatenate(gates, axis=0)


def _outproj_kernel(a_ref, r_ref, c_ref, x_ref, w_ref, g_ref, b_ref, wr_ref, br_ref,
                    x1_ref, x1b_ref, gates_ref, *, alpha):
    mix = (_dot(a_ref[...], w_ref[0:A_WIDTH, :])
           + _dot(r_ref[...], w_ref[A_WIDTH:A_WIDTH + R_WIDTH, :])
           + _dot(c_ref[...], w_ref[A_WIDTH + R_WIDTH:, :]))
    x1 = _layer_norm(alpha * x_ref[...] + mix, g_ref[...], b_ref[...])
    x1_ref[...] = x1
    x1b_ref[...] = x1.astype(BF16)
    logits_t = lax.dot_general(wr_ref[...], x1, (((1,), (1,)), ((), ())),
                               precision=lax.Precision.HIGHEST,
                               preferred_element_type=F32) + br_ref[...]
    gates_t = _route(logits_t)
    tm = gates_t.shape[1]
    padded = jnp.concatenate([gates_t, jnp.zeros((LANES - N_EXPERTS, tm), F32)], axis=0)
    gates_ref[...] = padded.T


def _outproj(attn, ret, conv, x, w_bf, ln_g, ln_b, wr_t, br, alpha, tm):
    n = x.shape[0]
    row = lambda i: (i, 0)
    const = lambda i: (0, 0)
    return pl.pallas_call(
        functools.partial(_outproj_kernel, alpha=alpha),
        grid=(n // tm,),
        in_specs=[pl.BlockSpec((tm, A_WIDTH), row), pl.BlockSpec((tm, R_WIDTH), row),
                  pl.BlockSpec((tm, C_WIDTH), row), pl.BlockSpec((tm, D_MODEL), row),
                  pl.BlockSpec((D_MODEL, D_MODEL), const),
                  pl.BlockSpec((1, D_MODEL), const), pl.BlockSpec((1, D_MODEL), const),
                  pl.BlockSpec((N_EXPERTS, D_MODEL), const), pl.BlockSpec((N_EXPERTS, 1), const)],
        out_specs=[pl.BlockSpec((tm, D_MODEL), row), pl.BlockSpec((tm, D_MODEL), row),
                   pl.BlockSpec((tm, LANES), row)],
        out_shape=[jax.ShapeDtypeStruct((n, D_MODEL), F32), jax.ShapeDtypeStruct((n, D_MODEL), BF16),
                   jax.ShapeDtypeStruct((n, LANES), F32)],
        compiler_params=_params(("arbitrary",), 40),
        name="outproj",
    )(attn, ret, conv, x, w_bf, ln_g.reshape(1, D_MODEL), ln_b.reshape(1, D_MODEL), wr_t, br)


def _moe_kernel(x1b_ref, x1_ref, gates_ref, p_ref, wg_ref, wu_ref, wd_ref, g_ref, b_ref,
                wpg_ref, bpg_ref, wpp_ref, out_ref, acc_scr, *, alpha):
    e = pl.program_id(1)

    @pl.when(e == 0)
    def _():
        acc_scr[...] = jnp.zeros(acc_scr.shape, F32)

    xb = x1b_ref[...]
    hg = _dot(xb, wg_ref[...])
    hu = _dot(xb, wu_ref[...])
    gates = gates_ref[...]
    lane = lax.broadcasted_iota(jnp.int32, gates.shape, 1)
    ge = jnp.sum(jnp.where(lane == e, gates, 0.0), axis=1, keepdims=True)
    h = (hg * _sigmoid(hg) * hu * ge).astype(BF16)
    acc_scr[...] += _dot(h, wd_ref[...])

    @pl.when(e == N_EXPERTS - 1)
    def _():
        x2 = _layer_norm(alpha * x1_ref[...] + acc_scr[...], g_ref[...], b_ref[...])
        gate = _sigmoid(_dot(x2.astype(BF16), wpg_ref[...]) + bpg_ref[...])
        out_ref[...] = x2 + gate * _dot(p_ref[...].astype(BF16), wpp_ref[...])


def _moe(x1b, x1, gates, p, wg, wu, wd, ln_g, ln_b, wpg, bpg, wpp, alpha, tm):
    n = x1.shape[0]
    row = lambda i, e: (i, 0)
    const = lambda i, e: (0, 0)
    return pl.pallas_call(
        functools.partial(_moe_kernel, alpha=alpha),
        grid=(n // tm, N_EXPERTS),
        in_specs=[pl.BlockSpec((tm, D_MODEL), row), pl.BlockSpec((tm, D_MODEL), row),
                  pl.BlockSpec((tm, LANES), row), pl.BlockSpec((tm, PLE_DIM), row),
                  pl.BlockSpec((None, D_MODEL, D_EXPERT), lambda i, e: (e, 0, 0)),
                  pl.BlockSpec((None, D_MODEL, D_EXPERT), lambda i, e: (e, 0, 0)),
                  pl.BlockSpec((None, D_EXPERT, D_MODEL), lambda i, e: (e, 0, 0)),
                  pl.BlockSpec((1, D_MODEL), const), pl.BlockSpec((1, D_MODEL), const),
                  pl.BlockSpec((D_MODEL, D_MODEL), const), pl.BlockSpec((1, D_MODEL), const),
                  pl.BlockSpec((PLE_DIM, D_MODEL), const)],
        out_specs=pl.BlockSpec((tm, D_MODEL), row),
        out_shape=jax.ShapeDtypeStruct((n, D_MODEL), F32),
        scratch_shapes=[pltpu.VMEM((tm, D_MODEL), F32)],
        compiler_params=_params(("arbitrary", "arbitrary"), 48),
        name="moe",
    )(x1b, x1, gates, p, wg, wu, wd, ln_g.reshape(1, D_MODEL), ln_b.reshape(1, D_MODEL),
      wpg, bpg.reshape(1, D_MODEL), wpp)


def _rope_tables(pos, rows):
    half = R_KEY_DIM // 2
    inv_freq = 1.0 / (ROPE_BASE ** jnp.linspace(0.0, 1.0, half, dtype=jnp.float32))
    ang = pos.astype(jnp.float32)[:, None] * inv_freq[None, :]
    cos = jnp.cos(ang)
    sin = jnp.sin(ang)
    cos_t = jnp.tile(cos, (rows // pos.shape[0], 4))
    sin_t = jnp.tile(jnp.concatenate([-sin, sin], axis=1), (rows // pos.shape[0], 2))
    return cos_t, sin_t


def kernel(x_prompt, x_sample, p_prompt, p_sample, cache_k, cache_v, state_ret, state_conv, ln_emb_g, ln_emb_b, rel_bias, w_router, b_router, w_in, lam_q1, lam_k1, lam_q2, lam_k2, subln_g, dw_w, dw_b, conv_ln_g, conv_ln_b, w_out, ln1_g, ln1_b, w_exp_gate, w_exp_up, w_exp_down, ln2_g, ln2_b, w_ple_gate, b_ple_gate, w_ple_proj):
    B, T, D = x_prompt.shape
    Bs, Ts, _ = x_sample.shape
    depth = w_in.shape[0]
    past = cache_k.shape[2]
    n_p, n_s = B * T, Bs * Ts
    alpha = (2 * depth) ** 0.25

    tm_p = 512 if n_p % 512 == 0 else n_p
    tm_s = n_s
    TB = 2 * LANES
    c_p = min(T, 2 * LANES)

    pos_p = jnp.arange(T, dtype=jnp.int32)
    pos_s = past + jnp.arange(Ts, dtype=jnp.int32)
    rope_p = _rope_tables(pos_p, max(T, tm_p))
    rope_s = _rope_tables(pos_s, max(Ts, tm_s))
    rel_p = (jnp.arange(2 * TB, dtype=jnp.int32)[None, :] - TB) - jnp.arange(TB, dtype=jnp.int32)[:, None]
    padk = -(-(past + Ts) // LANES) * LANES
    rel_s = jnp.arange(padk, dtype=jnp.int32)[None, :] - pos_s[:, None]
    bias_p = _bias_table(rel_bias, _t5_bucket(rel_p))
    bias_s = _bias_table(rel_bias, _t5_bucket(rel_s))
    tabs_p = _retention_tables(c_p)
    tabs_s = _retention_tables(Ts)

    wr_t = w_router.T
    br = b_router.reshape(N_EXPERTS, 1)
    ck = cache_k.reshape(depth, Bs, past, A_WIDTH)
    cv = cache_v.reshape(depth, Bs, past, A_WIDTH)
    ret0_p = jnp.zeros((B, R_HEADS, R_KEY_DIM, R_V_DIM), F32)
    conv0_p = jnp.zeros((B, CONV_WIDTH - 1, C_WIDTH), F32)

    xp = x_prompt.reshape(n_p, D)
    xs = x_sample.reshape(n_s, D)
    outs = {k: [] for k in ("kp", "vp", "rp", "cp", "ks", "vs", "rs", "cs")}
    for l in range(depth):
        lam_init = 0.8 - 0.6 * math.exp(-0.3 * l)
        lam4 = jnp.stack([lam_q1[l], lam_k1[l], lam_q2[l], lam_k2[l]])
        g_sub = subln_g[l].reshape(1, A_V_DIM)
        w_in_b = w_in[l].astype(BF16)
        w_out_b = w_out[l].astype(BF16)
        wg, wu, wd = (w_exp_gate[l].astype(BF16), w_exp_up[l].astype(BF16), w_exp_down[l].astype(BF16))
        wpg, wpp = w_ple_gate[l].astype(BF16), w_ple_proj[l].astype(BF16)
        ln = (ln_emb_g, ln_emb_b) if l == 0 else None

        def channel(x, attn, ret, conv, p_l, tm):
            x1, x1b, gates = _outproj(attn, ret, conv, x, w_out_b, ln1_g[l], ln1_b[l], wr_t, br, alpha, tm)
            return _moe(x1b, x1, gates, p_l, wg, wu, wd, ln2_g[l], ln2_b[l], wpg, b_ple_gate[l], wpp, alpha, tm)

        xp, q, k, kb, v, vb, rq, rk, rv, rg, u = _inproj(xp, w_in_b, rope_p[0], rope_p[1], ln, tm_p)
        attn = _attn_prompt(lam4, g_sub, bias_p, q, kb, vb, B, T, lam_init)
        ret, rstate = _retention(rq, rk, rv, rg, ret0_p, tabs_p, B, T, c_p)
        conv, ctail = _conv(u, conv0_p, dw_w[l], dw_b[l], conv_ln_g[l], conv_ln_b[l], B, T)
        xp = channel(xp, attn, ret, conv, p_prompt[l].reshape(n_p, PLE_DIM), tm_p)
        outs["kp"].append(k.reshape(B, T, A_HEADS, 2 * HEAD_DIM))
        outs["vp"].append(v.reshape(B, T, A_HEADS, A_V_DIM))
        outs["rp"].append(rstate)
        outs["cp"].append(ctail)

        xs, q, k, kb, v, vb, rq, rk, rv, rg, u = _inproj(xs, w_in_b, rope_s[0], rope_s[1], ln, tm_s)
        attn = _attn_decode(lam4, g_sub, bias_s, q, ck, cv, kb, vb, l, Bs, Ts, lam_init)
        ret, rstate = _retention(rq, rk, rv, rg, state_ret[l], tabs_s, Bs, Ts, Ts)
        conv, ctail = _conv(u, state_conv[l], dw_w[l], dw_b[l], conv_ln_g[l], conv_ln_b[l], Bs, Ts)
        xs = channel(xs, attn, ret, conv, p_sample[l].reshape(n_s, PLE_DIM), tm_s)
        outs["ks"].append(k.reshape(Bs, Ts, A_HEADS, 2 * HEAD_DIM))
        outs["vs"].append(v.reshape(Bs, Ts, A_HEADS, A_V_DIM))
        outs["rs"].append(rstate)
        outs["cs"].append(ctail)

    return (xp.reshape(B, T, D), xs.reshape(Bs, Ts, D),
            jnp.stack(outs["kp"]), jnp.stack(outs["vp"]), jnp.stack(outs["rp"]), jnp.stack(outs["cp"]),
            jnp.stack(outs["ks"]), jnp.stack(outs["vs"]), jnp.stack(outs["rs"]), jnp.stack(outs["cs"]))
```

```python
import functools
import math

import jax
import jax.numpy as jnp
from jax import lax
from jax.experimental import pallas as pl
from jax.experimental.pallas import tpu as pltpu

F32 = jnp.float32
BF16 = jnp.bfloat16

D_MODEL = 1024
CHUNK = 64
HEAD_DIM = 64
A_HEADS = 4
A_V_DIM = 128
A_WIDTH = 512
R_HEADS = 4
R_KEY_DIM = 64
R_V_DIM = 64
R_WIDTH = 256
C_WIDTH = 256
CONV_WIDTH = 31
IN_WIDTH = 3072
N_BUCKETS = 32
MAX_DISTANCE = 128
ROPE_BASE = 10000.0
N_EXPERTS = 16
N_GROUPS = 4
EXPERTS_PER_GROUP = 4
D_EXPERT = 512
PLE_DIM = 256
LN_EPS = 1e-5
NEG_INF = -1e30
LOG2E = 1.4426950408889634

LANES = 128
SUBLANES = 8
_LOG2_CHUNK = 6
assert CHUNK == HEAD_DIM == R_KEY_DIM == R_V_DIM == 1 << _LOG2_CHUNK
MIB = 1024 * 1024

_OFF_Q, _OFF_K, _OFF_V = 0, 512, 1024
_OFF_RQ, _OFF_RK, _OFF_RV, _OFF_RG, _OFF_C = 1536, 1792, 2048, 2304, 2560


def _params(sem, vmem_mib):
    return pltpu.CompilerParams(dimension_semantics=sem, vmem_limit_bytes=vmem_mib * MIB)


def _layer_norm(x, g, b):
    mu = jnp.mean(x, axis=-1, keepdims=True)
    xc = x - mu
    var = jnp.mean(xc * xc, axis=-1, keepdims=True)
    return xc * lax.rsqrt(var + LN_EPS) * g + b


def _sigmoid(x):
    return 1.0 / (1.0 + jnp.exp(-x))


def _dot(a, b):
    return jnp.dot(a, b, preferred_element_type=F32)


def _dot_nt(a, b):
    return lax.dot_general(a, b, (((1,), (1,)), ((), ())), preferred_element_type=F32)


def _dot_tn(a, b):
    return lax.dot_general(a, b, (((0,), (0,)), ((), ())), preferred_element_type=F32)


def _bias_kernel(relb_ref, idx_ref, out_ref):
    idx = idx_ref[...]
    for h in range(A_HEADS):
        acc = jnp.zeros(idx.shape, F32)
        for b in range(N_BUCKETS):
            acc = jnp.where(idx == b, relb_ref[b, h], acc)
        out_ref[h] = acc


def _bias_table(rel_bias, idx):
    r, c = idx.shape
    return pl.pallas_call(
        _bias_kernel,
        out_shape=jax.ShapeDtypeStruct((A_HEADS, r, c), F32),
        in_specs=[pl.BlockSpec(memory_space=pltpu.SMEM),
                  pl.BlockSpec(memory_space=pltpu.VMEM)],
        out_specs=pl.BlockSpec(memory_space=pltpu.VMEM),
        name="bias_table",
    )(rel_bias, idx)


def _t5_bucket(rel):
    nb = N_BUCKETS // 2
    max_exact = nb // 2
    n = jnp.abs(rel)
    nf = jnp.maximum(n, 1).astype(jnp.float32)
    large = max_exact + (jnp.log(nf / max_exact) / math.log(MAX_DISTANCE / max_exact)
                         * (nb - max_exact)).astype(jnp.int32)
    large = jnp.minimum(large, nb - 1)
    return jnp.where(rel > 0, nb, 0) + jnp.where(n < max_exact, n, large)


def _rotary128(x, cos, sin_signed, lo32):
    partner = jnp.where(lo32, pltpu.roll(x, 96, 1), pltpu.roll(x, 32, 1))
    return x * cos + partner * sin_signed


def _inproj_kernel(*refs, apply_ln):
    if apply_ln:
        x_ref, g_ref, b_ref, w_ref, cos_ref, sin_ref = refs[:6]
        xn_ref = refs[6]
        outs = refs[7:]
    else:
        x_ref, w_ref, cos_ref, sin_ref = refs[:4]
        outs = refs[4:]
    q_ref, k_ref, kb_ref, v_ref, vb_ref, rq_ref, rk_ref, rv_ref, rg_ref, u_ref = outs

    x = x_ref[...]
    if apply_ln:
        x = _layer_norm(x, g_ref[...], b_ref[...])
        xn_ref[...] = x
    xb = x.astype(BF16)

    def mm(c0, c1):
        return _dot(xb, w_ref[:, c0:c1])

    q_ref[...] = (mm(_OFF_Q, _OFF_K) * (LOG2E * HEAD_DIM ** -0.5)).astype(BF16)
    a = mm(_OFF_K, _OFF_V)
    k_ref[...] = a
    kb_ref[...] = a.astype(BF16)
    a = mm(_OFF_V, _OFF_RQ)
    v_ref[...] = a
    vb_ref[...] = a.astype(BF16)

    cos = cos_ref[...]
    sin = sin_ref[...]
    lane = lax.broadcasted_iota(jnp.int32, cos.shape, 1)
    lo32 = (lane & 63) < 32

    def rot(a, scale):
        parts = [_rotary128(a[:, c * LANES:(c + 1) * LANES], cos, sin, lo32) for c in range(2)]
        r = jnp.concatenate(parts, axis=1)
        if scale != 1.0:
            r = r * scale
        return r.astype(BF16)

    rq_ref[...] = rot(mm(_OFF_RQ, _OFF_RK), 1.0)
    rk_ref[...] = rot(mm(_OFF_RK, _OFF_RV), R_KEY_DIM ** -0.5)
    rv_ref[...] = mm(_OFF_RV, _OFF_RG).astype(BF16)
    a = mm(_OFF_RG, _OFF_C)
    rg_ref[...] = a * _sigmoid(a)
    a = mm(_OFF_C, IN_WIDTH)
    u_ref[...] = a[:, :C_WIDTH] * _sigmoid(a[:, C_WIDTH:])


def _inproj(x, w_bf, cos_t, sin_t, ln, tm):
    n = x.shape[0]
    nblk = cos_t.shape[0] // tm
    row = lambda i: (i, 0)
    const = lambda i: (0, 0)
    tab = lambda i: (i % nblk, 0)
    in_specs = [pl.BlockSpec((tm, D_MODEL), row)]
    args = [x]
    if ln is not None:
        in_specs += [pl.BlockSpec((1, D_MODEL), const)] * 2
        args += [ln[0].reshape(1, D_MODEL), ln[1].reshape(1, D_MODEL)]
    in_specs += [pl.BlockSpec((D_MODEL, IN_WIDTH), const),
                 pl.BlockSpec((tm, LANES), tab), pl.BlockSpec((tm, LANES), tab)]
    args += [w_bf, cos_t, sin_t]

    def o(width, dt):
        return jax.ShapeDtypeStruct((n, width), dt), pl.BlockSpec((tm, width), row)

    outs = []
    if ln is not None:
        outs.append(o(D_MODEL, F32))
    outs += [o(512, BF16), o(512, F32), o(512, BF16), o(512, F32), o(512, BF16),
             o(256, BF16), o(256, BF16), o(256, BF16), o(256, F32), o(256, F32)]
    res = pl.pallas_call(
        functools.partial(_inproj_kernel, apply_ln=ln is not None),
        grid=(n // tm,),
        in_specs=in_specs,
        out_specs=[s for _, s in outs],
        out_shape=[s for s, _ in outs],
        compiler_params=_params(("arbitrary",), 52),
        name="inproj",
    )(*args)
    if ln is None:
        res = [x] + list(res)
    return res


def _lambda(lam_ref, lam_init):
    lv = lam_ref[...]
    s1 = jnp.sum(lv[0:1] * lv[1:2], axis=-1, keepdims=True)
    s2 = jnp.sum(lv[2:3] * lv[3:4], axis=-1, keepdims=True)
    return jnp.exp(s1) - jnp.exp(s2) + lam_init


def _stack_maps(q):
    lane = lax.broadcasted_iota(jnp.int32, q.shape, 1)
    lo = lane < HEAD_DIM
    z = jnp.zeros_like(q)
    return jnp.concatenate([jnp.where(lo, q, z), jnp.where(lo, z, q)], axis=0)


def _diff_finish(o2, t, lam, g, lam_init):
    o = o2[:t] - lam * o2[t:]
    ms = jnp.mean(o * o, axis=-1, keepdims=True)
    return (o * lax.rsqrt(ms + LN_EPS) * g * (1.0 - lam_init)).astype(BF16)


def _attn_prompt_kernel(lam_ref, g_ref, bias_ref, q_ref, k_ref, v_ref, o_ref,
                        s_scr, m_scr, bp_scr, bd_scr, *, T, TB, lam_init):
    lam = _lambda(lam_ref, lam_init)
    g = g_ref[...]
    bt = bias_ref[...]
    bt = (bt - bt[0:1, 0:1]) * LOG2E
    bp_scr[...] = bt[:, :TB]
    bd_scr[...] = bt[:, TB:]
    half = TB // 2

    for qi in range(T // TB):
        r0 = qi * TB
        q2 = _stack_maps(q_ref[r0:r0 + TB, :])
        m = None
        for ki in range(qi + 1):
            s = _dot_nt(q2, k_ref[ki * TB:(ki + 1) * TB, :])
            if ki == qi - 1:
                s = (s.reshape(2, TB, TB) + bp_scr[...][None]).reshape(2 * TB, TB)
            elif ki == qi:
                row = lax.broadcasted_iota(jnp.int32, (TB, TB), 0)
                col = lax.broadcasted_iota(jnp.int32, (TB, TB), 1)
                vis = (col >> _LOG2_CHUNK) <= (row >> _LOG2_CHUNK)
                s = s.reshape(2, TB, TB) + bd_scr[...][None]
                s = jnp.where(vis[None], s, NEG_INF).reshape(2 * TB, TB)
            s_scr[ki] = s
            mt = jnp.maximum(s[:, :half], s[:, half:])
            m = mt if m is None else jnp.maximum(m, mt)
        m_scr[...] = jnp.broadcast_to(jnp.max(m, axis=1, keepdims=True), m_scr.shape)
        l = None
        acc = None
        for ki in range(qi + 1):
            mb = m_scr[...]
            p = jnp.exp2(s_scr[ki] - jnp.concatenate([mb, mb], axis=1))
            lt = p[:, :half] + p[:, half:]
            pv = _dot(p.astype(BF16), v_ref[ki * TB:(ki + 1) * TB, :])
            l = lt if l is None else l + lt
            acc = pv if acc is None else acc + pv
        l_row = jnp.sum(l, axis=1, keepdims=True)
        o_ref[r0:r0 + TB, :] = _diff_finish(acc / l_row, TB, lam, g, lam_init)


def _attn_prompt(lam4, g, bias, q, k, v, B, T, lam_init):
    TB = 2 * LANES
    assert T % TB == 0 and TB % CHUNK == 0
    n = B * T
    blk = pl.BlockSpec((T, A_V_DIM), lambda b, h: (b, h))
    return pl.pallas_call(
        functools.partial(_attn_prompt_kernel, T=T, TB=TB, lam_init=lam_init),
        grid=(B, A_HEADS),
        in_specs=[pl.BlockSpec((4, HEAD_DIM), lambda b, h: (0, 0)),
                  pl.BlockSpec((1, A_V_DIM), lambda b, h: (0, 0)),
                  pl.BlockSpec((None, TB, 2 * TB), lambda b, h: (h, 0, 0)),
                  blk, blk, blk],
        out_specs=blk,
        out_shape=jax.ShapeDtypeStruct((n, A_WIDTH), BF16),
        scratch_shapes=[pltpu.VMEM((T // TB, 2 * TB, TB), F32),
                        pltpu.VMEM((2 * TB, LANES), F32),
                        pltpu.VMEM((TB, TB), F32),
                        pltpu.VMEM((TB, TB), F32)],
        compiler_params=_params(("arbitrary", "arbitrary"), 32),
        name="attn_prompt",
    )(lam4, g, bias, q, k, v)


def _attn_decode_kernel(lam_ref, g_ref, bias_ref, q_ref, kc_ref, vc_ref, kn_ref, vn_ref, o_ref,
                        *, Ts, past, lam_init):
    lam = _lambda(lam_ref, lam_init)
    q2 = _stack_maps(q_ref[...])
    bias = bias_ref[...] * LOG2E
    s_p = _dot_nt(q2, kc_ref[...].astype(BF16)).reshape(2, Ts, past) + bias[:, :past][None]
    s_n = _dot_nt(q2, kn_ref[...]).reshape(2, Ts, Ts) + bias[:, past:past + Ts][None]
    s_p = s_p.reshape(2 * Ts, past)
    s_n = s_n.reshape(2 * Ts, Ts)
    m = jnp.maximum(jnp.max(s_p, axis=1, keepdims=True), jnp.max(s_n, axis=1, keepdims=True))
    p_p = jnp.exp2(s_p - m)
    p_n = jnp.exp2(s_n - m)
    l = jnp.sum(p_p, axis=1, keepdims=True) + jnp.sum(p_n, axis=1, keepdims=True)
    acc = _dot(p_p.astype(BF16), vc_ref[...].astype(BF16)) + _dot(p_n.astype(BF16), vn_ref[...])
    o_ref[...] = _diff_finish(acc / l, Ts, lam, g_ref[...], lam_init)


def _attn_decode(lam4, g, bias, q, cache_k, cache_v, kn, vn, layer, Bs, Ts, lam_init):
    past = cache_k.shape[2]
    assert past % CHUNK == 0 and Ts <= CHUNK
    padk = bias.shape[2]
    new = pl.BlockSpec((Ts, A_V_DIM), lambda b, h: (b, h))
    cache = pl.BlockSpec((None, None, past, A_V_DIM), lambda b, h: (layer, b, 0, h))
    return pl.pallas_call(
        functools.partial(_attn_decode_kernel, Ts=Ts, past=past, lam_init=lam_init),
        grid=(Bs, A_HEADS),
        in_specs=[pl.BlockSpec((4, HEAD_DIM), lambda b, h: (0, 0)),
                  pl.BlockSpec((1, A_V_DIM), lambda b, h: (0, 0)),
                  pl.BlockSpec((None, Ts, padk), lambda b, h: (h, 0, 0)),
                  new, cache, cache, new, new],
        out_specs=new,
        out_shape=jax.ShapeDtypeStruct((Bs * Ts, A_WIDTH), BF16),
        compiler_params=_params(("arbitrary", "arbitrary"), 40),
        name="attn_decode",
    )(lam4, g, bias, q, cache_k, cache_v, kn, vn)


def _ret_kernel(q_ref, k_ref, v_ref, g_ref, s0_ref, dm_ref, qd_ref, kd_ref, cd_ref,
                o_ref, sn_ref, *, T, C):
    z = jnp.zeros((R_KEY_DIM, R_V_DIM), F32)
    state = jnp.concatenate([jnp.concatenate([s0_ref[0], z], axis=1),
                             jnp.concatenate([z, s0_ref[1]], axis=1)], axis=0)
    r = lax.broadcasted_iota(jnp.int32, (LANES, LANES), 0)
    c = lax.broadcasted_iota(jnp.int32, (LANES, LANES), 1)
    same_head = (r >> _LOG2_CHUNK) == (c >> _LOG2_CHUNK)
    ones_bd = jnp.where(same_head, 1.0, 0.0).astype(BF16)
    lo = lax.broadcasted_iota(jnp.int32, (C, LANES), 1) < R_V_DIM
    cd = cd_ref[...]

    def chunk(n, state):
        r0 = pl.multiple_of(n * C, C)
        q = q_ref[pl.ds(r0, C), :]
        k = k_ref[pl.ds(r0, C), :]
        v = v_ref[pl.ds(r0, C), :]
        p = (_dot_nt(_stack_maps(q), k) * dm_ref[...]).astype(BF16)
        o2 = _dot(p, v)
        inner = jnp.where(lo, o2[:C], o2[C:])
        qd = (q.astype(F32) * qd_ref[...]).astype(BF16)
        o = inner + _dot(qd, state.astype(BF16))
        vk = (v.astype(F32) * kd_ref[...]).astype(BF16)
        new_state = cd * state + jnp.where(same_head, _dot_tn(k, vk), 0.0)
        oo = o * o
        hi = oo.astype(BF16)
        lo_part = (oo - hi.astype(F32)).astype(BF16)
        ss = _dot(hi, ones_bd) + _dot(lo_part, ones_bd)
        out = o * lax.rsqrt(ss * (1.0 / R_V_DIM) + LN_EPS) * g_ref[pl.ds(r0, C), :]
        o_ref[pl.ds(r0, C), :] = out.astype(BF16)
        return new_state

    state = lax.fori_loop(0, T // C, chunk, state)
    sn_ref[0] = state[:R_KEY_DIM, :R_V_DIM]
    sn_ref[1] = state[R_KEY_DIM:, R_V_DIM:]


def _retention(rq, rk, rv, rg, state0, tabs, B, T, C):
    dm, qd, kd, cd = tabs
    blk = pl.BlockSpec((T, LANES), lambda b, hp: (b, hp))
    st = pl.BlockSpec((None, 2, R_KEY_DIM, R_V_DIM), lambda b, hp: (b, hp, 0, 0))
    return pl.pallas_call(
        functools.partial(_ret_kernel, T=T, C=C),
        grid=(B, 2),
        in_specs=[blk, blk, blk, blk, st,
                  pl.BlockSpec((None, 2 * C, C), lambda b, hp: (hp, 0, 0)),
                  pl.BlockSpec((None, C, LANES), lambda b, hp: (hp, 0, 0)),
                  pl.BlockSpec((None, C, LANES), lambda b, hp: (hp, 0, 0)),
                  pl.BlockSpec((None, 1, LANES), lambda b, hp: (hp, 0, 0))],
        out_specs=[blk, st],
        out_shape=[jax.ShapeDtypeStruct((B * T, R_WIDTH), BF16),
                   jax.ShapeDtypeStruct((B, R_HEADS, R_KEY_DIM, R_V_DIM), F32)],
        compiler_params=_params(("arbitrary", "arbitrary"), 32),
        name="retention",
    )(rq, rk, rv, rg, state0, dm, qd, kd, cd)


def _retention_tables(C):
    h = jnp.arange(R_HEADS, dtype=F32)
    log_g = jnp.log1p(-jnp.exp2(-5.0 - h))
    i = jnp.arange(C, dtype=F32)
    diff = i[:, None] - i[None, :]
    dmask = jnp.where(diff >= 0, jnp.exp(jnp.maximum(diff, 0.0)[None] * log_g[:, None, None]), 0.0)
    dm = dmask.reshape(2, 2 * C, C)
    q_dec = jnp.exp((i + 1.0)[None, :] * log_g[:, None])
    k_dec = jnp.exp((C - 1 - i)[None, :] * log_g[:, None])
    c_dec = jnp.exp(C * log_g)

    def lanes(t):
        t = jnp.repeat(t[:, :, None], R_V_DIM, axis=2).reshape(2, 2, C, R_V_DIM)
        return jnp.concatenate([t[:, 0], t[:, 1]], axis=-1)

    cd = jnp.repeat(c_dec[:, None], R_V_DIM, axis=1).reshape(2, 1, LANES)
    return dm, lanes(q_dec), lanes(k_dec), cd


_CONV_PAD = 32


def _conv_kernel(u_ref, h_ref, w_ref, b_ref, g_ref, be_ref, o_ref, t_ref, up_scr, *, T, RT):
    hist = CONV_WIDTH - 1
    off = _CONV_PAD - hist
    up_scr[off:_CONV_PAD, :] = h_ref[...]
    up_scr[_CONV_PAD:_CONV_PAD + T, :] = u_ref[...]
    bias = b_ref[...]
    g = g_ref[...]
    be = be_ref[...]
    for t0 in range(0, T, RT):
        acc = jnp.zeros((RT, C_WIDTH), F32)
        for j in range(CONV_WIDTH):
            acc = acc + up_scr[t0 + j + off:t0 + j + off + RT, :] * w_ref[j:j + 1, :]
        y = _layer_norm(acc + bias, g, be)
        o_ref[t0:t0 + RT, :] = (y * _sigmoid(y)).astype(BF16)
    t_ref[...] = up_scr[T + off:T + _CONV_PAD, :]


def _conv(u, hist, dw_w, dw_b, ln_g, ln_b, B, T):
    RT = min(T, LANES)
    assert T % RT == 0
    hl = CONV_WIDTH - 1
    vec = pl.BlockSpec((1, C_WIDTH), lambda b: (0, 0))
    hb = pl.BlockSpec((None, hl, C_WIDTH), lambda b: (b, 0, 0))
    return pl.pallas_call(
        functools.partial(_conv_kernel, T=T, RT=RT),
        grid=(B,),
        in_specs=[pl.BlockSpec((T, C_WIDTH), lambda b: (b, 0)), hb,
                  pl.BlockSpec((CONV_WIDTH, C_WIDTH), lambda b: (0, 0)), vec, vec, vec],
        out_specs=[pl.BlockSpec((T, C_WIDTH), lambda b: (b, 0)), hb],
        out_shape=[jax.ShapeDtypeStruct((B * T, C_WIDTH), BF16),
                   jax.ShapeDtypeStruct((B, hl, C_WIDTH), F32)],
        scratch_shapes=[pltpu.VMEM((T + _CONV_PAD, C_WIDTH), F32)],
        compiler_params=_params(("arbitrary",), 32),
        name="conv",
    )(u, hist, dw_w, dw_b.reshape(1, C_WIDTH), ln_g.reshape(1, C_WIDTH), ln_b.reshape(1, C_WIDTH))


def _first_index(vals, target):
    idx = jnp.full(target.shape, len(vals) - 1, jnp.int32)
    for j in range(len(vals) - 2, -1, -1):
        idx = jnp.where(vals[j] == target, j, idx)
    return idx


def _select(idx, vals):
    out = vals[-1]
    for j in range(len(vals) - 2, -1, -1):
        out = jnp.where(idx == j, vals[j], out)
    return out


def _route(logits_t):
    rows = [logits_t[e:e + 1, :] for e in range(N_EXPERTS)]
    m = functools.reduce(jnp.maximum, rows)
    ex = [jnp.exp(r - m) for r in rows]
    z = functools.reduce(jnp.add, ex)
    sc = [e / z for e in ex]
    v1s, v2s, i1s, i2s, gss = [], [], [], [], []
    for g in range(N_GROUPS):
        a = sc[g * EXPERTS_PER_GROUP:(g + 1) * EXPERTS_PER_GROUP]
        v1 = functools.reduce(jnp.maximum, a)
        i1 = _first_index(a, v1)
        rest = [jnp.where(i1 == j, -1.0, a[j]) for j in range(EXPERTS_PER_GROUP)]
        v2 = functools.reduce(jnp.maximum, rest)
        i2 = _first_index(rest, v2)
        v1s.append(v1); v2s.append(v2); i1s.append(i1); i2s.append(i2); gss.append(v1 + v2)
    grp = _first_index(gss, functools.reduce(jnp.maximum, gss))
    v1 = _select(grp, v1s)
    v2 = _select(grp, v2s)
    e1 = _select(grp, i1s) + grp * EXPERTS_PER_GROUP
    e2 = _select(grp, i2s) + grp * EXPERTS_PER_GROUP
    den = v1 + v2
    return e1, e2, v1 / den, v2 / den


def _expert_onehot(e1, e2):
    rows = [jnp.where(e1 == e, 1.0, 0.0) + jnp.where(e2 == e, 1.0, 0.0) for e in range(N_EXPERTS)]
    return jnp.concatenate(rows, axis=0)


def _outproj_kernel(a_ref, r_ref, c_ref, x_ref, w_ref, g_ref, b_ref, wr_ref, br_ref,
                    x1_ref, x1b_ref, ri_ref, rw_ref, cnt_ref, *, alpha, tb):
    mix = (_dot(a_ref[...], w_ref[0:A_WIDTH, :])
           + _dot(r_ref[...], w_ref[A_WIDTH:A_WIDTH + R_WIDTH, :])
           + _dot(c_ref[...], w_ref[A_WIDTH + R_WIDTH:, :]))
    x1 = _layer_norm(alpha * x_ref[...] + mix, g_ref[...], b_ref[...])
    x1_ref[...] = x1
    x1b_ref[...] = x1.astype(BF16)
    logits_t = lax.dot_general(wr_ref[...], x1, (((1,), (1,)), ((), ())),
                               precision=lax.Precision.HIGHEST,
                               preferred_element_type=F32) + br_ref[...]
    e1, e2, w1, w2 = _route(logits_t)
    tm = e1.shape[1]
    ri_ref[...] = jnp.concatenate([e1, e2, jnp.zeros((SUBLANES - 2, tm), jnp.int32)], axis=0)
    rw_ref[...] = jnp.concatenate([w1, w2, jnp.zeros((SUBLANES - 2, tm), F32)], axis=0)
    onehot = _expert_onehot(e1, e2).astype(BF16)
    ones = jnp.ones((SUBLANES, tb), BF16)
    for k in range(tm // tb):
        c = _dot_nt(ones, onehot[:, k * tb:(k + 1) * tb])
        c = jnp.concatenate([c, jnp.zeros((SUBLANES, LANES - N_EXPERTS), F32)], axis=1)
        cnt_ref[k * SUBLANES:(k + 1) * SUBLANES, :] = c.astype(jnp.int32)


def _outproj(attn, ret, conv, x, w_bf, ln_g, ln_b, wr_t, br, alpha, tm, tb):
    n = x.shape[0]
    row = lambda i: (i, 0)
    col = lambda i: (0, i)
    const = lambda i: (0, 0)
    nsub = tm // tb
    return pl.pallas_call(
        functools.partial(_outproj_kernel, alpha=alpha, tb=tb),
        grid=(n // tm,),
        in_specs=[pl.BlockSpec((tm, A_WIDTH), row), pl.BlockSpec((tm, R_WIDTH), row),
                  pl.BlockSpec((tm, C_WIDTH), row), pl.BlockSpec((tm, D_MODEL), row),
                  pl.BlockSpec((D_MODEL, D_MODEL), const),
                  pl.BlockSpec((1, D_MODEL), const), pl.BlockSpec((1, D_MODEL), const),
                  pl.BlockSpec((N_EXPERTS, D_MODEL), const), pl.BlockSpec((N_EXPERTS, 1), const)],
        out_specs=[pl.BlockSpec((tm, D_MODEL), row), pl.BlockSpec((tm, D_MODEL), row),
                   pl.BlockSpec((SUBLANES, tm), col), pl.BlockSpec((SUBLANES, tm), col),
                   pl.BlockSpec((nsub * SUBLANES, LANES), row)],
        out_shape=[jax.ShapeDtypeStruct((n, D_MODEL), F32), jax.ShapeDtypeStruct((n, D_MODEL), BF16),
                   jax.ShapeDtypeStruct((SUBLANES, n), jnp.int32),
                   jax.ShapeDtypeStruct((SUBLANES, n), F32),
                   jax.ShapeDtypeStruct((n // tb * SUBLANES, LANES), jnp.int32)],
        compiler_params=_params(("arbitrary",), 40),
        name="outproj",
    )(attn, ret, conv, x, w_bf, ln_g.reshape(1, D_MODEL), ln_b.reshape(1, D_MODEL), wr_t, br)


_UNIT = 8
_XS_WIDTH = D_MODEL + LANES
_FFN_ROWS = 256


def _sorted_positions(e1, e2, lo_ref, base, tb):
    onehot = _expert_onehot(e1, e2).astype(BF16)
    r = lax.broadcasted_iota(jnp.int32, (tb, tb), 0)
    c = lax.broadcasted_iota(jnp.int32, (tb, tb), 1)
    earlier = jnp.where(r < c, 1.0, 0.0).astype(BF16)
    rank = _dot(onehot, earlier).astype(jnp.int32)
    pos1 = jnp.zeros(e1.shape, jnp.int32)
    pos2 = jnp.zeros(e1.shape, jnp.int32)
    for e in range(N_EXPERTS):
        p = rank[e:e + 1, :] + lo_ref[base + e]
        pos1 = jnp.where(e1 == e, p, pos1)
        pos2 = jnp.where(e2 == e, p, pos2)
    return pos1, pos2


def _permutation(pos1, pos2, rows):
    j = lax.broadcasted_iota(jnp.int32, (rows, pos1.shape[1]), 0)
    return jnp.where(j == pos1, 1.0, jnp.where(j == pos2, 1.0, 0.0)).astype(BF16)


def _unit_copy(src, dst, sem):
    return pltpu.make_async_copy(src, dst, sem)


def _dispatch_kernel(lo_ref, pc_ref, gs_ref, ri_ref, rw_ref, x_ref, xs_hbm, pos_ref,
                     stage, sems, issued, *, nblk, tb, cap):
    i = pl.program_id(0)
    slot = i % 2
    base = i * N_EXPERTS

    def wait_units(s, n):
        def body(u, c):
            _unit_copy(stage.at[s, pl.ds(0, _UNIT), :], xs_hbm.at[pl.ds(0, _UNIT), :], sems.at[s]).wait()
            return c
        lax.fori_loop(0, n, body, 0)

    @pl.when(i >= 2)
    def _():
        wait_units(slot, issued[slot])

    e1 = ri_ref[0:1, :]
    e2 = ri_ref[1:2, :]
    w1 = rw_ref[0:1, :]
    w2 = rw_ref[1:2, :]
    pos1, pos2 = _sorted_positions(e1, e2, lo_ref, base, tb)
    pos_ref[...] = jnp.concatenate([pos1, pos2, jnp.zeros((SUBLANES - 2, tb), jnp.int32)], axis=0)
    perm = _permutation(pos1, pos2, cap)
    j = lax.broadcasted_iota(jnp.int32, (cap, tb), 0)
    gate = jnp.sum(jnp.where(j == pos1, w1, jnp.where(j == pos2, w2, 0.0)), axis=1, keepdims=True)
    stage[slot, :, 0:D_MODEL] = _dot(perm, x_ref[...])
    stage[slot, :, D_MODEL:_XS_WIDTH] = jnp.broadcast_to(gate, (cap, LANES))

    total = 0
    for e in range(N_EXPERTS):
        n = pc_ref[base + e] // _UNIT
        src0 = lo_ref[base + e]
        dst0 = gs_ref[base + e]

        def body(u, c, src0=src0, dst0=dst0):
            s = pl.multiple_of(src0 + u * _UNIT, _UNIT)
            d = pl.multiple_of(dst0 + u * _UNIT, _UNIT)
            _unit_copy(stage.at[slot, pl.ds(s, _UNIT), :], xs_hbm.at[pl.ds(d, _UNIT), :], sems.at[slot]).start()
            return c

        lax.fori_loop(0, n, body, 0)
        total = total + n
    issued[slot] = total

    @pl.when(i == nblk - 1)
    def _():
        wait_units(slot, issued[slot])
        if nblk > 1:
            wait_units(1 - slot, issued[1 - slot])


def _ffn_kernel(te_ref, tv_ref, nt_ref, xs_ref, wg_ref, wu_ref, wd_ref, ys_ref):
    j = pl.program_id(0)

    @pl.when(j < nt_ref[0])
    def _():
        xv = xs_ref[...]
        rows = xv.shape[0]
        valid = lax.broadcasted_iota(jnp.int32, (rows, LANES), 0) < tv_ref[j]
        x = jnp.concatenate([jnp.where(valid, xv[:, c * LANES:(c + 1) * LANES], 0.0)
                             for c in range(D_MODEL // LANES)], axis=1).astype(BF16)
        g = jnp.where(valid, xv[:, D_MODEL:_XS_WIDTH], 0.0)
        hg = _dot(x, wg_ref[...])
        hu = _dot(x, wu_ref[...])
        h = hg * _sigmoid(hg) * hu * jnp.concatenate([g] * (D_EXPERT // LANES), axis=1)
        ys_ref[...] = _dot(h.astype(BF16), wd_ref[...])


def _combine_kernel(lo_ref, pc_ref, gs_ref, pos_ref, x1_ref, p_ref, g_ref, b_ref, wpg_ref, bpg_ref, wpp_ref,
                    ys_hbm, out_ref, stage, sems, *, nblk, tb, cap, alpha):
    i = pl.program_id(0)
    slot = i % 2

    def copies(blk, s, start):
        base = blk * N_EXPERTS
        for e in range(N_EXPERTS):
            n = pc_ref[base + e] // _UNIT
            src0 = gs_ref[base + e]
            dst0 = lo_ref[base + e]

            def body(u, c, src0=src0, dst0=dst0):
                sr = pl.multiple_of(src0 + u * _UNIT, _UNIT)
                ds = pl.multiple_of(dst0 + u * _UNIT, _UNIT)
                cp = _unit_copy(ys_hbm.at[pl.ds(sr, _UNIT), :], stage.at[s, pl.ds(ds, _UNIT), :], sems.at[s])
                if start:
                    cp.start()
                else:
                    cp.wait()
                return c

            lax.fori_loop(0, n, body, 0)

    @pl.when(i == 0)
    def _():
        stage[...] = jnp.zeros(stage.shape, F32)
        copies(0, 0, True)

    @pl.when(i + 1 < nblk)
    def _():
        copies(i + 1, 1 - slot, True)

    copies(i, slot, False)
    perm = _permutation(pos_ref[0:1, :], pos_ref[1:2, :], cap)
    y = _dot_tn(perm, stage[slot].astype(BF16))
    x2 = _layer_norm(alpha * x1_ref[...] + y, g_ref[...], b_ref[...])
    gate = _sigmoid(_dot(x2.astype(BF16), wpg_ref[...]) + bpg_ref[...])
    out_ref[...] = x2 + gate * _dot(p_ref[...].astype(BF16), wpp_ref[...])


def _moe_plan(cnt, nblk, tb):
    cnt = cnt.reshape(nblk, SUBLANES, LANES)[:, 0, :N_EXPERTS]
    pc = (cnt + (_UNIT - 1)) // _UNIT * _UNIT
    lo = jnp.cumsum(pc, axis=1) - pc
    tot = jnp.sum(pc, axis=0)
    reg = (tot + (_FFN_ROWS - 1)) // _FFN_ROWS * _FFN_ROWS
    rstart = jnp.cumsum(reg) - reg
    gs = rstart[None, :] + jnp.cumsum(pc, axis=0) - pc
    tiles_e = reg // _FFN_ROWS
    tile_end = jnp.cumsum(tiles_e)
    max_rows = 2 * nblk * tb + nblk * N_EXPERTS * _UNIT + N_EXPERTS * _FFN_ROWS
    max_tiles = -(-max_rows // _FFN_ROWS)
    tj =jnp.arange(max_tiles, dtype=jnp.int32)
    te = jnp.minimum(jnp.sum((tj[:, None] >= tile_end[None, :]).astype(jnp.int32), axis=1), N_EXPERTS - 1)
    tv = jnp.clip(tot[te] - (tj - (tile_end - tiles_e)[te]) * _FFN_ROWS, 0, _FFN_ROWS)
    nt = tile_end[-1:].astype(jnp.int32)
    flat = lambda a: a.reshape(-1).astype(jnp.int32)
    return flat(lo), flat(pc), flat(gs), te.astype(jnp.int32), tv.astype(jnp.int32), nt, max_tiles


def _moe(x1b, x1, ri, rw, cnt, p, wg, wu, wd, ln_g, ln_b, wpg, bpg, wpp, alpha, tb):
    n = x1.shape[0]
    nblk = n // tb
    cap = 2 * tb + N_EXPERTS * _UNIT
    lo, pc, gs, te, tv, nt, max_tiles = _moe_plan(cnt, nblk, tb)
    max_rows = max_tiles * _FFN_ROWS

    xs, pos = pl.pallas_call(
        functools.partial(_dispatch_kernel, nblk=nblk, tb=tb, cap=cap),
        grid_spec=pltpu.PrefetchScalarGridSpec(
            num_scalar_prefetch=3,
            grid=(nblk,),
            in_specs=[pl.BlockSpec((SUBLANES, tb), lambda i, *_: (0, i)),
                      pl.BlockSpec((SUBLANES, tb), lambda i, *_: (0, i)),
                      pl.BlockSpec((tb, D_MODEL), lambda i, *_: (i, 0))],
            out_specs=[pl.BlockSpec(memory_space=pl.ANY),
                       pl.BlockSpec((SUBLANES, tb), lambda i, *_: (0, i))],
            scratch_shapes=[pltpu.VMEM((2, cap, _XS_WIDTH), F32),
                            pltpu.SemaphoreType.DMA((2,)),
                            pltpu.SMEM((2,), jnp.int32)]),
        out_shape=[jax.ShapeDtypeStruct((max_rows, _XS_WIDTH), F32),
                   jax.ShapeDtypeStruct((SUBLANES, n), jnp.int32)],
        compiler_params=_params(("arbitrary",), 32),
        name="moe_dispatch",
    )(lo, pc, gs, ri, rw, x1b)

    def tile(j, te_ref, tv_ref, nt_ref):
        return jnp.minimum(j, nt_ref[0] - 1)

    ys = pl.pallas_call(
        _ffn_kernel,
        grid_spec=pltpu.PrefetchScalarGridSpec(
            num_scalar_prefetch=3,
            grid=(max_tiles,),
            in_specs=[pl.BlockSpec((_FFN_ROWS, _XS_WIDTH), lambda j, *s: (tile(j, *s), 0)),
                      pl.BlockSpec((None, D_MODEL, D_EXPERT), lambda j, *s: (s[0][tile(j, *s)], 0, 0)),
                      pl.BlockSpec((None, D_MODEL, D_EXPERT), lambda j, *s: (s[0][tile(j, *s)], 0, 0)),
                      pl.BlockSpec((None, D_EXPERT, D_MODEL), lambda j, *s: (s[0][tile(j, *s)], 0, 0))],
            out_specs=pl.BlockSpec((_FFN_ROWS, D_MODEL), lambda j, *s: (tile(j, *s), 0))),
        out_shape=jax.ShapeDtypeStruct((max_rows, D_MODEL), F32),
        compiler_params=_params(("arbitrary",), 32),
        name="moe_ffn",
    )(te, tv, nt, xs, wg, wu, wd)

    const = lambda i, *_: (0, 0)
    return pl.pallas_call(
        functools.partial(_combine_kernel, nblk=nblk, tb=tb, cap=cap, alpha=alpha),
        grid_spec=pltpu.PrefetchScalarGridSpec(
            num_scalar_prefetch=3,
            grid=(nblk,),
            in_specs=[pl.BlockSpec((SUBLANES, tb), lambda i, *_: (0, i)),
                      pl.BlockSpec((tb, D_MODEL), lambda i, *_: (i, 0)),
                      pl.BlockSpec((tb, PLE_DIM), lambda i, *_: (i, 0)),
                      pl.BlockSpec((1, D_MODEL), const), pl.BlockSpec((1, D_MODEL), const),
                      pl.BlockSpec((D_MODEL, D_MODEL), const), pl.BlockSpec((1, D_MODEL), const),
                      pl.BlockSpec((PLE_DIM, D_MODEL), const),
                      pl.BlockSpec(memory_space=pl.ANY)],
            out_specs=pl.BlockSpec((tb, D_MODEL), lambda i, *_: (i, 0)),
            scratch_shapes=[pltpu.VMEM((2, cap, D_MODEL), F32),
                            pltpu.SemaphoreType.DMA((2,))]),
        out_shape=jax.ShapeDtypeStruct((n, D_MODEL), F32),
        compiler_params=_params(("arbitrary",), 40),
        name="moe_combine",
    )(lo, pc, gs, pos, x1, p, ln_g.reshape(1, D_MODEL), ln_b.reshape(1, D_MODEL),
      wpg, bpg.reshape(1, D_MODEL), wpp, ys)


def _rope_tables(pos, rows):
    half = R_KEY_DIM // 2
    inv_freq = 1.0 / (ROPE_BASE ** jnp.linspace(0.0, 1.0, half, dtype=jnp.float32))
    ang = pos.astype(jnp.float32)[:, None] * inv_freq[None, :]
    cos = jnp.cos(ang)
    sin = jnp.sin(ang)
    cos_t = jnp.tile(cos, (rows // pos.shape[0], 4))
    sin_t = jnp.tile(jnp.concatenate([-sin, sin], axis=1), (rows // pos.shape[0], 2))
    return cos_t, sin_t


def kernel(x_prompt, x_sample, p_prompt, p_sample, cache_k, cache_v, state_ret, state_conv, ln_emb_g, ln_emb_b, rel_bias, w_router, b_router, w_in, lam_q1, lam_k1, lam_q2, lam_k2, subln_g, dw_w, dw_b, conv_ln_g, conv_ln_b, w_out, ln1_g, ln1_b, w_exp_gate, w_exp_up, w_exp_down, ln2_g, ln2_b, w_ple_gate, b_ple_gate, w_ple_proj):
    B, T, D = x_prompt.shape
    Bs, Ts, _ = x_sample.shape
    depth = w_in.shape[0]
    past = cache_k.shape[2]
    n_p, n_s = B * T, Bs * Ts
    alpha = (2 * depth) ** 0.25

    tm_p = 512 if n_p % 512 == 0 else n_p
    tm_s = n_s
    TB = 2 * LANES
    c_p = min(T, 2 * LANES)

    pos_p = jnp.arange(T, dtype=jnp.int32)
    pos_s = past + jnp.arange(Ts, dtype=jnp.int32)
    rope_p = _rope_tables(pos_p, max(T, tm_p))
    rope_s = _rope_tables(pos_s, max(Ts, tm_s))
    rel_p = (jnp.arange(2 * TB, dtype=jnp.int32)[None, :] - TB) - jnp.arange(TB, dtype=jnp.int32)[:, None]
    padk = -(-(past + Ts) // LANES) * LANES
    rel_s = jnp.arange(padk, dtype=jnp.int32)[None, :] - pos_s[:, None]
    bias_p = _bias_table(rel_bias, _t5_bucket(rel_p))
    bias_s = _bias_table(rel_bias, _t5_bucket(rel_s))
    tabs_p = _retention_tables(c_p)
    tabs_s = _retention_tables(Ts)

    wr_t = w_router.T
    br = b_router.reshape(N_EXPERTS, 1)
    ck = cache_k.reshape(depth, Bs, past, A_WIDTH)
    cv = cache_v.reshape(depth, Bs, past, A_WIDTH)
    ret0_p = jnp.zeros((B, R_HEADS, R_KEY_DIM, R_V_DIM), F32)
    conv0_p = jnp.zeros((B, CONV_WIDTH - 1, C_WIDTH), F32)

    xp = x_prompt.reshape(n_p, D)
    xs = x_sample.reshape(n_s, D)
    outs = {k: [] for k in ("kp", "vp", "rp", "cp", "ks", "vs", "rs", "cs")}
    for l in range(depth):
        lam_init = 0.8 - 0.6 * math.exp(-0.3 * l)
        lam4 = jnp.stack([lam_q1[l], lam_k1[l], lam_q2[l], lam_k2[l]])
        g_sub = subln_g[l].reshape(1, A_V_DIM)
        w_in_b = w_in[l].astype(BF16)
        w_out_b = w_out[l].astype(BF16)
        wg, wu, wd = (w_exp_gate[l].astype(BF16), w_exp_up[l].astype(BF16), w_exp_down[l].astype(BF16))
        wpg, wpp = w_ple_gate[l].astype(BF16), w_ple_proj[l].astype(BF16)
        ln = (ln_emb_g, ln_emb_b) if l == 0 else None

        def channel(x, attn, ret, conv, p_l, tm):
            tb = min(tm, _FFN_ROWS)
            x1, x1b, ri, rw, cnt = _outproj(attn, ret, conv, x, w_out_b, ln1_g[l], ln1_b[l], wr_t, br,
                                            alpha, tm, tb)
            return _moe(x1b, x1, ri, rw, cnt, p_l, wg, wu, wd, ln2_g[l], ln2_b[l], wpg, b_ple_gate[l], wpp,
                        alpha, tb)

        xp, q, k, kb, v, vb, rq, rk, rv, rg, u = _inproj(xp, w_in_b, rope_p[0], rope_p[1], ln, tm_p)
        attn = _attn_prompt(lam4, g_sub, bias_p, q, kb, vb, B, T, lam_init)
        ret, rstate = _retention(rq, rk, rv, rg, ret0_p, tabs_p, B, T, c_p)
        conv, ctail = _conv(u, conv0_p, dw_w[l], dw_b[l], conv_ln_g[l], conv_ln_b[l], B, T)
        xp = channel(xp, attn, ret, conv, p_prompt[l].reshape(n_p, PLE_DIM), tm_p)
        outs["kp"].append(k.reshape(B, T, A_HEADS, 2 * HEAD_DIM))
        outs["vp"].append(v.reshape(B, T, A_HEADS, A_V_DIM))
        outs["rp"].append(rstate)
        outs["cp"].append(ctail)

        xs, q, k, kb, v, vb, rq, rk, rv, rg, u = _inproj(xs, w_in_b, rope_s[0], rope_s[1], ln, tm_s)
        attn = _attn_decode(lam4, g_sub, bias_s, q, ck, cv, kb, vb, l, Bs, Ts, lam_init)
        ret, rstate = _retention(rq, rk, rv, rg, state_ret[l], tabs_s, Bs, Ts, Ts)
        conv, ctail = _conv(u, state_conv[l], dw_w[l], dw_b[l], conv_ln_g[l], conv_ln_b[l], Bs, Ts)
        xs = channel(xs, attn, ret, conv, p_sample[l].reshape(n_s, PLE_DIM), tm_s)
        outs["ks"].append(k.reshape(Bs, Ts, A_HEADS, 2 * HEAD_DIM))
        outs["vs"].append(v.reshape(Bs, Ts, A_HEADS, A_V_DIM))
        outs["rs"].append(rstate)
        outs["cs"].append(ctail)

    return (xp.reshape(B, T, D), xs.reshape(Bs, Ts, D),
            jnp.stack(outs["kp"]), jnp.stack(outs["vp"]), jnp.stack(outs["rp"]), jnp.stack(outs["cp"]),
            jnp.stack(outs["ks"]), jnp.stack(outs["vs"]), jnp.stack(outs["rs"]), jnp.stack(outs["cs"]))
```

```python
import functools
import math

import jax
import jax.numpy as jnp
from jax import lax
from jax.experimental import pallas as pl
from jax.experimental.pallas import tpu as pltpu

F32 = jnp.float32
BF16 = jnp.bfloat16

D_MODEL = 1024
CHUNK = 64
HEAD_DIM = 64
A_HEADS = 4
A_V_DIM = 128
A_WIDTH = 512
R_HEADS = 4
R_KEY_DIM = 64
R_V_DIM = 64
R_WIDTH = 256
C_WIDTH = 256
CONV_WIDTH = 31
IN_WIDTH = 3072
N_BUCKETS = 32
MAX_DISTANCE = 128
ROPE_BASE = 10000.0
N_EXPERTS = 16
N_GROUPS = 4
EXPERTS_PER_GROUP = 4
D_EXPERT = 512
PLE_DIM = 256
LN_EPS = 1e-5
NEG_INF = -1e30
LOG2E = 1.4426950408889634

LANES = 128
SUBLANES = 8
_LOG2_CHUNK = 6
assert CHUNK == HEAD_DIM == R_KEY_DIM == R_V_DIM == 1 << _LOG2_CHUNK
MIB = 1024 * 1024

_OFF_Q, _OFF_K, _OFF_V = 0, 512, 1024
_OFF_RQ, _OFF_RK, _OFF_RV, _OFF_RG, _OFF_C = 1536, 1792, 2048, 2304, 2560


def _params(sem, vmem_mib):
    return pltpu.CompilerParams(dimension_semantics=sem, vmem_limit_bytes=vmem_mib * MIB)


def _layer_norm(x, g, b):
    mu = jnp.mean(x, axis=-1, keepdims=True)
    xc = x - mu
    var = jnp.mean(xc * xc, axis=-1, keepdims=True)
    return xc * lax.rsqrt(var + LN_EPS) * g + b


def _sigmoid(x):
    return 1.0 / (1.0 + jnp.exp(-x))


def _dot(a, b):
    return jnp.dot(a, b, preferred_element_type=F32)


def _dot_nt(a, b):
    return lax.dot_general(a, b, (((1,), (1,)), ((), ())), preferred_element_type=F32)


def _dot_tn(a, b):
    return lax.dot_general(a, b, (((0,), (0,)), ((), ())), preferred_element_type=F32)


def _bias_kernel(relb_ref, idx_ref, out_ref):
    idx = idx_ref[...]
    for h in range(A_HEADS):
        acc = jnp.zeros(idx.shape, F32)
        for b in range(N_BUCKETS):
            acc = jnp.where(idx == b, relb_ref[b, h], acc)
        out_ref[h] = acc


def _bias_table(rel_bias, idx):
    r, c = idx.shape
    return pl.pallas_call(
        _bias_kernel,
        out_shape=jax.ShapeDtypeStruct((A_HEADS, r, c), F32),
        in_specs=[pl.BlockSpec(memory_space=pltpu.SMEM),
                  pl.BlockSpec(memory_space=pltpu.VMEM)],
        out_specs=pl.BlockSpec(memory_space=pltpu.VMEM),
        name="bias_table",
    )(rel_bias, idx)


def _t5_bucket(rel):
    nb = N_BUCKETS // 2
    max_exact = nb // 2
    n = jnp.abs(rel)
    nf = jnp.maximum(n, 1).astype(jnp.float32)
    large = max_exact + (jnp.log(nf / max_exact) / math.log(MAX_DISTANCE / max_exact)
                         * (nb - max_exact)).astype(jnp.int32)
    large = jnp.minimum(large, nb - 1)
    return jnp.where(rel > 0, nb, 0) + jnp.where(n < max_exact, n, large)


def _rotary128(x, cos, sin_signed, lo32):
    partner = jnp.where(lo32, pltpu.roll(x, 96, 1), pltpu.roll(x, 32, 1))
    return x * cos + partner * sin_signed


def _inproj_kernel(*refs, apply_ln, has_prev, layer, tm, T, nsteps):
    refs = list(refs)
    x_ref = refs.pop(0)
    if apply_ln:
        g_ref, b_ref = refs.pop(0), refs.pop(0)
    w_ref, cos_ref, sin_ref = refs.pop(0), refs.pop(0), refs.pop(0)
    if has_prev:
        refs.pop(0), refs.pop(0)
    if apply_ln:
        xn_ref = refs.pop(0)
    (q_ref, kb_ref, vb_ref, rq_ref, rk_ref, rv_ref, rg_ref, u_ref, k5_hbm, v5_hbm,
     kbuf, vbuf, sems) = refs

    i = pl.program_id(0)
    slot = i % 2
    rows = min(tm, T)

    def kv_copies(s, step):
        cps = []
        for bb in range(tm // rows):
            r0 = step * tm + bb * rows
            b = r0 // T
            t0 = pl.multiple_of(r0 % T, SUBLANES)
            for buf, out in ((kbuf, k5_hbm), (vbuf, v5_hbm)):
                for h in range(A_HEADS):
                    cps.append(pltpu.make_async_copy(
                        buf.at[s, pl.ds(bb * rows, rows), pl.ds(h * A_V_DIM, A_V_DIM)],
                        out.at[layer, b, pl.ds(t0, rows), h, :], sems.at[s]))
        return cps

    @pl.when(i >= 2)
    def _():
        for cp in kv_copies(slot, i - 2):
            cp.wait()

    x = x_ref[...]
    if apply_ln:
        x = _layer_norm(x, g_ref[...], b_ref[...])
        xn_ref[...] = x
    xb = x.astype(BF16)

    def mm(c0, c1):
        return _dot(xb, w_ref[:, c0:c1])

    q_ref[...] = (mm(_OFF_Q, _OFF_K) * (LOG2E * HEAD_DIM ** -0.5)).astype(BF16)
    a = mm(_OFF_K, _OFF_V)
    kbuf[slot] = a
    kb_ref[...] = a.astype(BF16)
    a = mm(_OFF_V, _OFF_RQ)
    vbuf[slot] = a
    vb_ref[...] = a.astype(BF16)
    for cp in kv_copies(slot, i):
        cp.start()

    cos = cos_ref[...]
    sin = sin_ref[...]
    lane = lax.broadcasted_iota(jnp.int32, cos.shape, 1)
    lo32 = (lane & 63) < 32

    def rot(a, scale):
        parts = [_rotary128(a[:, c * LANES:(c + 1) * LANES], cos, sin, lo32) for c in range(2)]
        r = jnp.concatenate(parts, axis=1)
        if scale != 1.0:
            r = r * scale
        return r.astype(BF16)

    rq_ref[...] = rot(mm(_OFF_RQ, _OFF_RK), 1.0)
    rk_ref[...] = rot(mm(_OFF_RK, _OFF_RV), R_KEY_DIM ** -0.5)
    rv_ref[...] = mm(_OFF_RV, _OFF_RG).astype(BF16)
    a = mm(_OFF_RG, _OFF_C)
    rg_ref[...] = a * _sigmoid(a)
    a = mm(_OFF_C, IN_WIDTH)
    u_ref[...] = a[:, :C_WIDTH] * _sigmoid(a[:, C_WIDTH:])

    @pl.when(i == nsteps - 1)
    def _():
        for cp in kv_copies(slot, i):
            cp.wait()
        if nsteps > 1:
            for cp in kv_copies(1 - slot, i - 1):
                cp.wait()


def _inproj(x, w_bf, cos_t, sin_t, ln, tm, kv_prev, layer, depth, B, T):
    n = x.shape[0]
    assert (tm % T == 0 or T % tm == 0) and n % tm == 0
    nblk = cos_t.shape[0] // tm
    nsteps = n // tm
    row = lambda i: (i, 0)
    const = lambda i: (0, 0)
    tab = lambda i: (i % nblk, 0)
    in_specs = [pl.BlockSpec((tm, D_MODEL), row)]
    args = [x]
    if ln is not None:
        in_specs += [pl.BlockSpec((1, D_MODEL), const)] * 2
        args += [ln[0].reshape(1, D_MODEL), ln[1].reshape(1, D_MODEL)]
    in_specs += [pl.BlockSpec((D_MODEL, IN_WIDTH), const),
                 pl.BlockSpec((tm, LANES), tab), pl.BlockSpec((tm, LANES), tab)]
    args += [w_bf, cos_t, sin_t]
    aliases = {}
    if kv_prev is not None:
        k5_index = (1 if ln is not None else 0) + 8
        aliases = {len(args): k5_index, len(args) + 1: k5_index + 1}
        in_specs += [pl.BlockSpec(memory_space=pl.ANY)] * 2
        args += list(kv_prev)

    def o(width, dt):
        return jax.ShapeDtypeStruct((n, width), dt), pl.BlockSpec((tm, width), row)

    outs = []
    if ln is not None:
        outs.append(o(D_MODEL, F32))
    outs += [o(512, BF16), o(512, BF16), o(512, BF16),
             o(256, BF16), o(256, BF16), o(256, BF16), o(256, F32), o(256, F32)]
    kv5 = jax.ShapeDtypeStruct((depth, B, T, A_HEADS, A_V_DIM), F32)
    outs += [(kv5, pl.BlockSpec(memory_space=pl.ANY))] * 2
    res = pl.pallas_call(
        functools.partial(_inproj_kernel, apply_ln=ln is not None, has_prev=kv_prev is not None,
                          layer=layer, tm=tm, T=T, nsteps=nsteps),
        grid=(nsteps,),
        in_specs=in_specs,
        out_specs=[s for _, s in outs],
        out_shape=[s for s, _ in outs],
        scratch_shapes=[pltpu.VMEM((2, tm, A_WIDTH), F32), pltpu.VMEM((2, tm, A_WIDTH), F32),
                        pltpu.SemaphoreType.DMA((2,))],
        input_output_aliases=aliases,
        compiler_params=_params(("arbitrary",), 52),
        name="inproj",
    )(*args)
    if ln is None:
        res = [x] + list(res)
    return res


def _lambda(lam_ref, lam_init):
    lv = lam_ref[...]
    s1 = jnp.sum(lv[0:1] * lv[1:2], axis=-1, keepdims=True)
    s2 = jnp.sum(lv[2:3] * lv[3:4], axis=-1, keepdims=True)
    return jnp.exp(s1) - jnp.exp(s2) + lam_init


def _stack_maps(q):
    lane = lax.broadcasted_iota(jnp.int32, q.shape, 1)
    lo = lane < HEAD_DIM
    z = jnp.zeros_like(q)
    return jnp.concatenate([jnp.where(lo, q, z), jnp.where(lo, z, q)], axis=0)


def _diff_finish(o2, t, lam, g, lam_init):
    o = o2[:t] - lam * o2[t:]
    ms = jnp.mean(o * o, axis=-1, keepdims=True)
    return (o * lax.rsqrt(ms + LN_EPS) * g * (1.0 - lam_init)).astype(BF16)


def _attn_prompt_kernel(lam_ref, g_ref, bias_ref, q_ref, k_ref, v_ref, o_ref,
                        s_scr, m_scr, bp_scr, bd_scr, *, T, TB, lam_init):
    lam = _lambda(lam_ref, lam_init)
    g = g_ref[...]
    bt = bias_ref[...]
    bt = (bt - bt[0:1, 0:1]) * LOG2E
    bp_scr[...] = bt[:, :TB]
    bd_scr[...] = bt[:, TB:]
    half = TB // 2

    for qi in range(T // TB):
        r0 = qi * TB
        q2 = _stack_maps(q_ref[r0:r0 + TB, :])
        m = None
        for ki in range(qi + 1):
            s = _dot_nt(q2, k_ref[ki * TB:(ki + 1) * TB, :])
            if ki == qi - 1:
                s = (s.reshape(2, TB, TB) + bp_scr[...][None]).reshape(2 * TB, TB)
            elif ki == qi:
                row = lax.broadcasted_iota(jnp.int32, (TB, TB), 0)
                col = lax.broadcasted_iota(jnp.int32, (TB, TB), 1)
                vis = (col >> _LOG2_CHUNK) <= (row >> _LOG2_CHUNK)
                s = s.reshape(2, TB, TB) + bd_scr[...][None]
                s = jnp.where(vis[None], s, NEG_INF).reshape(2 * TB, TB)
            s_scr[ki] = s
            mt = jnp.maximum(s[:, :half], s[:, half:])
            m = mt if m is None else jnp.maximum(m, mt)
        m_scr[...] = jnp.broadcast_to(jnp.max(m, axis=1, keepdims=True), m_scr.shape)
        l = None
        acc = None
        for ki in range(qi + 1):
            mb = m_scr[...]
            p = jnp.exp2(s_scr[ki] - jnp.concatenate([mb, mb], axis=1))
            lt = p[:, :half] + p[:, half:]
            pv = _dot(p.astype(BF16), v_ref[ki * TB:(ki + 1) * TB, :])
            l = lt if l is None else l + lt
            acc = pv if acc is None else acc + pv
        l_row = jnp.sum(l, axis=1, keepdims=True)
        o_ref[r0:r0 + TB, :] = _diff_finish(acc / l_row, TB, lam, g, lam_init)


def _attn_prompt(lam4, g, bias, q, k, v, B, T, lam_init):
    TB = 2 * LANES
    assert T % TB == 0 and TB % CHUNK == 0
    n = B * T
    blk = pl.BlockSpec((T, A_V_DIM), lambda b, h: (b, h))
    return pl.pallas_call(
        functools.partial(_attn_prompt_kernel, T=T, TB=TB, lam_init=lam_init),
        grid=(B, A_HEADS),
        in_specs=[pl.BlockSpec((4, HEAD_DIM), lambda b, h: (0, 0)),
                  pl.BlockSpec((1, A_V_DIM), lambda b, h: (0, 0)),
                  pl.BlockSpec((None, TB, 2 * TB), lambda b, h: (h, 0, 0)),
                  blk, blk, blk],
        out_specs=blk,
        out_shape=jax.ShapeDtypeStruct((n, A_WIDTH), BF16),
        scratch_shapes=[pltpu.VMEM((T // TB, 2 * TB, TB), F32),
                        pltpu.VMEM((2 * TB, LANES), F32),
                        pltpu.VMEM((TB, TB), F32),
                        pltpu.VMEM((TB, TB), F32)],
        compiler_params=_params(("arbitrary", "arbitrary"), 32),
        name="attn_prompt",
    )(lam4, g, bias, q, k, v)


def _attn_decode_kernel(lam_ref, g_ref, bias_ref, q_ref, kn_ref, vn_ref, ck_hbm, cv_hbm, o_ref,
                        kc_buf, vc_buf, sems, *, Ts, past, layer, lam_init):
    b = pl.program_id(0)
    h = pl.program_id(1)
    step = b * A_HEADS + h
    nsteps = pl.num_programs(0) * A_HEADS
    slot = step % 2

    def cache_copies(st, s):
        bb = st // A_HEADS
        hh = st % A_HEADS
        return [pltpu.make_async_copy(src.at[layer, bb, :, hh, :], dst.at[s], sems.at[s, j])
                for j, (src, dst) in enumerate(((ck_hbm, kc_buf), (cv_hbm, vc_buf)))]

    @pl.when(step == 0)
    def _():
        for cp in cache_copies(0, 0):
            cp.start()

    @pl.when(step + 1 < nsteps)
    def _():
        for cp in cache_copies(step + 1, 1 - slot):
            cp.start()

    for cp in cache_copies(step, slot):
        cp.wait()
    kc_ref = kc_buf.at[slot]
    vc_ref = vc_buf.at[slot]

    lam = _lambda(lam_ref, lam_init)
    q2 = _stack_maps(q_ref[...])
    bias = bias_ref[...] * LOG2E
    s_p = _dot_nt(q2, kc_ref[...].astype(BF16)).reshape(2, Ts, past) + bias[:, :past][None]
    s_n = _dot_nt(q2, kn_ref[...]).reshape(2, Ts, Ts) + bias[:, past:past + Ts][None]
    s_p = s_p.reshape(2 * Ts, past)
    s_n = s_n.reshape(2 * Ts, Ts)
    m = jnp.maximum(jnp.max(s_p, axis=1, keepdims=True), jnp.max(s_n, axis=1, keepdims=True))
    p_p = jnp.exp2(s_p - m)
    p_n = jnp.exp2(s_n - m)
    l = jnp.sum(p_p, axis=1, keepdims=True) + jnp.sum(p_n, axis=1, keepdims=True)
    acc = _dot(p_p.astype(BF16), vc_ref[...].astype(BF16)) + _dot(p_n.astype(BF16), vn_ref[...])
    o_ref[...] = _diff_finish(acc / l, Ts, lam, g_ref[...], lam_init)


def _attn_decode(lam4, g, bias, q, cache_k, cache_v, kn, vn, layer, Bs, Ts, lam_init):
    past = cache_k.shape[2]
    assert past % CHUNK == 0 and Ts <= CHUNK
    padk = bias.shape[2]
    new = pl.BlockSpec((Ts, A_V_DIM), lambda b, h: (b, h))
    cache = pl.BlockSpec(memory_space=pl.ANY)
    return pl.pallas_call(
        functools.partial(_attn_decode_kernel, Ts=Ts, past=past, layer=layer, lam_init=lam_init),
        grid=(Bs, A_HEADS),
        in_specs=[pl.BlockSpec((4, HEAD_DIM), lambda b, h: (0, 0)),
                  pl.BlockSpec((1, A_V_DIM), lambda b, h: (0, 0)),
                  pl.BlockSpec((None, Ts, padk), lambda b, h: (h, 0, 0)),
                  new, new, new, cache, cache],
        out_specs=new,
        out_shape=jax.ShapeDtypeStruct((Bs * Ts, A_WIDTH), BF16),
        scratch_shapes=[pltpu.VMEM((2, past, A_V_DIM), F32), pltpu.VMEM((2, past, A_V_DIM), F32),
                        pltpu.SemaphoreType.DMA((2, 2))],
        compiler_params=_params(("arbitrary", "arbitrary"), 40),
        name="attn_decode",
    )(lam4, g, bias, q, kn, vn, cache_k, cache_v)


def _ret_kernel(q_ref, k_ref, v_ref, g_ref, s0_ref, dm_ref, qd_ref, kd_ref, cd_ref,
                o_ref, sn_ref, *, T, C):
    z = jnp.zeros((R_KEY_DIM, R_V_DIM), F32)
    state = jnp.concatenate([jnp.concatenate([s0_ref[0], z], axis=1),
                             jnp.concatenate([z, s0_ref[1]], axis=1)], axis=0)
    r = lax.broadcasted_iota(jnp.int32, (LANES, LANES), 0)
    c = lax.broadcasted_iota(jnp.int32, (LANES, LANES), 1)
    same_head = (r >> _LOG2_CHUNK) == (c >> _LOG2_CHUNK)
    ones_bd = jnp.where(same_head, 1.0, 0.0).astype(BF16)
    lo = lax.broadcasted_iota(jnp.int32, (C, LANES), 1) < R_V_DIM
    cd = cd_ref[...]

    def chunk(n, state):
        r0 = pl.multiple_of(n * C, C)
        q = q_ref[pl.ds(r0, C), :]
        k = k_ref[pl.ds(r0, C), :]
        v = v_ref[pl.ds(r0, C), :]
        p = (_dot_nt(_stack_maps(q), k) * dm_ref[...]).astype(BF16)
        o2 = _dot(p, v)
        inner = jnp.where(lo, o2[:C], o2[C:])
        qd = (q.astype(F32) * qd_ref[...]).astype(BF16)
        o = inner + _dot(qd, state.astype(BF16))
        vk = (v.astype(F32) * kd_ref[...]).astype(BF16)
        new_state = cd * state + jnp.where(same_head, _dot_tn(k, vk), 0.0)
        oo = o * o
        hi = oo.astype(BF16)
        lo_part = (oo - hi.astype(F32)).astype(BF16)
        ss = _dot(hi, ones_bd) + _dot(lo_part, ones_bd)
        out = o * lax.rsqrt(ss * (1.0 / R_V_DIM) + LN_EPS) * g_ref[pl.ds(r0, C), :]
        o_ref[pl.ds(r0, C), :] = out.astype(BF16)
        return new_state

    state = lax.fori_loop(0, T // C, chunk, state)
    sn_ref[0] = state[:R_KEY_DIM, :R_V_DIM]
    sn_ref[1] = state[R_KEY_DIM:, R_V_DIM:]


def _retention(rq, rk, rv, rg, state0, tabs, B, T, C):
    dm, qd, kd, cd = tabs
    blk = pl.BlockSpec((T, LANES), lambda b, hp: (b, hp))
    st = pl.BlockSpec((None, 2, R_KEY_DIM, R_V_DIM), lambda b, hp: (b, hp, 0, 0))
    return pl.pallas_call(
        functools.partial(_ret_kernel, T=T, C=C),
        grid=(B, 2),
        in_specs=[blk, blk, blk, blk, st,
                  pl.BlockSpec((None, 2 * C, C), lambda b, hp: (hp, 0, 0)),
                  pl.BlockSpec((None, C, LANES), lambda b, hp: (hp, 0, 0)),
                  pl.BlockSpec((None, C, LANES), lambda b, hp: (hp, 0, 0)),
                  pl.BlockSpec((None, 1, LANES), lambda b, hp: (hp, 0, 0))],
        out_specs=[blk, st],
        out_shape=[jax.ShapeDtypeStruct((B * T, R_WIDTH), BF16),
                   jax.ShapeDtypeStruct((B, R_HEADS, R_KEY_DIM, R_V_DIM), F32)],
        compiler_params=_params(("arbitrary", "arbitrary"), 32),
        name="retention",
    )(rq, rk, rv, rg, state0, dm, qd, kd, cd)


def _retention_tables(C):
    h = jnp.arange(R_HEADS, dtype=F32)
    log_g = jnp.log1p(-jnp.exp2(-5.0 - h))
    i = jnp.arange(C, dtype=F32)
    diff = i[:, None] - i[None, :]
    dmask = jnp.where(diff >= 0, jnp.exp(jnp.maximum(diff, 0.0)[None] * log_g[:, None, None]), 0.0)
    dm = dmask.reshape(2, 2 * C, C)
    q_dec = jnp.exp((i + 1.0)[None, :] * log_g[:, None])
    k_dec = jnp.exp((C - 1 - i)[None, :] * log_g[:, None])
    c_dec = jnp.exp(C * log_g)

    def lanes(t):
        t = jnp.repeat(t[:, :, None], R_V_DIM, axis=2).reshape(2, 2, C, R_V_DIM)
        return jnp.concatenate([t[:, 0], t[:, 1]], axis=-1)

    cd = jnp.repeat(c_dec[:, None], R_V_DIM, axis=1).reshape(2, 1, LANES)
    return dm, lanes(q_dec), lanes(k_dec), cd


_CONV_PAD = 32


def _conv_kernel(u_ref, h_ref, w_ref, b_ref, g_ref, be_ref, o_ref, t_ref, up_scr, *, T, RT):
    hist = CONV_WIDTH - 1
    off = _CONV_PAD - hist
    up_scr[off:_CONV_PAD, :] = h_ref[...]
    up_scr[_CONV_PAD:_CONV_PAD + T, :] = u_ref[...]
    bias = b_ref[...]
    g = g_ref[...]
    be = be_ref[...]
    for t0 in range(0, T, RT):
        acc = jnp.zeros((RT, C_WIDTH), F32)
        for j in range(CONV_WIDTH):
            acc = acc + up_scr[t0 + j + off:t0 + j + off + RT, :] * w_ref[j:j + 1, :]
        y = _layer_norm(acc + bias, g, be)
        o_ref[t0:t0 + RT, :] = (y * _sigmoid(y)).astype(BF16)
    t_ref[...] = up_scr[T + off:T + _CONV_PAD, :]


def _conv(u, hist, dw_w, dw_b, ln_g, ln_b, B, T):
    RT = min(T, LANES)
    assert T % RT == 0
    hl = CONV_WIDTH - 1
    vec = pl.BlockSpec((1, C_WIDTH), lambda b: (0, 0))
    hb = pl.BlockSpec((None, hl, C_WIDTH), lambda b: (b, 0, 0))
    return pl.pallas_call(
        functools.partial(_conv_kernel, T=T, RT=RT),
        grid=(B,),
        in_specs=[pl.BlockSpec((T, C_WIDTH), lambda b: (b, 0)), hb,
                  pl.BlockSpec((CONV_WIDTH, C_WIDTH), lambda b: (0, 0)), vec, vec, vec],
        out_specs=[pl.BlockSpec((T, C_WIDTH), lambda b: (b, 0)), hb],
        out_shape=[jax.ShapeDtypeStruct((B * T, C_WIDTH), BF16),
                   jax.ShapeDtypeStruct((B, hl, C_WIDTH), F32)],
        scratch_shapes=[pltpu.VMEM((T + _CONV_PAD, C_WIDTH), F32)],
        compiler_params=_params(("arbitrary",), 32),
        name="conv",
    )(u, hist, dw_w, dw_b.reshape(1, C_WIDTH), ln_g.reshape(1, C_WIDTH), ln_b.reshape(1, C_WIDTH))


def _first_index(vals, target):
    idx = jnp.full(target.shape, len(vals) - 1, jnp.int32)
    for j in range(len(vals) - 2, -1, -1):
        idx = jnp.where(vals[j] == target, j, idx)
    return idx


def _select(idx, vals):
    out = vals[-1]
    for j in range(len(vals) - 2, -1, -1):
        out = jnp.where(idx == j, vals[j], out)
    return out


def _route(logits_t):
    rows = [logits_t[e:e + 1, :] for e in range(N_EXPERTS)]
    m = functools.reduce(jnp.maximum, rows)
    ex = [jnp.exp(r - m) for r in rows]
    z = functools.reduce(jnp.add, ex)
    sc = [e / z for e in ex]
    v1s, v2s, i1s, i2s, gss = [], [], [], [], []
    for g in range(N_GROUPS):
        a = sc[g * EXPERTS_PER_GROUP:(g + 1) * EXPERTS_PER_GROUP]
        v1 = functools.reduce(jnp.maximum, a)
        i1 = _first_index(a, v1)
        rest = [jnp.where(i1 == j, -1.0, a[j]) for j in range(EXPERTS_PER_GROUP)]
        v2 = functools.reduce(jnp.maximum, rest)
        i2 = _first_index(rest, v2)
        v1s.append(v1); v2s.append(v2); i1s.append(i1); i2s.append(i2); gss.append(v1 + v2)
    grp = _first_index(gss, functools.reduce(jnp.maximum, gss))
    v1 = _select(grp, v1s)
    v2 = _select(grp, v2s)
    e1 = _select(grp, i1s) + grp * EXPERTS_PER_GROUP
    e2 = _select(grp, i2s) + grp * EXPERTS_PER_GROUP
    den = v1 + v2
    return e1, e2, v1 / den, v2 / den


def _expert_onehot(e1, e2):
    rows = [jnp.where(e1 == e, 1.0, 0.0) + jnp.where(e2 == e, 1.0, 0.0) for e in range(N_EXPERTS)]
    return jnp.concatenate(rows, axis=0)


def _outproj_kernel(a_ref, r_ref, c_ref, x_ref, w_ref, g_ref, b_ref, wr_ref, br_ref,
                    x1_ref, x1b_ref, ri_ref, rw_ref, cnt_ref, *, alpha, tb):
    mix = (_dot(a_ref[...], w_ref[0:A_WIDTH, :])
           + _dot(r_ref[...], w_ref[A_WIDTH:A_WIDTH + R_WIDTH, :])
           + _dot(c_ref[...], w_ref[A_WIDTH + R_WIDTH:, :]))
    x1 = _layer_norm(alpha * x_ref[...] + mix, g_ref[...], b_ref[...])
    x1_ref[...] = x1
    x1b_ref[...] = x1.astype(BF16)
    logits_t = lax.dot_general(wr_ref[...], x1, (((1,), (1,)), ((), ())),
                               precision=lax.Precision.HIGHEST,
                               preferred_element_type=F32) + br_ref[...]
    e1, e2, w1, w2 = _route(logits_t)
    tm = e1.shape[1]
    ri_ref[...] = jnp.concatenate([e1, e2, jnp.zeros((SUBLANES - 2, tm), jnp.int32)], axis=0)
    rw_ref[...] = jnp.concatenate([w1, w2, jnp.zeros((SUBLANES - 2, tm), F32)], axis=0)
    onehot = _expert_onehot(e1, e2).astype(BF16)
    ones = jnp.ones((SUBLANES, tb), BF16)
    for k in range(tm // tb):
        c = _dot_nt(ones, onehot[:, k * tb:(k + 1) * tb])
        c = jnp.concatenate([c, jnp.zeros((SUBLANES, LANES - N_EXPERTS), F32)], axis=1)
        cnt_ref[k * SUBLANES:(k + 1) * SUBLANES, :] = c.astype(jnp.int32)


def _outproj(attn, ret, conv, x, w_bf, ln_g, ln_b, wr_t, br, alpha, tm, tb):
    n = x.shape[0]
    row = lambda i: (i, 0)
    col = lambda i: (0, i)
    const = lambda i: (0, 0)
    nsub = tm // tb
    return pl.pallas_call(
        functools.partial(_outproj_kernel, alpha=alpha, tb=tb),
        grid=(n // tm,),
        in_specs=[pl.BlockSpec((tm, A_WIDTH), row), pl.BlockSpec((tm, R_WIDTH), row),
                  pl.BlockSpec((tm, C_WIDTH), row), pl.BlockSpec((tm, D_MODEL), row),
                  pl.BlockSpec((D_MODEL, D_MODEL), const),
                  pl.BlockSpec((1, D_MODEL), const), pl.BlockSpec((1, D_MODEL), const),
                  pl.BlockSpec((N_EXPERTS, D_MODEL), const), pl.BlockSpec((N_EXPERTS, 1), const)],
        out_specs=[pl.BlockSpec((tm, D_MODEL), row), pl.BlockSpec((tm, D_MODEL), row),
                   pl.BlockSpec((SUBLANES, tm), col), pl.BlockSpec((SUBLANES, tm), col),
                   pl.BlockSpec((nsub * SUBLANES, LANES), row)],
        out_shape=[jax.ShapeDtypeStruct((n, D_MODEL), F32), jax.ShapeDtypeStruct((n, D_MODEL), BF16),
                   jax.ShapeDtypeStruct((SUBLANES, n), jnp.int32),
                   jax.ShapeDtypeStruct((SUBLANES, n), F32),
                   jax.ShapeDtypeStruct((n // tb * SUBLANES, LANES), jnp.int32)],
        compiler_params=_params(("arbitrary",), 40),
        name="outproj",
    )(attn, ret, conv, x, w_bf, ln_g.reshape(1, D_MODEL), ln_b.reshape(1, D_MODEL), wr_t, br)


_UNIT = 8
_XS_WIDTH = D_MODEL + LANES
_FFN_ROWS = 512
_DISPATCH_TOKENS = 256


def _sorted_positions(e1, e2, lo_ref, base, tb):
    onehot = _expert_onehot(e1, e2).astype(BF16)
    r = lax.broadcasted_iota(jnp.int32, (tb, tb), 0)
    c = lax.broadcasted_iota(jnp.int32, (tb, tb), 1)
    earlier = jnp.where(r < c, 1.0, 0.0).astype(BF16)
    rank = _dot(onehot, earlier).astype(jnp.int32)
    pos1 = jnp.zeros(e1.shape, jnp.int32)
    pos2 = jnp.zeros(e1.shape, jnp.int32)
    for e in range(N_EXPERTS):
        p = rank[e:e + 1, :] + lo_ref[base + e]
        pos1 = jnp.where(e1 == e, p, pos1)
        pos2 = jnp.where(e2 == e, p, pos2)
    return pos1, pos2


def _permutation(pos1, pos2, rows):
    j = lax.broadcasted_iota(jnp.int32, (rows, pos1.shape[1]), 0)
    return jnp.where(j == pos1, 1.0, jnp.where(j == pos2, 1.0, 0.0)).astype(BF16)


def _unit_copy(src, dst, sem):
    return pltpu.make_async_copy(src, dst, sem)


def _dispatch_kernel(lo_ref, pc_ref, gs_ref, ri_ref, rw_ref, x_ref, xs_hbm, pos_ref,
                     stage, sems, issued, *, nblk, tb, cap):
    i = pl.program_id(0)
    slot = i % 2
    base = i * N_EXPERTS

    def wait_units(s, n):
        def body(u, c):
            _unit_copy(stage.at[s, pl.ds(0, _UNIT), :], xs_hbm.at[pl.ds(0, _UNIT), :], sems.at[s]).wait()
            return c
        lax.fori_loop(0, n, body, 0)

    @pl.when(i >= 2)
    def _():
        wait_units(slot, issued[slot])

    e1 = ri_ref[0:1, :]
    e2 = ri_ref[1:2, :]
    w1 = rw_ref[0:1, :]
    w2 = rw_ref[1:2, :]
    pos1, pos2 = _sorted_positions(e1, e2, lo_ref, base, tb)
    pos_ref[...] = jnp.concatenate([pos1, pos2, jnp.zeros((SUBLANES - 2, tb), jnp.int32)], axis=0)
    perm = _permutation(pos1, pos2, cap)
    j = lax.broadcasted_iota(jnp.int32, (cap, tb), 0)
    gate = jnp.sum(jnp.where(j == pos1, w1, jnp.where(j == pos2, w2, 0.0)), axis=1, keepdims=True)
    stage[slot, :, 0:D_MODEL] = _dot(perm, x_ref[...])
    stage[slot, :, D_MODEL:_XS_WIDTH] = jnp.broadcast_to(gate, (cap, LANES))

    total = 0
    for e in range(N_EXPERTS):
        n = pc_ref[base + e] // _UNIT
        src0 = lo_ref[base + e]
        dst0 = gs_ref[base + e]

        def body(u, c, src0=src0, dst0=dst0):
            s = pl.multiple_of(src0 + u * _UNIT, _UNIT)
            d = pl.multiple_of(dst0 + u * _UNIT, _UNIT)
            _unit_copy(stage.at[slot, pl.ds(s, _UNIT), :], xs_hbm.at[pl.ds(d, _UNIT), :], sems.at[slot]).start()
            return c

        lax.fori_loop(0, n, body, 0)
        total = total + n
    issued[slot] = total

    @pl.when(i == nblk - 1)
    def _():
        wait_units(slot, issued[slot])
        if nblk > 1:
            wait_units(1 - slot, issued[1 - slot])


def _ffn_kernel(te_ref, tv_ref, nt_ref, xs_ref, wg_ref, wu_ref, wd_ref, ys_ref):
    j = pl.program_id(0)

    @pl.when(j < nt_ref[0])
    def _():
        xv = xs_ref[...]
        rows = xv.shape[0]
        valid = lax.broadcasted_iota(jnp.int32, (rows, LANES), 0) < tv_ref[j]
        x = jnp.concatenate([jnp.where(valid, xv[:, c * LANES:(c + 1) * LANES], 0.0)
                             for c in range(D_MODEL // LANES)], axis=1).astype(BF16)
        g = jnp.where(valid, xv[:, D_MODEL:_XS_WIDTH], 0.0)
        hg = _dot(x, wg_ref[...])
        hu = _dot(x, wu_ref[...])
        h = hg * _sigmoid(hg) * hu * jnp.concatenate([g] * (D_EXPERT // LANES), axis=1)
        ys_ref[...] = _dot(h.astype(BF16), wd_ref[...])


def _combine_kernel(lo_ref, pc_ref, gs_ref, pos_ref, x1_ref, p_ref, g_ref, b_ref, wpg_ref, bpg_ref, wpp_ref,
                    ys_hbm, out_ref, stage, sems, *, nblk, tb, cap, alpha):
    i = pl.program_id(0)
    slot = i % 2

    def copies(blk, s, start):
        base = blk * N_EXPERTS
        for e in range(N_EXPERTS):
            n = pc_ref[base + e] // _UNIT
            src0 = gs_ref[base + e]
            dst0 = lo_ref[base + e]

            def body(u, c, src0=src0, dst0=dst0):
                sr = pl.multiple_of(src0 + u * _UNIT, _UNIT)
                ds = pl.multiple_of(dst0 + u * _UNIT, _UNIT)
                cp = _unit_copy(ys_hbm.at[pl.ds(sr, _UNIT), :], stage.at[s, pl.ds(ds, _UNIT), :], sems.at[s])
                if start:
                    cp.start()
                else:
                    cp.wait()
                return c

            lax.fori_loop(0, n, body, 0)

    @pl.when(i == 0)
    def _():
        stage[...] = jnp.zeros(stage.shape, F32)
        copies(0, 0, True)

    @pl.when(i + 1 < nblk)
    def _():
        copies(i + 1, 1 - slot, True)

    copies(i, slot, False)
    perm = _permutation(pos_ref[0:1, :], pos_ref[1:2, :], cap)
    y = _dot_tn(perm, stage[slot].astype(BF16))
    x2 = _layer_norm(alpha * x1_ref[...] + y, g_ref[...], b_ref[...])
    gate = _sigmoid(_dot(x2.astype(BF16), wpg_ref[...]) + bpg_ref[...])
    out_ref[...] = x2 + gate * _dot(p_ref[...].astype(BF16), wpp_ref[...])


def _moe_plan(cnt, nblk, tb):
    cnt = cnt.reshape(nblk, SUBLANES, LANES)[:, 0, :N_EXPERTS]
    pc = (cnt + (_UNIT - 1)) // _UNIT * _UNIT
    lo = jnp.cumsum(pc, axis=1) - pc
    tot = jnp.sum(pc, axis=0)
    reg = (tot + (_FFN_ROWS - 1)) // _FFN_ROWS * _FFN_ROWS
    rstart = jnp.cumsum(reg) - reg
    gs = rstart[None, :] + jnp.cumsum(pc, axis=0) - pc
    tiles_e = reg // _FFN_ROWS
    tile_end = jnp.cumsum(tiles_e)
    max_rows = 2 * nblk * tb + nblk * N_EXPERTS * _UNIT + N_EXPERTS * _FFN_ROWS
    max_tiles = -(-max_rows // _FFN_ROWS)
    tj =jnp.arange(max_tiles, dtype=jnp.int32)
    te = jnp.minimum(jnp.sum((tj[:, None] >= tile_end[None, :]).astype(jnp.int32), axis=1), N_EXPERTS - 1)
    tv = jnp.clip(tot[te] - (tj - (tile_end - tiles_e)[te]) * _FFN_ROWS, 0, _FFN_ROWS)
    nt = tile_end[-1:].astype(jnp.int32)
    flat = lambda a: a.reshape(-1).astype(jnp.int32)
    return flat(lo), flat(pc), flat(gs), te.astype(jnp.int32), tv.astype(jnp.int32), nt, max_tiles


def _moe(x1b, x1, ri, rw, cnt, p, wg, wu, wd, ln_g, ln_b, wpg, bpg, wpp, alpha, tb):
    n = x1.shape[0]
    nblk = n // tb
    cap = 2 * tb + N_EXPERTS * _UNIT
    lo, pc, gs, te, tv, nt, max_tiles = _moe_plan(cnt, nblk, tb)
    max_rows = max_tiles * _FFN_ROWS

    xs, pos = pl.pallas_call(
        functools.partial(_dispatch_kernel, nblk=nblk, tb=tb, cap=cap),
        grid_spec=pltpu.PrefetchScalarGridSpec(
            num_scalar_prefetch=3,
            grid=(nblk,),
            in_specs=[pl.BlockSpec((SUBLANES, tb), lambda i, *_: (0, i)),
                      pl.BlockSpec((SUBLANES, tb), lambda i, *_: (0, i)),
                      pl.BlockSpec((tb, D_MODEL), lambda i, *_: (i, 0))],
            out_specs=[pl.BlockSpec(memory_space=pl.ANY),
                       pl.BlockSpec((SUBLANES, tb), lambda i, *_: (0, i))],
            scratch_shapes=[pltpu.VMEM((2, cap, _XS_WIDTH), F32),
                            pltpu.SemaphoreType.DMA((2,)),
                            pltpu.SMEM((2,), jnp.int32)]),
        out_shape=[jax.ShapeDtypeStruct((max_rows, _XS_WIDTH), F32),
                   jax.ShapeDtypeStruct((SUBLANES, n), jnp.int32)],
        compiler_params=_params(("arbitrary",), 32),
        name="moe_dispatch",
    )(lo, pc, gs, ri, rw, x1b)

    def tile(j, te_ref, tv_ref, nt_ref):
        return jnp.minimum(j, nt_ref[0] - 1)

    ys = pl.pallas_call(
        _ffn_kernel,
        grid_spec=pltpu.PrefetchScalarGridSpec(
            num_scalar_prefetch=3,
            grid=(max_tiles,),
            in_specs=[pl.BlockSpec((_FFN_ROWS, _XS_WIDTH), lambda j, *s: (tile(j, *s), 0)),
                      pl.BlockSpec((None, D_MODEL, D_EXPERT), lambda j, *s: (s[0][tile(j, *s)], 0, 0)),
                      pl.BlockSpec((None, D_MODEL, D_EXPERT), lambda j, *s: (s[0][tile(j, *s)], 0, 0)),
                      pl.BlockSpec((None, D_EXPERT, D_MODEL), lambda j, *s: (s[0][tile(j, *s)], 0, 0))],
            out_specs=pl.BlockSpec((_FFN_ROWS, D_MODEL), lambda j, *s: (tile(j, *s), 0))),
        out_shape=jax.ShapeDtypeStruct((max_rows, D_MODEL), F32),
        compiler_params=_params(("arbitrary",), 32),
        name="moe_ffn",
    )(te, tv, nt, xs, wg, wu, wd)

    const = lambda i, *_: (0, 0)
    return pl.pallas_call(
        functools.partial(_combine_kernel, nblk=nblk, tb=tb, cap=cap, alpha=alpha),
        grid_spec=pltpu.PrefetchScalarGridSpec(
            num_scalar_prefetch=3,
            grid=(nblk,),
            in_specs=[pl.BlockSpec((SUBLANES, tb), lambda i, *_: (0, i)),
                      pl.BlockSpec((tb, D_MODEL), lambda i, *_: (i, 0)),
                      pl.BlockSpec((tb, PLE_DIM), lambda i, *_: (i, 0)),
                      pl.BlockSpec((1, D_MODEL), const), pl.BlockSpec((1, D_MODEL), const),
                      pl.BlockSpec((D_MODEL, D_MODEL), const), pl.BlockSpec((1, D_MODEL), const),
                      pl.BlockSpec((PLE_DIM, D_MODEL), const),
                      pl.BlockSpec(memory_space=pl.ANY)],
            out_specs=pl.BlockSpec((tb, D_MODEL), lambda i, *_: (i, 0)),
            scratch_shapes=[pltpu.VMEM((2, cap, D_MODEL), F32),
                            pltpu.SemaphoreType.DMA((2,))]),
        out_shape=jax.ShapeDtypeStruct((n, D_MODEL), F32),
        compiler_params=_params(("arbitrary",), 40),
        name="moe_combine",
    )(lo, pc, gs, pos, x1, p, ln_g.reshape(1, D_MODEL), ln_b.reshape(1, D_MODEL),
      wpg, bpg.reshape(1, D_MODEL), wpp, ys)


def _rope_tables(pos, rows):
    half = R_KEY_DIM // 2
    inv_freq = 1.0 / (ROPE_BASE ** jnp.linspace(0.0, 1.0, half, dtype=jnp.float32))
    ang = pos.astype(jnp.float32)[:, None] * inv_freq[None, :]
    cos = jnp.cos(ang)
    sin = jnp.sin(ang)
    cos_t = jnp.tile(cos, (rows // pos.shape[0], 4))
    sin_t = jnp.tile(jnp.concatenate([-sin, sin], axis=1), (rows // pos.shape[0], 2))
    return cos_t, sin_t


def kernel(x_prompt, x_sample, p_prompt, p_sample, cache_k, cache_v, state_ret, state_conv, ln_emb_g, ln_emb_b, rel_bias, w_router, b_router, w_in, lam_q1, lam_k1, lam_q2, lam_k2, subln_g, dw_w, dw_b, conv_ln_g, conv_ln_b, w_out, ln1_g, ln1_b, w_exp_gate, w_exp_up, w_exp_down, ln2_g, ln2_b, w_ple_gate, b_ple_gate, w_ple_proj):
    B, T, D = x_prompt.shape
    Bs, Ts, _ = x_sample.shape
    depth = w_in.shape[0]
    past = cache_k.shape[2]
    n_p, n_s = B * T, Bs * Ts
    alpha = (2 * depth) ** 0.25

    tm_p = 512 if n_p % 512 == 0 else n_p
    tm_s = n_s
    TB = 2 * LANES
    c_p = min(T, 2 * LANES)

    pos_p = jnp.arange(T, dtype=jnp.int32)
    pos_s = past + jnp.arange(Ts, dtype=jnp.int32)
    rope_p = _rope_tables(pos_p, max(T, tm_p))
    rope_s = _rope_tables(pos_s, max(Ts, tm_s))
    rel_p = (jnp.arange(2 * TB, dtype=jnp.int32)[None, :] - TB) - jnp.arange(TB, dtype=jnp.int32)[:, None]
    padk = -(-(past + Ts) // LANES) * LANES
    rel_s = jnp.arange(padk, dtype=jnp.int32)[None, :] - pos_s[:, None]
    bias_p = _bias_table(rel_bias, _t5_bucket(rel_p))
    bias_s = _bias_table(rel_bias, _t5_bucket(rel_s))
    tabs_p = _retention_tables(c_p)
    tabs_s = _retention_tables(Ts)

    wr_t = w_router.T
    br = b_router.reshape(N_EXPERTS, 1)
    ret0_p = jnp.zeros((B, R_HEADS, R_KEY_DIM, R_V_DIM), F32)
    conv0_p = jnp.zeros((B, CONV_WIDTH - 1, C_WIDTH), F32)

    xp = x_prompt.reshape(n_p, D)
    xs = x_sample.reshape(n_s, D)
    outs = {k: [] for k in ("rp", "cp", "rs", "cs")}
    kv_p = kv_s = None
    for l in range(depth):
        lam_init = 0.8 - 0.6 * math.exp(-0.3 * l)
        lam4 = jnp.stack([lam_q1[l], lam_k1[l], lam_q2[l], lam_k2[l]])
        g_sub = subln_g[l].reshape(1, A_V_DIM)
        w_in_b = w_in[l].astype(BF16)
        w_out_b = w_out[l].astype(BF16)
        wg, wu, wd = (w_exp_gate[l].astype(BF16), w_exp_up[l].astype(BF16), w_exp_down[l].astype(BF16))
        wpg, wpp = w_ple_gate[l].astype(BF16), w_ple_proj[l].astype(BF16)
        ln = (ln_emb_g, ln_emb_b) if l == 0 else None

        def channel(x, attn, ret, conv, p_l, tm):
            tb = min(tm, _DISPATCH_TOKENS)
            x1, x1b, ri, rw, cnt = _outproj(attn, ret, conv, x, w_out_b, ln1_g[l], ln1_b[l], wr_t, br,
                                            alpha, tm, tb)
            return _moe(x1b, x1, ri, rw, cnt, p_l, wg, wu, wd, ln2_g[l], ln2_b[l], wpg, b_ple_gate[l], wpp,
                        alpha, tb)

        xp, q, kb, vb, rq, rk, rv, rg, u, k5, v5 = _inproj(xp, w_in_b, rope_p[0], rope_p[1], ln, tm_p,
                                                           kv_p, l, depth, B, T)
        kv_p = (k5, v5)
        attn = _attn_prompt(lam4, g_sub, bias_p, q, kb, vb, B, T, lam_init)
        ret, rstate = _retention(rq, rk, rv, rg, ret0_p, tabs_p, B, T, c_p)
        conv, ctail = _conv(u, conv0_p, dw_w[l], dw_b[l], conv_ln_g[l], conv_ln_b[l], B, T)
        xp = channel(xp, attn, ret, conv, p_prompt[l].reshape(n_p, PLE_DIM), tm_p)
        outs["rp"].append(rstate)
        outs["cp"].append(ctail)

        xs, q, kb, vb, rq, rk, rv, rg, u, k5, v5 = _inproj(xs, w_in_b, rope_s[0], rope_s[1], ln, tm_s,
                                                           kv_s, l, depth, Bs, Ts)
        kv_s = (k5, v5)
        attn = _attn_decode(lam4, g_sub, bias_s, q, cache_k, cache_v, kb, vb, l, Bs, Ts, lam_init)
        ret, rstate = _retention(rq, rk, rv, rg, state_ret[l], tabs_s, Bs, Ts, Ts)
        conv, ctail = _conv(u, state_conv[l], dw_w[l], dw_b[l], conv_ln_g[l], conv_ln_b[l], Bs, Ts)
        xs = channel(xs, attn, ret, conv, p_sample[l].reshape(n_s, PLE_DIM), tm_s)
        outs["rs"].append(rstate)
        outs["cs"].append(ctail)

    return (xp.reshape(B, T, D), xs.reshape(Bs, Ts, D),
            kv_p[0], kv_p[1], jnp.stack(outs["rp"]), jnp.stack(outs["cp"]),
            kv_s[0], kv_s[1], jnp.stack(outs["rs"]), jnp.stack(outs["cs"]))
```

```python
import functools
import math

import jax
import jax.numpy as jnp
from jax import lax
from jax.experimental import pallas as pl
from jax.experimental.pallas import tpu as pltpu

F32 = jnp.float32
BF16 = jnp.bfloat16

D_MODEL = 1024
CHUNK = 64
HEAD_DIM = 64
A_HEADS = 4
A_V_DIM = 128
A_WIDTH = 512
R_HEADS = 4
R_KEY_DIM = 64
R_V_DIM = 64
R_WIDTH = 256
C_WIDTH = 256
CONV_WIDTH = 31
IN_WIDTH = 3072
N_BUCKETS = 32
MAX_DISTANCE = 128
ROPE_BASE = 10000.0
N_EXPERTS = 16
N_GROUPS = 4
EXPERTS_PER_GROUP = 4
D_EXPERT = 512
PLE_DIM = 256
LN_EPS = 1e-5
NEG_INF = -1e30
LOG2E = 1.4426950408889634

LANES = 128
SUBLANES = 8
_LOG2_CHUNK = 6
assert CHUNK == HEAD_DIM == R_KEY_DIM == R_V_DIM == 1 << _LOG2_CHUNK
MIB = 1024 * 1024

_OFF_Q, _OFF_K, _OFF_V = 0, 512, 1024
_OFF_RQ, _OFF_RK, _OFF_RV, _OFF_RG, _OFF_C = 1536, 1792, 2048, 2304, 2560


def _params(sem, vmem_mib):
    return pltpu.CompilerParams(dimension_semantics=sem, vmem_limit_bytes=vmem_mib * MIB)


def _layer_norm(x, g, b):
    mu = jnp.mean(x, axis=-1, keepdims=True)
    xc = x - mu
    var = jnp.mean(xc * xc, axis=-1, keepdims=True)
    return xc * lax.rsqrt(var + LN_EPS) * g + b


def _sigmoid(x):
    return 1.0 / (1.0 + jnp.exp(-x))


def _dot(a, b):
    return jnp.dot(a, b, preferred_element_type=F32)


def _dot_nt(a, b):
    return lax.dot_general(a, b, (((1,), (1,)), ((), ())), preferred_element_type=F32)


def _dot_tn(a, b):
    return lax.dot_general(a, b, (((0,), (0,)), ((), ())), preferred_element_type=F32)


def _bias_kernel(relb_ref, idx_ref, out_ref):
    idx = idx_ref[...]
    for h in range(A_HEADS):
        acc = jnp.zeros(idx.shape, F32)
        for b in range(N_BUCKETS):
            acc = jnp.where(idx == b, relb_ref[b, h], acc)
        out_ref[h] = acc


def _bias_table(rel_bias, idx):
    r, c = idx.shape
    return pl.pallas_call(
        _bias_kernel,
        out_shape=jax.ShapeDtypeStruct((A_HEADS, r, c), F32),
        in_specs=[pl.BlockSpec(memory_space=pltpu.SMEM),
                  pl.BlockSpec(memory_space=pltpu.VMEM)],
        out_specs=pl.BlockSpec(memory_space=pltpu.VMEM),
        name="bias_table",
    )(rel_bias, idx)


def _t5_bucket(rel):
    nb = N_BUCKETS // 2
    max_exact = nb // 2
    n = jnp.abs(rel)
    nf = jnp.maximum(n, 1).astype(jnp.float32)
    large = max_exact + (jnp.log(nf / max_exact) / math.log(MAX_DISTANCE / max_exact)
                         * (nb - max_exact)).astype(jnp.int32)
    large = jnp.minimum(large, nb - 1)
    return jnp.where(rel > 0, nb, 0) + jnp.where(n < max_exact, n, large)


def _rotary128(x, cos, sin_signed, lo32):
    partner = jnp.where(lo32, pltpu.roll(x, 96, 1), pltpu.roll(x, 32, 1))
    return x * cos + partner * sin_signed


def _inproj_kernel(*refs, apply_ln, has_prev, layer, tm, T, nsteps):
    refs = list(refs)
    x_ref = refs.pop(0)
    if apply_ln:
        g_ref, b_ref = refs.pop(0), refs.pop(0)
    w_ref, cos_ref, sin_ref = refs.pop(0), refs.pop(0), refs.pop(0)
    if has_prev:
        refs.pop(0), refs.pop(0)
    if apply_ln:
        xn_ref = refs.pop(0)
    (q_ref, kb_ref, vb_ref, rq_ref, rk_ref, rv_ref, rg_ref, u_ref, k5_hbm, v5_hbm,
     kbuf, vbuf, sems) = refs

    i = pl.program_id(0)
    slot = i % 2
    rows = min(tm, T)

    def kv_copies(s, step):
        cps = []
        for bb in range(tm // rows):
            r0 = step * tm + bb * rows
            b = r0 // T
            t0 = pl.multiple_of(r0 % T, SUBLANES)
            for buf, out in ((kbuf, k5_hbm), (vbuf, v5_hbm)):
                for h in range(A_HEADS):
                    cps.append(pltpu.make_async_copy(
                        buf.at[s, pl.ds(bb * rows, rows), pl.ds(h * A_V_DIM, A_V_DIM)],
                        out.at[layer, b, pl.ds(t0, rows), h, :], sems.at[s]))
        return cps

    @pl.when(i >= 2)
    def _():
        for cp in kv_copies(slot, i - 2):
            cp.wait()

    x = x_ref[...]
    if apply_ln:
        x = _layer_norm(x, g_ref[...], b_ref[...])
        xn_ref[...] = x
    xb = x.astype(BF16)

    def mm(c0, c1):
        return _dot(xb, w_ref[:, c0:c1])

    q_ref[...] = (mm(_OFF_Q, _OFF_K) * (LOG2E * HEAD_DIM ** -0.5)).astype(BF16)
    a = mm(_OFF_K, _OFF_V)
    kbuf[slot] = a
    kb_ref[...] = a.astype(BF16)
    a = mm(_OFF_V, _OFF_RQ)
    vbuf[slot] = a
    vb_ref[...] = a.astype(BF16)
    for cp in kv_copies(slot, i):
        cp.start()

    cos = cos_ref[...]
    sin = sin_ref[...]
    lane = lax.broadcasted_iota(jnp.int32, cos.shape, 1)
    lo32 = (lane & 63) < 32

    def rot(a, scale):
        parts = [_rotary128(a[:, c * LANES:(c + 1) * LANES], cos, sin, lo32) for c in range(2)]
        r = jnp.concatenate(parts, axis=1)
        if scale != 1.0:
            r = r * scale
        return r.astype(BF16)

    rq_ref[...] = rot(mm(_OFF_RQ, _OFF_RK), 1.0)
    rk_ref[...] = rot(mm(_OFF_RK, _OFF_RV), R_KEY_DIM ** -0.5)
    rv_ref[...] = mm(_OFF_RV, _OFF_RG).astype(BF16)
    a = mm(_OFF_RG, _OFF_C)
    rg_ref[...] = a * _sigmoid(a)
    a = mm(_OFF_C, IN_WIDTH)
    u_ref[...] = a[:, :C_WIDTH] * _sigmoid(a[:, C_WIDTH:])

    @pl.when(i == nsteps - 1)
    def _():
        for cp in kv_copies(slot, i):
            cp.wait()
        if nsteps > 1:
            for cp in kv_copies(1 - slot, i - 1):
                cp.wait()


def _inproj(x, w_bf, cos_t, sin_t, ln, tm, kv_prev, layer, depth, B, T):
    n = x.shape[0]
    assert (tm % T == 0 or T % tm == 0) and n % tm == 0
    nblk = cos_t.shape[0] // tm
    nsteps = n // tm
    row = lambda i: (i, 0)
    const = lambda i: (0, 0)
    tab = lambda i: (i % nblk, 0)
    in_specs = [pl.BlockSpec((tm, D_MODEL), row)]
    args = [x]
    if ln is not None:
        in_specs += [pl.BlockSpec((1, D_MODEL), const)] * 2
        args += [ln[0].reshape(1, D_MODEL), ln[1].reshape(1, D_MODEL)]
    in_specs += [pl.BlockSpec((D_MODEL, IN_WIDTH), const),
                 pl.BlockSpec((tm, LANES), tab), pl.BlockSpec((tm, LANES), tab)]
    args += [w_bf, cos_t, sin_t]
    aliases = {}
    if kv_prev is not None:
        k5_index = (1 if ln is not None else 0) + 8
        aliases = {len(args): k5_index, len(args) + 1: k5_index + 1}
        in_specs += [pl.BlockSpec(memory_space=pl.ANY)] * 2
        args += list(kv_prev)

    def o(width, dt):
        return jax.ShapeDtypeStruct((n, width), dt), pl.BlockSpec((tm, width), row)

    outs = []
    if ln is not None:
        outs.append(o(D_MODEL, F32))
    outs += [o(512, BF16), o(512, BF16), o(512, BF16),
             o(256, BF16), o(256, BF16), o(256, BF16), o(256, F32), o(256, F32)]
    kv5 = jax.ShapeDtypeStruct((depth, B, T, A_HEADS, A_V_DIM), F32)
    outs += [(kv5, pl.BlockSpec(memory_space=pl.ANY))] * 2
    res = pl.pallas_call(
        functools.partial(_inproj_kernel, apply_ln=ln is not None, has_prev=kv_prev is not None,
                          layer=layer, tm=tm, T=T, nsteps=nsteps),
        grid=(nsteps,),
        in_specs=in_specs,
        out_specs=[s for _, s in outs],
        out_shape=[s for s, _ in outs],
        scratch_shapes=[pltpu.VMEM((2, tm, A_WIDTH), F32), pltpu.VMEM((2, tm, A_WIDTH), F32),
                        pltpu.SemaphoreType.DMA((2,))],
        input_output_aliases=aliases,
        compiler_params=_params(("arbitrary",), 52),
        name="inproj",
    )(*args)
    if ln is None:
        res = [x] + list(res)
    return res


def _lambda(lam_ref, lam_init):
    lv = lam_ref[...]
    s1 = jnp.sum(lv[0:1] * lv[1:2], axis=-1, keepdims=True)
    s2 = jnp.sum(lv[2:3] * lv[3:4], axis=-1, keepdims=True)
    return jnp.exp(s1) - jnp.exp(s2) + lam_init


def _stack_maps(q):
    lane = lax.broadcasted_iota(jnp.int32, q.shape, 1)
    lo = lane < HEAD_DIM
    z = jnp.zeros_like(q)
    return jnp.concatenate([jnp.where(lo, q, z), jnp.where(lo, z, q)], axis=0)


def _diff_finish(o2, t, lam, g, lam_init):
    o = o2[:t] - lam * o2[t:]
    ms = jnp.mean(o * o, axis=-1, keepdims=True)
    return (o * lax.rsqrt(ms + LN_EPS) * g * (1.0 - lam_init)).astype(BF16)


def _attn_prompt_kernel(lam_ref, g_ref, bias_ref, q_ref, k_ref, v_ref, o_ref,
                        s_scr, m_scr, bp_scr, bd_scr, *, T, TB, lam_init):
    lam = _lambda(lam_ref, lam_init)
    g = g_ref[...]
    bt = bias_ref[...]
    bt = (bt - bt[0:1, 0:1]) * LOG2E
    bp_scr[...] = bt[:, :TB]
    bd_scr[...] = bt[:, TB:]
    half = TB // 2

    for qi in range(T // TB):
        r0 = qi * TB
        q2 = _stack_maps(q_ref[r0:r0 + TB, :])
        m = None
        for ki in range(qi + 1):
            s = _dot_nt(q2, k_ref[ki * TB:(ki + 1) * TB, :])
            if ki == qi - 1:
                s = (s.reshape(2, TB, TB) + bp_scr[...][None]).reshape(2 * TB, TB)
            elif ki == qi:
                row = lax.broadcasted_iota(jnp.int32, (TB, TB), 0)
                col = lax.broadcasted_iota(jnp.int32, (TB, TB), 1)
                vis = (col >> _LOG2_CHUNK) <= (row >> _LOG2_CHUNK)
                s = s.reshape(2, TB, TB) + bd_scr[...][None]
                s = jnp.where(vis[None], s, NEG_INF).reshape(2 * TB, TB)
            s_scr[ki] = s
            mt = jnp.maximum(s[:, :half], s[:, half:])
            m = mt if m is None else jnp.maximum(m, mt)
        m_scr[...] = jnp.broadcast_to(jnp.max(m, axis=1, keepdims=True), m_scr.shape)
        l = None
        acc = None
        for ki in range(qi + 1):
            mb = m_scr[...]
            p = jnp.exp2(s_scr[ki] - jnp.concatenate([mb, mb], axis=1))
            lt = p[:, :half] + p[:, half:]
            pv = _dot(p.astype(BF16), v_ref[ki * TB:(ki + 1) * TB, :])
            l = lt if l is None else l + lt
            acc = pv if acc is None else acc + pv
        l_row = jnp.sum(l, axis=1, keepdims=True)
        o_ref[r0:r0 + TB, :] = _diff_finish(acc / l_row, TB, lam, g, lam_init)


def _attn_prompt(lam4, g, bias, q, k, v, B, T, lam_init):
    TB = 2 * LANES
    assert T % TB == 0 and TB % CHUNK == 0
    n = B * T
    blk = pl.BlockSpec((T, A_V_DIM), lambda b, h: (b, h))
    return pl.pallas_call(
        functools.partial(_attn_prompt_kernel, T=T, TB=TB, lam_init=lam_init),
        grid=(B, A_HEADS),
        in_specs=[pl.BlockSpec((4, HEAD_DIM), lambda b, h: (0, 0)),
                  pl.BlockSpec((1, A_V_DIM), lambda b, h: (0, 0)),
                  pl.BlockSpec((None, TB, 2 * TB), lambda b, h: (h, 0, 0)),
                  blk, blk, blk],
        out_specs=blk,
        out_shape=jax.ShapeDtypeStruct((n, A_WIDTH), BF16),
        scratch_shapes=[pltpu.VMEM((T // TB, 2 * TB, TB), F32),
                        pltpu.VMEM((2 * TB, LANES), F32),
                        pltpu.VMEM((TB, TB), F32),
                        pltpu.VMEM((TB, TB), F32)],
        compiler_params=_params(("arbitrary", "arbitrary"), 32),
        name="attn_prompt",
    )(lam4, g, bias, q, k, v)


def _attn_decode_kernel(lam_ref, g_ref, bias_ref, q_ref, kn_ref, vn_ref, ck_hbm, cv_hbm, o_ref,
                        kc_buf, vc_buf, sems, *, Ts, past, layer, lam_init):
    b = pl.program_id(0)
    h = pl.program_id(1)
    step = b * A_HEADS + h
    nsteps = pl.num_programs(0) * A_HEADS
    slot = step % 2

    def cache_copies(st, s):
        bb = st // A_HEADS
        hh = st % A_HEADS
        return [pltpu.make_async_copy(src.at[layer, bb, :, hh, :], dst.at[s], sems.at[s, j])
                for j, (src, dst) in enumerate(((ck_hbm, kc_buf), (cv_hbm, vc_buf)))]

    @pl.when(step == 0)
    def _():
        for cp in cache_copies(0, 0):
            cp.start()

    @pl.when(step + 1 < nsteps)
    def _():
        for cp in cache_copies(step + 1, 1 - slot):
            cp.start()

    for cp in cache_copies(step, slot):
        cp.wait()
    kc_ref = kc_buf.at[slot]
    vc_ref = vc_buf.at[slot]

    lam = _lambda(lam_ref, lam_init)
    q2 = _stack_maps(q_ref[...])
    bias = bias_ref[...] * LOG2E
    s_p = _dot_nt(q2, kc_ref[...].astype(BF16)).reshape(2, Ts, past) + bias[:, :past][None]
    s_n = _dot_nt(q2, kn_ref[...]).reshape(2, Ts, Ts) + bias[:, past:past + Ts][None]
    s_p = s_p.reshape(2 * Ts, past)
    s_n = s_n.reshape(2 * Ts, Ts)
    m = jnp.maximum(jnp.max(s_p, axis=1, keepdims=True), jnp.max(s_n, axis=1, keepdims=True))
    p_p = jnp.exp2(s_p - m)
    p_n = jnp.exp2(s_n - m)
    l = jnp.sum(p_p, axis=1, keepdims=True) + jnp.sum(p_n, axis=1, keepdims=True)
    acc = _dot(p_p.astype(BF16), vc_ref[...].astype(BF16)) + _dot(p_n.astype(BF16), vn_ref[...])
    o_ref[...] = _diff_finish(acc / l, Ts, lam, g_ref[...], lam_init)


def _attn_decode(lam4, g, bias, q, cache_k, cache_v, kn, vn, layer, Bs, Ts, lam_init):
    past = cache_k.shape[2]
    assert past % CHUNK == 0 and Ts <= CHUNK
    padk = bias.shape[2]
    new = pl.BlockSpec((Ts, A_V_DIM), lambda b, h: (b, h))
    cache = pl.BlockSpec(memory_space=pl.ANY)
    return pl.pallas_call(
        functools.partial(_attn_decode_kernel, Ts=Ts, past=past, layer=layer, lam_init=lam_init),
        grid=(Bs, A_HEADS),
        in_specs=[pl.BlockSpec((4, HEAD_DIM), lambda b, h: (0, 0)),
                  pl.BlockSpec((1, A_V_DIM), lambda b, h: (0, 0)),
                  pl.BlockSpec((None, Ts, padk), lambda b, h: (h, 0, 0)),
                  new, new, new, cache, cache],
        out_specs=new,
        out_shape=jax.ShapeDtypeStruct((Bs * Ts, A_WIDTH), BF16),
        scratch_shapes=[pltpu.VMEM((2, past, A_V_DIM), F32), pltpu.VMEM((2, past, A_V_DIM), F32),
                        pltpu.SemaphoreType.DMA((2, 2))],
        compiler_params=_params(("arbitrary", "arbitrary"), 40),
        name="attn_decode",
    )(lam4, g, bias, q, kn, vn, cache_k, cache_v)


def _ret_kernel(q_ref, k_ref, v_ref, g_ref, s0_ref, dm_ref, qd_ref, kd_ref, cd_ref,
                o_ref, sn_ref, *, T, C):
    z = jnp.zeros((R_KEY_DIM, R_V_DIM), F32)
    state = jnp.concatenate([jnp.concatenate([s0_ref[0], z], axis=1),
                             jnp.concatenate([z, s0_ref[1]], axis=1)], axis=0)
    r = lax.broadcasted_iota(jnp.int32, (LANES, LANES), 0)
    c = lax.broadcasted_iota(jnp.int32, (LANES, LANES), 1)
    same_head = (r >> _LOG2_CHUNK) == (c >> _LOG2_CHUNK)
    ones_bd = jnp.where(same_head, 1.0, 0.0).astype(BF16)
    lo = lax.broadcasted_iota(jnp.int32, (C, LANES), 1) < R_V_DIM
    cd = cd_ref[...]

    def chunk(n, state):
        r0 = pl.multiple_of(n * C, C)
        q = q_ref[pl.ds(r0, C), :]
        k = k_ref[pl.ds(r0, C), :]
        v = v_ref[pl.ds(r0, C), :]
        p = (_dot_nt(_stack_maps(q), k) * dm_ref[...]).astype(BF16)
        o2 = _dot(p, v)
        inner = jnp.where(lo, o2[:C], o2[C:])
        qd = (q.astype(F32) * qd_ref[...]).astype(BF16)
        o = inner + _dot(qd, state.astype(BF16))
        vk = (v.astype(F32) * kd_ref[...]).astype(BF16)
        new_state = cd * state + jnp.where(same_head, _dot_tn(k, vk), 0.0)
        oo = o * o
        hi = oo.astype(BF16)
        lo_part = (oo - hi.astype(F32)).astype(BF16)
        ss = _dot(hi, ones_bd) + _dot(lo_part, ones_bd)
        out = o * lax.rsqrt(ss * (1.0 / R_V_DIM) + LN_EPS) * g_ref[pl.ds(r0, C), :]
        o_ref[pl.ds(r0, C), :] = out.astype(BF16)
        return new_state

    state = lax.fori_loop(0, T // C, chunk, state)
    sn_ref[0] = state[:R_KEY_DIM, :R_V_DIM]
    sn_ref[1] = state[R_KEY_DIM:, R_V_DIM:]


def _retention(rq, rk, rv, rg, state0, tabs, B, T, C):
    dm, qd, kd, cd = tabs
    blk = pl.BlockSpec((T, LANES), lambda b, hp: (b, hp))
    st = pl.BlockSpec((None, 2, R_KEY_DIM, R_V_DIM), lambda b, hp: (b, hp, 0, 0))
    return pl.pallas_call(
        functools.partial(_ret_kernel, T=T, C=C),
        grid=(B, 2),
        in_specs=[blk, blk, blk, blk, st,
                  pl.BlockSpec((None, 2 * C, C), lambda b, hp: (hp, 0, 0)),
                  pl.BlockSpec((None, C, LANES), lambda b, hp: (hp, 0, 0)),
                  pl.BlockSpec((None, C, LANES), lambda b, hp: (hp, 0, 0)),
                  pl.BlockSpec((None, 1, LANES), lambda b, hp: (hp, 0, 0))],
        out_specs=[blk, st],
        out_shape=[jax.ShapeDtypeStruct((B * T, R_WIDTH), BF16),
                   jax.ShapeDtypeStruct((B, R_HEADS, R_KEY_DIM, R_V_DIM), F32)],
        compiler_params=_params(("arbitrary", "arbitrary"), 32),
        name="retention",
    )(rq, rk, rv, rg, state0, dm, qd, kd, cd)


def _retention_tables(C):
    h = jnp.arange(R_HEADS, dtype=F32)
    log_g = jnp.log1p(-jnp.exp2(-5.0 - h))
    i = jnp.arange(C, dtype=F32)
    diff = i[:, None] - i[None, :]
    dmask = jnp.where(diff >= 0, jnp.exp(jnp.maximum(diff, 0.0)[None] * log_g[:, None, None]), 0.0)
    dm = dmask.reshape(2, 2 * C, C)
    q_dec = jnp.exp((i + 1.0)[None, :] * log_g[:, None])
    k_dec = jnp.exp((C - 1 - i)[None, :] * log_g[:, None])
    c_dec = jnp.exp(C * log_g)

    def lanes(t):
        t = jnp.repeat(t[:, :, None], R_V_DIM, axis=2).reshape(2, 2, C, R_V_DIM)
        return jnp.concatenate([t[:, 0], t[:, 1]], axis=-1)

    cd = jnp.repeat(c_dec[:, None], R_V_DIM, axis=1).reshape(2, 1, LANES)
    return dm, lanes(q_dec), lanes(k_dec), cd


_CONV_PAD = 32


def _conv_kernel(u_ref, h_ref, w_ref, b_ref, g_ref, be_ref, o_ref, t_ref, up_scr, *, T, RT):
    hist = CONV_WIDTH - 1
    off = _CONV_PAD - hist
    up_scr[off:_CONV_PAD, :] = h_ref[...]
    up_scr[_CONV_PAD:_CONV_PAD + T, :] = u_ref[...]
    bias = b_ref[...]
    g = g_ref[...]
    be = be_ref[...]
    for t0 in range(0, T, RT):
        acc = jnp.zeros((RT, C_WIDTH), F32)
        for j in range(CONV_WIDTH):
            acc = acc + up_scr[t0 + j + off:t0 + j + off + RT, :] * w_ref[j:j + 1, :]
        y = _layer_norm(acc + bias, g, be)
        o_ref[t0:t0 + RT, :] = (y * _sigmoid(y)).astype(BF16)
    t_ref[...] = up_scr[T + off:T + _CONV_PAD, :]


def _conv(u, hist, dw_w, dw_b, ln_g, ln_b, B, T):
    RT = min(T, LANES)
    assert T % RT == 0
    hl = CONV_WIDTH - 1
    vec = pl.BlockSpec((1, C_WIDTH), lambda b: (0, 0))
    hb = pl.BlockSpec((None, hl, C_WIDTH), lambda b: (b, 0, 0))
    return pl.pallas_call(
        functools.partial(_conv_kernel, T=T, RT=RT),
        grid=(B,),
        in_specs=[pl.BlockSpec((T, C_WIDTH), lambda b: (b, 0)), hb,
                  pl.BlockSpec((CONV_WIDTH, C_WIDTH), lambda b: (0, 0)), vec, vec, vec],
        out_specs=[pl.BlockSpec((T, C_WIDTH), lambda b: (b, 0)), hb],
        out_shape=[jax.ShapeDtypeStruct((B * T, C_WIDTH), BF16),
                   jax.ShapeDtypeStruct((B, hl, C_WIDTH), F32)],
        scratch_shapes=[pltpu.VMEM((T + _CONV_PAD, C_WIDTH), F32)],
        compiler_params=_params(("arbitrary",), 32),
        name="conv",
    )(u, hist, dw_w, dw_b.reshape(1, C_WIDTH), ln_g.reshape(1, C_WIDTH), ln_b.reshape(1, C_WIDTH))


def _first_index(vals, target):
    idx = jnp.full(target.shape, len(vals) - 1, jnp.int32)
    for j in range(len(vals) - 2, -1, -1):
        idx = jnp.where(vals[j] == target, j, idx)
    return idx


def _select(idx, vals):
    out = vals[-1]
    for j in range(len(vals) - 2, -1, -1):
        out = jnp.where(idx == j, vals[j], out)
    return out


def _route(logits_t):
    rows = [logits_t[e:e + 1, :] for e in range(N_EXPERTS)]
    m = functools.reduce(jnp.maximum, rows)
    ex = [jnp.exp(r - m) for r in rows]
    z = functools.reduce(jnp.add, ex)
    sc = [e / z for e in ex]
    v1s, v2s, i1s, i2s, gss = [], [], [], [], []
    for g in range(N_GROUPS):
        a = sc[g * EXPERTS_PER_GROUP:(g + 1) * EXPERTS_PER_GROUP]
        v1 = functools.reduce(jnp.maximum, a)
        i1 = _first_index(a, v1)
        rest = [jnp.where(i1 == j, -1.0, a[j]) for j in range(EXPERTS_PER_GROUP)]
        v2 = functools.reduce(jnp.maximum, rest)
        i2 = _first_index(rest, v2)
        v1s.append(v1); v2s.append(v2); i1s.append(i1); i2s.append(i2); gss.append(v1 + v2)
    grp = _first_index(gss, functools.reduce(jnp.maximum, gss))
    v1 = _select(grp, v1s)
    v2 = _select(grp, v2s)
    e1 = _select(grp, i1s) + grp * EXPERTS_PER_GROUP
    e2 = _select(grp, i2s) + grp * EXPERTS_PER_GROUP
    den = v1 + v2
    return e1, e2, v1 / den, v2 / den


def _expert_onehot(e1, e2):
    rows = [jnp.where(e1 == e, 1.0, 0.0) + jnp.where(e2 == e, 1.0, 0.0) for e in range(N_EXPERTS)]
    return jnp.concatenate(rows, axis=0)


def _outproj_kernel(a_ref, r_ref, c_ref, x_ref, w_ref, g_ref, b_ref, wr_ref, br_ref,
                    x1_ref, x1b_ref, ri_ref, rw_ref, cnt_ref, *, alpha, tb):
    mix = (_dot(a_ref[...], w_ref[0:A_WIDTH, :])
           + _dot(r_ref[...], w_ref[A_WIDTH:A_WIDTH + R_WIDTH, :])
           + _dot(c_ref[...], w_ref[A_WIDTH + R_WIDTH:, :]))
    x1 = _layer_norm(alpha * x_ref[...] + mix, g_ref[...], b_ref[...])
    x1_ref[...] = x1
    x1b_ref[...] = x1.astype(BF16)
    logits_t = lax.dot_general(wr_ref[...], x1, (((1,), (1,)), ((), ())),
                               precision=lax.Precision.HIGHEST,
                               preferred_element_type=F32) + br_ref[...]
    e1, e2, w1, w2 = _route(logits_t)
    tm = e1.shape[1]
    ri_ref[...] = jnp.concatenate([e1, e2, jnp.zeros((SUBLANES - 2, tm), jnp.int32)], axis=0)
    rw_ref[...] = jnp.concatenate([w1, w2, jnp.zeros((SUBLANES - 2, tm), F32)], axis=0)
    onehot = _expert_onehot(e1, e2).astype(BF16)
    ones = jnp.ones((SUBLANES, tb), BF16)
    for k in range(tm // tb):
        c = _dot_nt(ones, onehot[:, k * tb:(k + 1) * tb])
        c = jnp.concatenate([c, jnp.zeros((SUBLANES, LANES - N_EXPERTS), F32)], axis=1)
        cnt_ref[k * SUBLANES:(k + 1) * SUBLANES, :] = c.astype(jnp.int32)


def _outproj(attn, ret, conv, x, w_bf, ln_g, ln_b, wr_t, br, alpha, tm, tb):
    n = x.shape[0]
    row = lambda i: (i, 0)
    col = lambda i: (0, i)
    const = lambda i: (0, 0)
    nsub = tm // tb
    return pl.pallas_call(
        functools.partial(_outproj_kernel, alpha=alpha, tb=tb),
        grid=(n // tm,),
        in_specs=[pl.BlockSpec((tm, A_WIDTH), row), pl.BlockSpec((tm, R_WIDTH), row),
                  pl.BlockSpec((tm, C_WIDTH), row), pl.BlockSpec((tm, D_MODEL), row),
                  pl.BlockSpec((D_MODEL, D_MODEL), const),
                  pl.BlockSpec((1, D_MODEL), const), pl.BlockSpec((1, D_MODEL), const),
                  pl.BlockSpec((N_EXPERTS, D_MODEL), const), pl.BlockSpec((N_EXPERTS, 1), const)],
        out_specs=[pl.BlockSpec((tm, D_MODEL), row), pl.BlockSpec((tm, D_MODEL), row),
                   pl.BlockSpec((SUBLANES, tm), col), pl.BlockSpec((SUBLANES, tm), col),
                   pl.BlockSpec((nsub * SUBLANES, LANES), row)],
        out_shape=[jax.ShapeDtypeStruct((n, D_MODEL), F32), jax.ShapeDtypeStruct((n, D_MODEL), BF16),
                   jax.ShapeDtypeStruct((SUBLANES, n), jnp.int32),
                   jax.ShapeDtypeStruct((SUBLANES, n), F32),
                   jax.ShapeDtypeStruct((n // tb * SUBLANES, LANES), jnp.int32)],
        compiler_params=_params(("arbitrary",), 40),
        name="outproj",
    )(attn, ret, conv, x, w_bf, ln_g.reshape(1, D_MODEL), ln_b.reshape(1, D_MODEL), wr_t, br)


_UNIT = 16
_XS_WIDTH = D_MODEL + LANES
_FFN_ROWS = 512
_DISPATCH_TOKENS = 512


def _sorted_positions(e1, e2, lo_ref, base, tb):
    onehot = _expert_onehot(e1, e2).astype(BF16)
    r = lax.broadcasted_iota(jnp.int32, (tb, tb), 0)
    c = lax.broadcasted_iota(jnp.int32, (tb, tb), 1)
    earlier = jnp.where(r < c, 1.0, 0.0).astype(BF16)
    rank = _dot(onehot, earlier).astype(jnp.int32)
    pos1 = jnp.zeros(e1.shape, jnp.int32)
    pos2 = jnp.zeros(e1.shape, jnp.int32)
    for e in range(N_EXPERTS):
        p = rank[e:e + 1, :] + lo_ref[base + e]
        pos1 = jnp.where(e1 == e, p, pos1)
        pos2 = jnp.where(e2 == e, p, pos2)
    return pos1, pos2


def _permutation(pos1, pos2, rows):
    j = lax.broadcasted_iota(jnp.int32, (rows, pos1.shape[1]), 0)
    return jnp.where(j == pos1, 1.0, jnp.where(j == pos2, 1.0, 0.0)).astype(BF16)


def _dispatch_kernel(lo_ref, urow_ref, ri_ref, rw_ref, x_ref, xs_hbm, pos_ref,
                     stage, sems, *, nblk, tb, cap, spare_row):
    i = pl.program_id(0)
    slot = i % 2
    units = cap // _UNIT

    def wait_slot(s):
        pltpu.make_async_copy(stage.at[s], xs_hbm.at[pl.ds(0, cap), :], sems.at[s]).wait()

    @pl.when(i >= 2)
    def _():
        wait_slot(slot)

    e1 = ri_ref[0:1, :]
    e2 = ri_ref[1:2, :]
    w1 = rw_ref[0:1, :]
    w2 = rw_ref[1:2, :]
    pos1, pos2 = _sorted_positions(e1, e2, lo_ref, i * N_EXPERTS, tb)
    pos_ref[...] = jnp.concatenate([pos1, pos2, jnp.zeros((SUBLANES - 2, tb), jnp.int32)], axis=0)
    perm = _permutation(pos1, pos2, cap)
    stage[slot, :, 0:D_MODEL] = _dot(perm, x_ref[...]).astype(BF16)
    j = lax.broadcasted_iota(jnp.int32, (cap, tb), 0)
    gate = jnp.sum(jnp.where(j == pos1, w1, jnp.where(j == pos2, w2, 0.0)), axis=1, keepdims=True)
    g0 = gate.astype(BF16).astype(F32)
    g1 = (gate - g0).astype(BF16).astype(F32)
    g2 = gate - g0 - g1
    lane = lax.broadcasted_iota(jnp.int32, (cap, LANES), 1)
    pieces = jnp.where(lane == 0, g0, jnp.where(lane == 1, g1, jnp.where(lane == 2, g2, 0.0)))
    stage[slot, :, D_MODEL:_XS_WIDTH] = pieces.astype(BF16)

    def body(u, c):
        t = urow_ref[i * units + u]
        s = pl.multiple_of(u * _UNIT, _UNIT)
        d = pl.multiple_of(jnp.where(t >= 0, t, spare_row + slot * cap + s), _UNIT)
        pltpu.make_async_copy(stage.at[slot, pl.ds(s, _UNIT), :], xs_hbm.at[pl.ds(d, _UNIT), :],
                              sems.at[slot]).start()
        return c

    lax.fori_loop(0, units, body, 0)

    @pl.when(i == nblk - 1)
    def _():
        wait_slot(slot)
        if nblk > 1:
            wait_slot(1 - slot)


def _ffn_kernel(te_ref, tv_ref, nt_ref, xs_ref, wg_ref, wu_ref, wd_ref, ys_ref, wg_b, wu_b, wd_b):
    j = pl.program_id(0)
    jm = jnp.maximum(j, 1)

    @pl.when((j == 0) | (te_ref[j] != te_ref[jm - 1]))
    def _():
        wg_b[...] = wg_ref[...].astype(BF16)
        wu_b[...] = wu_ref[...].astype(BF16)
        wd_b[...] = wd_ref[...].astype(BF16)

    @pl.when(j < nt_ref[0])
    def _():
        rows = xs_ref.shape[0]
        valid = lax.broadcasted_iota(jnp.int32, (rows, LANES), 0) < tv_ref[j]
        zero = jnp.zeros((rows, LANES), BF16)
        x = jnp.concatenate([jnp.where(valid, xs_ref[:, c * LANES:(c + 1) * LANES], zero)
                             for c in range(D_MODEL // LANES)], axis=1)
        gp = jnp.where(valid, xs_ref[:, D_MODEL:_XS_WIDTH], zero).astype(F32)
        g = gp[:, 0:1] + gp[:, 1:2] + gp[:, 2:3]
        hg = _dot(x, wg_b[...])
        hu = _dot(x, wu_b[...])
        h = hg * _sigmoid(hg) * hu * g
        ys_ref[...] = _dot(h.astype(BF16), wd_b[...]).astype(BF16)


def _combine_kernel(urow_ref, pos_ref, x1_ref, p_ref, g_ref, b_ref, wpg_ref, bpg_ref, wpp_ref,
                    ys_hbm, out_ref, stage, sems, *, nblk, tb, cap, alpha):
    i = pl.program_id(0)
    slot = i % 2
    units = cap // _UNIT

    def fetch(blk, s):
        def body(u, c):
            sr = pl.multiple_of(jnp.maximum(urow_ref[blk * units + u], 0), _UNIT)
            ds = pl.multiple_of(u * _UNIT, _UNIT)
            pltpu.make_async_copy(ys_hbm.at[pl.ds(sr, _UNIT), :], stage.at[s, pl.ds(ds, _UNIT), :],
                                  sems.at[s]).start()
            return c

        lax.fori_loop(0, units, body, 0)

    @pl.when(i == 0)
    def _():
        fetch(0, 0)

    @pl.when(i + 1 < nblk)
    def _():
        fetch(i + 1, 1 - slot)

    pltpu.make_async_copy(ys_hbm.at[pl.ds(0, cap), :], stage.at[slot], sems.at[slot]).wait()
    perm = _permutation(pos_ref[0:1, :], pos_ref[1:2, :], cap)
    y = _dot_tn(perm, stage[slot])
    x2 = _layer_norm(alpha * x1_ref[...] + y, g_ref[...], b_ref[...])
    gate = _sigmoid(_dot(x2.astype(BF16), wpg_ref[...]) + bpg_ref[...])
    out_ref[...] = x2 + gate * _dot(p_ref[...].astype(BF16), wpp_ref[...])


def _moe_plan(cnt, nblk, tb):
    cnt = cnt.reshape(nblk, SUBLANES, LANES)[:, 0, :N_EXPERTS]
    pc = (cnt + (_UNIT - 1)) // _UNIT * _UNIT
    lo = jnp.cumsum(pc, axis=1) - pc
    tot = jnp.sum(pc, axis=0)
    reg = (tot + (_FFN_ROWS - 1)) // _FFN_ROWS * _FFN_ROWS
    rstart = jnp.cumsum(reg) - reg
    gs = rstart[None, :] + jnp.cumsum(pc, axis=0) - pc
    tiles_e = reg // _FFN_ROWS
    tile_end = jnp.cumsum(tiles_e)
    max_rows = 2 * nblk * tb + nblk * N_EXPERTS * _UNIT + N_EXPERTS * _FFN_ROWS
    max_tiles = -(-max_rows // _FFN_ROWS)
    tj =jnp.arange(max_tiles, dtype=jnp.int32)
    te = jnp.minimum(jnp.sum((tj[:, None] >= tile_end[None, :]).astype(jnp.int32), axis=1), N_EXPERTS - 1)
    tv = jnp.clip(tot[te] - (tj - (tile_end - tiles_e)[te]) * _FFN_ROWS, 0, _FFN_ROWS)
    nt = tile_end[-1:].astype(jnp.int32)
    cap = 2 * tb + N_EXPERTS * _UNIT
    urow0 = jnp.arange(cap // _UNIT, dtype=jnp.int32) * _UNIT
    run = jnp.sum((urow0[None, :, None] >= (lo + pc)[:, None, :]).astype(jnp.int32), axis=2)
    run = jnp.minimum(run, N_EXPERTS - 1)
    urow = jnp.take_along_axis(gs - lo, run, axis=1) + urow0[None, :]
    urow = jnp.where(urow0[None, :] < jnp.sum(pc, axis=1, keepdims=True), urow, -1)
    flat = lambda a: a.reshape(-1).astype(jnp.int32)
    return flat(lo), flat(urow), te.astype(jnp.int32), tv.astype(jnp.int32), nt, max_tiles, cap


def _moe(x1b, x1, ri, rw, cnt, p, wg, wu, wd, layer, ln_g, ln_b, wpg, bpg, wpp, alpha, tb):
    n = x1.shape[0]
    nblk = n // tb
    lo, urow, te, tv, nt, max_tiles, cap = _moe_plan(cnt, nblk, tb)
    max_rows = max_tiles * _FFN_ROWS

    xs, pos = pl.pallas_call(
        functools.partial(_dispatch_kernel, nblk=nblk, tb=tb, cap=cap, spare_row=max_rows),
        grid_spec=pltpu.PrefetchScalarGridSpec(
            num_scalar_prefetch=2,
            grid=(nblk,),
            in_specs=[pl.BlockSpec((SUBLANES, tb), lambda i, *_: (0, i)),
                      pl.BlockSpec((SUBLANES, tb), lambda i, *_: (0, i)),
                      pl.BlockSpec((tb, D_MODEL), lambda i, *_: (i, 0))],
            out_specs=[pl.BlockSpec(memory_space=pl.ANY),
                       pl.BlockSpec((SUBLANES, tb), lambda i, *_: (0, i))],
            scratch_shapes=[pltpu.VMEM((2, cap, _XS_WIDTH), BF16),
                            pltpu.SemaphoreType.DMA((2,))]),
        out_shape=[jax.ShapeDtypeStruct((max_rows + 2 * cap, _XS_WIDTH), BF16),
                   jax.ShapeDtypeStruct((SUBLANES, n), jnp.int32)],
        compiler_params=_params(("arbitrary",), 32),
        name="moe_dispatch",
    )(lo, urow, ri, rw, x1b)

    def tile(j, te_ref, tv_ref, nt_ref):
        return jnp.minimum(j, nt_ref[0] - 1)

    ys = pl.pallas_call(
        _ffn_kernel,
        grid_spec=pltpu.PrefetchScalarGridSpec(
            num_scalar_prefetch=3,
            grid=(max_tiles,),
            in_specs=[pl.BlockSpec((_FFN_ROWS, _XS_WIDTH), lambda j, *s: (tile(j, *s), 0)),
                      pl.BlockSpec((None, None, D_MODEL, D_EXPERT),
                                   lambda j, *s: (layer, s[0][tile(j, *s)], 0, 0)),
                      pl.BlockSpec((None, None, D_MODEL, D_EXPERT),
                                   lambda j, *s: (layer, s[0][tile(j, *s)], 0, 0)),
                      pl.BlockSpec((None, None, D_EXPERT, D_MODEL),
                                   lambda j, *s: (layer, s[0][tile(j, *s)], 0, 0))],
            out_specs=pl.BlockSpec((_FFN_ROWS, D_MODEL), lambda j, *s: (tile(j, *s), 0)),
            scratch_shapes=[pltpu.VMEM((D_MODEL, D_EXPERT), BF16), pltpu.VMEM((D_MODEL, D_EXPERT), BF16),
                            pltpu.VMEM((D_EXPERT, D_MODEL), BF16)]),
        out_shape=jax.ShapeDtypeStruct((max_rows, D_MODEL), BF16),
        compiler_params=_params(("arbitrary",), 40),
        name="moe_ffn",
    )(te, tv, nt, xs, wg, wu, wd)

    const = lambda i, *_: (0, 0)
    return pl.pallas_call(
        functools.partial(_combine_kernel, nblk=nblk, tb=tb, cap=cap, alpha=alpha),
        grid_spec=pltpu.PrefetchScalarGridSpec(
            num_scalar_prefetch=1,
            grid=(nblk,),
            in_specs=[pl.BlockSpec((SUBLANES, tb), lambda i, *_: (0, i)),
                      pl.BlockSpec((tb, D_MODEL), lambda i, *_: (i, 0)),
                      pl.BlockSpec((tb, PLE_DIM), lambda i, *_: (i, 0)),
                      pl.BlockSpec((1, D_MODEL), const), pl.BlockSpec((1, D_MODEL), const),
                      pl.BlockSpec((D_MODEL, D_MODEL), const), pl.BlockSpec((1, D_MODEL), const),
                      pl.BlockSpec((PLE_DIM, D_MODEL), const),
                      pl.BlockSpec(memory_space=pl.ANY)],
            out_specs=pl.BlockSpec((tb, D_MODEL), lambda i, *_: (i, 0)),
            scratch_shapes=[pltpu.VMEM((2, cap, D_MODEL), BF16),
                            pltpu.SemaphoreType.DMA((2,))]),
        out_shape=jax.ShapeDtypeStruct((n, D_MODEL), F32),
        compiler_params=_params(("arbitrary",), 40),
        name="moe_combine",
    )(urow, pos, x1, p, ln_g.reshape(1, D_MODEL), ln_b.reshape(1, D_MODEL),
      wpg, bpg.reshape(1, D_MODEL), wpp, ys)


def _rope_tables(pos, rows):
    half = R_KEY_DIM // 2
    inv_freq = 1.0 / (ROPE_BASE ** jnp.linspace(0.0, 1.0, half, dtype=jnp.float32))
    ang = pos.astype(jnp.float32)[:, None] * inv_freq[None, :]
    cos = jnp.cos(ang)
    sin = jnp.sin(ang)
    cos_t = jnp.tile(cos, (rows // pos.shape[0], 4))
    sin_t = jnp.tile(jnp.concatenate([-sin, sin], axis=1), (rows // pos.shape[0], 2))
    return cos_t, sin_t


def kernel(x_prompt, x_sample, p_prompt, p_sample, cache_k, cache_v, state_ret, state_conv, ln_emb_g, ln_emb_b, rel_bias, w_router, b_router, w_in, lam_q1, lam_k1, lam_q2, lam_k2, subln_g, dw_w, dw_b, conv_ln_g, conv_ln_b, w_out, ln1_g, ln1_b, w_exp_gate, w_exp_up, w_exp_down, ln2_g, ln2_b, w_ple_gate, b_ple_gate, w_ple_proj):
    B, T, D = x_prompt.shape
    Bs, Ts, _ = x_sample.shape
    depth = w_in.shape[0]
    past = cache_k.shape[2]
    n_p, n_s = B * T, Bs * Ts
    alpha = (2 * depth) ** 0.25

    tm_p = 512 if n_p % 512 == 0 else n_p
    tm_s = n_s
    TB = 2 * LANES
    c_p = min(T, 2 * LANES)

    pos_p = jnp.arange(T, dtype=jnp.int32)
    pos_s = past + jnp.arange(Ts, dtype=jnp.int32)
    rope_p = _rope_tables(pos_p, max(T, tm_p))
    rope_s = _rope_tables(pos_s, max(Ts, tm_s))
    rel_p = (jnp.arange(2 * TB, dtype=jnp.int32)[None, :] - TB) - jnp.arange(TB, dtype=jnp.int32)[:, None]
    padk = -(-(past + Ts) // LANES) * LANES
    rel_s = jnp.arange(padk, dtype=jnp.int32)[None, :] - pos_s[:, None]
    bias_p = _bias_table(rel_bias, _t5_bucket(rel_p))
    bias_s = _bias_table(rel_bias, _t5_bucket(rel_s))
    tabs_p = _retention_tables(c_p)
    tabs_s = _retention_tables(Ts)

    wr_t = w_router.T
    br = b_router.reshape(N_EXPERTS, 1)
    ret0_p = jnp.zeros((B, R_HEADS, R_KEY_DIM, R_V_DIM), F32)
    conv0_p = jnp.zeros((B, CONV_WIDTH - 1, C_WIDTH), F32)

    xp = x_prompt.reshape(n_p, D)
    xs = x_sample.reshape(n_s, D)
    outs = {k: [] for k in ("rp", "cp", "rs", "cs")}
    kv_p = kv_s = None
    for l in range(depth):
        lam_init = 0.8 - 0.6 * math.exp(-0.3 * l)
        lam4 = jnp.stack([lam_q1[l], lam_k1[l], lam_q2[l], lam_k2[l]])
        g_sub = subln_g[l].reshape(1, A_V_DIM)
        w_in_b = w_in[l].astype(BF16)
        w_out_b = w_out[l].astype(BF16)
        wpg, wpp = w_ple_gate[l].astype(BF16), w_ple_proj[l].astype(BF16)
        ln = (ln_emb_g, ln_emb_b) if l == 0 else None

        def channel(x, attn, ret, conv, p_l, tm):
            tb = min(tm, _DISPATCH_TOKENS)
            x1, x1b, ri, rw, cnt = _outproj(attn, ret, conv, x, w_out_b, ln1_g[l], ln1_b[l], wr_t, br,
                                            alpha, tm, tb)
            return _moe(x1b, x1, ri, rw, cnt, p_l, w_exp_gate, w_exp_up, w_exp_down, l, ln2_g[l], ln2_b[l],
                        wpg, b_ple_gate[l], wpp, alpha, tb)

        xp, q, kb, vb, rq, rk, rv, rg, u, k5, v5 = _inproj(xp, w_in_b, rope_p[0], rope_p[1], ln, tm_p,
                                                           kv_p, l, depth, B, T)
        kv_p = (k5, v5)
        attn = _attn_prompt(lam4, g_sub, bias_p, q, kb, vb, B, T, lam_init)
        ret, rstate = _retention(rq, rk, rv, rg, ret0_p, tabs_p, B, T, c_p)
        conv, ctail = _conv(u, conv0_p, dw_w[l], dw_b[l], conv_ln_g[l], conv_ln_b[l], B, T)
        xp = channel(xp, attn, ret, conv, p_prompt[l].reshape(n_p, PLE_DIM), tm_p)
        outs["rp"].append(rstate)
        outs["cp"].append(ctail)

        xs, q, kb, vb, rq, rk, rv, rg, u, k5, v5 = _inproj(xs, w_in_b, rope_s[0], rope_s[1], ln, tm_s,
                                                           kv_s, l, depth, Bs, Ts)
        kv_s = (k5, v5)
        attn = _attn_decode(lam4, g_sub, bias_s, q, cache_k, cache_v, kb, vb, l, Bs, Ts, lam_init)
        ret, rstate = _retention(rq, rk, rv, rg, state_ret[l], tabs_s, Bs, Ts, Ts)
        conv, ctail = _conv(u, state_conv[l], dw_w[l], dw_b[l], conv_ln_g[l], conv_ln_b[l], Bs, Ts)
        xs = channel(xs, attn, ret, conv, p_sample[l].reshape(n_s, PLE_DIM), tm_s)
        outs["rs"].append(rstate)
        outs["cs"].append(ctail)

    return (xp.reshape(B, T, D), xs.reshape(Bs, Ts, D),
            kv_p[0], kv_p[1], jnp.stack(outs["rp"]), jnp.stack(outs["cp"]),
            kv_s[0], kv_s[1], jnp.stack(outs["rs"]), jnp.stack(outs["cs"]))
```

```python
import functools
import math

import jax
import jax.numpy as jnp
from jax import lax
from jax.experimental import pallas as pl
from jax.experimental.pallas import tpu as pltpu

F32 = jnp.float32
BF16 = jnp.bfloat16

D_MODEL = 1024
CHUNK = 64
HEAD_DIM = 64
A_HEADS = 4
A_V_DIM = 128
A_WIDTH = 512
R_HEADS = 4
R_KEY_DIM = 64
R_V_DIM = 64
R_WIDTH = 256
C_WIDTH = 256
CONV_WIDTH = 31
IN_WIDTH = 3072
N_BUCKETS = 32
MAX_DISTANCE = 128
ROPE_BASE = 10000.0
N_EXPERTS = 16
N_GROUPS = 4
EXPERTS_PER_GROUP = 4
D_EXPERT = 512
PLE_DIM = 256
LN_EPS = 1e-5
NEG_INF = -1e30
LOG2E = 1.4426950408889634

LANES = 128
SUBLANES = 8
_LOG2_CHUNK = 6
assert CHUNK == HEAD_DIM == R_KEY_DIM == R_V_DIM == 1 << _LOG2_CHUNK
MIB = 1024 * 1024

_OFF_Q, _OFF_K, _OFF_V = 0, 512, 1024
_OFF_RQ, _OFF_RK, _OFF_RV, _OFF_RG, _OFF_C = 1536, 1792, 2048, 2304, 2560


def _params(sem, vmem_mib):
    return pltpu.CompilerParams(dimension_semantics=sem, vmem_limit_bytes=vmem_mib * MIB)


def _layer_norm(x, g, b):
    mu = jnp.mean(x, axis=-1, keepdims=True)
    xc = x - mu
    var = jnp.mean(xc * xc, axis=-1, keepdims=True)
    return xc * lax.rsqrt(var + LN_EPS) * g + b


def _sigmoid(x):
    return 1.0 / (1.0 + jnp.exp(-x))


def _dot(a, b):
    return jnp.dot(a, b, preferred_element_type=F32)


def _dot_nt(a, b):
    return lax.dot_general(a, b, (((1,), (1,)), ((), ())), preferred_element_type=F32)


def _dot_tn(a, b):
    return lax.dot_general(a, b, (((0,), (0,)), ((), ())), preferred_element_type=F32)


def _bias_kernel(relb_ref, idx_ref, out_ref):
    idx = idx_ref[...]
    for h in range(A_HEADS):
        acc = jnp.zeros(idx.shape, F32)
        for b in range(N_BUCKETS):
            acc = jnp.where(idx == b, relb_ref[b, h], acc)
        out_ref[h] = acc


def _bias_table(rel_bias, idx):
    r, c = idx.shape
    return pl.pallas_call(
        _bias_kernel,
        out_shape=jax.ShapeDtypeStruct((A_HEADS, r, c), F32),
        in_specs=[pl.BlockSpec(memory_space=pltpu.SMEM),
                  pl.BlockSpec(memory_space=pltpu.VMEM)],
        out_specs=pl.BlockSpec(memory_space=pltpu.VMEM),
        name="bias_table",
    )(rel_bias, idx)


def _t5_bucket(rel):
    nb = N_BUCKETS // 2
    max_exact = nb // 2
    n = jnp.abs(rel)
    nf = jnp.maximum(n, 1).astype(jnp.float32)
    large = max_exact + (jnp.log(nf / max_exact) / math.log(MAX_DISTANCE / max_exact)
                         * (nb - max_exact)).astype(jnp.int32)
    large = jnp.minimum(large, nb - 1)
    return jnp.where(rel > 0, nb, 0) + jnp.where(n < max_exact, n, large)


def _rotary128(x, cos, sin_signed, lo32):
    partner = jnp.where(lo32, pltpu.roll(x, 96, 1), pltpu.roll(x, 32, 1))
    return x * cos + partner * sin_signed


def _inproj_kernel(*refs, apply_ln, has_prev, layer, tm, T, nsteps):
    refs = list(refs)
    x_ref = refs.pop(0)
    if apply_ln:
        g_ref, b_ref = refs.pop(0), refs.pop(0)
    w_ref, cos_ref, sin_ref = refs.pop(0), refs.pop(0), refs.pop(0)
    if has_prev:
        refs.pop(0), refs.pop(0)
    if apply_ln:
        xn_ref = refs.pop(0)
    (q_ref, kb_ref, vb_ref, rq_ref, rk_ref, rv_ref, rg_ref, u_ref, k5_hbm, v5_hbm,
     kbuf, vbuf, sems) = refs

    i = pl.program_id(0)
    slot = i % 2
    rows = min(tm, T)

    def kv_copies(s, step):
        cps = []
        for bb in range(tm // rows):
            r0 = step * tm + bb * rows
            b = r0 // T
            t0 = pl.multiple_of(r0 % T, SUBLANES)
            for buf, out in ((kbuf, k5_hbm), (vbuf, v5_hbm)):
                for h in range(A_HEADS):
                    cps.append(pltpu.make_async_copy(
                        buf.at[s, pl.ds(bb * rows, rows), pl.ds(h * A_V_DIM, A_V_DIM)],
                        out.at[layer, b, pl.ds(t0, rows), h, :], sems.at[s]))
        return cps

    @pl.when(i >= 2)
    def _():
        for cp in kv_copies(slot, i - 2):
            cp.wait()

    x = x_ref[...]
    if apply_ln:
        x = _layer_norm(x, g_ref[...], b_ref[...])
        xn_ref[...] = x
    xb = x.astype(BF16)

    def mm(c0, c1):
        return _dot(xb, w_ref[:, c0:c1])

    q_ref[...] = (mm(_OFF_Q, _OFF_K) * (LOG2E * HEAD_DIM ** -0.5)).astype(BF16)
    a = mm(_OFF_K, _OFF_V)
    kbuf[slot] = a
    kb_ref[...] = a.astype(BF16)
    a = mm(_OFF_V, _OFF_RQ)
    vbuf[slot] = a
    vb_ref[...] = a.astype(BF16)
    for cp in kv_copies(slot, i):
        cp.start()

    cos = cos_ref[...]
    sin = sin_ref[...]
    lane = lax.broadcasted_iota(jnp.int32, cos.shape, 1)
    lo32 = (lane & 63) < 32

    def rot(a, scale):
        parts = [_rotary128(a[:, c * LANES:(c + 1) * LANES], cos, sin, lo32) for c in range(2)]
        r = jnp.concatenate(parts, axis=1)
        if scale != 1.0:
            r = r * scale
        return r.astype(BF16)

    rq_ref[...] = rot(mm(_OFF_RQ, _OFF_RK), 1.0)
    rk_ref[...] = rot(mm(_OFF_RK, _OFF_RV), R_KEY_DIM ** -0.5)
    rv_ref[...] = mm(_OFF_RV, _OFF_RG).astype(BF16)
    a = mm(_OFF_RG, _OFF_C)
    rg_ref[...] = a * _sigmoid(a)
    a = mm(_OFF_C, IN_WIDTH)
    u_ref[...] = a[:, :C_WIDTH] * _sigmoid(a[:, C_WIDTH:])

    @pl.when(i == nsteps - 1)
    def _():
        for cp in kv_copies(slot, i):
            cp.wait()
        if nsteps > 1:
            for cp in kv_copies(1 - slot, i - 1):
                cp.wait()


def _inproj(x, w_bf, cos_t, sin_t, ln, tm, kv_prev, layer, depth, B, T):
    n = x.shape[0]
    assert (tm % T == 0 or T % tm == 0) and n % tm == 0
    nblk = cos_t.shape[0] // tm
    nsteps = n // tm
    row = lambda i: (i, 0)
    const = lambda i: (0, 0)
    tab = lambda i: (i % nblk, 0)
    in_specs = [pl.BlockSpec((tm, D_MODEL), row)]
    args = [x]
    if ln is not None:
        in_specs += [pl.BlockSpec((1, D_MODEL), const)] * 2
        args += [ln[0].reshape(1, D_MODEL), ln[1].reshape(1, D_MODEL)]
    in_specs += [pl.BlockSpec((D_MODEL, IN_WIDTH), const),
                 pl.BlockSpec((tm, LANES), tab), pl.BlockSpec((tm, LANES), tab)]
    args += [w_bf, cos_t, sin_t]
    aliases = {}
    if kv_prev is not None:
        k5_index = (1 if ln is not None else 0) + 8
        aliases = {len(args): k5_index, len(args) + 1: k5_index + 1}
        in_specs += [pl.BlockSpec(memory_space=pl.ANY)] * 2
        args += list(kv_prev)

    def o(width, dt):
        return jax.ShapeDtypeStruct((n, width), dt), pl.BlockSpec((tm, width), row)

    outs = []
    if ln is not None:
        outs.append(o(D_MODEL, F32))
    outs += [o(512, BF16), o(512, BF16), o(512, BF16),
             o(256, BF16), o(256, BF16), o(256, BF16), o(256, F32), o(256, F32)]
    kv5 = jax.ShapeDtypeStruct((depth, B, T, A_HEADS, A_V_DIM), F32)
    outs += [(kv5, pl.BlockSpec(memory_space=pl.ANY))] * 2
    res = pl.pallas_call(
        functools.partial(_inproj_kernel, apply_ln=ln is not None, has_prev=kv_prev is not None,
                          layer=layer, tm=tm, T=T, nsteps=nsteps),
        grid=(nsteps,),
        in_specs=in_specs,
        out_specs=[s for _, s in outs],
        out_shape=[s for s, _ in outs],
        scratch_shapes=[pltpu.VMEM((2, tm, A_WIDTH), F32), pltpu.VMEM((2, tm, A_WIDTH), F32),
                        pltpu.SemaphoreType.DMA((2,))],
        input_output_aliases=aliases,
        compiler_params=_params(("arbitrary",), 52),
        name="inproj",
    )(*args)
    if ln is None:
        res = [x] + list(res)
    return res


def _lambda(lam_ref, lam_init):
    lv = lam_ref[...]
    s1 = jnp.sum(lv[0:1] * lv[1:2], axis=-1, keepdims=True)
    s2 = jnp.sum(lv[2:3] * lv[3:4], axis=-1, keepdims=True)
    return jnp.exp(s1) - jnp.exp(s2) + lam_init


def _stack_maps(q):
    lane = lax.broadcasted_iota(jnp.int32, q.shape, 1)
    lo = lane < HEAD_DIM
    z = jnp.zeros_like(q)
    return jnp.concatenate([jnp.where(lo, q, z), jnp.where(lo, z, q)], axis=0)


def _diff_finish(o2, t, lam, g, lam_init):
    o = o2[:t] - lam * o2[t:]
    ms = jnp.mean(o * o, axis=-1, keepdims=True)
    return (o * lax.rsqrt(ms + LN_EPS) * g * (1.0 - lam_init)).astype(BF16)


def _attn_prompt_kernel(lam_ref, g_ref, bias_ref, q_ref, k_ref, v_ref, o_ref,
                        s_scr, m_scr, p_scr, vo_scr, bp_scr, bd_scr, *, T, TB, lam_init):
    lam = _lambda(lam_ref, lam_init)
    g = g_ref[...]
    bt = bias_ref[...]
    bt = (bt - bt[0:1, 0:1]) * LOG2E
    bp_scr[...] = bt[:, :TB]
    bd_scr[...] = bt[:, TB:]
    half = TB // 2
    vo_scr[:, :A_V_DIM] = v_ref[...]
    vo_scr[:, A_V_DIM:] = jnp.ones((T, A_V_DIM), BF16)

    for qi in range(T // TB):
        r0 = qi * TB
        buf = qi % 2
        q = q_ref[r0:r0 + TB, :]
        lo = lax.broadcasted_iota(jnp.int32, q.shape, 1) < HEAD_DIM
        zq = jnp.zeros_like(q)
        maps = (jnp.where(lo, q, zq), jnp.where(lo, zq, q))
        for mp in range(2):
            m = None
            for ki in range(qi + 1):
                s = _dot_nt(maps[mp], k_ref[ki * TB:(ki + 1) * TB, :])
                if ki == qi - 1:
                    s = s + bp_scr[...]
                elif ki == qi:
                    row = lax.broadcasted_iota(jnp.int32, (TB, TB), 0)
                    col = lax.broadcasted_iota(jnp.int32, (TB, TB), 1)
                    vis = (col >> _LOG2_CHUNK) <= (row >> _LOG2_CHUNK)
                    s = jnp.where(vis, s + bd_scr[...], NEG_INF)
                s_scr[buf, ki, mp * TB:(mp + 1) * TB, :] = s
                mt = jnp.maximum(s[:, :half], s[:, half:])
                m = mt if m is None else jnp.maximum(m, mt)
            m_scr[buf, mp * TB:(mp + 1) * TB, :] = jnp.broadcast_to(jnp.max(m, axis=1, keepdims=True),
                                                                    (TB, LANES))
        mb = m_scr[buf]
        mb2 = jnp.concatenate([mb, mb], axis=1)
        for ki in range(qi + 1):
            p_scr[buf, :, ki * TB:(ki + 1) * TB] = jnp.exp2(s_scr[buf, ki] - mb2).astype(BF16)
        kk = (qi + 1) * TB
        out = _dot(p_scr[buf, :, 0:kk], vo_scr[0:kk, :])
        o_ref[r0:r0 + TB, :] = _diff_finish(out[:, :A_V_DIM] / out[:, A_V_DIM:], TB, lam, g, lam_init)


def _attn_prompt(lam4, g, bias, q, k, v, B, T, lam_init):
    TB = 2 * LANES
    assert T % TB == 0 and TB % CHUNK == 0
    n = B * T
    blk = pl.BlockSpec((T, A_V_DIM), lambda b, h: (b, h))
    return pl.pallas_call(
        functools.partial(_attn_prompt_kernel, T=T, TB=TB, lam_init=lam_init),
        grid=(B, A_HEADS),
        in_specs=[pl.BlockSpec((4, HEAD_DIM), lambda b, h: (0, 0)),
                  pl.BlockSpec((1, A_V_DIM), lambda b, h: (0, 0)),
                  pl.BlockSpec((None, TB, 2 * TB), lambda b, h: (h, 0, 0)),
                  blk, blk, blk],
        out_specs=blk,
        out_shape=jax.ShapeDtypeStruct((n, A_WIDTH), BF16),
        scratch_shapes=[pltpu.VMEM((2, T // TB, 2 * TB, TB), F32),
                        pltpu.VMEM((2, 2 * TB, LANES), F32),
                        pltpu.VMEM((2, 2 * TB, T), BF16),
                        pltpu.VMEM((T, 2 * A_V_DIM), BF16),
                        pltpu.VMEM((TB, TB), F32),
                        pltpu.VMEM((TB, TB), F32)],
        compiler_params=_params(("arbitrary", "arbitrary"), 32),
        name="attn_prompt",
    )(lam4, g, bias, q, k, v)


def _attn_decode_kernel(lam_ref, g_ref, bias_ref, q_ref, kn_ref, vn_ref, ck_hbm, cv_hbm, o_ref,
                        kc_buf, vc_buf, sems, *, Ts, past, layer, lam_init):
    b = pl.program_id(0)
    h = pl.program_id(1)
    step = b * A_HEADS + h
    nsteps = pl.num_programs(0) * A_HEADS
    slot = step % 2

    def cache_copies(st, s):
        bb = st // A_HEADS
        hh = st % A_HEADS
        return [pltpu.make_async_copy(src.at[layer, bb, :, hh, :], dst.at[s], sems.at[s, j])
                for j, (src, dst) in enumerate(((ck_hbm, kc_buf), (cv_hbm, vc_buf)))]

    @pl.when(step == 0)
    def _():
        for cp in cache_copies(0, 0):
            cp.start()

    @pl.when(step + 1 < nsteps)
    def _():
        for cp in cache_copies(step + 1, 1 - slot):
            cp.start()

    for cp in cache_copies(step, slot):
        cp.wait()
    kc_ref = kc_buf.at[slot]
    vc_ref = vc_buf.at[slot]

    lam = _lambda(lam_ref, lam_init)
    q2 = _stack_maps(q_ref[...])
    bias = bias_ref[...] * LOG2E
    s_p = _dot_nt(q2, kc_ref[...].astype(BF16)).reshape(2, Ts, past) + bias[:, :past][None]
    s_n = _dot_nt(q2, kn_ref[...]).reshape(2, Ts, Ts) + bias[:, past:past + Ts][None]
    s_p = s_p.reshape(2 * Ts, past)
    s_n = s_n.reshape(2 * Ts, Ts)
    m = jnp.maximum(jnp.max(s_p, axis=1, keepdims=True), jnp.max(s_n, axis=1, keepdims=True))
    p_p = jnp.exp2(s_p - m)
    p_n = jnp.exp2(s_n - m)
    l = jnp.sum(p_p, axis=1, keepdims=True) + jnp.sum(p_n, axis=1, keepdims=True)
    acc = _dot(p_p.astype(BF16), vc_ref[...].astype(BF16)) + _dot(p_n.astype(BF16), vn_ref[...])
    o_ref[...] = _diff_finish(acc / l, Ts, lam, g_ref[...], lam_init)


def _attn_decode(lam4, g, bias, q, cache_k, cache_v, kn, vn, layer, Bs, Ts, lam_init):
    past = cache_k.shape[2]
    assert past % CHUNK == 0 and Ts <= CHUNK
    padk = bias.shape[2]
    new = pl.BlockSpec((Ts, A_V_DIM), lambda b, h: (b, h))
    cache = pl.BlockSpec(memory_space=pl.ANY)
    return pl.pallas_call(
        functools.partial(_attn_decode_kernel, Ts=Ts, past=past, layer=layer, lam_init=lam_init),
        grid=(Bs, A_HEADS),
        in_specs=[pl.BlockSpec((4, HEAD_DIM), lambda b, h: (0, 0)),
                  pl.BlockSpec((1, A_V_DIM), lambda b, h: (0, 0)),
                  pl.BlockSpec((None, Ts, padk), lambda b, h: (h, 0, 0)),
                  new, new, new, cache, cache],
        out_specs=new,
        out_shape=jax.ShapeDtypeStruct((Bs * Ts, A_WIDTH), BF16),
        scratch_shapes=[pltpu.VMEM((2, past, A_V_DIM), F32), pltpu.VMEM((2, past, A_V_DIM), F32),
                        pltpu.SemaphoreType.DMA((2, 2))],
        compiler_params=_params(("arbitrary", "arbitrary"), 40),
        name="attn_decode",
    )(lam4, g, bias, q, kn, vn, cache_k, cache_v)


def _ret_kernel(q_ref, k_ref, v_ref, g_ref, s0_ref, dm_ref, qd_ref, kd_ref, cd_ref,
                o_ref, sn_ref, *, T, C):
    z = jnp.zeros((R_KEY_DIM, R_V_DIM), F32)
    state = jnp.concatenate([jnp.concatenate([s0_ref[0], z], axis=1),
                             jnp.concatenate([z, s0_ref[1]], axis=1)], axis=0)
    r = lax.broadcasted_iota(jnp.int32, (LANES, LANES), 0)
    c = lax.broadcasted_iota(jnp.int32, (LANES, LANES), 1)
    same_head = (r >> _LOG2_CHUNK) == (c >> _LOG2_CHUNK)
    ones_bd = jnp.where(same_head, 1.0, 0.0).astype(BF16)
    lo = lax.broadcasted_iota(jnp.int32, (C, LANES), 1) < R_V_DIM
    cd = cd_ref[...]

    def chunk(n, state):
        r0 = pl.multiple_of(n * C, C)
        q = q_ref[pl.ds(r0, C), :]
        k = k_ref[pl.ds(r0, C), :]
        v = v_ref[pl.ds(r0, C), :]
        p = (_dot_nt(_stack_maps(q), k) * dm_ref[...]).astype(BF16)
        o2 = _dot(p, v)
        inner = jnp.where(lo, o2[:C], o2[C:])
        qd = (q.astype(F32) * qd_ref[...]).astype(BF16)
        o = inner + _dot(qd, state.astype(BF16))
        vk = (v.astype(F32) * kd_ref[...]).astype(BF16)
        new_state = cd * state + jnp.where(same_head, _dot_tn(k, vk), 0.0)
        oo = o * o
        hi = oo.astype(BF16)
        lo_part = (oo - hi.astype(F32)).astype(BF16)
        ss = _dot(hi, ones_bd) + _dot(lo_part, ones_bd)
        out = o * lax.rsqrt(ss * (1.0 / R_V_DIM) + LN_EPS) * g_ref[pl.ds(r0, C), :]
        o_ref[pl.ds(r0, C), :] = out.astype(BF16)
        return new_state

    state = lax.fori_loop(0, T // C, chunk, state)
    sn_ref[0] = state[:R_KEY_DIM, :R_V_DIM]
    sn_ref[1] = state[R_KEY_DIM:, R_V_DIM:]


def _retention(rq, rk, rv, rg, state0, tabs, B, T, C):
    dm, qd, kd, cd = tabs
    blk = pl.BlockSpec((T, LANES), lambda b, hp: (b, hp))
    st = pl.BlockSpec((None, 2, R_KEY_DIM, R_V_DIM), lambda b, hp: (b, hp, 0, 0))
    return pl.pallas_call(
        functools.partial(_ret_kernel, T=T, C=C),
        grid=(B, 2),
        in_specs=[blk, blk, blk, blk, st,
                  pl.BlockSpec((None, 2 * C, C), lambda b, hp: (hp, 0, 0)),
                  pl.BlockSpec((None, C, LANES), lambda b, hp: (hp, 0, 0)),
                  pl.BlockSpec((None, C, LANES), lambda b, hp: (hp, 0, 0)),
                  pl.BlockSpec((None, 1, LANES), lambda b, hp: (hp, 0, 0))],
        out_specs=[blk, st],
        out_shape=[jax.ShapeDtypeStruct((B * T, R_WIDTH), BF16),
                   jax.ShapeDtypeStruct((B, R_HEADS, R_KEY_DIM, R_V_DIM), F32)],
        compiler_params=_params(("arbitrary", "arbitrary"), 32),
        name="retention",
    )(rq, rk, rv, rg, state0, dm, qd, kd, cd)


def _retention_tables(C):
    h = jnp.arange(R_HEADS, dtype=F32)
    log_g = jnp.log1p(-jnp.exp2(-5.0 - h))
    i = jnp.arange(C, dtype=F32)
    diff = i[:, None] - i[None, :]
    dmask = jnp.where(diff >= 0, jnp.exp(jnp.maximum(diff, 0.0)[None] * log_g[:, None, None]), 0.0)
    dm = dmask.reshape(2, 2 * C, C)
    q_dec = jnp.exp((i + 1.0)[None, :] * log_g[:, None])
    k_dec = jnp.exp((C - 1 - i)[None, :] * log_g[:, None])
    c_dec = jnp.exp(C * log_g)

    def lanes(t):
        t = jnp.repeat(t[:, :, None], R_V_DIM, axis=2).reshape(2, 2, C, R_V_DIM)
        return jnp.concatenate([t[:, 0], t[:, 1]], axis=-1)

    cd = jnp.repeat(c_dec[:, None], R_V_DIM, axis=1).reshape(2, 1, LANES)
    return dm, lanes(q_dec), lanes(k_dec), cd


_CONV_PAD = 32


def _conv_kernel(u_ref, h_ref, w_ref, b_ref, g_ref, be_ref, o_ref, t_ref, up_scr, *, T, RT):
    hist = CONV_WIDTH - 1
    off = _CONV_PAD - hist
    up_scr[off:_CONV_PAD, :] = h_ref[...]
    up_scr[_CONV_PAD:_CONV_PAD + T, :] = u_ref[...]
    bias = b_ref[...]
    g = g_ref[...]
    be = be_ref[...]
    win_rows = RT + _CONV_PAD
    for t0 in range(0, T, RT):
        win = up_scr[t0:t0 + win_rows, :]
        acc = jnp.zeros((RT, C_WIDTH), F32)
        for s in range(SUBLANES):
            rolled = win if s == 0 else pltpu.roll(win, win_rows - s, 0)
            for a in range(_CONV_PAD // SUBLANES + 1):
                j = a * SUBLANES + s - off
                if 0 <= j < CONV_WIDTH:
                    acc = acc + rolled[a * SUBLANES:a * SUBLANES + RT, :] * w_ref[j:j + 1, :]
        y = _layer_norm(acc + bias, g, be)
        o_ref[t0:t0 + RT, :] = (y * _sigmoid(y)).astype(BF16)
    t_ref[...] = up_scr[T + off:T + _CONV_PAD, :]


def _conv(u, hist, dw_w, dw_b, ln_g, ln_b, B, T):
    RT = min(T, LANES)
    assert T % RT == 0
    hl = CONV_WIDTH - 1
    vec = pl.BlockSpec((1, C_WIDTH), lambda b: (0, 0))
    hb = pl.BlockSpec((None, hl, C_WIDTH), lambda b: (b, 0, 0))
    return pl.pallas_call(
        functools.partial(_conv_kernel, T=T, RT=RT),
        grid=(B,),
        in_specs=[pl.BlockSpec((T, C_WIDTH), lambda b: (b, 0)), hb,
                  pl.BlockSpec((CONV_WIDTH, C_WIDTH), lambda b: (0, 0)), vec, vec, vec],
        out_specs=[pl.BlockSpec((T, C_WIDTH), lambda b: (b, 0)), hb],
        out_shape=[jax.ShapeDtypeStruct((B * T, C_WIDTH), BF16),
                   jax.ShapeDtypeStruct((B, hl, C_WIDTH), F32)],
        scratch_shapes=[pltpu.VMEM((T + _CONV_PAD, C_WIDTH), F32)],
        compiler_params=_params(("arbitrary",), 32),
        name="conv",
    )(u, hist, dw_w, dw_b.reshape(1, C_WIDTH), ln_g.reshape(1, C_WIDTH), ln_b.reshape(1, C_WIDTH))


def _first_index(vals, target):
    idx = jnp.full(target.shape, len(vals) - 1, jnp.int32)
    for j in range(len(vals) - 2, -1, -1):
        idx = jnp.where(vals[j] == target, j, idx)
    return idx


def _select(idx, vals):
    out = vals[-1]
    for j in range(len(vals) - 2, -1, -1):
        out = jnp.where(idx == j, vals[j], out)
    return out


def _route(logits_t):
    rows = [logits_t[e:e + 1, :] for e in range(N_EXPERTS)]
    m = functools.reduce(jnp.maximum, rows)
    ex = [jnp.exp(r - m) for r in rows]
    z = functools.reduce(jnp.add, ex)
    sc = [e / z for e in ex]
    v1s, v2s, i1s, i2s, gss = [], [], [], [], []
    for g in range(N_GROUPS):
        a = sc[g * EXPERTS_PER_GROUP:(g + 1) * EXPERTS_PER_GROUP]
        v1 = functools.reduce(jnp.maximum, a)
        i1 = _first_index(a, v1)
        rest = [jnp.where(i1 == j, -1.0, a[j]) for j in range(EXPERTS_PER_GROUP)]
        v2 = functools.reduce(jnp.maximum, rest)
        i2 = _first_index(rest, v2)
        v1s.append(v1); v2s.append(v2); i1s.append(i1); i2s.append(i2); gss.append(v1 + v2)
    grp = _first_index(gss, functools.reduce(jnp.maximum, gss))
    v1 = _select(grp, v1s)
    v2 = _select(grp, v2s)
    e1 = _select(grp, i1s) + grp * EXPERTS_PER_GROUP
    e2 = _select(grp, i2s) + grp * EXPERTS_PER_GROUP
    den = v1 + v2
    return e1, e2, v1 / den, v2 / den


def _expert_onehot(e1, e2):
    rows = [jnp.where(e1 == e, 1.0, 0.0) + jnp.where(e2 == e, 1.0, 0.0) for e in range(N_EXPERTS)]
    return jnp.concatenate(rows, axis=0)


def _outproj_kernel(a_ref, r_ref, c_ref, x_ref, w_ref, g_ref, b_ref, wr_ref, br_ref,
                    x1_ref, x1b_ref, ri_ref, rw_ref, cnt_ref, *, alpha, tb):
    mix = (_dot(a_ref[...], w_ref[0:A_WIDTH, :])
           + _dot(r_ref[...], w_ref[A_WIDTH:A_WIDTH + R_WIDTH, :])
           + _dot(c_ref[...], w_ref[A_WIDTH + R_WIDTH:, :]))
    x1 = _layer_norm(alpha * x_ref[...] + mix, g_ref[...], b_ref[...])
    x1_ref[...] = x1
    x1b_ref[...] = x1.astype(BF16)
    logits_t = lax.dot_general(wr_ref[...], x1, (((1,), (1,)), ((), ())),
                               precision=lax.Precision.HIGHEST,
                               preferred_element_type=F32) + br_ref[...]
    e1, e2, w1, w2 = _route(logits_t)
    tm = e1.shape[1]
    ri_ref[...] = jnp.concatenate([e1, e2, jnp.zeros((SUBLANES - 2, tm), jnp.int32)], axis=0)
    rw_ref[...] = jnp.concatenate([w1, w2, jnp.zeros((SUBLANES - 2, tm), F32)], axis=0)
    onehot = _expert_onehot(e1, e2).astype(BF16)
    ones = jnp.ones((SUBLANES, tb), BF16)
    for k in range(tm // tb):
        c = _dot_nt(ones, onehot[:, k * tb:(k + 1) * tb])
        c = jnp.concatenate([c, jnp.zeros((SUBLANES, LANES - N_EXPERTS), F32)], axis=1)
        cnt_ref[k * SUBLANES:(k + 1) * SUBLANES, :] = c.astype(jnp.int32)


def _outproj(attn, ret, conv, x, w_bf, ln_g, ln_b, wr_t, br, alpha, tm, tb):
    n = x.shape[0]
    row = lambda i: (i, 0)
    col = lambda i: (0, i)
    const = lambda i: (0, 0)
    nsub = tm // tb
    return pl.pallas_call(
        functools.partial(_outproj_kernel, alpha=alpha, tb=tb),
        grid=(n // tm,),
        in_specs=[pl.BlockSpec((tm, A_WIDTH), row), pl.BlockSpec((tm, R_WIDTH), row),
                  pl.BlockSpec((tm, C_WIDTH), row), pl.BlockSpec((tm, D_MODEL), row),
                  pl.BlockSpec((D_MODEL, D_MODEL), const),
                  pl.BlockSpec((1, D_MODEL), const), pl.BlockSpec((1, D_MODEL), const),
                  pl.BlockSpec((N_EXPERTS, D_MODEL), const), pl.BlockSpec((N_EXPERTS, 1), const)],
        out_specs=[pl.BlockSpec((tm, D_MODEL), row), pl.BlockSpec((tm, D_MODEL), row),
                   pl.BlockSpec((SUBLANES, tm), col), pl.BlockSpec((SUBLANES, tm), col),
                   pl.BlockSpec((nsub * SUBLANES, LANES), row)],
        out_shape=[jax.ShapeDtypeStruct((n, D_MODEL), F32), jax.ShapeDtypeStruct((n, D_MODEL), BF16),
                   jax.ShapeDtypeStruct((SUBLANES, n), jnp.int32),
                   jax.ShapeDtypeStruct((SUBLANES, n), F32),
                   jax.ShapeDtypeStruct((n // tb * SUBLANES, LANES), jnp.int32)],
        compiler_params=_params(("arbitrary",), 40),
        name="outproj",
    )(attn, ret, conv, x, w_bf, ln_g.reshape(1, D_MODEL), ln_b.reshape(1, D_MODEL), wr_t, br)


_UNIT = 16
_XS_WIDTH = D_MODEL + LANES
_FFN_ROWS = 512
_DISPATCH_TOKENS = 512


def _sorted_positions(e1, e2, lo_ref, base, tb):
    onehot = _expert_onehot(e1, e2).astype(BF16)
    r = lax.broadcasted_iota(jnp.int32, (tb, tb), 0)
    c = lax.broadcasted_iota(jnp.int32, (tb, tb), 1)
    earlier = jnp.where(r < c, 1.0, 0.0).astype(BF16)
    rank = _dot(onehot, earlier).astype(jnp.int32)
    pos1 = jnp.zeros(e1.shape, jnp.int32)
    pos2 = jnp.zeros(e1.shape, jnp.int32)
    for e in range(N_EXPERTS):
        p = rank[e:e + 1, :] + lo_ref[base + e]
        pos1 = jnp.where(e1 == e, p, pos1)
        pos2 = jnp.where(e2 == e, p, pos2)
    return pos1, pos2


def _permutation(pos1, pos2, rows):
    j = lax.broadcasted_iota(jnp.int32, (rows, pos1.shape[1]), 0)
    return jnp.where(j == pos1, 1.0, jnp.where(j == pos2, 1.0, 0.0)).astype(BF16)


def _dispatch_kernel(lo_ref, urow_ref, ri_ref, rw_ref, x_ref, xs_hbm, pos_ref,
                     stage, sems, *, nblk, tb, cap, spare_row):
    i = pl.program_id(0)
    slot = i % 2
    units = cap // _UNIT

    def wait_slot(s):
        pltpu.make_async_copy(stage.at[s], xs_hbm.at[pl.ds(0, cap), :], sems.at[s]).wait()

    @pl.when(i >= 2)
    def _():
        wait_slot(slot)

    e1 = ri_ref[0:1, :]
    e2 = ri_ref[1:2, :]
    w1 = rw_ref[0:1, :]
    w2 = rw_ref[1:2, :]
    pos1, pos2 = _sorted_positions(e1, e2, lo_ref, i * N_EXPERTS, tb)
    pos_ref[...] = jnp.concatenate([pos1, pos2, jnp.zeros((SUBLANES - 2, tb), jnp.int32)], axis=0)
    j = lax.broadcasted_iota(jnp.int32, (cap, tb), 0)
    perm1 = jnp.where(j == pos1, 1.0, 0.0).astype(BF16)
    perm2 = jnp.where(j == pos2, 1.0, 0.0).astype(BF16)
    stage[slot, :, 0:D_MODEL] = _dot(perm1 + perm2, x_ref[...]).astype(BF16)

    def pieces(w):
        a = w.astype(BF16).astype(F32)
        b = (w - a).astype(BF16).astype(F32)
        return jnp.concatenate([a, b, w - a - b, jnp.zeros((SUBLANES - 3, tb), F32)], axis=0).astype(BF16)

    gp = _dot_nt(perm1, pieces(w1)) + _dot_nt(perm2, pieces(w2))
    gp = jnp.concatenate([gp, jnp.zeros((cap, LANES - SUBLANES), F32)], axis=1)
    stage[slot, :, D_MODEL:_XS_WIDTH] = gp.astype(BF16)

    def body(u, c):
        t = urow_ref[i * units + u]
        s = pl.multiple_of(u * _UNIT, _UNIT)
        d = pl.multiple_of(jnp.where(t >= 0, t, spare_row + slot * cap + s), _UNIT)
        pltpu.make_async_copy(stage.at[slot, pl.ds(s, _UNIT), :], xs_hbm.at[pl.ds(d, _UNIT), :],
                              sems.at[slot]).start()
        return c

    lax.fori_loop(0, units, body, 0)

    @pl.when(i == nblk - 1)
    def _():
        wait_slot(slot)
        if nblk > 1:
            wait_slot(1 - slot)


def _ffn_kernel(te_ref, tv_ref, nt_ref, xs_ref, wg_ref, wu_ref, wd_ref, ys_ref, wg_b, wu_b, wd_b):
    j = pl.program_id(0)
    jm = jnp.maximum(j, 1)

    @pl.when((j == 0) | (te_ref[j] != te_ref[jm - 1]))
    def _():
        wg_b[...] = wg_ref[...].astype(BF16)
        wu_b[...] = wu_ref[...].astype(BF16)
        wd_b[...] = wd_ref[...].astype(BF16)

    @pl.when(j < nt_ref[0])
    def _():
        rows = xs_ref.shape[0]
        valid = lax.broadcasted_iota(jnp.int32, (rows, LANES), 0) < tv_ref[j]
        zero = jnp.zeros((rows, LANES), BF16)
        x = jnp.concatenate([jnp.where(valid, xs_ref[:, c * LANES:(c + 1) * LANES], zero)
                             for c in range(D_MODEL // LANES)], axis=1)
        gp = jnp.where(valid, xs_ref[:, D_MODEL:_XS_WIDTH], zero).astype(F32)
        g = gp[:, 0:1] + gp[:, 1:2] + gp[:, 2:3]
        hg = _dot(x, wg_b[...])
        hu = _dot(x, wu_b[...])
        h = hg * _sigmoid(hg) * hu * g
        ys_ref[...] = _dot(h.astype(BF16), wd_b[...]).astype(BF16)


def _combine_kernel(urow_ref, pos_ref, x1_ref, p_ref, g_ref, b_ref, wpg_ref, bpg_ref, wpp_ref,
                    ys_hbm, out_ref, stage, sems, *, nblk, tb, cap, alpha):
    i = pl.program_id(0)
    slot = i % 2
    units = cap // _UNIT

    def fetch(blk, s):
        def body(u, c):
            sr = pl.multiple_of(jnp.maximum(urow_ref[blk * units + u], 0), _UNIT)
            ds = pl.multiple_of(u * _UNIT, _UNIT)
            pltpu.make_async_copy(ys_hbm.at[pl.ds(sr, _UNIT), :], stage.at[s, pl.ds(ds, _UNIT), :],
                                  sems.at[s]).start()
            return c

        lax.fori_loop(0, units, body, 0)

    @pl.when(i == 0)
    def _():
        fetch(0, 0)

    @pl.when(i + 1 < nblk)
    def _():
        fetch(i + 1, 1 - slot)

    pltpu.make_async_copy(ys_hbm.at[pl.ds(0, cap), :], stage.at[slot], sems.at[slot]).wait()
    perm = _permutation(pos_ref[0:1, :], pos_ref[1:2, :], cap)
    y = _dot_tn(perm, stage[slot])
    x2 = _layer_norm(alpha * x1_ref[...] + y, g_ref[...], b_ref[...])
    gate = _sigmoid(_dot(x2.astype(BF16), wpg_ref[...]) + bpg_ref[...])
    out_ref[...] = x2 + gate * _dot(p_ref[...].astype(BF16), wpp_ref[...])


def _moe_plan(cnt, nblk, tb):
    cnt = cnt.reshape(nblk, SUBLANES, LANES)[:, 0, :N_EXPERTS]
    pc = (cnt + (_UNIT - 1)) // _UNIT * _UNIT
    lo = jnp.cumsum(pc, axis=1) - pc
    tot = jnp.sum(pc, axis=0)
    reg = (tot + (_FFN_ROWS - 1)) // _FFN_ROWS * _FFN_ROWS
    rstart = jnp.cumsum(reg) - reg
    gs = rstart[None, :] + jnp.cumsum(pc, axis=0) - pc
    tiles_e = reg // _FFN_ROWS
    tile_end = jnp.cumsum(tiles_e)
    max_rows = 2 * nblk * tb + nblk * N_EXPERTS * _UNIT + N_EXPERTS * _FFN_ROWS
    max_tiles = -(-max_rows // _FFN_ROWS)
    tj =jnp.arange(max_tiles, dtype=jnp.int32)
    te = jnp.minimum(jnp.sum((tj[:, None] >= tile_end[None, :]).astype(jnp.int32), axis=1), N_EXPERTS - 1)
    tv = jnp.clip(tot[te] - (tj - (tile_end - tiles_e)[te]) * _FFN_ROWS, 0, _FFN_ROWS)
    nt = tile_end[-1:].astype(jnp.int32)
    cap = 2 * tb + N_EXPERTS * _UNIT
    urow0 = jnp.arange(cap // _UNIT, dtype=jnp.int32) * _UNIT
    run = jnp.sum((urow0[None, :, None] >= (lo + pc)[:, None, :]).astype(jnp.int32), axis=2)
    pick = run[:, :, None] == jnp.arange(N_EXPERTS, dtype=jnp.int32)[None, None, :]
    urow = jnp.sum(jnp.where(pick, (gs - lo)[:, None, :], 0), axis=2) + urow0[None, :]
    urow = jnp.where(urow0[None, :] < jnp.sum(pc, axis=1, keepdims=True), urow, -1)
    flat = lambda a: a.reshape(-1).astype(jnp.int32)
    return flat(lo), flat(urow), te.astype(jnp.int32), tv.astype(jnp.int32), nt, max_tiles, cap


def _moe(x1b, x1, ri, rw, cnt, p, wg, wu, wd, layer, ln_g, ln_b, wpg, bpg, wpp, alpha, tb):
    n = x1.shape[0]
    nblk = n // tb
    lo, urow, te, tv, nt, max_tiles, cap = _moe_plan(cnt, nblk, tb)
    max_rows = max_tiles * _FFN_ROWS

    xs, pos = pl.pallas_call(
        functools.partial(_dispatch_kernel, nblk=nblk, tb=tb, cap=cap, spare_row=max_rows),
        grid_spec=pltpu.PrefetchScalarGridSpec(
            num_scalar_prefetch=2,
            grid=(nblk,),
            in_specs=[pl.BlockSpec((SUBLANES, tb), lambda i, *_: (0, i)),
                      pl.BlockSpec((SUBLANES, tb), lambda i, *_: (0, i)),
                      pl.BlockSpec((tb, D_MODEL), lambda i, *_: (i, 0))],
            out_specs=[pl.BlockSpec(memory_space=pl.ANY),
                       pl.BlockSpec((SUBLANES, tb), lambda i, *_: (0, i))],
            scratch_shapes=[pltpu.VMEM((2, cap, _XS_WIDTH), BF16),
                            pltpu.SemaphoreType.DMA((2,))]),
        out_shape=[jax.ShapeDtypeStruct((max_rows + 2 * cap, _XS_WIDTH), BF16),
                   jax.ShapeDtypeStruct((SUBLANES, n), jnp.int32)],
        compiler_params=_params(("arbitrary",), 32),
        name="moe_dispatch",
    )(lo, urow, ri, rw, x1b)

    def tile(j, te_ref, tv_ref, nt_ref):
        return jnp.minimum(j, nt_ref[0] - 1)

    ys = pl.pallas_call(
        _ffn_kernel,
        grid_spec=pltpu.PrefetchScalarGridSpec(
            num_scalar_prefetch=3,
            grid=(max_tiles,),
            in_specs=[pl.BlockSpec((_FFN_ROWS, _XS_WIDTH), lambda j, *s: (tile(j, *s), 0)),
                      pl.BlockSpec((None, None, D_MODEL, D_EXPERT),
                                   lambda j, *s: (layer, s[0][tile(j, *s)], 0, 0)),
                      pl.BlockSpec((None, None, D_MODEL, D_EXPERT),
                                   lambda j, *s: (layer, s[0][tile(j, *s)], 0, 0)),
                      pl.BlockSpec((None, None, D_EXPERT, D_MODEL),
                                   lambda j, *s: (layer, s[0][tile(j, *s)], 0, 0))],
            out_specs=pl.BlockSpec((_FFN_ROWS, D_MODEL), lambda j, *s: (tile(j, *s), 0)),
            scratch_shapes=[pltpu.VMEM((D_MODEL, D_EXPERT), BF16), pltpu.VMEM((D_MODEL, D_EXPERT), BF16),
                            pltpu.VMEM((D_EXPERT, D_MODEL), BF16)]),
        out_shape=jax.ShapeDtypeStruct((max_rows, D_MODEL), BF16),
        compiler_params=_params(("arbitrary",), 40),
        name="moe_ffn",
    )(te, tv, nt, xs, wg, wu, wd)

    const = lambda i, *_: (0, 0)
    return pl.pallas_call(
        functools.partial(_combine_kernel, nblk=nblk, tb=tb, cap=cap, alpha=alpha),
        grid_spec=pltpu.PrefetchScalarGridSpec(
            num_scalar_prefetch=1,
            grid=(nblk,),
            in_specs=[pl.BlockSpec((SUBLANES, tb), lambda i, *_: (0, i)),
                      pl.BlockSpec((tb, D_MODEL), lambda i, *_: (i, 0)),
                      pl.BlockSpec((None, tb, PLE_DIM), lambda i, *_: (layer, i, 0)),
                      pl.BlockSpec((1, D_MODEL), const), pl.BlockSpec((1, D_MODEL), const),
                      pl.BlockSpec((D_MODEL, D_MODEL), const), pl.BlockSpec((1, D_MODEL), const),
                      pl.BlockSpec((PLE_DIM, D_MODEL), const),
                      pl.BlockSpec(memory_space=pl.ANY)],
            out_specs=pl.BlockSpec((tb, D_MODEL), lambda i, *_: (i, 0)),
            scratch_shapes=[pltpu.VMEM((2, cap, D_MODEL), BF16),
                            pltpu.SemaphoreType.DMA((2,))]),
        out_shape=jax.ShapeDtypeStruct((n, D_MODEL), F32),
        compiler_params=_params(("arbitrary",), 40),
        name="moe_combine",
    )(urow, pos, x1, p, ln_g.reshape(1, D_MODEL), ln_b.reshape(1, D_MODEL),
      wpg, bpg.reshape(1, D_MODEL), wpp, ys)


def _rope_tables(pos, rows):
    half = R_KEY_DIM // 2
    inv_freq = 1.0 / (ROPE_BASE ** jnp.linspace(0.0, 1.0, half, dtype=jnp.float32))
    ang = pos.astype(jnp.float32)[:, None] * inv_freq[None, :]
    cos = jnp.cos(ang)
    sin = jnp.sin(ang)
    cos_t = jnp.tile(cos, (rows // pos.shape[0], 4))
    sin_t = jnp.tile(jnp.concatenate([-sin, sin], axis=1), (rows // pos.shape[0], 2))
    return cos_t, sin_t


def kernel(x_prompt, x_sample, p_prompt, p_sample, cache_k, cache_v, state_ret, state_conv, ln_emb_g, ln_emb_b, rel_bias, w_router, b_router, w_in, lam_q1, lam_k1, lam_q2, lam_k2, subln_g, dw_w, dw_b, conv_ln_g, conv_ln_b, w_out, ln1_g, ln1_b, w_exp_gate, w_exp_up, w_exp_down, ln2_g, ln2_b, w_ple_gate, b_ple_gate, w_ple_proj):
    B, T, D = x_prompt.shape
    Bs, Ts, _ = x_sample.shape
    depth = w_in.shape[0]
    past = cache_k.shape[2]
    n_p, n_s = B * T, Bs * Ts
    alpha = (2 * depth) ** 0.25

    tm_p = 512 if n_p % 512 == 0 else n_p
    tm_s = n_s
    TB = 2 * LANES
    c_p = min(T, 2 * LANES)

    pos_p = jnp.arange(T, dtype=jnp.int32)
    pos_s = past + jnp.arange(Ts, dtype=jnp.int32)
    rope_p = _rope_tables(pos_p, max(T, tm_p))
    rope_s = _rope_tables(pos_s, max(Ts, tm_s))
    rel_p = (jnp.arange(2 * TB, dtype=jnp.int32)[None, :] - TB) - jnp.arange(TB, dtype=jnp.int32)[:, None]
    padk = -(-(past + Ts) // LANES) * LANES
    rel_s = jnp.arange(padk, dtype=jnp.int32)[None, :] - pos_s[:, None]
    bias_p = _bias_table(rel_bias, _t5_bucket(rel_p))
    bias_s = _bias_table(rel_bias, _t5_bucket(rel_s))
    tabs_p = _retention_tables(c_p)
    tabs_s = _retention_tables(Ts)

    wr_t = w_router.T
    br = b_router.reshape(N_EXPERTS, 1)
    ret0_p = jnp.zeros((B, R_HEADS, R_KEY_DIM, R_V_DIM), F32)
    conv0_p = jnp.zeros((B, CONV_WIDTH - 1, C_WIDTH), F32)

    xp = x_prompt.reshape(n_p, D)
    xs = x_sample.reshape(n_s, D)
    outs = {k: [] for k in ("rp", "cp", "rs", "cs")}
    kv_p = kv_s = None
    for l in range(depth):
        lam_init = 0.8 - 0.6 * math.exp(-0.3 * l)
        lam4 = jnp.stack([lam_q1[l], lam_k1[l], lam_q2[l], lam_k2[l]])
        g_sub = subln_g[l].reshape(1, A_V_DIM)
        w_in_b = w_in[l].astype(BF16)
        w_out_b = w_out[l].astype(BF16)
        wpg, wpp = w_ple_gate[l].astype(BF16), w_ple_proj[l].astype(BF16)
        ln = (ln_emb_g, ln_emb_b) if l == 0 else None

        def channel(x, attn, ret, conv, p_l, tm):
            tb = min(tm, _DISPATCH_TOKENS)
            x1, x1b, ri, rw, cnt = _outproj(attn, ret, conv, x, w_out_b, ln1_g[l], ln1_b[l], wr_t, br,
                                            alpha, tm, tb)
            return _moe(x1b, x1, ri, rw, cnt, p_l, w_exp_gate, w_exp_up, w_exp_down, l, ln2_g[l], ln2_b[l],
                        wpg, b_ple_gate[l], wpp, alpha, tb)

        xp, q, kb, vb, rq, rk, rv, rg, u, k5, v5 = _inproj(xp, w_in_b, rope_p[0], rope_p[1], ln, tm_p,
                                                           kv_p, l, depth, B, T)
        kv_p = (k5, v5)
        attn = _attn_prompt(lam4, g_sub, bias_p, q, kb, vb, B, T, lam_init)
        ret, rstate = _retention(rq, rk, rv, rg, ret0_p, tabs_p, B, T, c_p)
        conv, ctail = _conv(u, conv0_p, dw_w[l], dw_b[l], conv_ln_g[l], conv_ln_b[l], B, T)
        xp = channel(xp, attn, ret, conv, p_prompt.reshape(depth, n_p, PLE_DIM), tm_p)
        outs["rp"].append(rstate)
        outs["cp"].append(ctail)

        xs, q, kb, vb, rq, rk, rv, rg, u, k5, v5 = _inproj(xs, w_in_b, rope_s[0], rope_s[1], ln, tm_s,
                                                           kv_s, l, depth, Bs, Ts)
        kv_s = (k5, v5)
        attn = _attn_decode(lam4, g_sub, bias_s, q, cache_k, cache_v, kb, vb, l, Bs, Ts, lam_init)
        ret, rstate = _retention(rq, rk, rv, rg, state_ret[l], tabs_s, Bs, Ts, Ts)
        conv, ctail = _conv(u, state_conv[l], dw_w[l], dw_b[l], conv_ln_g[l], conv_ln_b[l], Bs, Ts)
        xs = channel(xs, attn, ret, conv, p_sample.reshape(depth, n_s, PLE_DIM), tm_s)
        outs["rs"].append(rstate)
        outs["cs"].append(ctail)

    return (xp.reshape(B, T, D), xs.reshape(Bs, Ts, D),
            kv_p[0], kv_p[1], jnp.stack(outs["rp"]), jnp.stack(outs["cp"]),
            kv_s[0], kv_s[1], jnp.stack(outs["rs"]), jnp.stack(outs["cs"]))
```

```python
import functools
import math

import jax
import jax.numpy as jnp
from jax import lax
from jax.experimental import pallas as pl
from jax.experimental.pallas import tpu as pltpu

F32 = jnp.float32
BF16 = jnp.bfloat16

D_MODEL = 1024
CHUNK = 64
HEAD_DIM = 64
A_HEADS = 4
A_V_DIM = 128
A_WIDTH = 512
R_HEADS = 4
R_KEY_DIM = 64
R_V_DIM = 64
R_WIDTH = 256
C_WIDTH = 256
CONV_WIDTH = 31
IN_WIDTH = 3072
N_BUCKETS = 32
MAX_DISTANCE = 128
ROPE_BASE = 10000.0
N_EXPERTS = 16
N_GROUPS = 4
EXPERTS_PER_GROUP = 4
D_EXPERT = 512
PLE_DIM = 256
LN_EPS = 1e-5
NEG_INF = -1e30
LOG2E = 1.4426950408889634

LANES = 128
SUBLANES = 8
_LOG2_CHUNK = 6
assert CHUNK == HEAD_DIM == R_KEY_DIM == R_V_DIM == 1 << _LOG2_CHUNK
MIB = 1024 * 1024

_OFF_Q, _OFF_K, _OFF_V = 0, 512, 1024
_OFF_RQ, _OFF_RK, _OFF_RV, _OFF_RG, _OFF_C = 1536, 1792, 2048, 2304, 2560


def _params(sem, vmem_mib):
    return pltpu.CompilerParams(dimension_semantics=sem, vmem_limit_bytes=vmem_mib * MIB)


def _layer_norm(x, g, b):
    mu = jnp.mean(x, axis=-1, keepdims=True)
    xc = x - mu
    var = jnp.mean(xc * xc, axis=-1, keepdims=True)
    return xc * lax.rsqrt(var + LN_EPS) * g + b


def _sigmoid(x):
    return 1.0 / (1.0 + jnp.exp(-x))


def _dot(a, b):
    return jnp.dot(a, b, preferred_element_type=F32)


def _dot_nt(a, b):
    return lax.dot_general(a, b, (((1,), (1,)), ((), ())), preferred_element_type=F32)


def _dot_tn(a, b):
    return lax.dot_general(a, b, (((0,), (0,)), ((), ())), preferred_element_type=F32)


def _bias_kernel(relb_ref, idx_ref, out_ref):
    idx = idx_ref[...]
    for h in range(A_HEADS):
        acc = jnp.zeros(idx.shape, F32)
        for b in range(N_BUCKETS):
            acc = jnp.where(idx == b, relb_ref[b, h], acc)
        out_ref[h] = acc


def _bias_table(rel_bias, idx):
    r, c = idx.shape
    return pl.pallas_call(
        _bias_kernel,
        out_shape=jax.ShapeDtypeStruct((A_HEADS, r, c), F32),
        in_specs=[pl.BlockSpec(memory_space=pltpu.SMEM),
                  pl.BlockSpec(memory_space=pltpu.VMEM)],
        out_specs=pl.BlockSpec(memory_space=pltpu.VMEM),
        name="bias_table",
    )(rel_bias, idx)


def _t5_bucket(rel):
    nb = N_BUCKETS // 2
    max_exact = nb // 2
    n = jnp.abs(rel)
    nf = jnp.maximum(n, 1).astype(jnp.float32)
    large = max_exact + (jnp.log(nf / max_exact) / math.log(MAX_DISTANCE / max_exact)
                         * (nb - max_exact)).astype(jnp.int32)
    large = jnp.minimum(large, nb - 1)
    return jnp.where(rel > 0, nb, 0) + jnp.where(n < max_exact, n, large)


def _rotary128(x, cos, sin_signed, lo32):
    partner = jnp.where(lo32, pltpu.roll(x, 96, 1), pltpu.roll(x, 32, 1))
    return x * cos + partner * sin_signed


def _inproj_kernel(*refs, apply_ln, has_prev, layer, tm, T, nsteps):
    refs = list(refs)
    x_ref = refs.pop(0)
    if apply_ln:
        g_ref, b_ref = refs.pop(0), refs.pop(0)
    w_ref, cos_ref, sin_ref = refs.pop(0), refs.pop(0), refs.pop(0)
    if has_prev:
        refs.pop(0), refs.pop(0)
    if apply_ln:
        xn_ref = refs.pop(0)
    (q_ref, kb_ref, vb_ref, rq_ref, rk_ref, rv_ref, rg_ref, u_ref, k5_hbm, v5_hbm,
     kbuf, vbuf, sems) = refs

    i = pl.program_id(0)
    slot = i % 2
    rows = min(tm, T)

    def kv_copies(s, step):
        cps = []
        for bb in range(tm // rows):
            r0 = step * tm + bb * rows
            b = r0 // T
            t0 = pl.multiple_of(r0 % T, SUBLANES)
            for buf, out in ((kbuf, k5_hbm), (vbuf, v5_hbm)):
                for h in range(A_HEADS):
                    cps.append(pltpu.make_async_copy(
                        buf.at[s, pl.ds(bb * rows, rows), pl.ds(h * A_V_DIM, A_V_DIM)],
                        out.at[layer, b, pl.ds(t0, rows), h, :], sems.at[s]))
        return cps

    @pl.when(i >= 2)
    def _():
        for cp in kv_copies(slot, i - 2):
            cp.wait()

    x = x_ref[...]
    if apply_ln:
        x = _layer_norm(x, g_ref[...], b_ref[...])
        xn_ref[...] = x
    xb = x.astype(BF16)

    def mm(c0, c1):
        return _dot(xb, w_ref[:, c0:c1])

    q_ref[...] = (mm(_OFF_Q, _OFF_K) * (LOG2E * HEAD_DIM ** -0.5)).astype(BF16)
    a = mm(_OFF_K, _OFF_V)
    kbuf[slot] = a
    kb_ref[...] = a.astype(BF16)
    a = mm(_OFF_V, _OFF_RQ)
    vbuf[slot] = a
    vb_ref[...] = a.astype(BF16)
    for cp in kv_copies(slot, i):
        cp.start()

    cos = cos_ref[...]
    sin = sin_ref[...]
    lane = lax.broadcasted_iota(jnp.int32, cos.shape, 1)
    lo32 = (lane & 63) < 32

    def rot(a, scale):
        parts = [_rotary128(a[:, c * LANES:(c + 1) * LANES], cos, sin, lo32) for c in range(2)]
        r = jnp.concatenate(parts, axis=1)
        if scale != 1.0:
            r = r * scale
        return r.astype(BF16)

    rq_ref[...] = rot(mm(_OFF_RQ, _OFF_RK), 1.0)
    rk_ref[...] = rot(mm(_OFF_RK, _OFF_RV), R_KEY_DIM ** -0.5)
    rv_ref[...] = mm(_OFF_RV, _OFF_RG).astype(BF16)
    a = mm(_OFF_RG, _OFF_C)
    rg_ref[...] = a * _sigmoid(a)
    a = mm(_OFF_C, IN_WIDTH)
    u_ref[...] = a[:, :C_WIDTH] * _sigmoid(a[:, C_WIDTH:])

    @pl.when(i == nsteps - 1)
    def _():
        for cp in kv_copies(slot, i):
            cp.wait()
        if nsteps > 1:
            for cp in kv_copies(1 - slot, i - 1):
                cp.wait()


def _inproj(x, w_bf, cos_t, sin_t, ln, tm, kv_prev, layer, depth, B, T):
    n = x.shape[0]
    assert (tm % T == 0 or T % tm == 0) and n % tm == 0
    nblk = cos_t.shape[0] // tm
    nsteps = n // tm
    row = lambda i: (i, 0)
    const = lambda i: (0, 0)
    tab = lambda i: (i % nblk, 0)
    in_specs = [pl.BlockSpec((tm, D_MODEL), row)]
    args = [x]
    if ln is not None:
        in_specs += [pl.BlockSpec((1, D_MODEL), const)] * 2
        args += [ln[0].reshape(1, D_MODEL), ln[1].reshape(1, D_MODEL)]
    in_specs += [pl.BlockSpec((D_MODEL, IN_WIDTH), const),
                 pl.BlockSpec((tm, LANES), tab), pl.BlockSpec((tm, LANES), tab)]
    args += [w_bf, cos_t, sin_t]
    aliases = {}
    if kv_prev is not None:
        k5_index = (1 if ln is not None else 0) + 8
        aliases = {len(args): k5_index, len(args) + 1: k5_index + 1}
        in_specs += [pl.BlockSpec(memory_space=pl.ANY)] * 2
        args += list(kv_prev)

    def o(width, dt):
        return jax.ShapeDtypeStruct((n, width), dt), pl.BlockSpec((tm, width), row)

    outs = []
    if ln is not None:
        outs.append(o(D_MODEL, F32))
    outs += [o(512, BF16), o(512, BF16), o(512, BF16),
             o(256, BF16), o(256, BF16), o(256, BF16), o(256, F32), o(256, F32)]
    kv5 = jax.ShapeDtypeStruct((depth, B, T, A_HEADS, A_V_DIM), F32)
    outs += [(kv5, pl.BlockSpec(memory_space=pl.ANY))] * 2
    res = pl.pallas_call(
        functools.partial(_inproj_kernel, apply_ln=ln is not None, has_prev=kv_prev is not None,
                          layer=layer, tm=tm, T=T, nsteps=nsteps),
        grid=(nsteps,),
        in_specs=in_specs,
        out_specs=[s for _, s in outs],
        out_shape=[s for s, _ in outs],
        scratch_shapes=[pltpu.VMEM((2, tm, A_WIDTH), F32), pltpu.VMEM((2, tm, A_WIDTH), F32),
                        pltpu.SemaphoreType.DMA((2,))],
        input_output_aliases=aliases,
        compiler_params=_params(("arbitrary",), 52),
        name="inproj",
    )(*args)
    if ln is None:
        res = [x] + list(res)
    return res


def _lambda(lam_ref, lam_init):
    lv = lam_ref[...]
    s1 = jnp.sum(lv[0:1] * lv[1:2], axis=-1, keepdims=True)
    s2 = jnp.sum(lv[2:3] * lv[3:4], axis=-1, keepdims=True)
    return jnp.exp(s1) - jnp.exp(s2) + lam_init


def _stack_maps(q):
    lane = lax.broadcasted_iota(jnp.int32, q.shape, 1)
    lo = lane < HEAD_DIM
    z = jnp.zeros_like(q)
    return jnp.concatenate([jnp.where(lo, q, z), jnp.where(lo, z, q)], axis=0)


def _diff_finish(o2, t, lam, g, lam_init):
    o = o2[:t] - lam * o2[t:]
    ms = jnp.mean(o * o, axis=-1, keepdims=True)
    return (o * lax.rsqrt(ms + LN_EPS) * g * (1.0 - lam_init)).astype(BF16)


def _attn_prompt_kernel(lam_ref, g_ref, bias_ref, q_ref, k_ref, v_ref, o_ref,
                        s_scr, m_scr, p_scr, vo_scr, bp_scr, bd_scr, *, T, TB, lam_init):
    lam = _lambda(lam_ref, lam_init)
    g = g_ref[...]
    bt = bias_ref[...]
    bt = (bt - bt[0:1, 0:1]) * LOG2E
    bp_scr[...] = bt[:, :TB]
    bd_scr[...] = bt[:, TB:]
    half = TB // 2
    vo_scr[:, :A_V_DIM] = v_ref[...]
    vo_scr[:, A_V_DIM:] = jnp.ones((T, A_V_DIM), BF16)

    for qi in range(T // TB):
        r0 = qi * TB
        buf = qi % 2
        q = q_ref[r0:r0 + TB, :]
        lo = lax.broadcasted_iota(jnp.int32, q.shape, 1) < HEAD_DIM
        zq = jnp.zeros_like(q)
        maps = (jnp.where(lo, q, zq), jnp.where(lo, zq, q))
        for mp in range(2):
            m = None
            for ki in range(qi + 1):
                s = _dot_nt(maps[mp], k_ref[ki * TB:(ki + 1) * TB, :])
                if ki == qi - 1:
                    s = s + bp_scr[...]
                elif ki == qi:
                    row = lax.broadcasted_iota(jnp.int32, (TB, TB), 0)
                    col = lax.broadcasted_iota(jnp.int32, (TB, TB), 1)
                    vis = (col >> _LOG2_CHUNK) <= (row >> _LOG2_CHUNK)
                    s = jnp.where(vis, s + bd_scr[...], NEG_INF)
                s_scr[buf, ki, mp * TB:(mp + 1) * TB, :] = s
                mt = jnp.maximum(s[:, :half], s[:, half:])
                m = mt if m is None else jnp.maximum(m, mt)
            m_scr[buf, mp * TB:(mp + 1) * TB, :] = jnp.broadcast_to(jnp.max(m, axis=1, keepdims=True),
                                                                    (TB, LANES))
        mb = m_scr[buf]
        mb2 = jnp.concatenate([mb, mb], axis=1)
        for ki in range(qi + 1):
            p_scr[buf, :, ki * TB:(ki + 1) * TB] = jnp.exp2(s_scr[buf, ki] - mb2).astype(BF16)
        kk = (qi + 1) * TB
        out = _dot(p_scr[buf, :, 0:kk], vo_scr[0:kk, :])
        o_ref[r0:r0 + TB, :] = _diff_finish(out[:, :A_V_DIM] / out[:, A_V_DIM:], TB, lam, g, lam_init)


def _attn_prompt(lam4, g, bias, q, k, v, B, T, lam_init):
    TB = 2 * LANES
    assert T % TB == 0 and TB % CHUNK == 0
    n = B * T
    blk = pl.BlockSpec((T, A_V_DIM), lambda b, h: (b, h))
    return pl.pallas_call(
        functools.partial(_attn_prompt_kernel, T=T, TB=TB, lam_init=lam_init),
        grid=(B, A_HEADS),
        in_specs=[pl.BlockSpec((4, HEAD_DIM), lambda b, h: (0, 0)),
                  pl.BlockSpec((1, A_V_DIM), lambda b, h: (0, 0)),
                  pl.BlockSpec((None, TB, 2 * TB), lambda b, h: (h, 0, 0)),
                  blk, blk, blk],
        out_specs=blk,
        out_shape=jax.ShapeDtypeStruct((n, A_WIDTH), BF16),
        scratch_shapes=[pltpu.VMEM((2, T // TB, 2 * TB, TB), F32),
                        pltpu.VMEM((2, 2 * TB, LANES), F32),
                        pltpu.VMEM((2, 2 * TB, T), BF16),
                        pltpu.VMEM((T, 2 * A_V_DIM), BF16),
                        pltpu.VMEM((TB, TB), F32),
                        pltpu.VMEM((TB, TB), F32)],
        compiler_params=_params(("arbitrary", "arbitrary"), 32),
        name="attn_prompt",
    )(lam4, g, bias, q, k, v)


def _attn_decode_kernel(lam_ref, g_ref, bias_ref, q_ref, kn_ref, vn_ref, ck_hbm, cv_hbm, o_ref,
                        kc_buf, vc_buf, sems, *, Ts, past, layer, lam_init):
    b = pl.program_id(0)
    h = pl.program_id(1)
    step = b * A_HEADS + h
    nsteps = pl.num_programs(0) * A_HEADS
    slot = step % 2

    def cache_copies(st, s):
        bb = st // A_HEADS
        hh = st % A_HEADS
        return [pltpu.make_async_copy(src.at[layer, bb, :, hh, :], dst.at[s], sems.at[s, j])
                for j, (src, dst) in enumerate(((ck_hbm, kc_buf), (cv_hbm, vc_buf)))]

    @pl.when(step == 0)
    def _():
        for cp in cache_copies(0, 0):
            cp.start()

    @pl.when(step + 1 < nsteps)
    def _():
        for cp in cache_copies(step + 1, 1 - slot):
            cp.start()

    for cp in cache_copies(step, slot):
        cp.wait()
    kc_ref = kc_buf.at[slot]
    vc_ref = vc_buf.at[slot]

    lam = _lambda(lam_ref, lam_init)
    q2 = _stack_maps(q_ref[...])
    bias = bias_ref[...] * LOG2E
    s_p = _dot_nt(q2, kc_ref[...].astype(BF16)).reshape(2, Ts, past) + bias[:, :past][None]
    s_n = _dot_nt(q2, kn_ref[...]).reshape(2, Ts, Ts) + bias[:, past:past + Ts][None]
    s_p = s_p.reshape(2 * Ts, past)
    s_n = s_n.reshape(2 * Ts, Ts)
    m = jnp.maximum(jnp.max(s_p, axis=1, keepdims=True), jnp.max(s_n, axis=1, keepdims=True))
    p_p = jnp.exp2(s_p - m)
    p_n = jnp.exp2(s_n - m)
    l = jnp.sum(p_p, axis=1, keepdims=True) + jnp.sum(p_n, axis=1, keepdims=True)
    acc = _dot(p_p.astype(BF16), vc_ref[...].astype(BF16)) + _dot(p_n.astype(BF16), vn_ref[...])
    o_ref[...] = _diff_finish(acc / l, Ts, lam, g_ref[...], lam_init)


def _attn_decode(lam4, g, bias, q, cache_k, cache_v, kn, vn, layer, Bs, Ts, lam_init):
    past = cache_k.shape[2]
    assert past % CHUNK == 0 and Ts <= CHUNK
    padk = bias.shape[2]
    new = pl.BlockSpec((Ts, A_V_DIM), lambda b, h: (b, h))
    cache = pl.BlockSpec(memory_space=pl.ANY)
    return pl.pallas_call(
        functools.partial(_attn_decode_kernel, Ts=Ts, past=past, layer=layer, lam_init=lam_init),
        grid=(Bs, A_HEADS),
        in_specs=[pl.BlockSpec((4, HEAD_DIM), lambda b, h: (0, 0)),
                  pl.BlockSpec((1, A_V_DIM), lambda b, h: (0, 0)),
                  pl.BlockSpec((None, Ts, padk), lambda b, h: (h, 0, 0)),
                  new, new, new, cache, cache],
        out_specs=new,
        out_shape=jax.ShapeDtypeStruct((Bs * Ts, A_WIDTH), BF16),
        scratch_shapes=[pltpu.VMEM((2, past, A_V_DIM), F32), pltpu.VMEM((2, past, A_V_DIM), F32),
                        pltpu.SemaphoreType.DMA((2, 2))],
        compiler_params=_params(("arbitrary", "arbitrary"), 40),
        name="attn_decode",
    )(lam4, g, bias, q, kn, vn, cache_k, cache_v)


def _ret_kernel(q_ref, k_ref, v_ref, g_ref, s0_ref, dm_ref, qd_ref, kd_ref, cd_ref,
                o_ref, sn_ref, *, T, C):
    z = jnp.zeros((R_KEY_DIM, R_V_DIM), F32)
    state = jnp.concatenate([jnp.concatenate([s0_ref[0], z], axis=1),
                             jnp.concatenate([z, s0_ref[1]], axis=1)], axis=0)
    r = lax.broadcasted_iota(jnp.int32, (LANES, LANES), 0)
    c = lax.broadcasted_iota(jnp.int32, (LANES, LANES), 1)
    same_head = (r >> _LOG2_CHUNK) == (c >> _LOG2_CHUNK)
    ones_bd = jnp.where(same_head, 1.0, 0.0).astype(BF16)
    lo = lax.broadcasted_iota(jnp.int32, (C, LANES), 1) < R_V_DIM
    cd = cd_ref[...]

    def chunk(n, state):
        r0 = n * C
        q = q_ref[pl.ds(r0, C), :]
        k = k_ref[pl.ds(r0, C), :]
        v = v_ref[pl.ds(r0, C), :]
        p = (_dot_nt(_stack_maps(q), k) * dm_ref[...]).astype(BF16)
        o2 = _dot(p, v)
        inner = jnp.where(lo, o2[:C], o2[C:])
        qd = (q.astype(F32) * qd_ref[...]).astype(BF16)
        o = inner + _dot(qd, state.astype(BF16))
        vk = (v.astype(F32) * kd_ref[...]).astype(BF16)
        new_state = cd * state + jnp.where(same_head, _dot_tn(k, vk), 0.0)
        oo = o * o
        hi = oo.astype(BF16)
        lo_part = (oo - hi.astype(F32)).astype(BF16)
        ss = _dot(hi, ones_bd) + _dot(lo_part, ones_bd)
        out = o * lax.rsqrt(ss * (1.0 / R_V_DIM) + LN_EPS) * g_ref[pl.ds(r0, C), :]
        o_ref[pl.ds(r0, C), :] = out.astype(BF16)
        return new_state

    for n in range(T // C):
        state = chunk(n, state)
    sn_ref[0] = state[:R_KEY_DIM, :R_V_DIM]
    sn_ref[1] = state[R_KEY_DIM:, R_V_DIM:]


def _retention(rq, rk, rv, rg, state0, tabs, B, T, C):
    dm, qd, kd, cd = tabs
    blk = pl.BlockSpec((T, LANES), lambda b, hp: (b, hp))
    st = pl.BlockSpec((None, 2, R_KEY_DIM, R_V_DIM), lambda b, hp: (b, hp, 0, 0))
    return pl.pallas_call(
        functools.partial(_ret_kernel, T=T, C=C),
        grid=(B, 2),
        in_specs=[blk, blk, blk, blk, st,
                  pl.BlockSpec((None, 2 * C, C), lambda b, hp: (hp, 0, 0)),
                  pl.BlockSpec((None, C, LANES), lambda b, hp: (hp, 0, 0)),
                  pl.BlockSpec((None, C, LANES), lambda b, hp: (hp, 0, 0)),
                  pl.BlockSpec((None, 1, LANES), lambda b, hp: (hp, 0, 0))],
        out_specs=[blk, st],
        out_shape=[jax.ShapeDtypeStruct((B * T, R_WIDTH), BF16),
                   jax.ShapeDtypeStruct((B, R_HEADS, R_KEY_DIM, R_V_DIM), F32)],
        compiler_params=_params(("arbitrary", "arbitrary"), 32),
        name="retention",
    )(rq, rk, rv, rg, state0, dm, qd, kd, cd)


def _retention_tables(C):
    h = jnp.arange(R_HEADS, dtype=F32)
    log_g = jnp.log1p(-jnp.exp2(-5.0 - h))
    i = jnp.arange(C, dtype=F32)
    diff = i[:, None] - i[None, :]
    dmask = jnp.where(diff >= 0, jnp.exp(jnp.maximum(diff, 0.0)[None] * log_g[:, None, None]), 0.0)
    dm = dmask.reshape(2, 2 * C, C)
    q_dec = jnp.exp((i + 1.0)[None, :] * log_g[:, None])
    k_dec = jnp.exp((C - 1 - i)[None, :] * log_g[:, None])
    c_dec = jnp.exp(C * log_g)

    def lanes(t):
        t = jnp.repeat(t[:, :, None], R_V_DIM, axis=2).reshape(2, 2, C, R_V_DIM)
        return jnp.concatenate([t[:, 0], t[:, 1]], axis=-1)

    cd = jnp.repeat(c_dec[:, None], R_V_DIM, axis=1).reshape(2, 1, LANES)
    return dm, lanes(q_dec), lanes(k_dec), cd


_CONV_PAD = 32


def _conv_kernel(u_ref, h_ref, w_ref, b_ref, g_ref, be_ref, o_ref, t_ref, up_scr, *, T, RT):
    hist = CONV_WIDTH - 1
    off = _CONV_PAD - hist
    up_scr[off:_CONV_PAD, :] = h_ref[...]
    up_scr[_CONV_PAD:_CONV_PAD + T, :] = u_ref[...]
    bias = b_ref[...]
    g = g_ref[...]
    be = be_ref[...]
    win_rows = RT + _CONV_PAD
    for t0 in range(0, T, RT):
        win = up_scr[t0:t0 + win_rows, :]
        acc = jnp.zeros((RT, C_WIDTH), F32)
        for s in range(SUBLANES):
            rolled = win if s == 0 else pltpu.roll(win, win_rows - s, 0)
            for a in range(_CONV_PAD // SUBLANES + 1):
                j = a * SUBLANES + s - off
                if 0 <= j < CONV_WIDTH:
                    acc = acc + rolled[a * SUBLANES:a * SUBLANES + RT, :] * w_ref[j:j + 1, :]
        y = _layer_norm(acc + bias, g, be)
        o_ref[t0:t0 + RT, :] = (y * _sigmoid(y)).astype(BF16)
    t_ref[...] = up_scr[T + off:T + _CONV_PAD, :]


def _conv(u, hist, dw_w, dw_b, ln_g, ln_b, B, T):
    RT = min(T, LANES)
    assert T % RT == 0
    hl = CONV_WIDTH - 1
    vec = pl.BlockSpec((1, C_WIDTH), lambda b: (0, 0))
    hb = pl.BlockSpec((None, hl, C_WIDTH), lambda b: (b, 0, 0))
    return pl.pallas_call(
        functools.partial(_conv_kernel, T=T, RT=RT),
        grid=(B,),
        in_specs=[pl.BlockSpec((T, C_WIDTH), lambda b: (b, 0)), hb,
                  pl.BlockSpec((CONV_WIDTH, C_WIDTH), lambda b: (0, 0)), vec, vec, vec],
        out_specs=[pl.BlockSpec((T, C_WIDTH), lambda b: (b, 0)), hb],
        out_shape=[jax.ShapeDtypeStruct((B * T, C_WIDTH), BF16),
                   jax.ShapeDtypeStruct((B, hl, C_WIDTH), F32)],
        scratch_shapes=[pltpu.VMEM((T + _CONV_PAD, C_WIDTH), F32)],
        compiler_params=_params(("arbitrary",), 32),
        name="conv",
    )(u, hist, dw_w, dw_b.reshape(1, C_WIDTH), ln_g.reshape(1, C_WIDTH), ln_b.reshape(1, C_WIDTH))


def _first_index(vals, target):
    idx = jnp.full(target.shape, len(vals) - 1, jnp.int32)
    for j in range(len(vals) - 2, -1, -1):
        idx = jnp.where(vals[j] == target, j, idx)
    return idx


def _select(idx, vals):
    out = vals[-1]
    for j in range(len(vals) - 2, -1, -1):
        out = jnp.where(idx == j, vals[j], out)
    return out


def _route(logits_t):
    rows = [logits_t[e:e + 1, :] for e in range(N_EXPERTS)]
    m = functools.reduce(jnp.maximum, rows)
    ex = [jnp.exp(r - m) for r in rows]
    z = functools.reduce(jnp.add, ex)
    sc = [e / z for e in ex]
    v1s, v2s, i1s, i2s, gss = [], [], [], [], []
    for g in range(N_GROUPS):
        a = sc[g * EXPERTS_PER_GROUP:(g + 1) * EXPERTS_PER_GROUP]
        v1 = functools.reduce(jnp.maximum, a)
        i1 = _first_index(a, v1)
        rest = [jnp.where(i1 == j, -1.0, a[j]) for j in range(EXPERTS_PER_GROUP)]
        v2 = functools.reduce(jnp.maximum, rest)
        i2 = _first_index(rest, v2)
        v1s.append(v1); v2s.append(v2); i1s.append(i1); i2s.append(i2); gss.append(v1 + v2)
    grp = _first_index(gss, functools.reduce(jnp.maximum, gss))
    v1 = _select(grp, v1s)
    v2 = _select(grp, v2s)
    e1 = _select(grp, i1s) + grp * EXPERTS_PER_GROUP
    e2 = _select(grp, i2s) + grp * EXPERTS_PER_GROUP
    den = v1 + v2
    return e1, e2, v1 / den, v2 / den


def _expert_onehot(e1, e2):
    rows = [jnp.where(e1 == e, 1.0, 0.0) + jnp.where(e2 == e, 1.0, 0.0) for e in range(N_EXPERTS)]
    return jnp.concatenate(rows, axis=0)


def _outproj_kernel(a_ref, r_ref, c_ref, x_ref, w_ref, g_ref, b_ref, wrh_ref, wrl_ref, br_ref,
                    x1_ref, x1b_ref, ri_ref, rw_ref, cnt_ref, *, alpha, tb, sub):
    tm = x_ref.shape[0]
    ones = jnp.ones((SUBLANES, sub), BF16)
    counts = []
    for h in range(tm // sub):
        rs = slice(h * sub, (h + 1) * sub)
        mix = (_dot(a_ref[rs, :], w_ref[0:A_WIDTH, :])
               + _dot(r_ref[rs, :], w_ref[A_WIDTH:A_WIDTH + R_WIDTH, :])
               + _dot(c_ref[rs, :], w_ref[A_WIDTH + R_WIDTH:, :]))
        x1 = _layer_norm(alpha * x_ref[rs, :] + mix, g_ref[...], b_ref[...])
        x_hi = x1.astype(BF16)
        x1_ref[rs, :] = x1
        x1b_ref[rs, :] = x_hi
        x_lo = (x1 - x_hi.astype(F32)).astype(BF16)
        logits = _dot(x_hi, wrh_ref[...]) + _dot(x_lo, wrh_ref[...]) + _dot(x_hi, wrl_ref[...])
        logits_t = logits.T[0:N_EXPERTS, :] + br_ref[...]
        e1, e2, w1, w2 = _route(logits_t)
        ri_ref[:, rs] = jnp.concatenate([e1, e2, jnp.zeros((SUBLANES - 2, sub), jnp.int32)], axis=0)
        rw_ref[:, rs] = jnp.concatenate([w1, w2, jnp.zeros((SUBLANES - 2, sub), F32)], axis=0)
        counts.append(_dot_nt(ones, _expert_onehot(e1, e2).astype(BF16)))
    per = tb // sub
    for k in range(tm // tb):
        c = functools.reduce(jnp.add, counts[k * per:(k + 1) * per])
        c = jnp.concatenate([c, jnp.zeros((SUBLANES, LANES - N_EXPERTS), F32)], axis=1)
        cnt_ref[k * SUBLANES:(k + 1) * SUBLANES, :] = c.astype(jnp.int32)


def _outproj(attn, ret, conv, x, w_bf, ln_g, ln_b, w_router, br, alpha, tm, tb):
    n = x.shape[0]
    wr = jnp.pad(w_router, ((0, 0), (0, LANES - N_EXPERTS)))
    wr_hi = wr.astype(BF16)
    wr_lo = (wr - wr_hi.astype(F32)).astype(BF16)
    row = lambda i: (i, 0)
    col = lambda i: (0, i)
    const = lambda i: (0, 0)
    nsub = tm // tb
    return pl.pallas_call(
        functools.partial(_outproj_kernel, alpha=alpha, tb=tb, sub=min(tb, 2 * LANES)),
        grid=(n // tm,),
        in_specs=[pl.BlockSpec((tm, A_WIDTH), row), pl.BlockSpec((tm, R_WIDTH), row),
                  pl.BlockSpec((tm, C_WIDTH), row), pl.BlockSpec((tm, D_MODEL), row),
                  pl.BlockSpec((D_MODEL, D_MODEL), const),
                  pl.BlockSpec((1, D_MODEL), const), pl.BlockSpec((1, D_MODEL), const),
                  pl.BlockSpec((D_MODEL, LANES), const), pl.BlockSpec((D_MODEL, LANES), const),
                  pl.BlockSpec((N_EXPERTS, 1), const)],
        out_specs=[pl.BlockSpec((tm, D_MODEL), row), pl.BlockSpec((tm, D_MODEL), row),
                   pl.BlockSpec((SUBLANES, tm), col), pl.BlockSpec((SUBLANES, tm), col),
                   pl.BlockSpec((nsub * SUBLANES, LANES), row)],
        out_shape=[jax.ShapeDtypeStruct((n, D_MODEL), F32), jax.ShapeDtypeStruct((n, D_MODEL), BF16),
                   jax.ShapeDtypeStruct((SUBLANES, n), jnp.int32),
                   jax.ShapeDtypeStruct((SUBLANES, n), F32),
                   jax.ShapeDtypeStruct((n // tb * SUBLANES, LANES), jnp.int32)],
        compiler_params=_params(("arbitrary",), 40),
        name="outproj",
    )(attn, ret, conv, x, w_bf, ln_g.reshape(1, D_MODEL), ln_b.reshape(1, D_MODEL), wr_hi, wr_lo, br)


_UNIT = 16
_XS_WIDTH = D_MODEL + LANES
_FFN_ROWS = 512
_DISPATCH_TOKENS = 512


def _sorted_positions(e1, e2, lo_ref, base, tb):
    onehot = _expert_onehot(e1, e2).astype(BF16)
    r = lax.broadcasted_iota(jnp.int32, (tb, tb), 0)
    c = lax.broadcasted_iota(jnp.int32, (tb, tb), 1)
    earlier = jnp.where(r < c, 1.0, 0.0).astype(BF16)
    rank = _dot(onehot, earlier).astype(jnp.int32)
    pos1 = jnp.zeros(e1.shape, jnp.int32)
    pos2 = jnp.zeros(e1.shape, jnp.int32)
    for e in range(N_EXPERTS):
        p = rank[e:e + 1, :] + lo_ref[base + e]
        pos1 = jnp.where(e1 == e, p, pos1)
        pos2 = jnp.where(e2 == e, p, pos2)
    return pos1, pos2


def _permutation(pos1, pos2, rows):
    j = lax.broadcasted_iota(jnp.int32, (rows, pos1.shape[1]), 0)
    return jnp.where(j == pos1, 1.0, jnp.where(j == pos2, 1.0, 0.0)).astype(BF16)


def _dispatch_kernel(lo_ref, urow_ref, ri_ref, rw_ref, x_ref, xs_hbm, pos_ref,
                     stage, sems, *, nblk, tb, cap, spare_row):
    i = pl.program_id(0)
    slot = i % 2
    units = cap // _UNIT

    def wait_slot(s):
        pltpu.make_async_copy(stage.at[s], xs_hbm.at[pl.ds(0, cap), :], sems.at[s]).wait()

    @pl.when(i >= 2)
    def _():
        wait_slot(slot)

    e1 = ri_ref[0:1, :]
    e2 = ri_ref[1:2, :]
    w1 = rw_ref[0:1, :]
    w2 = rw_ref[1:2, :]
    pos1, pos2 = _sorted_positions(e1, e2, lo_ref, i * N_EXPERTS, tb)
    pos_ref[...] = jnp.concatenate([pos1, pos2, jnp.zeros((SUBLANES - 2, tb), jnp.int32)], axis=0)
    perm = _permutation(pos1, pos2, cap)
    stage[slot, :, 0:D_MODEL] = _dot(perm, x_ref[...]).astype(BF16)
    j = lax.broadcasted_iota(jnp.int32, (cap, tb), 0)
    gate = jnp.sum(jnp.where(j == pos1, w1, jnp.where(j == pos2, w2, 0.0)), axis=1, keepdims=True)
    g0 = gate.astype(BF16).astype(F32)
    g1 = (gate - g0).astype(BF16).astype(F32)
    g2 = gate - g0 - g1
    lane = lax.broadcasted_iota(jnp.int32, (cap, LANES), 1)
    pieces = jnp.where(lane == 0, g0, jnp.where(lane == 1, g1, jnp.where(lane == 2, g2, 0.0)))
    stage[slot, :, D_MODEL:_XS_WIDTH] = pieces.astype(BF16)

    def body(u, c):
        t = urow_ref[i * units + u]
        s = pl.multiple_of(u * _UNIT, _UNIT)
        d = pl.multiple_of(jnp.where(t >= 0, t, spare_row + slot * cap + s), _UNIT)
        pltpu.make_async_copy(stage.at[slot, pl.ds(s, _UNIT), :], xs_hbm.at[pl.ds(d, _UNIT), :],
                              sems.at[slot]).start()
        return c

    lax.fori_loop(0, units, body, 0)

    @pl.when(i == nblk - 1)
    def _():
        wait_slot(slot)
        if nblk > 1:
            wait_slot(1 - slot)


def _ffn_kernel(te_ref, tv_ref, nt_ref, xs_ref, wg_ref, wu_ref, wd_ref, ys_ref, wg_b, wu_b, wd_b):
    j = pl.program_id(0)
    jm = jnp.maximum(j, 1)

    @pl.when((j == 0) | (te_ref[j] != te_ref[jm - 1]))
    def _():
        wg_b[...] = wg_ref[...].astype(BF16)
        wu_b[...] = wu_ref[...].astype(BF16)
        wd_b[...] = wd_ref[...].astype(BF16)

    @pl.when(j < nt_ref[0])
    def _():
        rows = xs_ref.shape[0]
        valid = lax.broadcasted_iota(jnp.int32, (rows, LANES), 0) < tv_ref[j]
        zero = jnp.zeros((rows, LANES), BF16)
        x = jnp.concatenate([jnp.where(valid, xs_ref[:, c * LANES:(c + 1) * LANES], zero)
                             for c in range(D_MODEL // LANES)], axis=1)
        gp = jnp.where(valid, xs_ref[:, D_MODEL:_XS_WIDTH], zero).astype(F32)
        g = gp[:, 0:1] + gp[:, 1:2] + gp[:, 2:3]
        hg = _dot(x, wg_b[...])
        hu = _dot(x, wu_b[...])
        h = hg * _sigmoid(hg) * hu * g
        ys_ref[...] = _dot(h.astype(BF16), wd_b[...]).astype(BF16)


def _combine_kernel(urow_ref, pos_ref, x1_ref, p_ref, g_ref, b_ref, wpg_ref, bpg_ref, wpp_ref,
                    ys_hbm, out_ref, stage, sems, *, nblk, tb, cap, alpha):
    i = pl.program_id(0)
    slot = i % 2
    units = cap // _UNIT

    def fetch(blk, s):
        def body(u, c):
            sr = pl.multiple_of(jnp.maximum(urow_ref[blk * units + u], 0), _UNIT)
            ds = pl.multiple_of(u * _UNIT, _UNIT)
            pltpu.make_async_copy(ys_hbm.at[pl.ds(sr, _UNIT), :], stage.at[s, pl.ds(ds, _UNIT), :],
                                  sems.at[s]).start()
            return c

        lax.fori_loop(0, units, body, 0)

    @pl.when(i == 0)
    def _():
        fetch(0, 0)

    @pl.when(i + 1 < nblk)
    def _():
        fetch(i + 1, 1 - slot)

    pltpu.make_async_copy(ys_hbm.at[pl.ds(0, cap), :], stage.at[slot], sems.at[slot]).wait()
    perm = _permutation(pos_ref[0:1, :], pos_ref[1:2, :], cap)
    y = _dot_tn(perm, stage[slot])
    x2 = _layer_norm(alpha * x1_ref[...] + y, g_ref[...], b_ref[...])
    gate = _sigmoid(_dot(x2.astype(BF16), wpg_ref[...]) + bpg_ref[...])
    out_ref[...] = x2 + gate * _dot(p_ref[...].astype(BF16), wpp_ref[...])


def _moe_plan(cnt, nblk, tb):
    cnt = cnt.reshape(nblk, SUBLANES, LANES)[:, 0, :N_EXPERTS]
    pc = (cnt + (_UNIT - 1)) // _UNIT * _UNIT
    lo = jnp.cumsum(pc, axis=1) - pc
    tot = jnp.sum(pc, axis=0)
    reg = (tot + (_FFN_ROWS - 1)) // _FFN_ROWS * _FFN_ROWS
    rstart = jnp.cumsum(reg) - reg
    gs = rstart[None, :] + jnp.cumsum(pc, axis=0) - pc
    tiles_e = reg // _FFN_ROWS
    tile_end = jnp.cumsum(tiles_e)
    max_rows = 2 * nblk * tb + nblk * N_EXPERTS * _UNIT + N_EXPERTS * _FFN_ROWS
    max_tiles = -(-max_rows // _FFN_ROWS)
    tj =jnp.arange(max_tiles, dtype=jnp.int32)
    te = jnp.minimum(jnp.sum((tj[:, None] >= tile_end[None, :]).astype(jnp.int32), axis=1), N_EXPERTS - 1)
    tv = jnp.clip(tot[te] - (tj - (tile_end - tiles_e)[te]) * _FFN_ROWS, 0, _FFN_ROWS)
    nt = tile_end[-1:].astype(jnp.int32)
    cap = 2 * tb + N_EXPERTS * _UNIT
    urow0 = jnp.arange(cap // _UNIT, dtype=jnp.int32) * _UNIT
    run = jnp.sum((urow0[None, :, None] >= (lo + pc)[:, None, :]).astype(jnp.int32), axis=2)
    pick = run[:, :, None] == jnp.arange(N_EXPERTS, dtype=jnp.int32)[None, None, :]
    urow = jnp.sum(jnp.where(pick, (gs - lo)[:, None, :], 0), axis=2) + urow0[None, :]
    urow = jnp.where(urow0[None, :] < jnp.sum(pc, axis=1, keepdims=True), urow, -1)
    flat = lambda a: a.reshape(-1).astype(jnp.int32)
    return flat(lo), flat(urow), te.astype(jnp.int32), tv.astype(jnp.int32), nt, max_tiles, cap


def _moe(x1b, x1, ri, rw, cnt, p, wg, wu, wd, layer, ln_g, ln_b, wpg, bpg, wpp, alpha, tb):
    n = x1.shape[0]
    nblk = n // tb
    lo, urow, te, tv, nt, max_tiles, cap = _moe_plan(cnt, nblk, tb)
    max_rows = max_tiles * _FFN_ROWS

    xs, pos = pl.pallas_call(
        functools.partial(_dispatch_kernel, nblk=nblk, tb=tb, cap=cap, spare_row=max_rows),
        grid_spec=pltpu.PrefetchScalarGridSpec(
            num_scalar_prefetch=2,
            grid=(nblk,),
            in_specs=[pl.BlockSpec((SUBLANES, tb), lambda i, *_: (0, i)),
                      pl.BlockSpec((SUBLANES, tb), lambda i, *_: (0, i)),
                      pl.BlockSpec((tb, D_MODEL), lambda i, *_: (i, 0))],
            out_specs=[pl.BlockSpec(memory_space=pl.ANY),
                       pl.BlockSpec((SUBLANES, tb), lambda i, *_: (0, i))],
            scratch_shapes=[pltpu.VMEM((2, cap, _XS_WIDTH), BF16),
                            pltpu.SemaphoreType.DMA((2,))]),
        out_shape=[jax.ShapeDtypeStruct((max_rows + 2 * cap, _XS_WIDTH), BF16),
                   jax.ShapeDtypeStruct((SUBLANES, n), jnp.int32)],
        compiler_params=_params(("arbitrary",), 32),
        name="moe_dispatch",
    )(lo, urow, ri, rw, x1b)

    def tile(j, te_ref, tv_ref, nt_ref):
        return jnp.minimum(j, nt_ref[0] - 1)

    ys = pl.pallas_call(
        _ffn_kernel,
        grid_spec=pltpu.PrefetchScalarGridSpec(
            num_scalar_prefetch=3,
            grid=(max_tiles,),
            in_specs=[pl.BlockSpec((_FFN_ROWS, _XS_WIDTH), lambda j, *s: (tile(j, *s), 0)),
                      pl.BlockSpec((None, None, D_MODEL, D_EXPERT),
                                   lambda j, *s: (layer, s[0][tile(j, *s)], 0, 0)),
                      pl.BlockSpec((None, None, D_MODEL, D_EXPERT),
                                   lambda j, *s: (layer, s[0][tile(j, *s)], 0, 0)),
                      pl.BlockSpec((None, None, D_EXPERT, D_MODEL),
                                   lambda j, *s: (layer, s[0][tile(j, *s)], 0, 0))],
            out_specs=pl.BlockSpec((_FFN_ROWS, D_MODEL), lambda j, *s: (tile(j, *s), 0)),
            scratch_shapes=[pltpu.VMEM((D_MODEL, D_EXPERT), BF16), pltpu.VMEM((D_MODEL, D_EXPERT), BF16),
                            pltpu.VMEM((D_EXPERT, D_MODEL), BF16)]),
        out_shape=jax.ShapeDtypeStruct((max_rows, D_MODEL), BF16),
        compiler_params=_params(("arbitrary",), 40),
        name="moe_ffn",
    )(te, tv, nt, xs, wg, wu, wd)

    const = lambda i, *_: (0, 0)
    return pl.pallas_call(
        functools.partial(_combine_kernel, nblk=nblk, tb=tb, cap=cap, alpha=alpha),
        grid_spec=pltpu.PrefetchScalarGridSpec(
            num_scalar_prefetch=1,
            grid=(nblk,),
            in_specs=[pl.BlockSpec((SUBLANES, tb), lambda i, *_: (0, i)),
                      pl.BlockSpec((tb, D_MODEL), lambda i, *_: (i, 0)),
                      pl.BlockSpec((None, tb, PLE_DIM), lambda i, *_: (layer, i, 0)),
                      pl.BlockSpec((1, D_MODEL), const), pl.BlockSpec((1, D_MODEL), const),
                      pl.BlockSpec((D_MODEL, D_MODEL), const), pl.BlockSpec((1, D_MODEL), const),
                      pl.BlockSpec((PLE_DIM, D_MODEL), const),
                      pl.BlockSpec(memory_space=pl.ANY)],
            out_specs=pl.BlockSpec((tb, D_MODEL), lambda i, *_: (i, 0)),
            scratch_shapes=[pltpu.VMEM((2, cap, D_MODEL), BF16),
                            pltpu.SemaphoreType.DMA((2,))]),
        out_shape=jax.ShapeDtypeStruct((n, D_MODEL), F32),
        compiler_params=_params(("arbitrary",), 40),
        name="moe_combine",
    )(urow, pos, x1, p, ln_g.reshape(1, D_MODEL), ln_b.reshape(1, D_MODEL),
      wpg, bpg.reshape(1, D_MODEL), wpp, ys)


def _rope_tables(pos, rows):
    half = R_KEY_DIM // 2
    inv_freq = 1.0 / (ROPE_BASE ** jnp.linspace(0.0, 1.0, half, dtype=jnp.float32))
    ang = pos.astype(jnp.float32)[:, None] * inv_freq[None, :]
    cos = jnp.cos(ang)
    sin = jnp.sin(ang)
    cos_t = jnp.tile(cos, (rows // pos.shape[0], 4))
    sin_t = jnp.tile(jnp.concatenate([-sin, sin], axis=1), (rows // pos.shape[0], 2))
    return cos_t, sin_t


def kernel(x_prompt, x_sample, p_prompt, p_sample, cache_k, cache_v, state_ret, state_conv, ln_emb_g, ln_emb_b, rel_bias, w_router, b_router, w_in, lam_q1, lam_k1, lam_q2, lam_k2, subln_g, dw_w, dw_b, conv_ln_g, conv_ln_b, w_out, ln1_g, ln1_b, w_exp_gate, w_exp_up, w_exp_down, ln2_g, ln2_b, w_ple_gate, b_ple_gate, w_ple_proj):
    B, T, D = x_prompt.shape
    Bs, Ts, _ = x_sample.shape
    depth = w_in.shape[0]
    past = cache_k.shape[2]
    n_p, n_s = B * T, Bs * Ts
    alpha = (2 * depth) ** 0.25

    tm_p = 512 if n_p % 512 == 0 else n_p
    tm_s = n_s
    TB = 2 * LANES
    c_p = min(T, 2 * LANES)

    pos_p = jnp.arange(T, dtype=jnp.int32)
    pos_s = past + jnp.arange(Ts, dtype=jnp.int32)
    rope_p = _rope_tables(pos_p, max(T, tm_p))
    rope_s = _rope_tables(pos_s, max(Ts, tm_s))
    rel_p = (jnp.arange(2 * TB, dtype=jnp.int32)[None, :] - TB) - jnp.arange(TB, dtype=jnp.int32)[:, None]
    padk = -(-(past + Ts) // LANES) * LANES
    rel_s = jnp.arange(padk, dtype=jnp.int32)[None, :] - pos_s[:, None]
    bias_p = _bias_table(rel_bias, _t5_bucket(rel_p))
    bias_s = _bias_table(rel_bias, _t5_bucket(rel_s))
    tabs_p = _retention_tables(c_p)
    tabs_s = _retention_tables(Ts)

    br = b_router.reshape(N_EXPERTS, 1)
    ret0_p = jnp.zeros((B, R_HEADS, R_KEY_DIM, R_V_DIM), F32)
    conv0_p = jnp.zeros((B, CONV_WIDTH - 1, C_WIDTH), F32)

    xp = x_prompt.reshape(n_p, D)
    xs = x_sample.reshape(n_s, D)
    outs = {k: [] for k in ("rp", "cp", "rs", "cs")}
    kv_p = kv_s = None
    for l in range(depth):
        lam_init = 0.8 - 0.6 * math.exp(-0.3 * l)
        lam4 = jnp.stack([lam_q1[l], lam_k1[l], lam_q2[l], lam_k2[l]])
        g_sub = subln_g[l].reshape(1, A_V_DIM)
        w_in_b = w_in[l].astype(BF16)
        w_out_b = w_out[l].astype(BF16)
        wpg, wpp = w_ple_gate[l].astype(BF16), w_ple_proj[l].astype(BF16)
        ln = (ln_emb_g, ln_emb_b) if l == 0 else None

        def channel(x, attn, ret, conv, p_l, tm):
            tb = min(tm, _DISPATCH_TOKENS)
            x1, x1b, ri, rw, cnt = _outproj(attn, ret, conv, x, w_out_b, ln1_g[l], ln1_b[l], w_router, br,
                                            alpha, tm, tb)
            return _moe(x1b, x1, ri, rw, cnt, p_l, w_exp_gate, w_exp_up, w_exp_down, l, ln2_g[l], ln2_b[l],
                        wpg, b_ple_gate[l], wpp, alpha, tb)

        xp, q, kb, vb, rq, rk, rv, rg, u, k5, v5 = _inproj(xp, w_in_b, rope_p[0], rope_p[1], ln, tm_p,
                                                           kv_p, l, depth, B, T)
        kv_p = (k5, v5)
        attn = _attn_prompt(lam4, g_sub, bias_p, q, kb, vb, B, T, lam_init)
        ret, rstate = _retention(rq, rk, rv, rg, ret0_p, tabs_p, B, T, c_p)
        conv, ctail = _conv(u, conv0_p, dw_w[l], dw_b[l], conv_ln_g[l], conv_ln_b[l], B, T)
        xp = channel(xp, attn, ret, conv, p_prompt.reshape(depth, n_p, PLE_DIM), tm_p)
        outs["rp"].append(rstate)
        outs["cp"].append(ctail)

        xs, q, kb, vb, rq, rk, rv, rg, u, k5, v5 = _inproj(xs, w_in_b, rope_s[0], rope_s[1], ln, tm_s,
                                                           kv_s, l, depth, Bs, Ts)
        kv_s = (k5, v5)
        attn = _attn_decode(lam4, g_sub, bias_s, q, cache_k, cache_v, kb, vb, l, Bs, Ts, lam_init)
        ret, rstate = _retention(rq, rk, rv, rg, state_ret[l], tabs_s, Bs, Ts, Ts)
        conv, ctail = _conv(u, state_conv[l], dw_w[l], dw_b[l], conv_ln_g[l], conv_ln_b[l], Bs, Ts)
        xs = channel(xs, attn, ret, conv, p_sample.reshape(depth, n_s, PLE_DIM), tm_s)
        outs["rs"].append(rstate)
        outs["cs"].append(ctail)

    return (xp.reshape(B, T, D), xs.reshape(Bs, Ts, D),
            kv_p[0], kv_p[1], jnp.stack(outs["rp"]), jnp.stack(outs["cp"]),
            kv_s[0], kv_s[1], jnp.stack(outs["rs"]), jnp.stack(outs["cs"]))
```

```python
import functools
import math

import jax
import jax.numpy as jnp
from jax import lax
from jax.experimental import pallas as pl
from jax.experimental.pallas import tpu as pltpu

F32 = jnp.float32
BF16 = jnp.bfloat16

D_MODEL = 1024
CHUNK = 64
HEAD_DIM = 64
A_HEADS = 4
A_V_DIM = 128
A_WIDTH = 512
R_HEADS = 4
R_KEY_DIM = 64
R_V_DIM = 64
R_WIDTH = 256
C_WIDTH = 256
CONV_WIDTH = 31
IN_WIDTH = 3072
N_BUCKETS = 32
MAX_DISTANCE = 128
ROPE_BASE = 10000.0
N_EXPERTS = 16
N_GROUPS = 4
EXPERTS_PER_GROUP = 4
D_EXPERT = 512
PLE_DIM = 256
LN_EPS = 1e-5
NEG_INF = -1e30
LOG2E = 1.4426950408889634

LANES = 128
SUBLANES = 8
_LOG2_CHUNK = 6
assert CHUNK == HEAD_DIM == R_KEY_DIM == R_V_DIM == 1 << _LOG2_CHUNK
MIB = 1024 * 1024

_OFF_Q, _OFF_K, _OFF_V = 0, 512, 1024
_OFF_RQ, _OFF_RK, _OFF_RV, _OFF_RG, _OFF_C = 1536, 1792, 2048, 2304, 2560


def _params(sem, vmem_mib):
    return pltpu.CompilerParams(dimension_semantics=sem, vmem_limit_bytes=vmem_mib * MIB)


def _layer_norm(x, g, b):
    mu = jnp.mean(x, axis=-1, keepdims=True)
    xc = x - mu
    var = jnp.mean(xc * xc, axis=-1, keepdims=True)
    return xc * lax.rsqrt(var + LN_EPS) * g + b


def _sigmoid(x):
    return 1.0 / (1.0 + jnp.exp(-x))


def _dot(a, b):
    return jnp.dot(a, b, preferred_element_type=F32)


def _dot_nt(a, b):
    return lax.dot_general(a, b, (((1,), (1,)), ((), ())), preferred_element_type=F32)


def _dot_tn(a, b):
    return lax.dot_general(a, b, (((0,), (0,)), ((), ())), preferred_element_type=F32)


def _bias_kernel(relb_ref, idx_ref, out_ref):
    idx = idx_ref[...]
    for h in range(A_HEADS):
        acc = jnp.zeros(idx.shape, F32)
        for b in range(N_BUCKETS):
            acc = jnp.where(idx == b, relb_ref[b, h], acc)
        out_ref[h] = acc


def _bias_table(rel_bias, idx):
    r, c = idx.shape
    return pl.pallas_call(
        _bias_kernel,
        out_shape=jax.ShapeDtypeStruct((A_HEADS, r, c), F32),
        in_specs=[pl.BlockSpec(memory_space=pltpu.SMEM),
                  pl.BlockSpec(memory_space=pltpu.VMEM)],
        out_specs=pl.BlockSpec(memory_space=pltpu.VMEM),
        name="bias_table",
    )(rel_bias, idx)


def _t5_bucket(rel):
    nb = N_BUCKETS // 2
    max_exact = nb // 2
    n = jnp.abs(rel)
    nf = jnp.maximum(n, 1).astype(jnp.float32)
    large = max_exact + (jnp.log(nf / max_exact) / math.log(MAX_DISTANCE / max_exact)
                         * (nb - max_exact)).astype(jnp.int32)
    large = jnp.minimum(large, nb - 1)
    return jnp.where(rel > 0, nb, 0) + jnp.where(n < max_exact, n, large)


def _rotary128(x, cos, sin_signed, lo32):
    partner = jnp.where(lo32, pltpu.roll(x, 96, 1), pltpu.roll(x, 32, 1))
    return x * cos + partner * sin_signed


def _inproj_kernel(*refs, apply_ln, has_prev, layer, tm, T, nsteps):
    refs = list(refs)
    x_ref = refs.pop(0)
    if apply_ln:
        g_ref, b_ref = refs.pop(0), refs.pop(0)
    w_ref, cos_ref, sin_ref = refs.pop(0), refs.pop(0), refs.pop(0)
    if has_prev:
        refs.pop(0), refs.pop(0)
    if apply_ln:
        xn_ref = refs.pop(0)
    (q_ref, kb_ref, vb_ref, rq_ref, rk_ref, rv_ref, rg_ref, u_ref, k5_hbm, v5_hbm,
     kbuf, vbuf, sems) = refs

    i = pl.program_id(0)
    slot = i % 2
    rows = min(tm, T)

    def kv_copies(s, step):
        cps = []
        for bb in range(tm // rows):
            r0 = step * tm + bb * rows
            b = r0 // T
            t0 = pl.multiple_of(r0 % T, SUBLANES)
            for buf, out in ((kbuf, k5_hbm), (vbuf, v5_hbm)):
                for h in range(A_HEADS):
                    cps.append(pltpu.make_async_copy(
                        buf.at[s, pl.ds(bb * rows, rows), pl.ds(h * A_V_DIM, A_V_DIM)],
                        out.at[layer, b, pl.ds(t0, rows), h, :], sems.at[s]))
        return cps

    @pl.when(i >= 2)
    def _():
        for cp in kv_copies(slot, i - 2):
            cp.wait()

    x = x_ref[...]
    if apply_ln:
        x = _layer_norm(x, g_ref[...], b_ref[...])
        xn_ref[...] = x
    xb = x.astype(BF16)

    def mm(c0, c1):
        return _dot(xb, w_ref[:, c0:c1])

    q_ref[...] = (mm(_OFF_Q, _OFF_K) * (LOG2E * HEAD_DIM ** -0.5)).astype(BF16)
    a = mm(_OFF_K, _OFF_V)
    kbuf[slot] = a
    kb_ref[...] = a.astype(BF16)
    a = mm(_OFF_V, _OFF_RQ)
    vbuf[slot] = a
    vb_ref[...] = a.astype(BF16)
    for cp in kv_copies(slot, i):
        cp.start()

    cos = cos_ref[...]
    sin = sin_ref[...]
    lane = lax.broadcasted_iota(jnp.int32, cos.shape, 1)
    lo32 = (lane & 63) < 32

    def rot(a, scale):
        parts = [_rotary128(a[:, c * LANES:(c + 1) * LANES], cos, sin, lo32) for c in range(2)]
        r = jnp.concatenate(parts, axis=1)
        if scale != 1.0:
            r = r * scale
        return r.astype(BF16)

    rq_ref[...] = rot(mm(_OFF_RQ, _OFF_RK), 1.0)
    rk_ref[...] = rot(mm(_OFF_RK, _OFF_RV), R_KEY_DIM ** -0.5)
    rv_ref[...] = mm(_OFF_RV, _OFF_RG).astype(BF16)
    a = mm(_OFF_RG, _OFF_C)
    rg_ref[...] = a * _sigmoid(a)
    a = mm(_OFF_C, IN_WIDTH)
    u_ref[...] = a[:, :C_WIDTH] * _sigmoid(a[:, C_WIDTH:])

    @pl.when(i == nsteps - 1)
    def _():
        for cp in kv_copies(slot, i):
            cp.wait()
        if nsteps > 1:
            for cp in kv_copies(1 - slot, i - 1):
                cp.wait()


def _inproj(x, w_bf, cos_t, sin_t, ln, tm, kv_prev, layer, depth, B, T):
    n = x.shape[0]
    assert (tm % T == 0 or T % tm == 0) and n % tm == 0
    nblk = cos_t.shape[0] // tm
    nsteps = n // tm
    row = lambda i: (i, 0)
    const = lambda i: (0, 0)
    tab = lambda i: (i % nblk, 0)
    in_specs = [pl.BlockSpec((tm, D_MODEL), row)]
    args = [x]
    if ln is not None:
        in_specs += [pl.BlockSpec((1, D_MODEL), const)] * 2
        args += [ln[0].reshape(1, D_MODEL), ln[1].reshape(1, D_MODEL)]
    in_specs += [pl.BlockSpec((D_MODEL, IN_WIDTH), const),
                 pl.BlockSpec((tm, LANES), tab), pl.BlockSpec((tm, LANES), tab)]
    args += [w_bf, cos_t, sin_t]
    aliases = {}
    if kv_prev is not None:
        k5_index = (1 if ln is not None else 0) + 8
        aliases = {len(args): k5_index, len(args) + 1: k5_index + 1}
        in_specs += [pl.BlockSpec(memory_space=pl.ANY)] * 2
        args += list(kv_prev)

    def o(width, dt):
        return jax.ShapeDtypeStruct((n, width), dt), pl.BlockSpec((tm, width), row)

    outs = []
    if ln is not None:
        outs.append(o(D_MODEL, F32))
    outs += [o(512, BF16), o(512, BF16), o(512, BF16),
             o(256, BF16), o(256, BF16), o(256, BF16), o(256, F32), o(256, F32)]
    kv5 = jax.ShapeDtypeStruct((depth, B, T, A_HEADS, A_V_DIM), F32)
    outs += [(kv5, pl.BlockSpec(memory_space=pl.ANY))] * 2
    res = pl.pallas_call(
        functools.partial(_inproj_kernel, apply_ln=ln is not None, has_prev=kv_prev is not None,
                          layer=layer, tm=tm, T=T, nsteps=nsteps),
        grid=(nsteps,),
        in_specs=in_specs,
        out_specs=[s for _, s in outs],
        out_shape=[s for s, _ in outs],
        scratch_shapes=[pltpu.VMEM((2, tm, A_WIDTH), F32), pltpu.VMEM((2, tm, A_WIDTH), F32),
                        pltpu.SemaphoreType.DMA((2,))],
        input_output_aliases=aliases,
        compiler_params=_params(("arbitrary",), 52),
        name="inproj",
    )(*args)
    if ln is None:
        res = [x] + list(res)
    return res


def _lambda(lam_ref, lam_init):
    lv = lam_ref[...]
    s1 = jnp.sum(lv[0:1] * lv[1:2], axis=-1, keepdims=True)
    s2 = jnp.sum(lv[2:3] * lv[3:4], axis=-1, keepdims=True)
    return jnp.exp(s1) - jnp.exp(s2) + lam_init


def _stack_maps(q):
    lane = lax.broadcasted_iota(jnp.int32, q.shape, 1)
    lo = lane < HEAD_DIM
    z = jnp.zeros_like(q)
    return jnp.concatenate([jnp.where(lo, q, z), jnp.where(lo, z, q)], axis=0)


def _diff_finish(o2, t, lam, g, lam_init):
    o = o2[:t] - lam * o2[t:]
    ms = jnp.mean(o * o, axis=-1, keepdims=True)
    return (o * lax.rsqrt(ms + LN_EPS) * g * (1.0 - lam_init)).astype(BF16)


def _attn_prompt_kernel(lam_ref, g_ref, bias_ref, q_ref, k_ref, v_ref, o_ref,
                        s_scr, m_scr, p_scr, vo_scr, bp_scr, bd_scr, *, T, TB, lam_init):
    lam = _lambda(lam_ref, lam_init)
    g = g_ref[...]
    bt = bias_ref[...]
    bt = (bt - bt[0:1, 0:1]) * LOG2E
    bp_scr[...] = bt[:, :TB]
    bd_scr[...] = bt[:, TB:]
    half = TB // 2
    vo_scr[:, :A_V_DIM] = v_ref[...]
    vo_scr[:, A_V_DIM:] = jnp.ones((T, A_V_DIM), BF16)

    for qi in range(T // TB):
        r0 = qi * TB
        buf = qi % 2
        q = q_ref[r0:r0 + TB, :]
        lo = lax.broadcasted_iota(jnp.int32, q.shape, 1) < HEAD_DIM
        zq = jnp.zeros_like(q)
        maps = (jnp.where(lo, q, zq), jnp.where(lo, zq, q))
        for mp in range(2):
            m = None
            for ki in range(qi + 1):
                s = _dot_nt(maps[mp], k_ref[ki * TB:(ki + 1) * TB, :])
                if ki == qi - 1:
                    s = s + bp_scr[...]
                elif ki == qi:
                    row = lax.broadcasted_iota(jnp.int32, (TB, TB), 0)
                    col = lax.broadcasted_iota(jnp.int32, (TB, TB), 1)
                    vis = (col >> _LOG2_CHUNK) <= (row >> _LOG2_CHUNK)
                    s = jnp.where(vis, s + bd_scr[...], NEG_INF)
                s_scr[buf, ki, mp * TB:(mp + 1) * TB, :] = s
                mt = jnp.maximum(s[:, :half], s[:, half:])
                m = mt if m is None else jnp.maximum(m, mt)
            m_scr[buf, mp * TB:(mp + 1) * TB, :] = jnp.broadcast_to(jnp.max(m, axis=1, keepdims=True),
                                                                    (TB, LANES))
        mb = m_scr[buf]
        mb2 = jnp.concatenate([mb, mb], axis=1)
        for ki in range(qi + 1):
            p_scr[buf, :, ki * TB:(ki + 1) * TB] = jnp.exp2(s_scr[buf, ki] - mb2).astype(BF16)
        kk = (qi + 1) * TB
        out = _dot(p_scr[buf, :, 0:kk], vo_scr[0:kk, :])
        o_ref[r0:r0 + TB, :] = _diff_finish(out[:, :A_V_DIM] / out[:, A_V_DIM:], TB, lam, g, lam_init)


def _attn_prompt(lam4, g, bias, q, k, v, B, T, lam_init):
    TB = 2 * LANES
    assert T % TB == 0 and TB % CHUNK == 0
    n = B * T
    blk = pl.BlockSpec((T, A_V_DIM), lambda b, h: (b, h))
    return pl.pallas_call(
        functools.partial(_attn_prompt_kernel, T=T, TB=TB, lam_init=lam_init),
        grid=(B, A_HEADS),
        in_specs=[pl.BlockSpec((4, HEAD_DIM), lambda b, h: (0, 0)),
                  pl.BlockSpec((1, A_V_DIM), lambda b, h: (0, 0)),
                  pl.BlockSpec((None, TB, 2 * TB), lambda b, h: (h, 0, 0)),
                  blk, blk, blk],
        out_specs=blk,
        out_shape=jax.ShapeDtypeStruct((n, A_WIDTH), BF16),
        scratch_shapes=[pltpu.VMEM((2, T // TB, 2 * TB, TB), F32),
                        pltpu.VMEM((2, 2 * TB, LANES), F32),
                        pltpu.VMEM((2, 2 * TB, T), BF16),
                        pltpu.VMEM((T, 2 * A_V_DIM), BF16),
                        pltpu.VMEM((TB, TB), F32),
                        pltpu.VMEM((TB, TB), F32)],
        compiler_params=_params(("arbitrary", "arbitrary"), 32),
        name="attn_prompt",
    )(lam4, g, bias, q, k, v)


def _attn_decode_kernel(lam_ref, g_ref, bias_ref, q_ref, kn_ref, vn_ref, ck_hbm, cv_hbm, o_ref,
                        kc_buf, vc_buf, sems, *, Ts, past, layer, lam_init):
    b = pl.program_id(0)
    h = pl.program_id(1)
    step = b * A_HEADS + h
    nsteps = pl.num_programs(0) * A_HEADS
    slot = step % 2

    def cache_copies(st, s):
        bb = st // A_HEADS
        hh = st % A_HEADS
        return [pltpu.make_async_copy(src.at[layer, bb, :, hh, :], dst.at[s], sems.at[s, j])
                for j, (src, dst) in enumerate(((ck_hbm, kc_buf), (cv_hbm, vc_buf)))]

    @pl.when(step == 0)
    def _():
        for cp in cache_copies(0, 0):
            cp.start()

    @pl.when(step + 1 < nsteps)
    def _():
        for cp in cache_copies(step + 1, 1 - slot):
            cp.start()

    for cp in cache_copies(step, slot):
        cp.wait()
    kc_ref = kc_buf.at[slot]
    vc_ref = vc_buf.at[slot]

    lam = _lambda(lam_ref, lam_init)
    q2 = _stack_maps(q_ref[...])
    bias = bias_ref[...] * LOG2E
    s_p = _dot_nt(q2, kc_ref[...].astype(BF16)).reshape(2, Ts, past) + bias[:, :past][None]
    s_n = _dot_nt(q2, kn_ref[...]).reshape(2, Ts, Ts) + bias[:, past:past + Ts][None]
    s_p = s_p.reshape(2 * Ts, past)
    s_n = s_n.reshape(2 * Ts, Ts)
    m = jnp.maximum(jnp.max(s_p, axis=1, keepdims=True), jnp.max(s_n, axis=1, keepdims=True))
    p_p = jnp.exp2(s_p - m)
    p_n = jnp.exp2(s_n - m)
    l = jnp.sum(p_p, axis=1, keepdims=True) + jnp.sum(p_n, axis=1, keepdims=True)
    acc = _dot(p_p.astype(BF16), vc_ref[...].astype(BF16)) + _dot(p_n.astype(BF16), vn_ref[...])
    o_ref[...] = _diff_finish(acc / l, Ts, lam, g_ref[...], lam_init)


def _attn_decode(lam4, g, bias, q, cache_k, cache_v, kn, vn, layer, Bs, Ts, lam_init):
    past = cache_k.shape[2]
    assert past % CHUNK == 0 and Ts <= CHUNK
    padk = bias.shape[2]
    new = pl.BlockSpec((Ts, A_V_DIM), lambda b, h: (b, h))
    cache = pl.BlockSpec(memory_space=pl.ANY)
    return pl.pallas_call(
        functools.partial(_attn_decode_kernel, Ts=Ts, past=past, layer=layer, lam_init=lam_init),
        grid=(Bs, A_HEADS),
        in_specs=[pl.BlockSpec((4, HEAD_DIM), lambda b, h: (0, 0)),
                  pl.BlockSpec((1, A_V_DIM), lambda b, h: (0, 0)),
                  pl.BlockSpec((None, Ts, padk), lambda b, h: (h, 0, 0)),
                  new, new, new, cache, cache],
        out_specs=new,
        out_shape=jax.ShapeDtypeStruct((Bs * Ts, A_WIDTH), BF16),
        scratch_shapes=[pltpu.VMEM((2, past, A_V_DIM), F32), pltpu.VMEM((2, past, A_V_DIM), F32),
                        pltpu.SemaphoreType.DMA((2, 2))],
        compiler_params=_params(("arbitrary", "arbitrary"), 40),
        name="attn_decode",
    )(lam4, g, bias, q, kn, vn, cache_k, cache_v)


def _ret_kernel(q_ref, k_ref, v_ref, g_ref, s0_ref, dm_ref, qd_ref, kd_ref, cd_ref,
                o_ref, sn_ref, *, T, C):
    z = jnp.zeros((R_KEY_DIM, R_V_DIM), F32)
    state = jnp.concatenate([jnp.concatenate([s0_ref[0], z], axis=1),
                             jnp.concatenate([z, s0_ref[1]], axis=1)], axis=0)
    r = lax.broadcasted_iota(jnp.int32, (LANES, LANES), 0)
    c = lax.broadcasted_iota(jnp.int32, (LANES, LANES), 1)
    same_head = (r >> _LOG2_CHUNK) == (c >> _LOG2_CHUNK)
    ones_bd = jnp.where(same_head, 1.0, 0.0).astype(BF16)
    lo = lax.broadcasted_iota(jnp.int32, (C, LANES), 1) < R_V_DIM
    cd = cd_ref[...]

    def chunk(n, state):
        r0 = n * C
        q = q_ref[pl.ds(r0, C), :]
        k = k_ref[pl.ds(r0, C), :]
        v = v_ref[pl.ds(r0, C), :]
        p = (_dot_nt(_stack_maps(q), k) * dm_ref[...]).astype(BF16)
        o2 = _dot(p, v)
        inner = jnp.where(lo, o2[:C], o2[C:])
        qd = (q.astype(F32) * qd_ref[...]).astype(BF16)
        o = inner + _dot(qd, state.astype(BF16))
        vk = (v.astype(F32) * kd_ref[...]).astype(BF16)
        new_state = cd * state + jnp.where(same_head, _dot_tn(k, vk), 0.0)
        oo = o * o
        hi = oo.astype(BF16)
        lo_part = (oo - hi.astype(F32)).astype(BF16)
        ss = _dot(hi, ones_bd) + _dot(lo_part, ones_bd)
        out = o * lax.rsqrt(ss * (1.0 / R_V_DIM) + LN_EPS) * g_ref[pl.ds(r0, C), :]
        o_ref[pl.ds(r0, C), :] = out.astype(BF16)
        return new_state

    for n in range(T // C):
        state = chunk(n, state)
    sn_ref[0] = state[:R_KEY_DIM, :R_V_DIM]
    sn_ref[1] = state[R_KEY_DIM:, R_V_DIM:]


def _retention(rq, rk, rv, rg, state0, tabs, B, T, C):
    dm, qd, kd, cd = tabs
    blk = pl.BlockSpec((T, LANES), lambda b, hp: (b, hp))
    st = pl.BlockSpec((None, 2, R_KEY_DIM, R_V_DIM), lambda b, hp: (b, hp, 0, 0))
    return pl.pallas_call(
        functools.partial(_ret_kernel, T=T, C=C),
        grid=(B, 2),
        in_specs=[blk, blk, blk, blk, st,
                  pl.BlockSpec((None, 2 * C, C), lambda b, hp: (hp, 0, 0)),
                  pl.BlockSpec((None, C, LANES), lambda b, hp: (hp, 0, 0)),
                  pl.BlockSpec((None, C, LANES), lambda b, hp: (hp, 0, 0)),
                  pl.BlockSpec((None, 1, LANES), lambda b, hp: (hp, 0, 0))],
        out_specs=[blk, st],
        out_shape=[jax.ShapeDtypeStruct((B * T, R_WIDTH), BF16),
                   jax.ShapeDtypeStruct((B, R_HEADS, R_KEY_DIM, R_V_DIM), F32)],
        compiler_params=_params(("arbitrary", "arbitrary"), 32),
        name="retention",
    )(rq, rk, rv, rg, state0, dm, qd, kd, cd)


def _retention_tables(C):
    h = jnp.arange(R_HEADS, dtype=F32)
    log_g = jnp.log1p(-jnp.exp2(-5.0 - h))
    i = jnp.arange(C, dtype=F32)
    diff = i[:, None] - i[None, :]
    dmask = jnp.where(diff >= 0, jnp.exp(jnp.maximum(diff, 0.0)[None] * log_g[:, None, None]), 0.0)
    dm = dmask.reshape(2, 2 * C, C)
    q_dec = jnp.exp((i + 1.0)[None, :] * log_g[:, None])
    k_dec = jnp.exp((C - 1 - i)[None, :] * log_g[:, None])
    c_dec = jnp.exp(C * log_g)

    def lanes(t):
        t = jnp.repeat(t[:, :, None], R_V_DIM, axis=2).reshape(2, 2, C, R_V_DIM)
        return jnp.concatenate([t[:, 0], t[:, 1]], axis=-1)

    cd = jnp.repeat(c_dec[:, None], R_V_DIM, axis=1).reshape(2, 1, LANES)
    return dm, lanes(q_dec), lanes(k_dec), cd


_CONV_PAD = 32


def _conv_kernel(u_ref, h_ref, w_ref, b_ref, g_ref, be_ref, o_ref, t_ref, up_scr, *, T, RT):
    hist = CONV_WIDTH - 1
    off = _CONV_PAD - hist
    up_scr[off:_CONV_PAD, :] = h_ref[...]
    up_scr[_CONV_PAD:_CONV_PAD + T, :] = u_ref[...]
    bias = b_ref[...]
    g = g_ref[...]
    be = be_ref[...]
    win_rows = RT + _CONV_PAD
    for t0 in range(0, T, RT):
        win = up_scr[t0:t0 + win_rows, :]
        acc = jnp.zeros((RT, C_WIDTH), F32)
        for s in range(SUBLANES):
            rolled = win if s == 0 else pltpu.roll(win, win_rows - s, 0)
            for a in range(_CONV_PAD // SUBLANES + 1):
                j = a * SUBLANES + s - off
                if 0 <= j < CONV_WIDTH:
                    acc = acc + rolled[a * SUBLANES:a * SUBLANES + RT, :] * w_ref[j:j + 1, :]
        y = _layer_norm(acc + bias, g, be)
        o_ref[t0:t0 + RT, :] = (y * _sigmoid(y)).astype(BF16)
    t_ref[...] = up_scr[T + off:T + _CONV_PAD, :]


def _conv(u, hist, dw_w, dw_b, ln_g, ln_b, B, T):
    RT = min(T, LANES)
    assert T % RT == 0
    hl = CONV_WIDTH - 1
    vec = pl.BlockSpec((1, C_WIDTH), lambda b: (0, 0))
    hb = pl.BlockSpec((None, hl, C_WIDTH), lambda b: (b, 0, 0))
    return pl.pallas_call(
        functools.partial(_conv_kernel, T=T, RT=RT),
        grid=(B,),
        in_specs=[pl.BlockSpec((T, C_WIDTH), lambda b: (b, 0)), hb,
                  pl.BlockSpec((CONV_WIDTH, C_WIDTH), lambda b: (0, 0)), vec, vec, vec],
        out_specs=[pl.BlockSpec((T, C_WIDTH), lambda b: (b, 0)), hb],
        out_shape=[jax.ShapeDtypeStruct((B * T, C_WIDTH), BF16),
                   jax.ShapeDtypeStruct((B, hl, C_WIDTH), F32)],
        scratch_shapes=[pltpu.VMEM((T + _CONV_PAD, C_WIDTH), F32)],
        compiler_params=_params(("arbitrary",), 32),
        name="conv",
    )(u, hist, dw_w, dw_b.reshape(1, C_WIDTH), ln_g.reshape(1, C_WIDTH), ln_b.reshape(1, C_WIDTH))


def _first_index(vals, target):
    idx = jnp.full(target.shape, len(vals) - 1, jnp.int32)
    for j in range(len(vals) - 2, -1, -1):
        idx = jnp.where(vals[j] == target, j, idx)
    return idx


def _select(idx, vals):
    out = vals[-1]
    for j in range(len(vals) - 2, -1, -1):
        out = jnp.where(idx == j, vals[j], out)
    return out


def _route(logits_t):
    rows = [logits_t[e:e + 1, :] for e in range(N_EXPERTS)]
    m = functools.reduce(jnp.maximum, rows)
    ex = [jnp.exp(r - m) for r in rows]
    z = functools.reduce(jnp.add, ex)
    sc = [e / z for e in ex]
    v1s, v2s, i1s, i2s, gss = [], [], [], [], []
    for g in range(N_GROUPS):
        a = sc[g * EXPERTS_PER_GROUP:(g + 1) * EXPERTS_PER_GROUP]
        v1 = functools.reduce(jnp.maximum, a)
        i1 = _first_index(a, v1)
        rest = [jnp.where(i1 == j, -1.0, a[j]) for j in range(EXPERTS_PER_GROUP)]
        v2 = functools.reduce(jnp.maximum, rest)
        i2 = _first_index(rest, v2)
        v1s.append(v1); v2s.append(v2); i1s.append(i1); i2s.append(i2); gss.append(v1 + v2)
    grp = _first_index(gss, functools.reduce(jnp.maximum, gss))
    v1 = _select(grp, v1s)
    v2 = _select(grp, v2s)
    e1 = _select(grp, i1s) + grp * EXPERTS_PER_GROUP
    e2 = _select(grp, i2s) + grp * EXPERTS_PER_GROUP
    den = v1 + v2
    return e1, e2, v1 / den, v2 / den


def _expert_onehot(e1, e2):
    rows = [jnp.where(e1 == e, 1.0, 0.0) + jnp.where(e2 == e, 1.0, 0.0) for e in range(N_EXPERTS)]
    return jnp.concatenate(rows, axis=0)


def _outproj_kernel(a_ref, r_ref, c_ref, x_ref, w_ref, g_ref, b_ref, wrh_ref, wrl_ref, br_ref,
                    x1_ref, x1b_ref, ri_ref, rw_ref, cnt_ref, *, alpha, tb, sub):
    tm = x_ref.shape[0]
    ones = jnp.ones((SUBLANES, sub), BF16)
    counts = []
    for h in range(tm // sub):
        rs = slice(h * sub, (h + 1) * sub)
        mix = (_dot(a_ref[rs, :], w_ref[0:A_WIDTH, :])
               + _dot(r_ref[rs, :], w_ref[A_WIDTH:A_WIDTH + R_WIDTH, :])
               + _dot(c_ref[rs, :], w_ref[A_WIDTH + R_WIDTH:, :]))
        x1 = _layer_norm(alpha * x_ref[rs, :] + mix, g_ref[...], b_ref[...])
        x_hi = x1.astype(BF16)
        x1_ref[rs, :] = x1
        x1b_ref[rs, :] = x_hi
        x_lo = (x1 - x_hi.astype(F32)).astype(BF16)
        logits = _dot(x_hi, wrh_ref[...]) + _dot(x_lo, wrh_ref[...]) + _dot(x_hi, wrl_ref[...])
        logits_t = logits.T[0:N_EXPERTS, :] + br_ref[...]
        e1, e2, w1, w2 = _route(logits_t)
        ri_ref[:, rs] = jnp.concatenate([e1, e2, jnp.zeros((SUBLANES - 2, sub), jnp.int32)], axis=0)
        rw_ref[:, rs] = jnp.concatenate([w1, w2, jnp.zeros((SUBLANES - 2, sub), F32)], axis=0)
        counts.append(_dot_nt(ones, _expert_onehot(e1, e2).astype(BF16)))
    per = tb // sub
    for k in range(tm // tb):
        c = functools.reduce(jnp.add, counts[k * per:(k + 1) * per])
        c = jnp.concatenate([c, jnp.zeros((SUBLANES, LANES - N_EXPERTS), F32)], axis=1)
        cnt_ref[k * SUBLANES:(k + 1) * SUBLANES, :] = c.astype(jnp.int32)


def _outproj(attn, ret, conv, x, w_bf, ln_g, ln_b, w_router, br, alpha, tm, tb):
    n = x.shape[0]
    wr = jnp.pad(w_router, ((0, 0), (0, LANES - N_EXPERTS)))
    wr_hi = wr.astype(BF16)
    wr_lo = (wr - wr_hi.astype(F32)).astype(BF16)
    row = lambda i: (i, 0)
    col = lambda i: (0, i)
    const = lambda i: (0, 0)
    nsub = tm // tb
    return pl.pallas_call(
        functools.partial(_outproj_kernel, alpha=alpha, tb=tb, sub=min(tb, 2 * LANES)),
        grid=(n // tm,),
        in_specs=[pl.BlockSpec((tm, A_WIDTH), row), pl.BlockSpec((tm, R_WIDTH), row),
                  pl.BlockSpec((tm, C_WIDTH), row), pl.BlockSpec((tm, D_MODEL), row),
                  pl.BlockSpec((D_MODEL, D_MODEL), const),
                  pl.BlockSpec((1, D_MODEL), const), pl.BlockSpec((1, D_MODEL), const),
                  pl.BlockSpec((D_MODEL, LANES), const), pl.BlockSpec((D_MODEL, LANES), const),
                  pl.BlockSpec((N_EXPERTS, 1), const)],
        out_specs=[pl.BlockSpec((tm, D_MODEL), row), pl.BlockSpec((tm, D_MODEL), row),
                   pl.BlockSpec((SUBLANES, tm), col), pl.BlockSpec((SUBLANES, tm), col),
                   pl.BlockSpec((nsub * SUBLANES, LANES), row)],
        out_shape=[jax.ShapeDtypeStruct((n, D_MODEL), F32), jax.ShapeDtypeStruct((n, D_MODEL), BF16),
                   jax.ShapeDtypeStruct((SUBLANES, n), jnp.int32),
                   jax.ShapeDtypeStruct((SUBLANES, n), F32),
                   jax.ShapeDtypeStruct((n // tb * SUBLANES, LANES), jnp.int32)],
        compiler_params=_params(("arbitrary",), 40),
        name="outproj",
    )(attn, ret, conv, x, w_bf, ln_g.reshape(1, D_MODEL), ln_b.reshape(1, D_MODEL), wr_hi, wr_lo, br)


_UNIT = 16
_XS_WIDTH = D_MODEL + LANES
_FFN_ROWS = 512
_DISPATCH_TOKENS = 256


def _sorted_positions(e1, e2, lo_ref, base, tb):
    onehot = _expert_onehot(e1, e2).astype(BF16)
    r = lax.broadcasted_iota(jnp.int32, (tb, tb), 0)
    c = lax.broadcasted_iota(jnp.int32, (tb, tb), 1)
    earlier = jnp.where(r < c, 1.0, 0.0).astype(BF16)
    rank = _dot(onehot, earlier).astype(jnp.int32)
    pos1 = jnp.zeros(e1.shape, jnp.int32)
    pos2 = jnp.zeros(e1.shape, jnp.int32)
    for e in range(N_EXPERTS):
        p = rank[e:e + 1, :] + lo_ref[base + e]
        pos1 = jnp.where(e1 == e, p, pos1)
        pos2 = jnp.where(e2 == e, p, pos2)
    return pos1, pos2


def _permutation(pos1, pos2, rows):
    j = lax.broadcasted_iota(jnp.int32, (rows, pos1.shape[1]), 0)
    return jnp.where(j == pos1, 1.0, jnp.where(j == pos2, 1.0, 0.0)).astype(BF16)


def _dispatch_kernel(lo_ref, urow_ref, ri_ref, rw_ref, x_ref, xs_hbm, pos_ref,
                     stage, sems, *, nblk, tb, cap, spare_row):
    i = pl.program_id(0)
    slot = i % 2
    units = cap // _UNIT

    def wait_slot(s):
        pltpu.make_async_copy(stage.at[s], xs_hbm.at[pl.ds(0, cap), :], sems.at[s]).wait()

    @pl.when(i >= 2)
    def _():
        wait_slot(slot)

    e1 = ri_ref[0:1, :]
    e2 = ri_ref[1:2, :]
    w1 = rw_ref[0:1, :]
    w2 = rw_ref[1:2, :]
    pos1, pos2 = _sorted_positions(e1, e2, lo_ref, i * N_EXPERTS, tb)
    pos_ref[...] = jnp.concatenate([pos1, pos2, jnp.zeros((SUBLANES - 2, tb), jnp.int32)], axis=0)
    perm = _permutation(pos1, pos2, cap)
    stage[slot, :, 0:D_MODEL] = _dot(perm, x_ref[...]).astype(BF16)
    j = lax.broadcasted_iota(jnp.int32, (cap, tb), 0)
    gate = jnp.sum(jnp.where(j == pos1, w1, jnp.where(j == pos2, w2, 0.0)), axis=1, keepdims=True)
    g0 = gate.astype(BF16).astype(F32)
    g1 = (gate - g0).astype(BF16).astype(F32)
    g2 = gate - g0 - g1
    lane = lax.broadcasted_iota(jnp.int32, (cap, LANES), 1)
    pieces = jnp.where(lane == 0, g0, jnp.where(lane == 1, g1, jnp.where(lane == 2, g2, 0.0)))
    stage[slot, :, D_MODEL:_XS_WIDTH] = pieces.astype(BF16)

    def body(u, c):
        t = urow_ref[i * units + u]
        s = pl.multiple_of(u * _UNIT, _UNIT)
        d = pl.multiple_of(jnp.where(t >= 0, t, spare_row + slot * cap + s), _UNIT)
        pltpu.make_async_copy(stage.at[slot, pl.ds(s, _UNIT), :], xs_hbm.at[pl.ds(d, _UNIT), :],
                              sems.at[slot]).start()
        return c

    lax.fori_loop(0, units, body, 0)

    @pl.when(i == nblk - 1)
    def _():
        wait_slot(slot)
        if nblk > 1:
            wait_slot(1 - slot)


def _ffn_kernel(te_ref, tv_ref, nt_ref, xs_ref, wg_ref, wu_ref, wd_ref, ys_ref, wg_b, wu_b, wd_b):
    j = pl.program_id(0)
    jm = jnp.maximum(j, 1)

    @pl.when((j == 0) | (te_ref[j] != te_ref[jm - 1]))
    def _():
        wg_b[...] = wg_ref[...].astype(BF16)
        wu_b[...] = wu_ref[...].astype(BF16)
        wd_b[...] = wd_ref[...].astype(BF16)

    @pl.when(j < nt_ref[0])
    def _():
        rows = xs_ref.shape[0]
        valid = lax.broadcasted_iota(jnp.int32, (rows, LANES), 0) < tv_ref[j]
        zero = jnp.zeros((rows, LANES), BF16)
        x = jnp.concatenate([jnp.where(valid, xs_ref[:, c * LANES:(c + 1) * LANES], zero)
                             for c in range(D_MODEL // LANES)], axis=1)
        gp = jnp.where(valid, xs_ref[:, D_MODEL:_XS_WIDTH], zero).astype(F32)
        g = gp[:, 0:1] + gp[:, 1:2] + gp[:, 2:3]
        hg = _dot(x, wg_b[...])
        hu = _dot(x, wu_b[...])
        h = hg * _sigmoid(hg) * hu * g
        ys_ref[...] = _dot(h.astype(BF16), wd_b[...]).astype(BF16)


def _combine_kernel(urow_ref, pos_ref, x1_ref, p_ref, g_ref, b_ref, wpg_ref, bpg_ref, wpp_ref,
                    ys_hbm, out_ref, stage, sems, *, nblk, tb, cap, alpha):
    i = pl.program_id(0)
    slot = i % 2
    units = cap // _UNIT

    def fetch(blk, s):
        def body(u, c):
            sr = pl.multiple_of(jnp.maximum(urow_ref[blk * units + u], 0), _UNIT)
            ds = pl.multiple_of(u * _UNIT, _UNIT)
            pltpu.make_async_copy(ys_hbm.at[pl.ds(sr, _UNIT), :], stage.at[s, pl.ds(ds, _UNIT), :],
                                  sems.at[s]).start()
            return c

        lax.fori_loop(0, units, body, 0)

    @pl.when(i == 0)
    def _():
        fetch(0, 0)

    @pl.when(i + 1 < nblk)
    def _():
        fetch(i + 1, 1 - slot)

    pltpu.make_async_copy(ys_hbm.at[pl.ds(0, cap), :], stage.at[slot], sems.at[slot]).wait()
    perm = _permutation(pos_ref[0:1, :], pos_ref[1:2, :], cap)
    y = _dot_tn(perm, stage[slot])
    x2 = _layer_norm(alpha * x1_ref[...] + y, g_ref[...], b_ref[...])
    gate = _sigmoid(_dot(x2.astype(BF16), wpg_ref[...]) + bpg_ref[...])
    out_ref[...] = x2 + gate * _dot(p_ref[...].astype(BF16), wpp_ref[...])


def _moe_plan(cnt, nblk, tb, ffn_rows):
    cnt = cnt.reshape(nblk, SUBLANES, LANES)[:, 0, :N_EXPERTS]
    pc = (cnt + (_UNIT - 1)) // _UNIT * _UNIT
    lo = jnp.cumsum(pc, axis=1) - pc
    tot = jnp.sum(pc, axis=0)
    reg = (tot + (ffn_rows - 1)) // ffn_rows * ffn_rows
    rstart = jnp.cumsum(reg) - reg
    gs = rstart[None, :] + jnp.cumsum(pc, axis=0) - pc
    tiles_e = reg // ffn_rows
    tile_end = jnp.cumsum(tiles_e)
    max_rows = 2 * nblk * tb + nblk * N_EXPERTS * _UNIT + N_EXPERTS * ffn_rows
    max_tiles = -(-max_rows // ffn_rows)
    tj =jnp.arange(max_tiles, dtype=jnp.int32)
    te = jnp.minimum(jnp.sum((tj[:, None] >= tile_end[None, :]).astype(jnp.int32), axis=1), N_EXPERTS - 1)
    tv = jnp.clip(tot[te] - (tj - (tile_end - tiles_e)[te]) * ffn_rows, 0, ffn_rows)
    nt = tile_end[-1:].astype(jnp.int32)
    cap = 2 * tb + N_EXPERTS * _UNIT
    urow0 = jnp.arange(cap // _UNIT, dtype=jnp.int32) * _UNIT
    run = jnp.sum((urow0[None, :, None] >= (lo + pc)[:, None, :]).astype(jnp.int32), axis=2)
    pick = run[:, :, None] == jnp.arange(N_EXPERTS, dtype=jnp.int32)[None, None, :]
    urow = jnp.sum(jnp.where(pick, (gs - lo)[:, None, :], 0), axis=2) + urow0[None, :]
    urow = jnp.where(urow0[None, :] < jnp.sum(pc, axis=1, keepdims=True), urow, -1)
    flat = lambda a: a.reshape(-1).astype(jnp.int32)
    return flat(lo), flat(urow), te.astype(jnp.int32), tv.astype(jnp.int32), nt, max_tiles, cap


def _moe(x1b, x1, ri, rw, cnt, p, wg, wu, wd, layer, ln_g, ln_b, wpg, bpg, wpp, alpha, tb):
    n = x1.shape[0]
    nblk = n // tb
    ffn_rows = _FFN_ROWS if 2 * n >= N_EXPERTS * _FFN_ROWS else LANES
    lo, urow, te, tv, nt, max_tiles, cap = _moe_plan(cnt, nblk, tb, ffn_rows)
    max_rows = max_tiles * ffn_rows

    xs, pos = pl.pallas_call(
        functools.partial(_dispatch_kernel, nblk=nblk, tb=tb, cap=cap, spare_row=max_rows),
        grid_spec=pltpu.PrefetchScalarGridSpec(
            num_scalar_prefetch=2,
            grid=(nblk,),
            in_specs=[pl.BlockSpec((SUBLANES, tb), lambda i, *_: (0, i)),
                      pl.BlockSpec((SUBLANES, tb), lambda i, *_: (0, i)),
                      pl.BlockSpec((tb, D_MODEL), lambda i, *_: (i, 0))],
            out_specs=[pl.BlockSpec(memory_space=pl.ANY),
                       pl.BlockSpec((SUBLANES, tb), lambda i, *_: (0, i))],
            scratch_shapes=[pltpu.VMEM((2, cap, _XS_WIDTH), BF16),
                            pltpu.SemaphoreType.DMA((2,))]),
        out_shape=[jax.ShapeDtypeStruct((max_rows + 2 * cap, _XS_WIDTH), BF16),
                   jax.ShapeDtypeStruct((SUBLANES, n), jnp.int32)],
        compiler_params=_params(("arbitrary",), 32),
        name="moe_dispatch",
    )(lo, urow, ri, rw, x1b)

    def tile(j, te_ref, tv_ref, nt_ref):
        return jnp.minimum(j, nt_ref[0] - 1)

    ys = pl.pallas_call(
        _ffn_kernel,
        grid_spec=pltpu.PrefetchScalarGridSpec(
            num_scalar_prefetch=3,
            grid=(max_tiles,),
            in_specs=[pl.BlockSpec((ffn_rows, _XS_WIDTH), lambda j, *s: (tile(j, *s), 0)),
                      pl.BlockSpec((None, None, D_MODEL, D_EXPERT),
                                   lambda j, *s: (layer, s[0][tile(j, *s)], 0, 0)),
                      pl.BlockSpec((None, None, D_MODEL, D_EXPERT),
                                   lambda j, *s: (layer, s[0][tile(j, *s)], 0, 0)),
                      pl.BlockSpec((None, None, D_EXPERT, D_MODEL),
                                   lambda j, *s: (layer, s[0][tile(j, *s)], 0, 0))],
            out_specs=pl.BlockSpec((ffn_rows, D_MODEL), lambda j, *s: (tile(j, *s), 0)),
            scratch_shapes=[pltpu.VMEM((D_MODEL, D_EXPERT), BF16), pltpu.VMEM((D_MODEL, D_EXPERT), BF16),
                            pltpu.VMEM((D_EXPERT, D_MODEL), BF16)]),
        out_shape=jax.ShapeDtypeStruct((max_rows, D_MODEL), BF16),
        compiler_params=_params(("arbitrary",), 40),
        name="moe_ffn",
    )(te, tv, nt, xs, wg, wu, wd)

    const = lambda i, *_: (0, 0)
    return pl.pallas_call(
        functools.partial(_combine_kernel, nblk=nblk, tb=tb, cap=cap, alpha=alpha),
        grid_spec=pltpu.PrefetchScalarGridSpec(
            num_scalar_prefetch=1,
            grid=(nblk,),
            in_specs=[pl.BlockSpec((SUBLANES, tb), lambda i, *_: (0, i)),
                      pl.BlockSpec((tb, D_MODEL), lambda i, *_: (i, 0)),
                      pl.BlockSpec((None, tb, PLE_DIM), lambda i, *_: (layer, i, 0)),
                      pl.BlockSpec((1, D_MODEL), const), pl.BlockSpec((1, D_MODEL), const),
                      pl.BlockSpec((D_MODEL, D_MODEL), const), pl.BlockSpec((1, D_MODEL), const),
                      pl.BlockSpec((PLE_DIM, D_MODEL), const),
                      pl.BlockSpec(memory_space=pl.ANY)],
            out_specs=pl.BlockSpec((tb, D_MODEL), lambda i, *_: (i, 0)),
            scratch_shapes=[pltpu.VMEM((2, cap, D_MODEL), BF16),
                            pltpu.SemaphoreType.DMA((2,))]),
        out_shape=jax.ShapeDtypeStruct((n, D_MODEL), F32),
        compiler_params=_params(("arbitrary",), 40),
        name="moe_combine",
    )(urow, pos, x1, p, ln_g.reshape(1, D_MODEL), ln_b.reshape(1, D_MODEL),
      wpg, bpg.reshape(1, D_MODEL), wpp, ys)


def _rope_tables(pos, rows):
    half = R_KEY_DIM // 2
    inv_freq = 1.0 / (ROPE_BASE ** jnp.linspace(0.0, 1.0, half, dtype=jnp.float32))
    ang = pos.astype(jnp.float32)[:, None] * inv_freq[None, :]
    cos = jnp.cos(ang)
    sin = jnp.sin(ang)
    cos_t = jnp.tile(cos, (rows // pos.shape[0], 4))
    sin_t = jnp.tile(jnp.concatenate([-sin, sin], axis=1), (rows // pos.shape[0], 2))
    return cos_t, sin_t


def kernel(x_prompt, x_sample, p_prompt, p_sample, cache_k, cache_v, state_ret, state_conv, ln_emb_g, ln_emb_b, rel_bias, w_router, b_router, w_in, lam_q1, lam_k1, lam_q2, lam_k2, subln_g, dw_w, dw_b, conv_ln_g, conv_ln_b, w_out, ln1_g, ln1_b, w_exp_gate, w_exp_up, w_exp_down, ln2_g, ln2_b, w_ple_gate, b_ple_gate, w_ple_proj):
    B, T, D = x_prompt.shape
    Bs, Ts, _ = x_sample.shape
    depth = w_in.shape[0]
    past = cache_k.shape[2]
    n_p, n_s = B * T, Bs * Ts
    alpha = (2 * depth) ** 0.25

    tm_p = 512 if n_p % 512 == 0 else n_p
    tm_s = n_s
    TB = 2 * LANES
    c_p = min(T, 2 * LANES)

    pos_p = jnp.arange(T, dtype=jnp.int32)
    pos_s = past + jnp.arange(Ts, dtype=jnp.int32)
    rope_p = _rope_tables(pos_p, max(T, tm_p))
    rope_s = _rope_tables(pos_s, max(Ts, tm_s))
    rel_p = (jnp.arange(2 * TB, dtype=jnp.int32)[None, :] - TB) - jnp.arange(TB, dtype=jnp.int32)[:, None]
    padk = -(-(past + Ts) // LANES) * LANES
    rel_s = jnp.arange(padk, dtype=jnp.int32)[None, :] - pos_s[:, None]
    bias_p = _bias_table(rel_bias, _t5_bucket(rel_p))
    bias_s = _bias_table(rel_bias, _t5_bucket(rel_s))
    tabs_p = _retention_tables(c_p)
    tabs_s = _retention_tables(Ts)

    br = b_router.reshape(N_EXPERTS, 1)
    ret0_p = jnp.zeros((B, R_HEADS, R_KEY_DIM, R_V_DIM), F32)
    conv0_p = jnp.zeros((B, CONV_WIDTH - 1, C_WIDTH), F32)

    xp = x_prompt.reshape(n_p, D)
    xs = x_sample.reshape(n_s, D)
    outs = {k: [] for k in ("rp", "cp", "rs", "cs")}
    kv_p = kv_s = None
    for l in range(depth):
        lam_init = 0.8 - 0.6 * math.exp(-0.3 * l)
        lam4 = jnp.stack([lam_q1[l], lam_k1[l], lam_q2[l], lam_k2[l]])
        g_sub = subln_g[l].reshape(1, A_V_DIM)
        w_in_b = w_in[l].astype(BF16)
        w_out_b = w_out[l].astype(BF16)
        wpg, wpp = w_ple_gate[l].astype(BF16), w_ple_proj[l].astype(BF16)
        ln = (ln_emb_g, ln_emb_b) if l == 0 else None

        def channel(x, attn, ret, conv, p_l, tm):
            tb = min(tm, _DISPATCH_TOKENS)
            x1, x1b, ri, rw, cnt = _outproj(attn, ret, conv, x, w_out_b, ln1_g[l], ln1_b[l], w_router, br,
                                            alpha, tm, tb)
            return _moe(x1b, x1, ri, rw, cnt, p_l, w_exp_gate, w_exp_up, w_exp_down, l, ln2_g[l], ln2_b[l],
                        wpg, b_ple_gate[l], wpp, alpha, tb)

        xp, q, kb, vb, rq, rk, rv, rg, u, k5, v5 = _inproj(xp, w_in_b, rope_p[0], rope_p[1], ln, tm_p,
                                                           kv_p, l, depth, B, T)
        kv_p = (k5, v5)
        attn = _attn_prompt(lam4, g_sub, bias_p, q, kb, vb, B, T, lam_init)
        ret, rstate = _retention(rq, rk, rv, rg, ret0_p, tabs_p, B, T, c_p)
        conv, ctail = _conv(u, conv0_p, dw_w[l], dw_b[l], conv_ln_g[l], conv_ln_b[l], B, T)
        xp = channel(xp, attn, ret, conv, p_prompt.reshape(depth, n_p, PLE_DIM), tm_p)
        outs["rp"].append(rstate)
        outs["cp"].append(ctail)

        xs, q, kb, vb, rq, rk, rv, rg, u, k5, v5 = _inproj(xs, w_in_b, rope_s[0], rope_s[1], ln, tm_s,
                                                           kv_s, l, depth, Bs, Ts)
        kv_s = (k5, v5)
        attn = _attn_decode(lam4, g_sub, bias_s, q, cache_k, cache_v, kb, vb, l, Bs, Ts, lam_init)
        ret, rstate = _retention(rq, rk, rv, rg, state_ret[l], tabs_s, Bs, Ts, Ts)
        conv, ctail = _conv(u, state_conv[l], dw_w[l], dw_b[l], conv_ln_g[l], conv_ln_b[l], Bs, Ts)
        xs = channel(xs, attn, ret, conv, p_sample.reshape(depth, n_s, PLE_DIM), tm_s)
        outs["rs"].append(rstate)
        outs["cs"].append(ctail)

    return (xp.reshape(B, T, D), xs.reshape(Bs, Ts, D),
            kv_p[0], kv_p[1], jnp.stack(outs["rp"]), jnp.stack(outs["cp"]),
            kv_s[0], kv_s[1], jnp.stack(outs["rs"]), jnp.stack(outs["cs"]))
```

```python
import functools
import math

import jax
import jax.numpy as jnp
from jax import lax
from jax.experimental import pallas as pl
from jax.experimental.pallas import tpu as pltpu

F32 = jnp.float32
BF16 = jnp.bfloat16

D_MODEL = 1024
CHUNK = 64
HEAD_DIM = 64
A_HEADS = 4
A_V_DIM = 128
A_WIDTH = 512
R_HEADS = 4
R_KEY_DIM = 64
R_V_DIM = 64
R_WIDTH = 256
C_WIDTH = 256
CONV_WIDTH = 31
IN_WIDTH = 3072
N_BUCKETS = 32
MAX_DISTANCE = 128
ROPE_BASE = 10000.0
N_EXPERTS = 16
N_GROUPS = 4
EXPERTS_PER_GROUP = 4
D_EXPERT = 512
PLE_DIM = 256
LN_EPS = 1e-5
NEG_INF = -1e30
LOG2E = 1.4426950408889634

LANES = 128
SUBLANES = 8
_LOG2_CHUNK = 6
assert CHUNK == HEAD_DIM == R_KEY_DIM == R_V_DIM == 1 << _LOG2_CHUNK
MIB = 1024 * 1024

_OFF_Q, _OFF_K, _OFF_V = 0, 512, 1024
_OFF_RQ, _OFF_RK, _OFF_RV, _OFF_RG, _OFF_C = 1536, 1792, 2048, 2304, 2560


def _params(sem, vmem_mib):
    return pltpu.CompilerParams(dimension_semantics=sem, vmem_limit_bytes=vmem_mib * MIB)


def _layer_norm(x, g, b):
    mu = jnp.mean(x, axis=-1, keepdims=True)
    xc = x - mu
    var = jnp.mean(xc * xc, axis=-1, keepdims=True)
    return xc * lax.rsqrt(var + LN_EPS) * g + b


def _sigmoid(x):
    return 1.0 / (1.0 + jnp.exp(-x))


def _dot(a, b):
    return jnp.dot(a, b, preferred_element_type=F32)


def _dot_nt(a, b):
    return lax.dot_general(a, b, (((1,), (1,)), ((), ())), preferred_element_type=F32)


def _dot_tn(a, b):
    return lax.dot_general(a, b, (((0,), (0,)), ((), ())), preferred_element_type=F32)


def _bias_kernel(relb_ref, idx_ref, out_ref):
    idx = idx_ref[...]
    for h in range(A_HEADS):
        acc = jnp.zeros(idx.shape, F32)
        for b in range(N_BUCKETS):
            acc = jnp.where(idx == b, relb_ref[b, h], acc)
        out_ref[h] = acc


def _bias_table(rel_bias, idx):
    r, c = idx.shape
    return pl.pallas_call(
        _bias_kernel,
        out_shape=jax.ShapeDtypeStruct((A_HEADS, r, c), F32),
        in_specs=[pl.BlockSpec(memory_space=pltpu.SMEM),
                  pl.BlockSpec(memory_space=pltpu.VMEM)],
        out_specs=pl.BlockSpec(memory_space=pltpu.VMEM),
        name="bias_table",
    )(rel_bias, idx)


def _t5_bucket(rel):
    nb = N_BUCKETS // 2
    max_exact = nb // 2
    n = jnp.abs(rel)
    nf = jnp.maximum(n, 1).astype(jnp.float32)
    large = max_exact + (jnp.log(nf / max_exact) / math.log(MAX_DISTANCE / max_exact)
                         * (nb - max_exact)).astype(jnp.int32)
    large = jnp.minimum(large, nb - 1)
    return jnp.where(rel > 0, nb, 0) + jnp.where(n < max_exact, n, large)


def _rotary128(x, cos, sin_signed, lo32):
    partner = jnp.where(lo32, pltpu.roll(x, 96, 1), pltpu.roll(x, 32, 1))
    return x * cos + partner * sin_signed


def _inproj_kernel(*refs, apply_ln, has_prev, layer, tm, T, nsteps):
    refs = list(refs)
    x_ref = refs.pop(0)
    if apply_ln:
        g_ref, b_ref = refs.pop(0), refs.pop(0)
    w_ref, cos_ref, sin_ref = refs.pop(0), refs.pop(0), refs.pop(0)
    if has_prev:
        refs.pop(0), refs.pop(0)
    if apply_ln:
        xn_ref = refs.pop(0)
    (q_ref, kb_ref, vb_ref, rq_ref, rk_ref, rv_ref, rg_ref, u_ref, k5_hbm, v5_hbm,
     kbuf, vbuf, sems) = refs

    i = pl.program_id(0)
    slot = i % 2
    rows = min(tm, T)

    def kv_copies(s, step):
        cps = []
        for bb in range(tm // rows):
            r0 = step * tm + bb * rows
            b = r0 // T
            t0 = pl.multiple_of(r0 % T, SUBLANES)
            for buf, out in ((kbuf, k5_hbm), (vbuf, v5_hbm)):
                for h in range(A_HEADS):
                    cps.append(pltpu.make_async_copy(
                        buf.at[s, pl.ds(bb * rows, rows), pl.ds(h * A_V_DIM, A_V_DIM)],
                        out.at[layer, b, pl.ds(t0, rows), h, :], sems.at[s]))
        return cps

    @pl.when(i >= 2)
    def _():
        for cp in kv_copies(slot, i - 2):
            cp.wait()

    x = x_ref[...]
    if apply_ln:
        x = _layer_norm(x, g_ref[...], b_ref[...])
        xn_ref[...] = x
    xb = x.astype(BF16)

    def mm(c0, c1):
        return _dot(xb, w_ref[:, c0:c1])

    q_ref[...] = (mm(_OFF_Q, _OFF_K) * (LOG2E * HEAD_DIM ** -0.5)).astype(BF16)
    a = mm(_OFF_K, _OFF_V)
    kbuf[slot] = a
    kb_ref[...] = a.astype(BF16)
    a = mm(_OFF_V, _OFF_RQ)
    vbuf[slot] = a
    vb_ref[...] = a.astype(BF16)
    for cp in kv_copies(slot, i):
        cp.start()

    cos = cos_ref[...]
    sin = sin_ref[...]
    lane = lax.broadcasted_iota(jnp.int32, cos.shape, 1)
    lo32 = (lane & 63) < 32

    def rot(a, scale):
        parts = [_rotary128(a[:, c * LANES:(c + 1) * LANES], cos, sin, lo32) for c in range(2)]
        r = jnp.concatenate(parts, axis=1)
        if scale != 1.0:
            r = r * scale
        return r.astype(BF16)

    rq_ref[...] = rot(mm(_OFF_RQ, _OFF_RK), 1.0)
    rk_ref[...] = rot(mm(_OFF_RK, _OFF_RV), R_KEY_DIM ** -0.5)
    rv_ref[...] = mm(_OFF_RV, _OFF_RG).astype(BF16)
    a = mm(_OFF_RG, _OFF_C)
    rg_ref[...] = a * _sigmoid(a)
    a = mm(_OFF_C, IN_WIDTH)
    u_ref[...] = a[:, :C_WIDTH] * _sigmoid(a[:, C_WIDTH:])

    @pl.when(i == nsteps - 1)
    def _():
        for cp in kv_copies(slot, i):
            cp.wait()
        if nsteps > 1:
            for cp in kv_copies(1 - slot, i - 1):
                cp.wait()


def _inproj(x, w_bf, cos_t, sin_t, ln, tm, kv_prev, layer, depth, B, T):
    n = x.shape[0]
    assert (tm % T == 0 or T % tm == 0) and n % tm == 0
    nblk = cos_t.shape[0] // tm
    nsteps = n // tm
    row = lambda i: (i, 0)
    const = lambda i: (0, 0)
    tab = lambda i: (i % nblk, 0)
    in_specs = [pl.BlockSpec((tm, D_MODEL), row)]
    args = [x]
    if ln is not None:
        in_specs += [pl.BlockSpec((1, D_MODEL), const)] * 2
        args += [ln[0].reshape(1, D_MODEL), ln[1].reshape(1, D_MODEL)]
    in_specs += [pl.BlockSpec((D_MODEL, IN_WIDTH), const),
                 pl.BlockSpec((tm, LANES), tab), pl.BlockSpec((tm, LANES), tab)]
    args += [w_bf, cos_t, sin_t]
    aliases = {}
    if kv_prev is not None:
        k5_index = (1 if ln is not None else 0) + 8
        aliases = {len(args): k5_index, len(args) + 1: k5_index + 1}
        in_specs += [pl.BlockSpec(memory_space=pl.ANY)] * 2
        args += list(kv_prev)

    def o(width, dt):
        return jax.ShapeDtypeStruct((n, width), dt), pl.BlockSpec((tm, width), row)

    outs = []
    if ln is not None:
        outs.append(o(D_MODEL, F32))
    outs += [o(512, BF16), o(512, BF16), o(512, BF16),
             o(256, BF16), o(256, BF16), o(256, BF16), o(256, F32), o(256, F32)]
    kv5 = jax.ShapeDtypeStruct((depth, B, T, A_HEADS, A_V_DIM), F32)
    outs += [(kv5, pl.BlockSpec(memory_space=pl.ANY))] * 2
    res = pl.pallas_call(
        functools.partial(_inproj_kernel, apply_ln=ln is not None, has_prev=kv_prev is not None,
                          layer=layer, tm=tm, T=T, nsteps=nsteps),
        grid=(nsteps,),
        in_specs=in_specs,
        out_specs=[s for _, s in outs],
        out_shape=[s for s, _ in outs],
        scratch_shapes=[pltpu.VMEM((2, tm, A_WIDTH), F32), pltpu.VMEM((2, tm, A_WIDTH), F32),
                        pltpu.SemaphoreType.DMA((2,))],
        input_output_aliases=aliases,
        compiler_params=_params(("arbitrary",), 52),
        name="inproj",
    )(*args)
    if ln is None:
        res = [x] + list(res)
    return res


def _lambda(lam_ref, lam_init):
    lv = lam_ref[...]
    s1 = jnp.sum(lv[0:1] * lv[1:2], axis=-1, keepdims=True)
    s2 = jnp.sum(lv[2:3] * lv[3:4], axis=-1, keepdims=True)
    return jnp.exp(s1) - jnp.exp(s2) + lam_init


def _stack_maps(q):
    lane = lax.broadcasted_iota(jnp.int32, q.shape, 1)
    lo = lane < HEAD_DIM
    z = jnp.zeros_like(q)
    return jnp.concatenate([jnp.where(lo, q, z), jnp.where(lo, z, q)], axis=0)


def _diff_finish(o2, t, lam, g, lam_init):
    o = o2[:t] - lam * o2[t:]
    ms = jnp.mean(o * o, axis=-1, keepdims=True)
    return (o * lax.rsqrt(ms + LN_EPS) * g * (1.0 - lam_init)).astype(BF16)


def _attn_prompt_kernel(lam_ref, g_ref, bias_ref, q_ref, k_ref, v_ref, o_ref,
                        s_scr, m_scr, p_scr, vo_scr, bp_scr, bd_scr, *, T, TB, lam_init):
    lam = _lambda(lam_ref, lam_init)
    g = g_ref[...]
    bt = bias_ref[...]
    bt = (bt - bt[0:1, 0:1]) * LOG2E
    bp_scr[...] = bt[:, :TB]
    bd_scr[...] = bt[:, TB:]
    half = TB // 2
    vo_scr[:, :A_V_DIM] = v_ref[...]
    vo_scr[:, A_V_DIM:] = jnp.ones((T, A_V_DIM), BF16)

    def pass1(qi):
        r0 = qi * TB
        buf = qi % 2
        q = q_ref[r0:r0 + TB, :]
        lo = lax.broadcasted_iota(jnp.int32, q.shape, 1) < HEAD_DIM
        zq = jnp.zeros_like(q)
        maps = (jnp.where(lo, q, zq), jnp.where(lo, zq, q))
        for mp in range(2):
            m = None
            for ki in range(qi + 1):
                s = _dot_nt(maps[mp], k_ref[ki * TB:(ki + 1) * TB, :])
                if ki == qi - 1:
                    s = s + bp_scr[...]
                elif ki == qi:
                    row = lax.broadcasted_iota(jnp.int32, (TB, TB), 0)
                    col = lax.broadcasted_iota(jnp.int32, (TB, TB), 1)
                    vis = (col >> _LOG2_CHUNK) <= (row >> _LOG2_CHUNK)
                    s = jnp.where(vis, s + bd_scr[...], NEG_INF)
                s_scr[buf, ki, mp * TB:(mp + 1) * TB, :] = s
                mt = jnp.maximum(s[:, :half], s[:, half:])
                m = mt if m is None else jnp.maximum(m, mt)
            m_scr[buf, mp * TB:(mp + 1) * TB, :] = jnp.broadcast_to(jnp.max(m, axis=1, keepdims=True),
                                                                    (TB, LANES))

    def pass2(qi):
        r0 = qi * TB
        buf = qi % 2
        mb = m_scr[buf]
        mb2 = jnp.concatenate([mb, mb], axis=1)
        for ki in range(qi + 1):
            p_scr[buf, :, ki * TB:(ki + 1) * TB] = jnp.exp2(s_scr[buf, ki] - mb2).astype(BF16)
        kk = (qi + 1) * TB
        out = _dot(p_scr[buf, :, 0:kk], vo_scr[0:kk, :])
        o_ref[r0:r0 + TB, :] = _diff_finish(out[:, :A_V_DIM] / out[:, A_V_DIM:], TB, lam, g, lam_init)

    nq = T // TB
    pass1(0)
    for qi in range(nq):
        if qi + 1 < nq:
            pass1(qi + 1)
        pass2(qi)


def _attn_prompt(lam4, g, bias, q, k, v, B, T, lam_init):
    TB = 2 * LANES
    assert T % TB == 0 and TB % CHUNK == 0
    n = B * T
    blk = pl.BlockSpec((T, A_V_DIM), lambda b, h: (b, h))
    return pl.pallas_call(
        functools.partial(_attn_prompt_kernel, T=T, TB=TB, lam_init=lam_init),
        grid=(B, A_HEADS),
        in_specs=[pl.BlockSpec((4, HEAD_DIM), lambda b, h: (0, 0)),
                  pl.BlockSpec((1, A_V_DIM), lambda b, h: (0, 0)),
                  pl.BlockSpec((None, TB, 2 * TB), lambda b, h: (h, 0, 0)),
                  blk, blk, blk],
        out_specs=blk,
        out_shape=jax.ShapeDtypeStruct((n, A_WIDTH), BF16),
        scratch_shapes=[pltpu.VMEM((2, T // TB, 2 * TB, TB), F32),
                        pltpu.VMEM((2, 2 * TB, LANES), F32),
                        pltpu.VMEM((2, 2 * TB, T), BF16),
                        pltpu.VMEM((T, 2 * A_V_DIM), BF16),
                        pltpu.VMEM((TB, TB), F32),
                        pltpu.VMEM((TB, TB), F32)],
        compiler_params=_params(("arbitrary", "arbitrary"), 32),
        name="attn_prompt",
    )(lam4, g, bias, q, k, v)


def _attn_decode_kernel(lam_ref, g_ref, bias_ref, q_ref, kn_ref, vn_ref, ck_hbm, cv_hbm, o_ref,
                        kc_buf, vc_buf, sems, *, Ts, past, layer, lam_init):
    b = pl.program_id(0)
    h = pl.program_id(1)
    step = b * A_HEADS + h
    nsteps = pl.num_programs(0) * A_HEADS
    slot = step % 2

    def cache_copies(st, s):
        bb = st // A_HEADS
        hh = st % A_HEADS
        return [pltpu.make_async_copy(src.at[layer, bb, :, hh, :], dst.at[s], sems.at[s, j])
                for j, (src, dst) in enumerate(((ck_hbm, kc_buf), (cv_hbm, vc_buf)))]

    @pl.when(step == 0)
    def _():
        for cp in cache_copies(0, 0):
            cp.start()

    @pl.when(step + 1 < nsteps)
    def _():
        for cp in cache_copies(step + 1, 1 - slot):
            cp.start()

    for cp in cache_copies(step, slot):
        cp.wait()
    kc_ref = kc_buf.at[slot]
    vc_ref = vc_buf.at[slot]

    lam = _lambda(lam_ref, lam_init)
    q2 = _stack_maps(q_ref[...])
    bias = bias_ref[...] * LOG2E
    s_p = _dot_nt(q2, kc_ref[...].astype(BF16)).reshape(2, Ts, past) + bias[:, :past][None]
    s_n = _dot_nt(q2, kn_ref[...]).reshape(2, Ts, Ts) + bias[:, past:past + Ts][None]
    s_p = s_p.reshape(2 * Ts, past)
    s_n = s_n.reshape(2 * Ts, Ts)
    m = jnp.maximum(jnp.max(s_p, axis=1, keepdims=True), jnp.max(s_n, axis=1, keepdims=True))
    p_p = jnp.exp2(s_p - m)
    p_n = jnp.exp2(s_n - m)
    l = jnp.sum(p_p, axis=1, keepdims=True) + jnp.sum(p_n, axis=1, keepdims=True)
    acc = _dot(p_p.astype(BF16), vc_ref[...].astype(BF16)) + _dot(p_n.astype(BF16), vn_ref[...])
    o_ref[...] = _diff_finish(acc / l, Ts, lam, g_ref[...], lam_init)


def _attn_decode(lam4, g, bias, q, cache_k, cache_v, kn, vn, layer, Bs, Ts, lam_init):
    past = cache_k.shape[2]
    assert past % CHUNK == 0 and Ts <= CHUNK
    padk = bias.shape[2]
    new = pl.BlockSpec((Ts, A_V_DIM), lambda b, h: (b, h))
    cache = pl.BlockSpec(memory_space=pl.ANY)
    return pl.pallas_call(
        functools.partial(_attn_decode_kernel, Ts=Ts, past=past, layer=layer, lam_init=lam_init),
        grid=(Bs, A_HEADS),
        in_specs=[pl.BlockSpec((4, HEAD_DIM), lambda b, h: (0, 0)),
                  pl.BlockSpec((1, A_V_DIM), lambda b, h: (0, 0)),
                  pl.BlockSpec((None, Ts, padk), lambda b, h: (h, 0, 0)),
                  new, new, new, cache, cache],
        out_specs=new,
        out_shape=jax.ShapeDtypeStruct((Bs * Ts, A_WIDTH), BF16),
        scratch_shapes=[pltpu.VMEM((2, past, A_V_DIM), F32), pltpu.VMEM((2, past, A_V_DIM), F32),
                        pltpu.SemaphoreType.DMA((2, 2))],
        compiler_params=_params(("arbitrary", "arbitrary"), 40),
        name="attn_decode",
    )(lam4, g, bias, q, kn, vn, cache_k, cache_v)


def _ret_kernel(q_ref, k_ref, v_ref, g_ref, s0_ref, dm_ref, qd_ref, kd_ref, cd_ref,
                o_ref, sn_ref, *, T, C):
    z = jnp.zeros((R_KEY_DIM, R_V_DIM), F32)
    state = jnp.concatenate([jnp.concatenate([s0_ref[0], z], axis=1),
                             jnp.concatenate([z, s0_ref[1]], axis=1)], axis=0)
    r = lax.broadcasted_iota(jnp.int32, (LANES, LANES), 0)
    c = lax.broadcasted_iota(jnp.int32, (LANES, LANES), 1)
    same_head = (r >> _LOG2_CHUNK) == (c >> _LOG2_CHUNK)
    ones_bd = jnp.where(same_head, 1.0, 0.0).astype(BF16)
    lo = lax.broadcasted_iota(jnp.int32, (C, LANES), 1) < R_V_DIM
    cd = cd_ref[...]

    def chunk(n, state):
        r0 = n * C
        q = q_ref[pl.ds(r0, C), :]
        k = k_ref[pl.ds(r0, C), :]
        v = v_ref[pl.ds(r0, C), :]
        p = (_dot_nt(_stack_maps(q), k) * dm_ref[...]).astype(BF16)
        o2 = _dot(p, v)
        inner = jnp.where(lo, o2[:C], o2[C:])
        qd = (q.astype(F32) * qd_ref[...]).astype(BF16)
        o = inner + _dot(qd, state.astype(BF16))
        vk = (v.astype(F32) * kd_ref[...]).astype(BF16)
        new_state = cd * state + jnp.where(same_head, _dot_tn(k, vk), 0.0)
        oo = o * o
        hi = oo.astype(BF16)
        lo_part = (oo - hi.astype(F32)).astype(BF16)
        ss = _dot(hi, ones_bd) + _dot(lo_part, ones_bd)
        out = o * lax.rsqrt(ss * (1.0 / R_V_DIM) + LN_EPS) * g_ref[pl.ds(r0, C), :]
        o_ref[pl.ds(r0, C), :] = out.astype(BF16)
        return new_state

    for n in range(T // C):
        state = chunk(n, state)
    sn_ref[0] = state[:R_KEY_DIM, :R_V_DIM]
    sn_ref[1] = state[R_KEY_DIM:, R_V_DIM:]


def _retention(rq, rk, rv, rg, state0, tabs, B, T, C):
    dm, qd, kd, cd = tabs
    blk = pl.BlockSpec((T, LANES), lambda b, hp: (b, hp))
    st = pl.BlockSpec((None, 2, R_KEY_DIM, R_V_DIM), lambda b, hp: (b, hp, 0, 0))
    return pl.pallas_call(
        functools.partial(_ret_kernel, T=T, C=C),
        grid=(B, 2),
        in_specs=[blk, blk, blk, blk, st,
                  pl.BlockSpec((None, 2 * C, C), lambda b, hp: (hp, 0, 0)),
                  pl.BlockSpec((None, C, LANES), lambda b, hp: (hp, 0, 0)),
                  pl.BlockSpec((None, C, LANES), lambda b, hp: (hp, 0, 0)),
                  pl.BlockSpec((None, 1, LANES), lambda b, hp: (hp, 0, 0))],
        out_specs=[blk, st],
        out_shape=[jax.ShapeDtypeStruct((B * T, R_WIDTH), BF16),
                   jax.ShapeDtypeStruct((B, R_HEADS, R_KEY_DIM, R_V_DIM), F32)],
        compiler_params=_params(("arbitrary", "arbitrary"), 32),
        name="retention",
    )(rq, rk, rv, rg, state0, dm, qd, kd, cd)


def _retention_tables(C):
    h = jnp.arange(R_HEADS, dtype=F32)
    log_g = jnp.log1p(-jnp.exp2(-5.0 - h))
    i = jnp.arange(C, dtype=F32)
    diff = i[:, None] - i[None, :]
    dmask = jnp.where(diff >= 0, jnp.exp(jnp.maximum(diff, 0.0)[None] * log_g[:, None, None]), 0.0)
    dm = dmask.reshape(2, 2 * C, C)
    q_dec = jnp.exp((i + 1.0)[None, :] * log_g[:, None])
    k_dec = jnp.exp((C - 1 - i)[None, :] * log_g[:, None])
    c_dec = jnp.exp(C * log_g)

    def lanes(t):
        t = jnp.repeat(t[:, :, None], R_V_DIM, axis=2).reshape(2, 2, C, R_V_DIM)
        return jnp.concatenate([t[:, 0], t[:, 1]], axis=-1)

    cd = jnp.repeat(c_dec[:, None], R_V_DIM, axis=1).reshape(2, 1, LANES)
    return dm, lanes(q_dec), lanes(k_dec), cd


_CONV_PAD = 32


def _conv_kernel(u_ref, h_ref, w_ref, b_ref, g_ref, be_ref, o_ref, t_ref, up_scr, *, T, RT):
    hist = CONV_WIDTH - 1
    off = _CONV_PAD - hist
    up_scr[off:_CONV_PAD, :] = h_ref[...]
    up_scr[_CONV_PAD:_CONV_PAD + T, :] = u_ref[...]
    bias = b_ref[...]
    g = g_ref[...]
    be = be_ref[...]
    win_rows = RT + _CONV_PAD
    for t0 in range(0, T, RT):
        win = up_scr[t0:t0 + win_rows, :]
        acc = jnp.zeros((RT, C_WIDTH), F32)
        for s in range(SUBLANES):
            rolled = win if s == 0 else pltpu.roll(win, win_rows - s, 0)
            for a in range(_CONV_PAD // SUBLANES + 1):
                j = a * SUBLANES + s - off
                if 0 <= j < CONV_WIDTH:
                    acc = acc + rolled[a * SUBLANES:a * SUBLANES + RT, :] * w_ref[j:j + 1, :]
        y = _layer_norm(acc + bias, g, be)
        o_ref[t0:t0 + RT, :] = (y * _sigmoid(y)).astype(BF16)
    t_ref[...] = up_scr[T + off:T + _CONV_PAD, :]


def _conv(u, hist, dw_w, dw_b, ln_g, ln_b, B, T):
    RT = min(T, LANES)
    assert T % RT == 0
    hl = CONV_WIDTH - 1
    vec = pl.BlockSpec((1, C_WIDTH), lambda b: (0, 0))
    hb = pl.BlockSpec((None, hl, C_WIDTH), lambda b: (b, 0, 0))
    return pl.pallas_call(
        functools.partial(_conv_kernel, T=T, RT=RT),
        grid=(B,),
        in_specs=[pl.BlockSpec((T, C_WIDTH), lambda b: (b, 0)), hb,
                  pl.BlockSpec((CONV_WIDTH, C_WIDTH), lambda b: (0, 0)), vec, vec, vec],
        out_specs=[pl.BlockSpec((T, C_WIDTH), lambda b: (b, 0)), hb],
        out_shape=[jax.ShapeDtypeStruct((B * T, C_WIDTH), BF16),
                   jax.ShapeDtypeStruct((B, hl, C_WIDTH), F32)],
        scratch_shapes=[pltpu.VMEM((T + _CONV_PAD, C_WIDTH), F32)],
        compiler_params=_params(("arbitrary",), 32),
        name="conv",
    )(u, hist, dw_w, dw_b.reshape(1, C_WIDTH), ln_g.reshape(1, C_WIDTH), ln_b.reshape(1, C_WIDTH))


def _first_index(vals, target):
    idx = jnp.full(target.shape, len(vals) - 1, jnp.int32)
    for j in range(len(vals) - 2, -1, -1):
        idx = jnp.where(vals[j] == target, j, idx)
    return idx


def _select(idx, vals):
    out = vals[-1]
    for j in range(len(vals) - 2, -1, -1):
        out = jnp.where(idx == j, vals[j], out)
    return out


def _route(logits_t):
    rows = [logits_t[e:e + 1, :] for e in range(N_EXPERTS)]
    m = functools.reduce(jnp.maximum, rows)
    ex = [jnp.exp(r - m) for r in rows]
    z = functools.reduce(jnp.add, ex)
    sc = [e / z for e in ex]
    v1s, v2s, i1s, i2s, gss = [], [], [], [], []
    for g in range(N_GROUPS):
        a = sc[g * EXPERTS_PER_GROUP:(g + 1) * EXPERTS_PER_GROUP]
        v1 = functools.reduce(jnp.maximum, a)
        i1 = _first_index(a, v1)
        rest = [jnp.where(i1 == j, -1.0, a[j]) for j in range(EXPERTS_PER_GROUP)]
        v2 = functools.reduce(jnp.maximum, rest)
        i2 = _first_index(rest, v2)
        v1s.append(v1); v2s.append(v2); i1s.append(i1); i2s.append(i2); gss.append(v1 + v2)
    grp = _first_index(gss, functools.reduce(jnp.maximum, gss))
    v1 = _select(grp, v1s)
    v2 = _select(grp, v2s)
    e1 = _select(grp, i1s) + grp * EXPERTS_PER_GROUP
    e2 = _select(grp, i2s) + grp * EXPERTS_PER_GROUP
    den = v1 + v2
    return e1, e2, v1 / den, v2 / den


def _expert_onehot(e1, e2):
    rows = [jnp.where(e1 == e, 1.0, 0.0) + jnp.where(e2 == e, 1.0, 0.0) for e in range(N_EXPERTS)]
    return jnp.concatenate(rows, axis=0)


def _outproj_kernel(a_ref, r_ref, c_ref, x_ref, w_ref, g_ref, b_ref, wrh_ref, wrl_ref, br_ref,
                    x1_ref, x1b_ref, ri_ref, rw_ref, cnt_ref, *, alpha, tb, sub):
    tm = x_ref.shape[0]
    ones = jnp.ones((SUBLANES, sub), BF16)
    groups = [slice(h * sub, (h + 1) * sub) for h in range(tm // sub)]
    mixes = [(_dot(a_ref[rs, :], w_ref[0:A_WIDTH, :])
              + _dot(r_ref[rs, :], w_ref[A_WIDTH:A_WIDTH + R_WIDTH, :])
              + _dot(c_ref[rs, :], w_ref[A_WIDTH + R_WIDTH:, :])) for rs in groups]
    logits = []
    for rs, mix in zip(groups, mixes):
        x1 = _layer_norm(alpha * x_ref[rs, :] + mix, g_ref[...], b_ref[...])
        x_hi = x1.astype(BF16)
        x1_ref[rs, :] = x1
        x1b_ref[rs, :] = x_hi
        x_lo = (x1 - x_hi.astype(F32)).astype(BF16)
        logits.append(_dot(x_hi, wrh_ref[...]) + _dot(x_lo, wrh_ref[...]) + _dot(x_hi, wrl_ref[...]))
    counts = []
    for rs, lg in zip(groups, logits):
        e1, e2, w1, w2 = _route(lg.T[0:N_EXPERTS, :] + br_ref[...])
        ri_ref[:, rs] = jnp.concatenate([e1, e2, jnp.zeros((SUBLANES - 2, sub), jnp.int32)], axis=0)
        rw_ref[:, rs] = jnp.concatenate([w1, w2, jnp.zeros((SUBLANES - 2, sub), F32)], axis=0)
        counts.append(_dot_nt(ones, _expert_onehot(e1, e2).astype(BF16)))
    per = tb // sub
    for k in range(tm // tb):
        c = functools.reduce(jnp.add, counts[k * per:(k + 1) * per])
        c = jnp.concatenate([c, jnp.zeros((SUBLANES, LANES - N_EXPERTS), F32)], axis=1)
        cnt_ref[k * SUBLANES:(k + 1) * SUBLANES, :] = c.astype(jnp.int32)


def _outproj(attn, ret, conv, x, w_bf, ln_g, ln_b, w_router, br, alpha, tm, tb):
    n = x.shape[0]
    wr = jnp.pad(w_router, ((0, 0), (0, LANES - N_EXPERTS)))
    wr_hi = wr.astype(BF16)
    wr_lo = (wr - wr_hi.astype(F32)).astype(BF16)
    row = lambda i: (i, 0)
    col = lambda i: (0, i)
    const = lambda i: (0, 0)
    nsub = tm // tb
    return pl.pallas_call(
        functools.partial(_outproj_kernel, alpha=alpha, tb=tb, sub=min(tb, 2 * LANES)),
        grid=(n // tm,),
        in_specs=[pl.BlockSpec((tm, A_WIDTH), row), pl.BlockSpec((tm, R_WIDTH), row),
                  pl.BlockSpec((tm, C_WIDTH), row), pl.BlockSpec((tm, D_MODEL), row),
                  pl.BlockSpec((D_MODEL, D_MODEL), const),
                  pl.BlockSpec((1, D_MODEL), const), pl.BlockSpec((1, D_MODEL), const),
                  pl.BlockSpec((D_MODEL, LANES), const), pl.BlockSpec((D_MODEL, LANES), const),
                  pl.BlockSpec((N_EXPERTS, 1), const)],
        out_specs=[pl.BlockSpec((tm, D_MODEL), row), pl.BlockSpec((tm, D_MODEL), row),
                   pl.BlockSpec((SUBLANES, tm), col), pl.BlockSpec((SUBLANES, tm), col),
                   pl.BlockSpec((nsub * SUBLANES, LANES), row)],
        out_shape=[jax.ShapeDtypeStruct((n, D_MODEL), F32), jax.ShapeDtypeStruct((n, D_MODEL), BF16),
                   jax.ShapeDtypeStruct((SUBLANES, n), jnp.int32),
                   jax.ShapeDtypeStruct((SUBLANES, n), F32),
                   jax.ShapeDtypeStruct((n // tb * SUBLANES, LANES), jnp.int32)],
        compiler_params=_params(("arbitrary",), 40),
        name="outproj",
    )(attn, ret, conv, x, w_bf, ln_g.reshape(1, D_MODEL), ln_b.reshape(1, D_MODEL), wr_hi, wr_lo, br)


_UNIT = 16
_XS_WIDTH = D_MODEL + LANES
_FFN_ROWS = 512
_DISPATCH_TOKENS = 512


def _sorted_positions(e1, e2, lo_ref, base, tb):
    onehot = _expert_onehot(e1, e2).astype(BF16)
    r = lax.broadcasted_iota(jnp.int32, (tb, tb), 0)
    c = lax.broadcasted_iota(jnp.int32, (tb, tb), 1)
    earlier = jnp.where(r < c, 1.0, 0.0).astype(BF16)
    rank = _dot(onehot, earlier).astype(jnp.int32)
    pos1 = jnp.zeros(e1.shape, jnp.int32)
    pos2 = jnp.zeros(e1.shape, jnp.int32)
    for e in range(N_EXPERTS):
        p = rank[e:e + 1, :] + lo_ref[base + e]
        pos1 = jnp.where(e1 == e, p, pos1)
        pos2 = jnp.where(e2 == e, p, pos2)
    return pos1, pos2


def _permutation(pos1, pos2, rows):
    j = lax.broadcasted_iota(jnp.int32, (rows, pos1.shape[1]), 0)
    return jnp.where(j == pos1, 1.0, jnp.where(j == pos2, 1.0, 0.0)).astype(BF16)


def _dispatch_kernel(lo_ref, urow_ref, ri_ref, rw_ref, x_ref, xs_hbm, pos_ref,
                     stage, sems, *, nblk, tb, cap, spare_row):
    i = pl.program_id(0)
    slot = i % 2
    units = cap // _UNIT

    def wait_slot(s):
        pltpu.make_async_copy(stage.at[s], xs_hbm.at[pl.ds(0, cap), :], sems.at[s]).wait()

    @pl.when(i >= 2)
    def _():
        wait_slot(slot)

    e1 = ri_ref[0:1, :]
    e2 = ri_ref[1:2, :]
    w1 = rw_ref[0:1, :]
    w2 = rw_ref[1:2, :]
    pos1, pos2 = _sorted_positions(e1, e2, lo_ref, i * N_EXPERTS, tb)
    pos_ref[...] = jnp.concatenate([pos1, pos2, jnp.zeros((SUBLANES - 2, tb), jnp.int32)], axis=0)
    perm = _permutation(pos1, pos2, cap)
    stage[slot, :, 0:D_MODEL] = _dot(perm, x_ref[...]).astype(BF16)
    j = lax.broadcasted_iota(jnp.int32, (cap, tb), 0)
    gate = jnp.sum(jnp.where(j == pos1, w1, jnp.where(j == pos2, w2, 0.0)), axis=1, keepdims=True)
    g0 = gate.astype(BF16).astype(F32)
    g1 = (gate - g0).astype(BF16).astype(F32)
    g2 = gate - g0 - g1
    lane = lax.broadcasted_iota(jnp.int32, (cap, LANES), 1)
    pieces = jnp.where(lane == 0, g0, jnp.where(lane == 1, g1, jnp.where(lane == 2, g2, 0.0)))
    stage[slot, :, D_MODEL:_XS_WIDTH] = pieces.astype(BF16)

    def body(u, c):
        t = urow_ref[i * units + u]
        s = pl.multiple_of(u * _UNIT, _UNIT)
        d = pl.multiple_of(jnp.where(t >= 0, t, spare_row + slot * cap + s), _UNIT)
        pltpu.make_async_copy(stage.at[slot, pl.ds(s, _UNIT), :], xs_hbm.at[pl.ds(d, _UNIT), :],
                              sems.at[slot]).start()
        return c

    lax.fori_loop(0, units, body, 0)

    @pl.when(i == nblk - 1)
    def _():
        wait_slot(slot)
        if nblk > 1:
            wait_slot(1 - slot)


def _ffn_kernel(te_ref, tv_ref, nt_ref, xs_ref, wg_ref, wu_ref, wd_ref, ys_ref, wg_b, wu_b, wd_b):
    j = pl.program_id(0)
    jm = jnp.maximum(j, 1)

    @pl.when((j == 0) | (te_ref[j] != te_ref[jm - 1]))
    def _():
        wg_b[...] = wg_ref[...].astype(BF16)
        wu_b[...] = wu_ref[...].astype(BF16)
        wd_b[...] = wd_ref[...].astype(BF16)

    @pl.when(j < nt_ref[0])
    def _():
        rows = xs_ref.shape[0]
        valid = lax.broadcasted_iota(jnp.int32, (rows, LANES), 0) < tv_ref[j]
        zero = jnp.zeros((rows, LANES), BF16)
        x = jnp.concatenate([jnp.where(valid, xs_ref[:, c * LANES:(c + 1) * LANES], zero)
                             for c in range(D_MODEL // LANES)], axis=1)
        gp = jnp.where(valid, xs_ref[:, D_MODEL:_XS_WIDTH], zero).astype(F32)
        g = gp[:, 0:1] + gp[:, 1:2] + gp[:, 2:3]
        hg = _dot(x, wg_b[...])
        hu = _dot(x, wu_b[...])
        h = hg * _sigmoid(hg) * hu * g
        ys_ref[...] = _dot(h.astype(BF16), wd_b[...]).astype(BF16)


def _combine_kernel(urow_ref, pos_ref, x1_ref, p_ref, g_ref, b_ref, wpg_ref, bpg_ref, wpp_ref,
                    ys_hbm, out_ref, stage, sems, *, nblk, tb, cap, alpha):
    i = pl.program_id(0)
    slot = i % 2
    units = cap // _UNIT

    def fetch(blk, s):
        def body(u, c):
            sr = pl.multiple_of(jnp.maximum(urow_ref[blk * units + u], 0), _UNIT)
            ds = pl.multiple_of(u * _UNIT, _UNIT)
            pltpu.make_async_copy(ys_hbm.at[pl.ds(sr, _UNIT), :], stage.at[s, pl.ds(ds, _UNIT), :],
                                  sems.at[s]).start()
            return c

        lax.fori_loop(0, units, body, 0)

    @pl.when(i == 0)
    def _():
        fetch(0, 0)

    @pl.when(i + 1 < nblk)
    def _():
        fetch(i + 1, 1 - slot)

    pltpu.make_async_copy(ys_hbm.at[pl.ds(0, cap), :], stage.at[slot], sems.at[slot]).wait()
    perm = _permutation(pos_ref[0:1, :], pos_ref[1:2, :], cap)
    y = _dot_tn(perm, stage[slot])
    x2 = _layer_norm(alpha * x1_ref[...] + y, g_ref[...], b_ref[...])
    gate = _sigmoid(_dot(x2.astype(BF16), wpg_ref[...]) + bpg_ref[...])
    out_ref[...] = x2 + gate * _dot(p_ref[...].astype(BF16), wpp_ref[...])


def _moe_plan(cnt, nblk, tb, ffn_rows):
    cnt = cnt.reshape(nblk, SUBLANES, LANES)[:, 0, :N_EXPERTS]
    pc = (cnt + (_UNIT - 1)) // _UNIT * _UNIT
    lo = jnp.cumsum(pc, axis=1) - pc
    tot = jnp.sum(pc, axis=0)
    reg = (tot + (ffn_rows - 1)) // ffn_rows * ffn_rows
    rstart = jnp.cumsum(reg) - reg
    gs = rstart[None, :] + jnp.cumsum(pc, axis=0) - pc
    tiles_e = reg // ffn_rows
    tile_end = jnp.cumsum(tiles_e)
    max_rows = 2 * nblk * tb + nblk * N_EXPERTS * _UNIT + N_EXPERTS * ffn_rows
    max_tiles = -(-max_rows // ffn_rows)
    tj =jnp.arange(max_tiles, dtype=jnp.int32)
    te = jnp.minimum(jnp.sum((tj[:, None] >= tile_end[None, :]).astype(jnp.int32), axis=1), N_EXPERTS - 1)
    tv = jnp.clip(tot[te] - (tj - (tile_end - tiles_e)[te]) * ffn_rows, 0, ffn_rows)
    nt = tile_end[-1:].astype(jnp.int32)
    cap = 2 * tb + N_EXPERTS * _UNIT
    urow0 = jnp.arange(cap // _UNIT, dtype=jnp.int32) * _UNIT
    run = jnp.sum((urow0[None, :, None] >= (lo + pc)[:, None, :]).astype(jnp.int32), axis=2)
    pick = run[:, :, None] == jnp.arange(N_EXPERTS, dtype=jnp.int32)[None, None, :]
    urow = jnp.sum(jnp.where(pick, (gs - lo)[:, None, :], 0), axis=2) + urow0[None, :]
    urow = jnp.where(urow0[None, :] < jnp.sum(pc, axis=1, keepdims=True), urow, -1)
    flat = lambda a: a.reshape(-1).astype(jnp.int32)
    return flat(lo), flat(urow), te.astype(jnp.int32), tv.astype(jnp.int32), nt, max_tiles, cap


def _moe(x1b, x1, ri, rw, cnt, p, wg, wu, wd, layer, ln_g, ln_b, wpg, bpg, wpp, alpha, tb):
    n = x1.shape[0]
    nblk = n // tb
    ffn_rows = _FFN_ROWS if 2 * n >= N_EXPERTS * _FFN_ROWS else LANES
    lo, urow, te, tv, nt, max_tiles, cap = _moe_plan(cnt, nblk, tb, ffn_rows)
    max_rows = max_tiles * ffn_rows

    xs, pos = pl.pallas_call(
        functools.partial(_dispatch_kernel, nblk=nblk, tb=tb, cap=cap, spare_row=max_rows),
        grid_spec=pltpu.PrefetchScalarGridSpec(
            num_scalar_prefetch=2,
            grid=(nblk,),
            in_specs=[pl.BlockSpec((SUBLANES, tb), lambda i, *_: (0, i)),
                      pl.BlockSpec((SUBLANES, tb), lambda i, *_: (0, i)),
                      pl.BlockSpec((tb, D_MODEL), lambda i, *_: (i, 0))],
            out_specs=[pl.BlockSpec(memory_space=pl.ANY),
                       pl.BlockSpec((SUBLANES, tb), lambda i, *_: (0, i))],
            scratch_shapes=[pltpu.VMEM((2, cap, _XS_WIDTH), BF16),
                            pltpu.SemaphoreType.DMA((2,))]),
        out_shape=[jax.ShapeDtypeStruct((max_rows + 2 * cap, _XS_WIDTH), BF16),
                   jax.ShapeDtypeStruct((SUBLANES, n), jnp.int32)],
        compiler_params=_params(("arbitrary",), 32),
        name="moe_dispatch",
    )(lo, urow, ri, rw, x1b)

    def tile(j, te_ref, tv_ref, nt_ref):
        return jnp.minimum(j, nt_ref[0] - 1)

    ys = pl.pallas_call(
        _ffn_kernel,
        grid_spec=pltpu.PrefetchScalarGridSpec(
            num_scalar_prefetch=3,
            grid=(max_tiles,),
            in_specs=[pl.BlockSpec((ffn_rows, _XS_WIDTH), lambda j, *s: (tile(j, *s), 0)),
                      pl.BlockSpec((None, None, D_MODEL, D_EXPERT),
                                   lambda j, *s: (layer, s[0][tile(j, *s)], 0, 0)),
                      pl.BlockSpec((None, None, D_MODEL, D_EXPERT),
                                   lambda j, *s: (layer, s[0][tile(j, *s)], 0, 0)),
                      pl.BlockSpec((None, None, D_EXPERT, D_MODEL),
                                   lambda j, *s: (layer, s[0][tile(j, *s)], 0, 0))],
            out_specs=pl.BlockSpec((ffn_rows, D_MODEL), lambda j, *s: (tile(j, *s), 0)),
            scratch_shapes=[pltpu.VMEM((D_MODEL, D_EXPERT), BF16), pltpu.VMEM((D_MODEL, D_EXPERT), BF16),
                            pltpu.VMEM((D_EXPERT, D_MODEL), BF16)]),
        out_shape=jax.ShapeDtypeStruct((max_rows, D_MODEL), BF16),
        compiler_params=_params(("arbitrary",), 40),
        name="moe_ffn",
    )(te, tv, nt, xs, wg, wu, wd)

    const = lambda i, *_: (0, 0)
    return pl.pallas_call(
        functools.partial(_combine_kernel, nblk=nblk, tb=tb, cap=cap, alpha=alpha),
        grid_spec=pltpu.PrefetchScalarGridSpec(
            num_scalar_prefetch=1,
            grid=(nblk,),
            in_specs=[pl.BlockSpec((SUBLANES, tb), lambda i, *_: (0, i)),
                      pl.BlockSpec((tb, D_MODEL), lambda i, *_: (i, 0)),
                      pl.BlockSpec((None, tb, PLE_DIM), lambda i, *_: (layer, i, 0)),
                      pl.BlockSpec((1, D_MODEL), const), pl.BlockSpec((1, D_MODEL), const),
                      pl.BlockSpec((D_MODEL, D_MODEL), const), pl.BlockSpec((1, D_MODEL), const),
                      pl.BlockSpec((PLE_DIM, D_MODEL), const),
                      pl.BlockSpec(memory_space=pl.ANY)],
            out_specs=pl.BlockSpec((tb, D_MODEL), lambda i, *_: (i, 0)),
            scratch_shapes=[pltpu.VMEM((2, cap, D_MODEL), BF16),
                            pltpu.SemaphoreType.DMA((2,))]),
        out_shape=jax.ShapeDtypeStruct((n, D_MODEL), F32),
        compiler_params=_params(("arbitrary",), 40),
        name="moe_combine",
    )(urow, pos, x1, p, ln_g.reshape(1, D_MODEL), ln_b.reshape(1, D_MODEL),
      wpg, bpg.reshape(1, D_MODEL), wpp, ys)


def _rope_tables(pos, rows):
    half = R_KEY_DIM // 2
    inv_freq = 1.0 / (ROPE_BASE ** jnp.linspace(0.0, 1.0, half, dtype=jnp.float32))
    ang = pos.astype(jnp.float32)[:, None] * inv_freq[None, :]
    cos = jnp.cos(ang)
    sin = jnp.sin(ang)
    cos_t = jnp.tile(cos, (rows // pos.shape[0], 4))
    sin_t = jnp.tile(jnp.concatenate([-sin, sin], axis=1), (rows // pos.shape[0], 2))
    return cos_t, sin_t


def kernel(x_prompt, x_sample, p_prompt, p_sample, cache_k, cache_v, state_ret, state_conv, ln_emb_g, ln_emb_b, rel_bias, w_router, b_router, w_in, lam_q1, lam_k1, lam_q2, lam_k2, subln_g, dw_w, dw_b, conv_ln_g, conv_ln_b, w_out, ln1_g, ln1_b, w_exp_gate, w_exp_up, w_exp_down, ln2_g, ln2_b, w_ple_gate, b_ple_gate, w_ple_proj):
    B, T, D = x_prompt.shape
    Bs, Ts, _ = x_sample.shape
    depth = w_in.shape[0]
    past = cache_k.shape[2]
    n_p, n_s = B * T, Bs * Ts
    alpha = (2 * depth) ** 0.25

    tm_p = 512 if n_p % 512 == 0 else n_p
    tm_s = n_s
    TB = 2 * LANES
    c_p = min(T, 2 * LANES)

    pos_p = jnp.arange(T, dtype=jnp.int32)
    pos_s = past + jnp.arange(Ts, dtype=jnp.int32)
    rope_p = _rope_tables(pos_p, max(T, tm_p))
    rope_s = _rope_tables(pos_s, max(Ts, tm_s))
    rel_p = (jnp.arange(2 * TB, dtype=jnp.int32)[None, :] - TB) - jnp.arange(TB, dtype=jnp.int32)[:, None]
    padk = -(-(past + Ts) // LANES) * LANES
    rel_s = jnp.arange(padk, dtype=jnp.int32)[None, :] - pos_s[:, None]
    bias_p = _bias_table(rel_bias, _t5_bucket(rel_p))
    bias_s = _bias_table(rel_bias, _t5_bucket(rel_s))
    tabs_p = _retention_tables(c_p)
    tabs_s = _retention_tables(Ts)

    br = b_router.reshape(N_EXPERTS, 1)
    ret0_p = jnp.zeros((B, R_HEADS, R_KEY_DIM, R_V_DIM), F32)
    conv0_p = jnp.zeros((B, CONV_WIDTH - 1, C_WIDTH), F32)

    xp = x_prompt.reshape(n_p, D)
    xs = x_sample.reshape(n_s, D)
    outs = {k: [] for k in ("rp", "cp", "rs", "cs")}
    kv_p = kv_s = None
    for l in range(depth):
        lam_init = 0.8 - 0.6 * math.exp(-0.3 * l)
        lam4 = jnp.stack([lam_q1[l], lam_k1[l], lam_q2[l], lam_k2[l]])
        g_sub = subln_g[l].reshape(1, A_V_DIM)
        w_in_b = w_in[l].astype(BF16)
        w_out_b = w_out[l].astype(BF16)
        wpg, wpp = w_ple_gate[l].astype(BF16), w_ple_proj[l].astype(BF16)
        ln = (ln_emb_g, ln_emb_b) if l == 0 else None

        def channel(x, attn, ret, conv, p_l, tm):
            tb = min(tm, _DISPATCH_TOKENS)
            x1, x1b, ri, rw, cnt = _outproj(attn, ret, conv, x, w_out_b, ln1_g[l], ln1_b[l], w_router, br,
                                            alpha, tm, tb)
            return _moe(x1b, x1, ri, rw, cnt, p_l, w_exp_gate, w_exp_up, w_exp_down, l, ln2_g[l], ln2_b[l],
                        wpg, b_ple_gate[l], wpp, alpha, tb)

        xp, q, kb, vb, rq, rk, rv, rg, u, k5, v5 = _inproj(xp, w_in_b, rope_p[0], rope_p[1], ln, tm_p,
                                                           kv_p, l, depth, B, T)
        kv_p = (k5, v5)
        attn = _attn_prompt(lam4, g_sub, bias_p, q, kb, vb, B, T, lam_init)
        ret, rstate = _retention(rq, rk, rv, rg, ret0_p, tabs_p, B, T, c_p)
        conv, ctail = _conv(u, conv0_p, dw_w[l], dw_b[l], conv_ln_g[l], conv_ln_b[l], B, T)
        xp = channel(xp, attn, ret, conv, p_prompt.reshape(depth, n_p, PLE_DIM), tm_p)
        outs["rp"].append(rstate)
        outs["cp"].append(ctail)

        xs, q, kb, vb, rq, rk, rv, rg, u, k5, v5 = _inproj(xs, w_in_b, rope_s[0], rope_s[1], ln, tm_s,
                                                           kv_s, l, depth, Bs, Ts)
        kv_s = (k5, v5)
        attn = _attn_decode(lam4, g_sub, bias_s, q, cache_k, cache_v, kb, vb, l, Bs, Ts, lam_init)
        ret, rstate = _retention(rq, rk, rv, rg, state_ret[l], tabs_s, Bs, Ts, Ts)
        conv, ctail = _conv(u, state_conv[l], dw_w[l], dw_b[l], conv_ln_g[l], conv_ln_b[l], Bs, Ts)
        xs = channel(xs, attn, ret, conv, p_sample.reshape(depth, n_s, PLE_DIM), tm_s)
        outs["rs"].append(rstate)
        outs["cs"].append(ctail)

    return (xp.reshape(B, T, D), xs.reshape(Bs, Ts, D),
            kv_p[0], kv_p[1], jnp.stack(outs["rp"]), jnp.stack(outs["cp"]),
            kv_s[0], kv_s[1], jnp.stack(outs["rs"]), jnp.stack(outs["cs"]))
```

```python
import functools
import math

import jax
import jax.numpy as jnp
from jax import lax
from jax.experimental import pallas as pl
from jax.experimental.pallas import tpu as pltpu

F32 = jnp.float32
BF16 = jnp.bfloat16

D_MODEL = 1024
CHUNK = 64
HEAD_DIM = 64
A_HEADS = 4
A_V_DIM = 128
A_WIDTH = 512
R_HEADS = 4
R_KEY_DIM = 64
R_V_DIM = 64
R_WIDTH = 256
C_WIDTH = 256
CONV_WIDTH = 31
IN_WIDTH = 3072
N_BUCKETS = 32
MAX_DISTANCE = 128
ROPE_BASE = 10000.0
N_EXPERTS = 16
N_GROUPS = 4
EXPERTS_PER_GROUP = 4
D_EXPERT = 512
PLE_DIM = 256
LN_EPS = 1e-5
NEG_INF = -1e30
LOG2E = 1.4426950408889634

LANES = 128
SUBLANES = 8
_LOG2_CHUNK = 6
assert CHUNK == HEAD_DIM == R_KEY_DIM == R_V_DIM == 1 << _LOG2_CHUNK
MIB = 1024 * 1024

_OFF_Q, _OFF_K, _OFF_V = 0, 512, 1024
_OFF_RQ, _OFF_RK, _OFF_RV, _OFF_RG, _OFF_C = 1536, 1792, 2048, 2304, 2560


def _params(sem, vmem_mib):
    return pltpu.CompilerParams(dimension_semantics=sem, vmem_limit_bytes=vmem_mib * MIB)


def _layer_norm(x, g, b):
    mu = jnp.mean(x, axis=-1, keepdims=True)
    xc = x - mu
    var = jnp.mean(xc * xc, axis=-1, keepdims=True)
    return xc * lax.rsqrt(var + LN_EPS) * g + b


def _sigmoid(x):
    return 1.0 / (1.0 + jnp.exp(-x))


def _dot(a, b):
    return jnp.dot(a, b, preferred_element_type=F32)


def _dot_nt(a, b):
    return lax.dot_general(a, b, (((1,), (1,)), ((), ())), preferred_element_type=F32)


def _dot_tn(a, b):
    return lax.dot_general(a, b, (((0,), (0,)), ((), ())), preferred_element_type=F32)


def _bias_kernel(relb_ref, idx_ref, out_ref):
    idx = idx_ref[...]
    for h in range(A_HEADS):
        acc = jnp.zeros(idx.shape, F32)
        for b in range(N_BUCKETS):
            acc = jnp.where(idx == b, relb_ref[b, h], acc)
        out_ref[h] = acc


def _bias_table(rel_bias, idx):
    r, c = idx.shape
    return pl.pallas_call(
        _bias_kernel,
        out_shape=jax.ShapeDtypeStruct((A_HEADS, r, c), F32),
        in_specs=[pl.BlockSpec(memory_space=pltpu.SMEM),
                  pl.BlockSpec(memory_space=pltpu.VMEM)],
        out_specs=pl.BlockSpec(memory_space=pltpu.VMEM),
        name="bias_table",
    )(rel_bias, idx)


def _t5_bucket(rel):
    nb = N_BUCKETS // 2
    max_exact = nb // 2
    n = jnp.abs(rel)
    nf = jnp.maximum(n, 1).astype(jnp.float32)
    large = max_exact + (jnp.log(nf / max_exact) / math.log(MAX_DISTANCE / max_exact)
                         * (nb - max_exact)).astype(jnp.int32)
    large = jnp.minimum(large, nb - 1)
    return jnp.where(rel > 0, nb, 0) + jnp.where(n < max_exact, n, large)


def _rotary128(x, cos, sin_signed, lo32):
    partner = jnp.where(lo32, pltpu.roll(x, 96, 1), pltpu.roll(x, 32, 1))
    return x * cos + partner * sin_signed


def _inproj_kernel(*refs, apply_ln, has_prev, layer, tm, T, nsteps):
    refs = list(refs)
    x_ref = refs.pop(0)
    if apply_ln:
        g_ref, b_ref = refs.pop(0), refs.pop(0)
    w_ref, cos_ref, sin_ref = refs.pop(0), refs.pop(0), refs.pop(0)
    if has_prev:
        refs.pop(0), refs.pop(0)
    if apply_ln:
        xn_ref = refs.pop(0)
    (q_ref, kb_ref, vb_ref, rq_ref, rk_ref, rv_ref, rg_ref, u_ref, k5_hbm, v5_hbm,
     kbuf, vbuf, sems) = refs

    i = pl.program_id(0)
    slot = i % 2
    rows = min(tm, T)

    def kv_copies(s, step):
        cps = []
        for bb in range(tm // rows):
            r0 = step * tm + bb * rows
            b = r0 // T
            t0 = pl.multiple_of(r0 % T, SUBLANES)
            for buf, out in ((kbuf, k5_hbm), (vbuf, v5_hbm)):
                for h in range(A_HEADS):
                    cps.append(pltpu.make_async_copy(
                        buf.at[s, pl.ds(bb * rows, rows), pl.ds(h * A_V_DIM, A_V_DIM)],
                        out.at[layer, b, pl.ds(t0, rows), h, :], sems.at[s]))
        return cps

    @pl.when(i >= 2)
    def _():
        for cp in kv_copies(slot, i - 2):
            cp.wait()

    x = x_ref[...]
    if apply_ln:
        x = _layer_norm(x, g_ref[...], b_ref[...])
        xn_ref[...] = x
    xb = x.astype(BF16)

    def mm(c0, c1):
        return _dot(xb, w_ref[:, c0:c1])

    q_ref[...] = (mm(_OFF_Q, _OFF_K) * (LOG2E * HEAD_DIM ** -0.5)).astype(BF16)
    a = mm(_OFF_K, _OFF_V)
    kbuf[slot] = a
    kb_ref[...] = a.astype(BF16)
    a = mm(_OFF_V, _OFF_RQ)
    vbuf[slot] = a
    vb_ref[...] = a.astype(BF16)
    for cp in kv_copies(slot, i):
        cp.start()

    cos = cos_ref[...]
    sin = sin_ref[...]
    lane = lax.broadcasted_iota(jnp.int32, cos.shape, 1)
    lo32 = (lane & 63) < 32

    def rot(a, scale):
        parts = [_rotary128(a[:, c * LANES:(c + 1) * LANES], cos, sin, lo32) for c in range(2)]
        r = jnp.concatenate(parts, axis=1)
        if scale != 1.0:
            r = r * scale
        return r.astype(BF16)

    rq_ref[...] = rot(mm(_OFF_RQ, _OFF_RK), 1.0)
    rk_ref[...] = rot(mm(_OFF_RK, _OFF_RV), R_KEY_DIM ** -0.5)
    rv_ref[...] = mm(_OFF_RV, _OFF_RG).astype(BF16)
    a = mm(_OFF_RG, _OFF_C)
    rg_ref[...] = a * _sigmoid(a)
    a = mm(_OFF_C, IN_WIDTH)
    u_ref[...] = a[:, :C_WIDTH] * _sigmoid(a[:, C_WIDTH:])

    @pl.when(i == nsteps - 1)
    def _():
        for cp in kv_copies(slot, i):
            cp.wait()
        if nsteps > 1:
            for cp in kv_copies(1 - slot, i - 1):
                cp.wait()


def _inproj(x, w_bf, cos_t, sin_t, ln, tm, kv_prev, layer, depth, B, T):
    n = x.shape[0]
    assert (tm % T == 0 or T % tm == 0) and n % tm == 0
    nblk = cos_t.shape[0] // tm
    nsteps = n // tm
    row = lambda i: (i, 0)
    const = lambda i: (0, 0)
    tab = lambda i: (i % nblk, 0)
    in_specs = [pl.BlockSpec((tm, D_MODEL), row)]
    args = [x]
    if ln is not None:
        in_specs += [pl.BlockSpec((1, D_MODEL), const)] * 2
        args += [ln[0].reshape(1, D_MODEL), ln[1].reshape(1, D_MODEL)]
    in_specs += [pl.BlockSpec((D_MODEL, IN_WIDTH), const),
                 pl.BlockSpec((tm, LANES), tab), pl.BlockSpec((tm, LANES), tab)]
    args += [w_bf, cos_t, sin_t]
    aliases = {}
    if kv_prev is not None:
        k5_index = (1 if ln is not None else 0) + 8
        aliases = {len(args): k5_index, len(args) + 1: k5_index + 1}
        in_specs += [pl.BlockSpec(memory_space=pl.ANY)] * 2
        args += list(kv_prev)

    def o(width, dt):
        return jax.ShapeDtypeStruct((n, width), dt), pl.BlockSpec((tm, width), row)

    outs = []
    if ln is not None:
        outs.append(o(D_MODEL, F32))
    outs += [o(512, BF16), o(512, BF16), o(512, BF16),
             o(256, BF16), o(256, BF16), o(256, BF16), o(256, F32), o(256, F32)]
    kv5 = jax.ShapeDtypeStruct((depth, B, T, A_HEADS, A_V_DIM), F32)
    outs += [(kv5, pl.BlockSpec(memory_space=pl.ANY))] * 2
    res = pl.pallas_call(
        functools.partial(_inproj_kernel, apply_ln=ln is not None, has_prev=kv_prev is not None,
                          layer=layer, tm=tm, T=T, nsteps=nsteps),
        grid=(nsteps,),
        in_specs=in_specs,
        out_specs=[s for _, s in outs],
        out_shape=[s for s, _ in outs],
        scratch_shapes=[pltpu.VMEM((2, tm, A_WIDTH), F32), pltpu.VMEM((2, tm, A_WIDTH), F32),
                        pltpu.SemaphoreType.DMA((2,))],
        input_output_aliases=aliases,
        compiler_params=_params(("arbitrary",), 52),
        name="inproj",
    )(*args)
    if ln is None:
        res = [x] + list(res)
    return res


def _lambda(lam_ref, lam_init):
    lv = lam_ref[...]
    s1 = jnp.sum(lv[0:1] * lv[1:2], axis=-1, keepdims=True)
    s2 = jnp.sum(lv[2:3] * lv[3:4], axis=-1, keepdims=True)
    return jnp.exp(s1) - jnp.exp(s2) + lam_init


def _stack_maps(q):
    lane = lax.broadcasted_iota(jnp.int32, q.shape, 1)
    lo = lane < HEAD_DIM
    z = jnp.zeros_like(q)
    return jnp.concatenate([jnp.where(lo, q, z), jnp.where(lo, z, q)], axis=0)


def _diff_finish(o2, t, lam, g, lam_init):
    o = o2[:t] - lam * o2[t:]
    ms = jnp.mean(o * o, axis=-1, keepdims=True)
    return (o * lax.rsqrt(ms + LN_EPS) * g * (1.0 - lam_init)).astype(BF16)


_ATTN_AHEAD = 1


def _attn_prompt_kernel(lam_ref, g_ref, bias_ref, q_ref, k_ref, v_ref, o_ref,
                        s_scr, m_scr, p_scr, vo_scr, bp_scr, bd_scr, *, T, TB, lam_init):
    lam = _lambda(lam_ref, lam_init)
    g = g_ref[...]
    bt = bias_ref[...]
    bt = (bt - bt[0:1, 0:1]) * LOG2E
    bp_scr[...] = bt[:, :TB]
    bd_scr[...] = bt[:, TB:]
    half = TB // 2
    vo_scr[:, :A_V_DIM] = v_ref[...]
    vo_scr[:, A_V_DIM:] = jnp.ones((T, A_V_DIM), BF16)

    def pass1(qi):
        r0 = qi * TB
        buf = qi % (_ATTN_AHEAD + 1)
        q = q_ref[r0:r0 + TB, :]
        lo = lax.broadcasted_iota(jnp.int32, q.shape, 1) < HEAD_DIM
        zq = jnp.zeros_like(q)
        maps = (jnp.where(lo, q, zq), jnp.where(lo, zq, q))
        for mp in range(2):
            m = None
            for ki in range(qi + 1):
                s = _dot_nt(maps[mp], k_ref[ki * TB:(ki + 1) * TB, :])
                if ki == qi - 1:
                    s = s + bp_scr[...]
                elif ki == qi:
                    row = lax.broadcasted_iota(jnp.int32, (TB, TB), 0)
                    col = lax.broadcasted_iota(jnp.int32, (TB, TB), 1)
                    vis = (col >> _LOG2_CHUNK) <= (row >> _LOG2_CHUNK)
                    s = jnp.where(vis, s + bd_scr[...], NEG_INF)
                s_scr[buf, ki, mp * TB:(mp + 1) * TB, :] = s
                mt = jnp.maximum(s[:, :half], s[:, half:])
                m = mt if m is None else jnp.maximum(m, mt)
            m_scr[buf, mp * TB:(mp + 1) * TB, :] = jnp.broadcast_to(jnp.max(m, axis=1, keepdims=True),
                                                                    (TB, LANES))

    def pass2(qi):
        r0 = qi * TB
        buf = qi % (_ATTN_AHEAD + 1)
        mb = m_scr[buf]
        mb2 = jnp.concatenate([mb, mb], axis=1)
        for ki in range(qi + 1):
            p_scr[buf, :, ki * TB:(ki + 1) * TB] = jnp.exp2(s_scr[buf, ki] - mb2).astype(BF16)
        kk = (qi + 1) * TB
        out = _dot(p_scr[buf, :, 0:kk], vo_scr[0:kk, :])
        o_ref[r0:r0 + TB, :] = _diff_finish(out[:, :A_V_DIM] / out[:, A_V_DIM:], TB, lam, g, lam_init)

    nq = T // TB
    for qi in range(min(_ATTN_AHEAD, nq)):
        pass1(qi)
    for qi in range(nq):
        if qi + _ATTN_AHEAD < nq:
            pass1(qi + _ATTN_AHEAD)
        pass2(qi)


def _attn_prompt(lam4, g, bias, q, k, v, B, T, lam_init):
    TB = 2 * LANES
    assert T % TB == 0 and TB % CHUNK == 0
    n = B * T
    blk = pl.BlockSpec((T, A_V_DIM), lambda b, h: (b, h))
    return pl.pallas_call(
        functools.partial(_attn_prompt_kernel, T=T, TB=TB, lam_init=lam_init),
        grid=(B, A_HEADS),
        in_specs=[pl.BlockSpec((4, HEAD_DIM), lambda b, h: (0, 0)),
                  pl.BlockSpec((1, A_V_DIM), lambda b, h: (0, 0)),
                  pl.BlockSpec((None, TB, 2 * TB), lambda b, h: (h, 0, 0)),
                  blk, blk, blk],
        out_specs=blk,
        out_shape=jax.ShapeDtypeStruct((n, A_WIDTH), BF16),
        scratch_shapes=[pltpu.VMEM((_ATTN_AHEAD + 1, T // TB, 2 * TB, TB), F32),
                        pltpu.VMEM((_ATTN_AHEAD + 1, 2 * TB, LANES), F32),
                        pltpu.VMEM((_ATTN_AHEAD + 1, 2 * TB, T), BF16),
                        pltpu.VMEM((T, 2 * A_V_DIM), BF16),
                        pltpu.VMEM((TB, TB), F32),
                        pltpu.VMEM((TB, TB), F32)],
        compiler_params=_params(("arbitrary", "arbitrary"), 32),
        name="attn_prompt",
    )(lam4, g, bias, q, k, v)


def _attn_decode_kernel(lam_ref, g_ref, bias_ref, q_ref, kn_ref, vn_ref, ck_hbm, cv_hbm, o_ref,
                        kc_buf, vc_buf, sems, *, Ts, past, layer, lam_init):
    b = pl.program_id(0)
    h = pl.program_id(1)
    step = b * A_HEADS + h
    nsteps = pl.num_programs(0) * A_HEADS
    slot = step % 2

    def cache_copies(st, s):
        bb = st // A_HEADS
        hh = st % A_HEADS
        return [pltpu.make_async_copy(src.at[layer, bb, :, hh, :], dst.at[s], sems.at[s, j])
                for j, (src, dst) in enumerate(((ck_hbm, kc_buf), (cv_hbm, vc_buf)))]

    @pl.when(step == 0)
    def _():
        for cp in cache_copies(0, 0):
            cp.start()

    @pl.when(step + 1 < nsteps)
    def _():
        for cp in cache_copies(step + 1, 1 - slot):
            cp.start()

    for cp in cache_copies(step, slot):
        cp.wait()
    kc_ref = kc_buf.at[slot]
    vc_ref = vc_buf.at[slot]

    lam = _lambda(lam_ref, lam_init)
    q2 = _stack_maps(q_ref[...])
    bias = bias_ref[...] * LOG2E
    s_p = _dot_nt(q2, kc_ref[...].astype(BF16)).reshape(2, Ts, past) + bias[:, :past][None]
    s_n = _dot_nt(q2, kn_ref[...]).reshape(2, Ts, Ts) + bias[:, past:past + Ts][None]
    s_p = s_p.reshape(2 * Ts, past)
    s_n = s_n.reshape(2 * Ts, Ts)
    m = jnp.maximum(jnp.max(s_p, axis=1, keepdims=True), jnp.max(s_n, axis=1, keepdims=True))
    p_p = jnp.exp2(s_p - m)
    p_n = jnp.exp2(s_n - m)
    l = jnp.sum(p_p, axis=1, keepdims=True) + jnp.sum(p_n, axis=1, keepdims=True)
    acc = _dot(p_p.astype(BF16), vc_ref[...].astype(BF16)) + _dot(p_n.astype(BF16), vn_ref[...])
    o_ref[...] = _diff_finish(acc / l, Ts, lam, g_ref[...], lam_init)


def _attn_decode(lam4, g, bias, q, cache_k, cache_v, kn, vn, layer, Bs, Ts, lam_init):
    past = cache_k.shape[2]
    assert past % CHUNK == 0 and Ts <= CHUNK
    padk = bias.shape[2]
    new = pl.BlockSpec((Ts, A_V_DIM), lambda b, h: (b, h))
    cache = pl.BlockSpec(memory_space=pl.ANY)
    return pl.pallas_call(
        functools.partial(_attn_decode_kernel, Ts=Ts, past=past, layer=layer, lam_init=lam_init),
        grid=(Bs, A_HEADS),
        in_specs=[pl.BlockSpec((4, HEAD_DIM), lambda b, h: (0, 0)),
                  pl.BlockSpec((1, A_V_DIM), lambda b, h: (0, 0)),
                  pl.BlockSpec((None, Ts, padk), lambda b, h: (h, 0, 0)),
                  new, new, new, cache, cache],
        out_specs=new,
        out_shape=jax.ShapeDtypeStruct((Bs * Ts, A_WIDTH), BF16),
        scratch_shapes=[pltpu.VMEM((2, past, A_V_DIM), F32), pltpu.VMEM((2, past, A_V_DIM), F32),
                        pltpu.SemaphoreType.DMA((2, 2))],
        compiler_params=_params(("arbitrary", "arbitrary"), 40),
        name="attn_decode",
    )(lam4, g, bias, q, kn, vn, cache_k, cache_v)


def _ret_kernel(q_ref, k_ref, v_ref, g_ref, s0_ref, dm_ref, qd_ref, kd_ref, cd_ref,
                o_ref, sn_ref, *, T, C):
    z = jnp.zeros((R_KEY_DIM, R_V_DIM), F32)
    state = jnp.concatenate([jnp.concatenate([s0_ref[0], z], axis=1),
                             jnp.concatenate([z, s0_ref[1]], axis=1)], axis=0)
    r = lax.broadcasted_iota(jnp.int32, (LANES, LANES), 0)
    c = lax.broadcasted_iota(jnp.int32, (LANES, LANES), 1)
    same_head = (r >> _LOG2_CHUNK) == (c >> _LOG2_CHUNK)
    ones_bd = jnp.where(same_head, 1.0, 0.0).astype(BF16)
    lo = lax.broadcasted_iota(jnp.int32, (C, LANES), 1) < R_V_DIM
    cd = cd_ref[...]

    def chunk(n, state):
        r0 = n * C
        q = q_ref[pl.ds(r0, C), :]
        k = k_ref[pl.ds(r0, C), :]
        v = v_ref[pl.ds(r0, C), :]
        p = (_dot_nt(_stack_maps(q), k) * dm_ref[...]).astype(BF16)
        o2 = _dot(p, v)
        inner = jnp.where(lo, o2[:C], o2[C:])
        qd = (q.astype(F32) * qd_ref[...]).astype(BF16)
        o = inner + _dot(qd, state.astype(BF16))
        vk = (v.astype(F32) * kd_ref[...]).astype(BF16)
        new_state = cd * state + jnp.where(same_head, _dot_tn(k, vk), 0.0)
        oo = o * o
        hi = oo.astype(BF16)
        lo_part = (oo - hi.astype(F32)).astype(BF16)
        ss = _dot(hi, ones_bd) + _dot(lo_part, ones_bd)
        out = o * lax.rsqrt(ss * (1.0 / R_V_DIM) + LN_EPS) * g_ref[pl.ds(r0, C), :]
        o_ref[pl.ds(r0, C), :] = out.astype(BF16)
        return new_state

    for n in range(T // C):
        state = chunk(n, state)
    sn_ref[0] = state[:R_KEY_DIM, :R_V_DIM]
    sn_ref[1] = state[R_KEY_DIM:, R_V_DIM:]


def _retention(rq, rk, rv, rg, state0, tabs, B, T, C):
    dm, qd, kd, cd = tabs
    blk = pl.BlockSpec((T, LANES), lambda b, hp: (b, hp))
    st = pl.BlockSpec((None, 2, R_KEY_DIM, R_V_DIM), lambda b, hp: (b, hp, 0, 0))
    return pl.pallas_call(
        functools.partial(_ret_kernel, T=T, C=C),
        grid=(B, 2),
        in_specs=[blk, blk, blk, blk, st,
                  pl.BlockSpec((None, 2 * C, C), lambda b, hp: (hp, 0, 0)),
                  pl.BlockSpec((None, C, LANES), lambda b, hp: (hp, 0, 0)),
                  pl.BlockSpec((None, C, LANES), lambda b, hp: (hp, 0, 0)),
                  pl.BlockSpec((None, 1, LANES), lambda b, hp: (hp, 0, 0))],
        out_specs=[blk, st],
        out_shape=[jax.ShapeDtypeStruct((B * T, R_WIDTH), BF16),
                   jax.ShapeDtypeStruct((B, R_HEADS, R_KEY_DIM, R_V_DIM), F32)],
        compiler_params=_params(("arbitrary", "arbitrary"), 32),
        name="retention",
    )(rq, rk, rv, rg, state0, dm, qd, kd, cd)


def _retention_tables(C):
    h = jnp.arange(R_HEADS, dtype=F32)
    log_g = jnp.log1p(-jnp.exp2(-5.0 - h))
    i = jnp.arange(C, dtype=F32)
    diff = i[:, None] - i[None, :]
    dmask = jnp.where(diff >= 0, jnp.exp(jnp.maximum(diff, 0.0)[None] * log_g[:, None, None]), 0.0)
    dm = dmask.reshape(2, 2 * C, C)
    q_dec = jnp.exp((i + 1.0)[None, :] * log_g[:, None])
    k_dec = jnp.exp((C - 1 - i)[None, :] * log_g[:, None])
    c_dec = jnp.exp(C * log_g)

    def lanes(t):
        t = jnp.repeat(t[:, :, None], R_V_DIM, axis=2).reshape(2, 2, C, R_V_DIM)
        return jnp.concatenate([t[:, 0], t[:, 1]], axis=-1)

    cd = jnp.repeat(c_dec[:, None], R_V_DIM, axis=1).reshape(2, 1, LANES)
    return dm, lanes(q_dec), lanes(k_dec), cd


_CONV_PAD = 32


def _conv_kernel(u_ref, h_ref, w_ref, b_ref, g_ref, be_ref, o_ref, t_ref, up_scr, *, T, RT):
    hist = CONV_WIDTH - 1
    off = _CONV_PAD - hist
    up_scr[off:_CONV_PAD, :] = h_ref[...]
    up_scr[_CONV_PAD:_CONV_PAD + T, :] = u_ref[...]
    bias = b_ref[...]
    g = g_ref[...]
    be = be_ref[...]
    win_rows = RT + _CONV_PAD
    for t0 in range(0, T, RT):
        win = up_scr[t0:t0 + win_rows, :]
        acc = jnp.zeros((RT, C_WIDTH), F32)
        for s in range(SUBLANES):
            rolled = win if s == 0 else pltpu.roll(win, win_rows - s, 0)
            for a in range(_CONV_PAD // SUBLANES + 1):
                j = a * SUBLANES + s - off
                if 0 <= j < CONV_WIDTH:
                    acc = acc + rolled[a * SUBLANES:a * SUBLANES + RT, :] * w_ref[j:j + 1, :]
        y = _layer_norm(acc + bias, g, be)
        o_ref[t0:t0 + RT, :] = (y * _sigmoid(y)).astype(BF16)
    t_ref[...] = up_scr[T + off:T + _CONV_PAD, :]


def _conv(u, hist, dw_w, dw_b, ln_g, ln_b, B, T):
    RT = min(T, LANES)
    assert T % RT == 0
    hl = CONV_WIDTH - 1
    vec = pl.BlockSpec((1, C_WIDTH), lambda b: (0, 0))
    hb = pl.BlockSpec((None, hl, C_WIDTH), lambda b: (b, 0, 0))
    return pl.pallas_call(
        functools.partial(_conv_kernel, T=T, RT=RT),
        grid=(B,),
        in_specs=[pl.BlockSpec((T, C_WIDTH), lambda b: (b, 0)), hb,
                  pl.BlockSpec((CONV_WIDTH, C_WIDTH), lambda b: (0, 0)), vec, vec, vec],
        out_specs=[pl.BlockSpec((T, C_WIDTH), lambda b: (b, 0)), hb],
        out_shape=[jax.ShapeDtypeStruct((B * T, C_WIDTH), BF16),
                   jax.ShapeDtypeStruct((B, hl, C_WIDTH), F32)],
        scratch_shapes=[pltpu.VMEM((T + _CONV_PAD, C_WIDTH), F32)],
        compiler_params=_params(("arbitrary",), 32),
        name="conv",
    )(u, hist, dw_w, dw_b.reshape(1, C_WIDTH), ln_g.reshape(1, C_WIDTH), ln_b.reshape(1, C_WIDTH))


def _first_index(vals, target):
    idx = jnp.full(target.shape, len(vals) - 1, jnp.int32)
    for j in range(len(vals) - 2, -1, -1):
        idx = jnp.where(vals[j] == target, j, idx)
    return idx


def _select(idx, vals):
    out = vals[-1]
    for j in range(len(vals) - 2, -1, -1):
        out = jnp.where(idx == j, vals[j], out)
    return out


def _route(logits_t):
    rows = [logits_t[e:e + 1, :] for e in range(N_EXPERTS)]
    m = functools.reduce(jnp.maximum, rows)
    ex = [jnp.exp(r - m) for r in rows]
    z = functools.reduce(jnp.add, ex)
    sc = [e / z for e in ex]
    v1s, v2s, i1s, i2s, gss = [], [], [], [], []
    for g in range(N_GROUPS):
        a = sc[g * EXPERTS_PER_GROUP:(g + 1) * EXPERTS_PER_GROUP]
        v1 = functools.reduce(jnp.maximum, a)
        i1 = _first_index(a, v1)
        rest = [jnp.where(i1 == j, -1.0, a[j]) for j in range(EXPERTS_PER_GROUP)]
        v2 = functools.reduce(jnp.maximum, rest)
        i2 = _first_index(rest, v2)
        v1s.append(v1); v2s.append(v2); i1s.append(i1); i2s.append(i2); gss.append(v1 + v2)
    grp = _first_index(gss, functools.reduce(jnp.maximum, gss))
    v1 = _select(grp, v1s)
    v2 = _select(grp, v2s)
    e1 = _select(grp, i1s) + grp * EXPERTS_PER_GROUP
    e2 = _select(grp, i2s) + grp * EXPERTS_PER_GROUP
    den = v1 + v2
    return e1, e2, v1 / den, v2 / den


def _expert_onehot(e1, e2):
    rows = [jnp.where(e1 == e, 1.0, 0.0) + jnp.where(e2 == e, 1.0, 0.0) for e in range(N_EXPERTS)]
    return jnp.concatenate(rows, axis=0)


def _outproj_kernel(a_ref, r_ref, c_ref, x_ref, w_ref, g_ref, b_ref, wrh_ref, wrl_ref, br_ref,
                    x1_ref, x1b_ref, ri_ref, rw_ref, cnt_ref, *, alpha, tb, sub):
    tm = x_ref.shape[0]
    ones = jnp.ones((SUBLANES, sub), BF16)
    groups = [slice(h * sub, (h + 1) * sub) for h in range(tm // sub)]
    mixes = [(_dot(a_ref[rs, :], w_ref[0:A_WIDTH, :])
              + _dot(r_ref[rs, :], w_ref[A_WIDTH:A_WIDTH + R_WIDTH, :])
              + _dot(c_ref[rs, :], w_ref[A_WIDTH + R_WIDTH:, :])) for rs in groups]
    logits = []
    for rs, mix in zip(groups, mixes):
        x1 = _layer_norm(alpha * x_ref[rs, :] + mix, g_ref[...], b_ref[...])
        x_hi = x1.astype(BF16)
        x1_ref[rs, :] = x1
        x1b_ref[rs, :] = x_hi
        x_lo = (x1 - x_hi.astype(F32)).astype(BF16)
        logits.append(_dot(x_hi, wrh_ref[...]) + _dot(x_lo, wrh_ref[...]) + _dot(x_hi, wrl_ref[...]))
    counts = []
    for rs, lg in zip(groups, logits):
        e1, e2, w1, w2 = _route(lg.T[0:N_EXPERTS, :] + br_ref[...])
        ri_ref[:, rs] = jnp.concatenate([e1, e2, jnp.zeros((SUBLANES - 2, sub), jnp.int32)], axis=0)
        rw_ref[:, rs] = jnp.concatenate([w1, w2, jnp.zeros((SUBLANES - 2, sub), F32)], axis=0)
        counts.append(_dot_nt(ones, _expert_onehot(e1, e2).astype(BF16)))
    per = tb // sub
    for k in range(tm // tb):
        c = functools.reduce(jnp.add, counts[k * per:(k + 1) * per])
        c = jnp.concatenate([c, jnp.zeros((SUBLANES, LANES - N_EXPERTS), F32)], axis=1)
        cnt_ref[k * SUBLANES:(k + 1) * SUBLANES, :] = c.astype(jnp.int32)


def _outproj(attn, ret, conv, x, w_bf, ln_g, ln_b, w_router, br, alpha, tm, tb):
    n = x.shape[0]
    wr = jnp.pad(w_router, ((0, 0), (0, LANES - N_EXPERTS)))
    wr_hi = wr.astype(BF16)
    wr_lo = (wr - wr_hi.astype(F32)).astype(BF16)
    row = lambda i: (i, 0)
    col = lambda i: (0, i)
    const = lambda i: (0, 0)
    nsub = tm // tb
    return pl.pallas_call(
        functools.partial(_outproj_kernel, alpha=alpha, tb=tb, sub=min(tb, 2 * LANES)),
        grid=(n // tm,),
        in_specs=[pl.BlockSpec((tm, A_WIDTH), row), pl.BlockSpec((tm, R_WIDTH), row),
                  pl.BlockSpec((tm, C_WIDTH), row), pl.BlockSpec((tm, D_MODEL), row),
                  pl.BlockSpec((D_MODEL, D_MODEL), const),
                  pl.BlockSpec((1, D_MODEL), const), pl.BlockSpec((1, D_MODEL), const),
                  pl.BlockSpec((D_MODEL, LANES), const), pl.BlockSpec((D_MODEL, LANES), const),
                  pl.BlockSpec((N_EXPERTS, 1), const)],
        out_specs=[pl.BlockSpec((tm, D_MODEL), row), pl.BlockSpec((tm, D_MODEL), row),
                   pl.BlockSpec((SUBLANES, tm), col), pl.BlockSpec((SUBLANES, tm), col),
                   pl.BlockSpec((nsub * SUBLANES, LANES), row)],
        out_shape=[jax.ShapeDtypeStruct((n, D_MODEL), F32), jax.ShapeDtypeStruct((n, D_MODEL), BF16),
                   jax.ShapeDtypeStruct((SUBLANES, n), jnp.int32),
                   jax.ShapeDtypeStruct((SUBLANES, n), F32),
                   jax.ShapeDtypeStruct((n // tb * SUBLANES, LANES), jnp.int32)],
        compiler_params=_params(("arbitrary",), 40),
        name="outproj",
    )(attn, ret, conv, x, w_bf, ln_g.reshape(1, D_MODEL), ln_b.reshape(1, D_MODEL), wr_hi, wr_lo, br)


_UNIT = 16
_XS_WIDTH = D_MODEL + LANES
_FFN_ROWS = 512
_DISPATCH_TOKENS = 512


def _sorted_positions(e1, e2, lo_ref, base, tb):
    onehot = _expert_onehot(e1, e2).astype(BF16)
    r = lax.broadcasted_iota(jnp.int32, (tb, tb), 0)
    c = lax.broadcasted_iota(jnp.int32, (tb, tb), 1)
    earlier = jnp.where(r < c, 1.0, 0.0).astype(BF16)
    rank = _dot(onehot, earlier).astype(jnp.int32)
    pos1 = jnp.zeros(e1.shape, jnp.int32)
    pos2 = jnp.zeros(e1.shape, jnp.int32)
    for e in range(N_EXPERTS):
        p = rank[e:e + 1, :] + lo_ref[base + e]
        pos1 = jnp.where(e1 == e, p, pos1)
        pos2 = jnp.where(e2 == e, p, pos2)
    return pos1, pos2


def _permutation(pos1, pos2, rows):
    j = lax.broadcasted_iota(jnp.int32, (rows, pos1.shape[1]), 0)
    return jnp.where(j == pos1, 1.0, jnp.where(j == pos2, 1.0, 0.0)).astype(BF16)


def _dispatch_kernel(lo_ref, urow_ref, ri_ref, rw_ref, x_ref, xs_hbm, pos_ref,
                     stage, sems, *, nblk, tb, cap, spare_row):
    i = pl.program_id(0)
    slot = i % 2
    units = cap // _UNIT

    def wait_slot(s):
        pltpu.make_async_copy(stage.at[s], xs_hbm.at[pl.ds(0, cap), :], sems.at[s]).wait()

    @pl.when(i >= 2)
    def _():
        wait_slot(slot)

    e1 = ri_ref[0:1, :]
    e2 = ri_ref[1:2, :]
    w1 = rw_ref[0:1, :]
    w2 = rw_ref[1:2, :]
    pos1, pos2 = _sorted_positions(e1, e2, lo_ref, i * N_EXPERTS, tb)
    pos_ref[...] = jnp.concatenate([pos1, pos2, jnp.zeros((SUBLANES - 2, tb), jnp.int32)], axis=0)
    perm = _permutation(pos1, pos2, cap)
    stage[slot, :, 0:D_MODEL] = _dot(perm, x_ref[...]).astype(BF16)
    j = lax.broadcasted_iota(jnp.int32, (cap, tb), 0)
    gate = jnp.sum(jnp.where(j == pos1, w1, jnp.where(j == pos2, w2, 0.0)), axis=1, keepdims=True)
    g0 = gate.astype(BF16).astype(F32)
    g1 = (gate - g0).astype(BF16).astype(F32)
    g2 = gate - g0 - g1
    lane = lax.broadcasted_iota(jnp.int32, (cap, LANES), 1)
    pieces = jnp.where(lane == 0, g0, jnp.where(lane == 1, g1, jnp.where(lane == 2, g2, 0.0)))
    stage[slot, :, D_MODEL:_XS_WIDTH] = pieces.astype(BF16)

    def body(u, c):
        t = urow_ref[i * units + u]
        s = pl.multiple_of(u * _UNIT, _UNIT)
        d = pl.multiple_of(jnp.where(t >= 0, t, spare_row + slot * cap + s), _UNIT)
        pltpu.make_async_copy(stage.at[slot, pl.ds(s, _UNIT), :], xs_hbm.at[pl.ds(d, _UNIT), :],
                              sems.at[slot]).start()
        return c

    lax.fori_loop(0, units, body, 0)

    @pl.when(i == nblk - 1)
    def _():
        wait_slot(slot)
        if nblk > 1:
            wait_slot(1 - slot)


def _ffn_kernel(te_ref, tv_ref, nt_ref, xs_ref, wg_ref, wu_ref, wd_ref, ys_ref, wg_b, wu_b, wd_b):
    j = pl.program_id(0)
    jm = jnp.maximum(j, 1)

    @pl.when((j == 0) | (te_ref[j] != te_ref[jm - 1]))
    def _():
        wg_b[...] = wg_ref[...].astype(BF16)
        wu_b[...] = wu_ref[...].astype(BF16)
        wd_b[...] = wd_ref[...].astype(BF16)

    @pl.when(j < nt_ref[0])
    def _():
        rows = xs_ref.shape[0]
        valid = lax.broadcasted_iota(jnp.int32, (rows, LANES), 0) < tv_ref[j]
        zero = jnp.zeros((rows, LANES), BF16)
        x = jnp.concatenate([jnp.where(valid, xs_ref[:, c * LANES:(c + 1) * LANES], zero)
                             for c in range(D_MODEL // LANES)], axis=1)
        gp = jnp.where(valid, xs_ref[:, D_MODEL:_XS_WIDTH], zero).astype(F32)
        g = gp[:, 0:1] + gp[:, 1:2] + gp[:, 2:3]
        hg = _dot(x, wg_b[...])
        hu = _dot(x, wu_b[...])
        h = hg * _sigmoid(hg) * hu * g
        ys_ref[...] = _dot(h.astype(BF16), wd_b[...]).astype(BF16)


def _combine_kernel(urow_ref, pos_ref, x1_ref, p_ref, g_ref, b_ref, wpg_ref, bpg_ref, wpp_ref,
                    ys_hbm, out_ref, stage, sems, *, nblk, tb, cap, alpha):
    i = pl.program_id(0)
    slot = i % 2
    units = cap // _UNIT

    def fetch(blk, s):
        def body(u, c):
            sr = pl.multiple_of(jnp.maximum(urow_ref[blk * units + u], 0), _UNIT)
            ds = pl.multiple_of(u * _UNIT, _UNIT)
            pltpu.make_async_copy(ys_hbm.at[pl.ds(sr, _UNIT), :], stage.at[s, pl.ds(ds, _UNIT), :],
                                  sems.at[s]).start()
            return c

        lax.fori_loop(0, units, body, 0)

    @pl.when(i == 0)
    def _():
        fetch(0, 0)

    @pl.when(i + 1 < nblk)
    def _():
        fetch(i + 1, 1 - slot)

    pltpu.make_async_copy(ys_hbm.at[pl.ds(0, cap), :], stage.at[slot], sems.at[slot]).wait()
    perm = _permutation(pos_ref[0:1, :], pos_ref[1:2, :], cap)
    y = _dot_tn(perm, stage[slot])
    x2 = _layer_norm(alpha * x1_ref[...] + y, g_ref[...], b_ref[...])
    gate = _sigmoid(_dot(x2.astype(BF16), wpg_ref[...]) + bpg_ref[...])
    out_ref[...] = x2 + gate * _dot(p_ref[...].astype(BF16), wpp_ref[...])


def _moe_plan(cnt, nblk, tb, ffn_rows):
    cnt = cnt.reshape(nblk, SUBLANES, LANES)[:, 0, :N_EXPERTS]
    pc = (cnt + (_UNIT - 1)) // _UNIT * _UNIT
    lo = jnp.cumsum(pc, axis=1) - pc
    tot = jnp.sum(pc, axis=0)
    reg = (tot + (ffn_rows - 1)) // ffn_rows * ffn_rows
    rstart = jnp.cumsum(reg) - reg
    gs = rstart[None, :] + jnp.cumsum(pc, axis=0) - pc
    tiles_e = reg // ffn_rows
    tile_end = jnp.cumsum(tiles_e)
    max_rows = 2 * nblk * tb + nblk * N_EXPERTS * _UNIT + N_EXPERTS * ffn_rows
    max_tiles = -(-max_rows // ffn_rows)
    tj =jnp.arange(max_tiles, dtype=jnp.int32)
    te = jnp.minimum(jnp.sum((tj[:, None] >= tile_end[None, :]).astype(jnp.int32), axis=1), N_EXPERTS - 1)
    tv = jnp.clip(tot[te] - (tj - (tile_end - tiles_e)[te]) * ffn_rows, 0, ffn_rows)
    nt = tile_end[-1:].astype(jnp.int32)
    cap = 2 * tb + N_EXPERTS * _UNIT
    urow0 = jnp.arange(cap // _UNIT, dtype=jnp.int32) * _UNIT
    run = jnp.sum((urow0[None, :, None] >= (lo + pc)[:, None, :]).astype(jnp.int32), axis=2)
    pick = run[:, :, None] == jnp.arange(N_EXPERTS, dtype=jnp.int32)[None, None, :]
    urow = jnp.sum(jnp.where(pick, (gs - lo)[:, None, :], 0), axis=2) + urow0[None, :]
    urow = jnp.where(urow0[None, :] < jnp.sum(pc, axis=1, keepdims=True), urow, -1)
    flat = lambda a: a.reshape(-1).astype(jnp.int32)
    return flat(lo), flat(urow), te.astype(jnp.int32), tv.astype(jnp.int32), nt, max_tiles, cap


def _moe(x1b, x1, ri, rw, cnt, p, wg, wu, wd, layer, ln_g, ln_b, wpg, bpg, wpp, alpha, tb):
    n = x1.shape[0]
    nblk = n // tb
    ffn_rows = _FFN_ROWS if 2 * n >= N_EXPERTS * _FFN_ROWS else LANES
    lo, urow, te, tv, nt, max_tiles, cap = _moe_plan(cnt, nblk, tb, ffn_rows)
    max_rows = max_tiles * ffn_rows

    xs, pos = pl.pallas_call(
        functools.partial(_dispatch_kernel, nblk=nblk, tb=tb, cap=cap, spare_row=max_rows),
        grid_spec=pltpu.PrefetchScalarGridSpec(
            num_scalar_prefetch=2,
            grid=(nblk,),
            in_specs=[pl.BlockSpec((SUBLANES, tb), lambda i, *_: (0, i)),
                      pl.BlockSpec((SUBLANES, tb), lambda i, *_: (0, i)),
                      pl.BlockSpec((tb, D_MODEL), lambda i, *_: (i, 0))],
            out_specs=[pl.BlockSpec(memory_space=pl.ANY),
                       pl.BlockSpec((SUBLANES, tb), lambda i, *_: (0, i))],
            scratch_shapes=[pltpu.VMEM((2, cap, _XS_WIDTH), BF16),
                            pltpu.SemaphoreType.DMA((2,))]),
        out_shape=[jax.ShapeDtypeStruct((max_rows + 2 * cap, _XS_WIDTH), BF16),
                   jax.ShapeDtypeStruct((SUBLANES, n), jnp.int32)],
        compiler_params=_params(("arbitrary",), 32),
        name="moe_dispatch",
    )(lo, urow, ri, rw, x1b)

    def tile(j, te_ref, tv_ref, nt_ref):
        return jnp.minimum(j, nt_ref[0] - 1)

    ys = pl.pallas_call(
        _ffn_kernel,
        grid_spec=pltpu.PrefetchScalarGridSpec(
            num_scalar_prefetch=3,
            grid=(max_tiles,),
            in_specs=[pl.BlockSpec((ffn_rows, _XS_WIDTH), lambda j, *s: (tile(j, *s), 0)),
                      pl.BlockSpec((None, None, D_MODEL, D_EXPERT),
                                   lambda j, *s: (layer, s[0][tile(j, *s)], 0, 0)),
                      pl.BlockSpec((None, None, D_MODEL, D_EXPERT),
                                   lambda j, *s: (layer, s[0][tile(j, *s)], 0, 0)),
                      pl.BlockSpec((None, None, D_EXPERT, D_MODEL),
                                   lambda j, *s: (layer, s[0][tile(j, *s)], 0, 0))],
            out_specs=pl.BlockSpec((ffn_rows, D_MODEL), lambda j, *s: (tile(j, *s), 0)),
            scratch_shapes=[pltpu.VMEM((D_MODEL, D_EXPERT), BF16), pltpu.VMEM((D_MODEL, D_EXPERT), BF16),
                            pltpu.VMEM((D_EXPERT, D_MODEL), BF16)]),
        out_shape=jax.ShapeDtypeStruct((max_rows, D_MODEL), BF16),
        compiler_params=_params(("arbitrary",), 40),
        name="moe_ffn",
    )(te, tv, nt, xs, wg, wu, wd)

    const = lambda i, *_: (0, 0)
    return pl.pallas_call(
        functools.partial(_combine_kernel, nblk=nblk, tb=tb, cap=cap, alpha=alpha),
        grid_spec=pltpu.PrefetchScalarGridSpec(
            num_scalar_prefetch=1,
            grid=(nblk,),
            in_specs=[pl.BlockSpec((SUBLANES, tb), lambda i, *_: (0, i)),
                      pl.BlockSpec((tb, D_MODEL), lambda i, *_: (i, 0)),
                      pl.BlockSpec((None, tb, PLE_DIM), lambda i, *_: (layer, i, 0)),
                      pl.BlockSpec((1, D_MODEL), const), pl.BlockSpec((1, D_MODEL), const),
                      pl.BlockSpec((D_MODEL, D_MODEL), const), pl.BlockSpec((1, D_MODEL), const),
                      pl.BlockSpec((PLE_DIM, D_MODEL), const),
                      pl.BlockSpec(memory_space=pl.ANY)],
            out_specs=pl.BlockSpec((tb, D_MODEL), lambda i, *_: (i, 0)),
            scratch_shapes=[pltpu.VMEM((2, cap, D_MODEL), BF16),
                            pltpu.SemaphoreType.DMA((2,))]),
        out_shape=jax.ShapeDtypeStruct((n, D_MODEL), F32),
        compiler_params=_params(("arbitrary",), 40),
        name="moe_combine",
    )(urow, pos, x1, p, ln_g.reshape(1, D_MODEL), ln_b.reshape(1, D_MODEL),
      wpg, bpg.reshape(1, D_MODEL), wpp, ys)


def _rope_tables(pos, rows):
    half = R_KEY_DIM // 2
    inv_freq = 1.0 / (ROPE_BASE ** jnp.linspace(0.0, 1.0, half, dtype=jnp.float32))
    ang = pos.astype(jnp.float32)[:, None] * inv_freq[None, :]
    cos = jnp.cos(ang)
    sin = jnp.sin(ang)
    cos_t = jnp.tile(cos, (rows // pos.shape[0], 4))
    sin_t = jnp.tile(jnp.concatenate([-sin, sin], axis=1), (rows // pos.shape[0], 2))
    return cos_t, sin_t


def kernel(x_prompt, x_sample, p_prompt, p_sample, cache_k, cache_v, state_ret, state_conv, ln_emb_g, ln_emb_b, rel_bias, w_router, b_router, w_in, lam_q1, lam_k1, lam_q2, lam_k2, subln_g, dw_w, dw_b, conv_ln_g, conv_ln_b, w_out, ln1_g, ln1_b, w_exp_gate, w_exp_up, w_exp_down, ln2_g, ln2_b, w_ple_gate, b_ple_gate, w_ple_proj):
    B, T, D = x_prompt.shape
    Bs, Ts, _ = x_sample.shape
    depth = w_in.shape[0]
    past = cache_k.shape[2]
    n_p, n_s = B * T, Bs * Ts
    alpha = (2 * depth) ** 0.25

    tm_p = 512 if n_p % 512 == 0 else n_p
    tm_s = n_s
    TB = 2 * LANES
    c_p = min(T, 2 * LANES)

    pos_p = jnp.arange(T, dtype=jnp.int32)
    pos_s = past + jnp.arange(Ts, dtype=jnp.int32)
    rope_p = _rope_tables(pos_p, max(T, tm_p))
    rope_s = _rope_tables(pos_s, max(Ts, tm_s))
    rel_p = (jnp.arange(2 * TB, dtype=jnp.int32)[None, :] - TB) - jnp.arange(TB, dtype=jnp.int32)[:, None]
    padk = -(-(past + Ts) // LANES) * LANES
    rel_s = jnp.arange(padk, dtype=jnp.int32)[None, :] - pos_s[:, None]
    bias_p = _bias_table(rel_bias, _t5_bucket(rel_p))
    bias_s = _bias_table(rel_bias, _t5_bucket(rel_s))
    tabs_p = _retention_tables(c_p)
    tabs_s = _retention_tables(Ts)

    br = b_router.reshape(N_EXPERTS, 1)
    ret0_p = jnp.zeros((B, R_HEADS, R_KEY_DIM, R_V_DIM), F32)
    conv0_p = jnp.zeros((B, CONV_WIDTH - 1, C_WIDTH), F32)

    xp = x_prompt.reshape(n_p, D)
    xs = x_sample.reshape(n_s, D)
    outs = {k: [] for k in ("rp", "cp", "rs", "cs")}
    kv_p = kv_s = None
    for l in range(depth):
        lam_init = 0.8 - 0.6 * math.exp(-0.3 * l)
        lam4 = jnp.stack([lam_q1[l], lam_k1[l], lam_q2[l], lam_k2[l]])
        g_sub = subln_g[l].reshape(1, A_V_DIM)
        w_in_b = w_in[l].astype(BF16)
        w_out_b = w_out[l].astype(BF16)
        wpg, wpp = w_ple_gate[l].astype(BF16), w_ple_proj[l].astype(BF16)
        ln = (ln_emb_g, ln_emb_b) if l == 0 else None

        def channel(x, attn, ret, conv, p_l, tm):
            tb = min(tm, _DISPATCH_TOKENS)
            tm_o = 2 * tm if x.shape[0] % (2 * tm) == 0 else tm
            x1, x1b, ri, rw, cnt = _outproj(attn, ret, conv, x, w_out_b, ln1_g[l], ln1_b[l], w_router, br,
                                            alpha, tm_o, tb)
            return _moe(x1b, x1, ri, rw, cnt, p_l, w_exp_gate, w_exp_up, w_exp_down, l, ln2_g[l], ln2_b[l],
                        wpg, b_ple_gate[l], wpp, alpha, tb)

        xp, q, kb, vb, rq, rk, rv, rg, u, k5, v5 = _inproj(xp, w_in_b, rope_p[0], rope_p[1], ln, tm_p,
                                                           kv_p, l, depth, B, T)
        kv_p = (k5, v5)
        attn = _attn_prompt(lam4, g_sub, bias_p, q, kb, vb, B, T, lam_init)
        ret, rstate = _retention(rq, rk, rv, rg, ret0_p, tabs_p, B, T, c_p)
        conv, ctail = _conv(u, conv0_p, dw_w[l], dw_b[l], conv_ln_g[l], conv_ln_b[l], B, T)
        xp = channel(xp, attn, ret, conv, p_prompt.reshape(depth, n_p, PLE_DIM), tm_p)
        outs["rp"].append(rstate)
        outs["cp"].append(ctail)

        xs, q, kb, vb, rq, rk, rv, rg, u, k5, v5 = _inproj(xs, w_in_b, rope_s[0], rope_s[1], ln, tm_s,
                                                           kv_s, l, depth, Bs, Ts)
        kv_s = (k5, v5)
        attn = _attn_decode(lam4, g_sub, bias_s, q, cache_k, cache_v, kb, vb, l, Bs, Ts, lam_init)
        ret, rstate = _retention(rq, rk, rv, rg, state_ret[l], tabs_s, Bs, Ts, Ts)
        conv, ctail = _conv(u, state_conv[l], dw_w[l], dw_b[l], conv_ln_g[l], conv_ln_b[l], Bs, Ts)
        xs = channel(xs, attn, ret, conv, p_sample.reshape(depth, n_s, PLE_DIM), tm_s)
        outs["rs"].append(rstate)
        outs["cs"].append(ctail)

    return (xp.reshape(B, T, D), xs.reshape(Bs, Ts, D),
            kv_p[0], kv_p[1], jnp.stack(outs["rp"]), jnp.stack(outs["cp"]),
            kv_s[0], kv_s[1], jnp.stack(outs["rs"]), jnp.stack(outs["cs"]))
```

```python
import functools
import math

import jax
import jax.numpy as jnp
from jax import lax
from jax.experimental import pallas as pl
from jax.experimental.pallas import tpu as pltpu

F32 = jnp.float32
BF16 = jnp.bfloat16

D_MODEL = 1024
CHUNK = 64
HEAD_DIM = 64
A_HEADS = 4
A_V_DIM = 128
A_WIDTH = 512
R_HEADS = 4
R_KEY_DIM = 64
R_V_DIM = 64
R_WIDTH = 256
C_WIDTH = 256
CONV_WIDTH = 31
IN_WIDTH = 3072
N_BUCKETS = 32
MAX_DISTANCE = 128
ROPE_BASE = 10000.0
N_EXPERTS = 16
N_GROUPS = 4
EXPERTS_PER_GROUP = 4
D_EXPERT = 512
PLE_DIM = 256
LN_EPS = 1e-5
NEG_INF = -1e30
LOG2E = 1.4426950408889634

LANES = 128
SUBLANES = 8
_LOG2_CHUNK = 6
assert CHUNK == HEAD_DIM == R_KEY_DIM == R_V_DIM == 1 << _LOG2_CHUNK
MIB = 1024 * 1024

_OFF_Q, _OFF_K, _OFF_V = 0, 512, 1024
_OFF_RQ, _OFF_RK, _OFF_RV, _OFF_RG, _OFF_C = 1536, 1792, 2048, 2304, 2560


def _params(sem, vmem_mib):
    return pltpu.CompilerParams(dimension_semantics=sem, vmem_limit_bytes=vmem_mib * MIB)


def _layer_norm(x, g, b):
    mu = jnp.mean(x, axis=-1, keepdims=True)
    xc = x - mu
    var = jnp.mean(xc * xc, axis=-1, keepdims=True)
    return xc * lax.rsqrt(var + LN_EPS) * g + b


def _sigmoid(x):
    return 1.0 / (1.0 + jnp.exp(-x))


def _dot(a, b):
    return jnp.dot(a, b, preferred_element_type=F32)


def _dot_nt(a, b):
    return lax.dot_general(a, b, (((1,), (1,)), ((), ())), preferred_element_type=F32)


def _dot_tn(a, b):
    return lax.dot_general(a, b, (((0,), (0,)), ((), ())), preferred_element_type=F32)


def _bias_kernel(relb_ref, idx_ref, out_ref):
    idx = idx_ref[...]
    for h in range(A_HEADS):
        acc = jnp.zeros(idx.shape, F32)
        for b in range(N_BUCKETS):
            acc = jnp.where(idx == b, relb_ref[b, h], acc)
        out_ref[h] = acc


def _bias_table(rel_bias, idx):
    r, c = idx.shape
    return pl.pallas_call(
        _bias_kernel,
        out_shape=jax.ShapeDtypeStruct((A_HEADS, r, c), F32),
        in_specs=[pl.BlockSpec(memory_space=pltpu.SMEM),
                  pl.BlockSpec(memory_space=pltpu.VMEM)],
        out_specs=pl.BlockSpec(memory_space=pltpu.VMEM),
        name="bias_table",
    )(rel_bias, idx)


def _t5_bucket(rel):
    nb = N_BUCKETS // 2
    max_exact = nb // 2
    n = jnp.abs(rel)
    nf = jnp.maximum(n, 1).astype(jnp.float32)
    large = max_exact + (jnp.log(nf / max_exact) / math.log(MAX_DISTANCE / max_exact)
                         * (nb - max_exact)).astype(jnp.int32)
    large = jnp.minimum(large, nb - 1)
    return jnp.where(rel > 0, nb, 0) + jnp.where(n < max_exact, n, large)


def _rotary128(x, cos, sin_signed, lo32):
    partner = jnp.where(lo32, pltpu.roll(x, 96, 1), pltpu.roll(x, 32, 1))
    return x * cos + partner * sin_signed


def _inproj_kernel(*refs, apply_ln, has_prev, layer, tm, T, nsteps):
    refs = list(refs)
    x_ref = refs.pop(0)
    if apply_ln:
        g_ref, b_ref = refs.pop(0), refs.pop(0)
    w_ref, cos_ref, sin_ref = refs.pop(0), refs.pop(0), refs.pop(0)
    if has_prev:
        refs.pop(0), refs.pop(0)
    if apply_ln:
        xn_ref = refs.pop(0)
    (q_ref, kb_ref, vb_ref, rq_ref, rk_ref, rv_ref, rg_ref, u_ref, k5_hbm, v5_hbm,
     kbuf, vbuf, sems) = refs

    i = pl.program_id(0)
    slot = i % 2
    rows = min(tm, T)

    def kv_copies(s, step):
        cps = []
        for bb in range(tm // rows):
            r0 = step * tm + bb * rows
            b = r0 // T
            t0 = pl.multiple_of(r0 % T, SUBLANES)
            for buf, out in ((kbuf, k5_hbm), (vbuf, v5_hbm)):
                for h in range(A_HEADS):
                    cps.append(pltpu.make_async_copy(
                        buf.at[s, pl.ds(bb * rows, rows), pl.ds(h * A_V_DIM, A_V_DIM)],
                        out.at[layer, b, pl.ds(t0, rows), h, :], sems.at[s]))
        return cps

    @pl.when(i >= 2)
    def _():
        for cp in kv_copies(slot, i - 2):
            cp.wait()

    x = x_ref[...]
    if apply_ln:
        x = _layer_norm(x, g_ref[...], b_ref[...])
        xn_ref[...] = x
    xb = x.astype(BF16)

    def mm(c0, c1):
        return _dot(xb, w_ref[:, c0:c1])

    q_ref[...] = (mm(_OFF_Q, _OFF_K) * (LOG2E * HEAD_DIM ** -0.5)).astype(BF16)
    a = mm(_OFF_K, _OFF_V)
    kbuf[slot] = a
    kb_ref[...] = a.astype(BF16)
    a = mm(_OFF_V, _OFF_RQ)
    vbuf[slot] = a
    vb_ref[...] = a.astype(BF16)
    for cp in kv_copies(slot, i):
        cp.start()

    cos = cos_ref[...]
    sin = sin_ref[...]
    lane = lax.broadcasted_iota(jnp.int32, cos.shape, 1)
    lo32 = (lane & 63) < 32

    def rot(a, scale):
        parts = [_rotary128(a[:, c * LANES:(c + 1) * LANES], cos, sin, lo32) for c in range(2)]
        r = jnp.concatenate(parts, axis=1)
        if scale != 1.0:
            r = r * scale
        return r.astype(BF16)

    rq_ref[...] = rot(mm(_OFF_RQ, _OFF_RK), 1.0)
    rk_ref[...] = rot(mm(_OFF_RK, _OFF_RV), R_KEY_DIM ** -0.5)
    rv_ref[...] = mm(_OFF_RV, _OFF_RG).astype(BF16)
    a = mm(_OFF_RG, _OFF_C)
    rg_ref[...] = a * _sigmoid(a)
    a = mm(_OFF_C, IN_WIDTH)
    u_ref[...] = a[:, :C_WIDTH] * _sigmoid(a[:, C_WIDTH:])

    @pl.when(i == nsteps - 1)
    def _():
        for cp in kv_copies(slot, i):
            cp.wait()
        if nsteps > 1:
            for cp in kv_copies(1 - slot, i - 1):
                cp.wait()


def _inproj(x, w_bf, cos_t, sin_t, ln, tm, kv_prev, layer, depth, B, T):
    n = x.shape[0]
    assert (tm % T == 0 or T % tm == 0) and n % tm == 0
    nblk = cos_t.shape[0] // tm
    nsteps = n // tm
    row = lambda i: (i, 0)
    const = lambda i: (0, 0)
    tab = lambda i: (i % nblk, 0)
    in_specs = [pl.BlockSpec((tm, D_MODEL), row)]
    args = [x]
    if ln is not None:
        in_specs += [pl.BlockSpec((1, D_MODEL), const)] * 2
        args += [ln[0].reshape(1, D_MODEL), ln[1].reshape(1, D_MODEL)]
    in_specs += [pl.BlockSpec((D_MODEL, IN_WIDTH), const),
                 pl.BlockSpec((tm, LANES), tab), pl.BlockSpec((tm, LANES), tab)]
    args += [w_bf, cos_t, sin_t]
    aliases = {}
    if kv_prev is not None:
        k5_index = (1 if ln is not None else 0) + 8
        aliases = {len(args): k5_index, len(args) + 1: k5_index + 1}
        in_specs += [pl.BlockSpec(memory_space=pl.ANY)] * 2
        args += list(kv_prev)

    def o(width, dt):
        return jax.ShapeDtypeStruct((n, width), dt), pl.BlockSpec((tm, width), row)

    outs = []
    if ln is not None:
        outs.append(o(D_MODEL, F32))
    outs += [o(512, BF16), o(512, BF16), o(512, BF16),
             o(256, BF16), o(256, BF16), o(256, BF16), o(256, F32), o(256, F32)]
    kv5 = jax.ShapeDtypeStruct((depth, B, T, A_HEADS, A_V_DIM), F32)
    outs += [(kv5, pl.BlockSpec(memory_space=pl.ANY))] * 2
    res = pl.pallas_call(
        functools.partial(_inproj_kernel, apply_ln=ln is not None, has_prev=kv_prev is not None,
                          layer=layer, tm=tm, T=T, nsteps=nsteps),
        grid=(nsteps,),
        in_specs=in_specs,
        out_specs=[s for _, s in outs],
        out_shape=[s for s, _ in outs],
        scratch_shapes=[pltpu.VMEM((2, tm, A_WIDTH), F32), pltpu.VMEM((2, tm, A_WIDTH), F32),
                        pltpu.SemaphoreType.DMA((2,))],
        input_output_aliases=aliases,
        compiler_params=_params(("arbitrary",), 52),
        name="inproj",
    )(*args)
    if ln is None:
        res = [x] + list(res)
    return res


def _lambda(lam_ref, lam_init):
    lv = lam_ref[...]
    s1 = jnp.sum(lv[0:1] * lv[1:2], axis=-1, keepdims=True)
    s2 = jnp.sum(lv[2:3] * lv[3:4], axis=-1, keepdims=True)
    return jnp.exp(s1) - jnp.exp(s2) + lam_init


def _stack_maps(q):
    lane = lax.broadcasted_iota(jnp.int32, q.shape, 1)
    lo = lane < HEAD_DIM
    z = jnp.zeros_like(q)
    return jnp.concatenate([jnp.where(lo, q, z), jnp.where(lo, z, q)], axis=0)


def _diff_finish(o2, t, lam, g, lam_init):
    o = o2[:t] - lam * o2[t:]
    ms = jnp.mean(o * o, axis=-1, keepdims=True)
    return (o * lax.rsqrt(ms + LN_EPS) * g * (1.0 - lam_init)).astype(BF16)


_ATTN_AHEAD = 1


def _attn_prompt_kernel(lam_ref, g_ref, bias_ref, q_ref, k_ref, v_ref, o_ref,
                        s_scr, m_scr, p_scr, vo_scr, bp_scr, bd_scr, *, T, TB, lam_init):
    lam = _lambda(lam_ref, lam_init)
    g = g_ref[...]
    bt = bias_ref[...]
    bt = (bt - bt[0:1, 0:1]) * LOG2E
    bp_scr[...] = bt[:, :TB]
    bd_scr[...] = bt[:, TB:]
    half = TB // 2
    vo_scr[:, :A_V_DIM] = v_ref[...]
    vo_scr[:, A_V_DIM:] = jnp.ones((T, A_V_DIM), BF16)

    def pass1(qi):
        r0 = qi * TB
        buf = qi % (_ATTN_AHEAD + 1)
        q = q_ref[r0:r0 + TB, :]
        lo = lax.broadcasted_iota(jnp.int32, q.shape, 1) < HEAD_DIM
        zq = jnp.zeros_like(q)
        maps = (jnp.where(lo, q, zq), jnp.where(lo, zq, q))
        for mp in range(2):
            m = None
            for ki in range(qi + 1):
                s = _dot_nt(maps[mp], k_ref[ki * TB:(ki + 1) * TB, :])
                if ki == qi - 1:
                    s = s + bp_scr[...]
                elif ki == qi:
                    row = lax.broadcasted_iota(jnp.int32, (TB, TB), 0)
                    col = lax.broadcasted_iota(jnp.int32, (TB, TB), 1)
                    vis = (col >> _LOG2_CHUNK) <= (row >> _LOG2_CHUNK)
                    s = jnp.where(vis, s + bd_scr[...], NEG_INF)
                s_scr[buf, ki, mp * TB:(mp + 1) * TB, :] = s
                mt = jnp.maximum(s[:, :half], s[:, half:])
                m = mt if m is None else jnp.maximum(m, mt)
            m_scr[buf, mp * TB:(mp + 1) * TB, :] = jnp.broadcast_to(jnp.max(m, axis=1, keepdims=True),
                                                                    (TB, LANES))

    def pass2(qi):
        r0 = qi * TB
        buf = qi % (_ATTN_AHEAD + 1)
        mb = m_scr[buf]
        mb2 = jnp.concatenate([mb, mb], axis=1)
        for ki in range(qi + 1):
            p_scr[buf, :, ki * TB:(ki + 1) * TB] = jnp.exp2(s_scr[buf, ki] - mb2).astype(BF16)
        kk = (qi + 1) * TB
        out = _dot(p_scr[buf, :, 0:kk], vo_scr[0:kk, :])
        o_ref[r0:r0 + TB, :] = _diff_finish(out[:, :A_V_DIM] / out[:, A_V_DIM:], TB, lam, g, lam_init)

    nq = T // TB
    for qi in range(min(_ATTN_AHEAD, nq)):
        pass1(qi)
    for qi in range(nq):
        if qi + _ATTN_AHEAD < nq:
            pass1(qi + _ATTN_AHEAD)
        pass2(qi)


def _attn_prompt(lam4, g, bias, q, k, v, B, T, lam_init):
    TB = 2 * LANES
    assert T % TB == 0 and TB % CHUNK == 0
    n = B * T
    blk = pl.BlockSpec((T, A_V_DIM), lambda b, h: (b, h))
    return pl.pallas_call(
        functools.partial(_attn_prompt_kernel, T=T, TB=TB, lam_init=lam_init),
        grid=(B, A_HEADS),
        in_specs=[pl.BlockSpec((4, HEAD_DIM), lambda b, h: (0, 0)),
                  pl.BlockSpec((1, A_V_DIM), lambda b, h: (0, 0)),
                  pl.BlockSpec((None, TB, 2 * TB), lambda b, h: (h, 0, 0)),
                  blk, blk, blk],
        out_specs=blk,
        out_shape=jax.ShapeDtypeStruct((n, A_WIDTH), BF16),
        scratch_shapes=[pltpu.VMEM((_ATTN_AHEAD + 1, T // TB, 2 * TB, TB), F32),
                        pltpu.VMEM((_ATTN_AHEAD + 1, 2 * TB, LANES), F32),
                        pltpu.VMEM((_ATTN_AHEAD + 1, 2 * TB, T), BF16),
                        pltpu.VMEM((T, 2 * A_V_DIM), BF16),
                        pltpu.VMEM((TB, TB), F32),
                        pltpu.VMEM((TB, TB), F32)],
        compiler_params=_params(("arbitrary", "arbitrary"), 32),
        name="attn_prompt",
    )(lam4, g, bias, q, k, v)


def _attn_decode_kernel(lam_ref, g_ref, bias_ref, q_ref, kn_ref, vn_ref, ck_hbm, cv_hbm, o_ref,
                        kc_buf, vc_buf, sems, *, Ts, past, layer, lam_init):
    b = pl.program_id(0)
    h = pl.program_id(1)
    step = b * A_HEADS + h
    nsteps = pl.num_programs(0) * A_HEADS
    slot = step % 2

    def cache_copies(st, s):
        bb = st // A_HEADS
        hh = st % A_HEADS
        return [pltpu.make_async_copy(src.at[layer, bb, :, hh, :], dst.at[s], sems.at[s, j])
                for j, (src, dst) in enumerate(((ck_hbm, kc_buf), (cv_hbm, vc_buf)))]

    @pl.when(step == 0)
    def _():
        for cp in cache_copies(0, 0):
            cp.start()

    @pl.when(step + 1 < nsteps)
    def _():
        for cp in cache_copies(step + 1, 1 - slot):
            cp.start()

    for cp in cache_copies(step, slot):
        cp.wait()
    kc_ref = kc_buf.at[slot]
    vc_ref = vc_buf.at[slot]

    lam = _lambda(lam_ref, lam_init)
    q2 = _stack_maps(q_ref[...])
    bias = bias_ref[...] * LOG2E
    s_p = _dot_nt(q2, kc_ref[...].astype(BF16)).reshape(2, Ts, past) + bias[:, :past][None]
    s_n = _dot_nt(q2, kn_ref[...]).reshape(2, Ts, Ts) + bias[:, past:past + Ts][None]
    s_p = s_p.reshape(2 * Ts, past)
    s_n = s_n.reshape(2 * Ts, Ts)
    m = jnp.maximum(jnp.max(s_p, axis=1, keepdims=True), jnp.max(s_n, axis=1, keepdims=True))
    p_p = jnp.exp2(s_p - m)
    p_n = jnp.exp2(s_n - m)
    l = jnp.sum(p_p, axis=1, keepdims=True) + jnp.sum(p_n, axis=1, keepdims=True)
    acc = _dot(p_p.astype(BF16), vc_ref[...].astype(BF16)) + _dot(p_n.astype(BF16), vn_ref[...])
    o_ref[...] = _diff_finish(acc / l, Ts, lam, g_ref[...], lam_init)


def _attn_decode(lam4, g, bias, q, cache_k, cache_v, kn, vn, layer, Bs, Ts, lam_init):
    past = cache_k.shape[2]
    assert past % CHUNK == 0 and Ts <= CHUNK
    padk = bias.shape[2]
    new = pl.BlockSpec((Ts, A_V_DIM), lambda b, h: (b, h))
    cache = pl.BlockSpec(memory_space=pl.ANY)
    return pl.pallas_call(
        functools.partial(_attn_decode_kernel, Ts=Ts, past=past, layer=layer, lam_init=lam_init),
        grid=(Bs, A_HEADS),
        in_specs=[pl.BlockSpec((4, HEAD_DIM), lambda b, h: (0, 0)),
                  pl.BlockSpec((1, A_V_DIM), lambda b, h: (0, 0)),
                  pl.BlockSpec((None, Ts, padk), lambda b, h: (h, 0, 0)),
                  new, new, new, cache, cache],
        out_specs=new,
        out_shape=jax.ShapeDtypeStruct((Bs * Ts, A_WIDTH), BF16),
        scratch_shapes=[pltpu.VMEM((2, past, A_V_DIM), F32), pltpu.VMEM((2, past, A_V_DIM), F32),
                        pltpu.SemaphoreType.DMA((2, 2))],
        compiler_params=_params(("arbitrary", "arbitrary"), 40),
        name="attn_decode",
    )(lam4, g, bias, q, kn, vn, cache_k, cache_v)


def _ret_kernel(q_ref, k_ref, v_ref, g_ref, s0_ref, dm_ref, qd_ref, kd_ref, cd_ref,
                o_ref, sn_ref, *, T, C):
    z = jnp.zeros((R_KEY_DIM, R_V_DIM), F32)
    state = jnp.concatenate([jnp.concatenate([s0_ref[0], z], axis=1),
                             jnp.concatenate([z, s0_ref[1]], axis=1)], axis=0)
    r = lax.broadcasted_iota(jnp.int32, (LANES, LANES), 0)
    c = lax.broadcasted_iota(jnp.int32, (LANES, LANES), 1)
    same_head = (r >> _LOG2_CHUNK) == (c >> _LOG2_CHUNK)
    ones_bd = jnp.where(same_head, 1.0, 0.0).astype(BF16)
    lo = lax.broadcasted_iota(jnp.int32, (C, LANES), 1) < R_V_DIM
    cd = cd_ref[...]

    def chunk(n, state):
        r0 = n * C
        q = q_ref[pl.ds(r0, C), :]
        k = k_ref[pl.ds(r0, C), :]
        v = v_ref[pl.ds(r0, C), :]
        p = (_dot_nt(_stack_maps(q), k) * dm_ref[...]).astype(BF16)
        o2 = _dot(p, v)
        inner = jnp.where(lo, o2[:C], o2[C:])
        qd = (q.astype(F32) * qd_ref[...]).astype(BF16)
        o = inner + _dot(qd, state.astype(BF16))
        vk = (v.astype(F32) * kd_ref[...]).astype(BF16)
        new_state = cd * state + jnp.where(same_head, _dot_tn(k, vk), 0.0)
        oo = o * o
        hi = oo.astype(BF16)
        lo_part = (oo - hi.astype(F32)).astype(BF16)
        ss = _dot(hi, ones_bd) + _dot(lo_part, ones_bd)
        out = o * lax.rsqrt(ss * (1.0 / R_V_DIM) + LN_EPS) * g_ref[pl.ds(r0, C), :]
        o_ref[pl.ds(r0, C), :] = out.astype(BF16)
        return new_state

    for n in range(T // C):
        state = chunk(n, state)
    sn_ref[0] = state[:R_KEY_DIM, :R_V_DIM]
    sn_ref[1] = state[R_KEY_DIM:, R_V_DIM:]


def _retention(rq, rk, rv, rg, state0, tabs, B, T, C):
    dm, qd, kd, cd = tabs
    blk = pl.BlockSpec((T, LANES), lambda b, hp: (b, hp))
    st = pl.BlockSpec((None, 2, R_KEY_DIM, R_V_DIM), lambda b, hp: (b, hp, 0, 0))
    return pl.pallas_call(
        functools.partial(_ret_kernel, T=T, C=C),
        grid=(B, 2),
        in_specs=[blk, blk, blk, blk, st,
                  pl.BlockSpec((None, 2 * C, C), lambda b, hp: (hp, 0, 0)),
                  pl.BlockSpec((None, C, LANES), lambda b, hp: (hp, 0, 0)),
                  pl.BlockSpec((None, C, LANES), lambda b, hp: (hp, 0, 0)),
                  pl.BlockSpec((None, 1, LANES), lambda b, hp: (hp, 0, 0))],
        out_specs=[blk, st],
        out_shape=[jax.ShapeDtypeStruct((B * T, R_WIDTH), BF16),
                   jax.ShapeDtypeStruct((B, R_HEADS, R_KEY_DIM, R_V_DIM), F32)],
        compiler_params=_params(("arbitrary", "arbitrary"), 32),
        name="retention",
    )(rq, rk, rv, rg, state0, dm, qd, kd, cd)


def _retention_tables(C):
    h = jnp.arange(R_HEADS, dtype=F32)
    log_g = jnp.log1p(-jnp.exp2(-5.0 - h))
    i = jnp.arange(C, dtype=F32)
    diff = i[:, None] - i[None, :]
    dmask = jnp.where(diff >= 0, jnp.exp(jnp.maximum(diff, 0.0)[None] * log_g[:, None, None]), 0.0)
    dm = dmask.reshape(2, 2 * C, C)
    q_dec = jnp.exp((i + 1.0)[None, :] * log_g[:, None])
    k_dec = jnp.exp((C - 1 - i)[None, :] * log_g[:, None])
    c_dec = jnp.exp(C * log_g)

    def lanes(t):
        t = jnp.repeat(t[:, :, None], R_V_DIM, axis=2).reshape(2, 2, C, R_V_DIM)
        return jnp.concatenate([t[:, 0], t[:, 1]], axis=-1)

    cd = jnp.repeat(c_dec[:, None], R_V_DIM, axis=1).reshape(2, 1, LANES)
    return dm, lanes(q_dec), lanes(k_dec), cd


_CONV_PAD = 32


def _conv_kernel(u_ref, h_ref, w_ref, b_ref, g_ref, be_ref, o_ref, t_ref, up_scr, *, T, RT):
    hist = CONV_WIDTH - 1
    off = _CONV_PAD - hist
    up_scr[off:_CONV_PAD, :] = h_ref[...]
    up_scr[_CONV_PAD:_CONV_PAD + T, :] = u_ref[...]
    bias = b_ref[...]
    g = g_ref[...]
    be = be_ref[...]
    win_rows = RT + _CONV_PAD
    for t0 in range(0, T, RT):
        win = up_scr[t0:t0 + win_rows, :]
        acc = jnp.zeros((RT, C_WIDTH), F32)
        for s in range(SUBLANES):
            rolled = win if s == 0 else pltpu.roll(win, win_rows - s, 0)
            for a in range(_CONV_PAD // SUBLANES + 1):
                j = a * SUBLANES + s - off
                if 0 <= j < CONV_WIDTH:
                    acc = acc + rolled[a * SUBLANES:a * SUBLANES + RT, :] * w_ref[j:j + 1, :]
        y = _layer_norm(acc + bias, g, be)
        o_ref[t0:t0 + RT, :] = (y * _sigmoid(y)).astype(BF16)
    t_ref[...] = up_scr[T + off:T + _CONV_PAD, :]


def _conv(u, hist, dw_w, dw_b, ln_g, ln_b, B, T):
    RT = min(T, LANES)
    assert T % RT == 0
    hl = CONV_WIDTH - 1
    vec = pl.BlockSpec((1, C_WIDTH), lambda b: (0, 0))
    hb = pl.BlockSpec((None, hl, C_WIDTH), lambda b: (b, 0, 0))
    return pl.pallas_call(
        functools.partial(_conv_kernel, T=T, RT=RT),
        grid=(B,),
        in_specs=[pl.BlockSpec((T, C_WIDTH), lambda b: (b, 0)), hb,
                  pl.BlockSpec((CONV_WIDTH, C_WIDTH), lambda b: (0, 0)), vec, vec, vec],
        out_specs=[pl.BlockSpec((T, C_WIDTH), lambda b: (b, 0)), hb],
        out_shape=[jax.ShapeDtypeStruct((B * T, C_WIDTH), BF16),
                   jax.ShapeDtypeStruct((B, hl, C_WIDTH), F32)],
        scratch_shapes=[pltpu.VMEM((T + _CONV_PAD, C_WIDTH), F32)],
        compiler_params=_params(("arbitrary",), 32),
        name="conv",
    )(u, hist, dw_w, dw_b.reshape(1, C_WIDTH), ln_g.reshape(1, C_WIDTH), ln_b.reshape(1, C_WIDTH))


def _first_index(vals, target):
    idx = jnp.full(target.shape, len(vals) - 1, jnp.int32)
    for j in range(len(vals) - 2, -1, -1):
        idx = jnp.where(vals[j] == target, j, idx)
    return idx


def _select(idx, vals):
    out = vals[-1]
    for j in range(len(vals) - 2, -1, -1):
        out = jnp.where(idx == j, vals[j], out)
    return out


def _route(logits_t):
    rows = [logits_t[e:e + 1, :] for e in range(N_EXPERTS)]
    m = functools.reduce(jnp.maximum, rows)
    ex = [jnp.exp(r - m) for r in rows]
    z = functools.reduce(jnp.add, ex)
    sc = [e / z for e in ex]
    v1s, v2s, i1s, i2s, gss = [], [], [], [], []
    for g in range(N_GROUPS):
        a = sc[g * EXPERTS_PER_GROUP:(g + 1) * EXPERTS_PER_GROUP]
        v1 = functools.reduce(jnp.maximum, a)
        i1 = _first_index(a, v1)
        rest = [jnp.where(i1 == j, -1.0, a[j]) for j in range(EXPERTS_PER_GROUP)]
        v2 = functools.reduce(jnp.maximum, rest)
        i2 = _first_index(rest, v2)
        v1s.append(v1); v2s.append(v2); i1s.append(i1); i2s.append(i2); gss.append(v1 + v2)
    grp = _first_index(gss, functools.reduce(jnp.maximum, gss))
    v1 = _select(grp, v1s)
    v2 = _select(grp, v2s)
    e1 = _select(grp, i1s) + grp * EXPERTS_PER_GROUP
    e2 = _select(grp, i2s) + grp * EXPERTS_PER_GROUP
    den = v1 + v2
    return e1, e2, v1 / den, v2 / den


def _expert_onehot(e1, e2):
    rows = [jnp.where(e1 == e, 1.0, 0.0) + jnp.where(e2 == e, 1.0, 0.0) for e in range(N_EXPERTS)]
    return jnp.concatenate(rows, axis=0)


def _outproj_kernel(a_ref, r_ref, c_ref, x_ref, w_ref, g_ref, b_ref, wrh_ref, wrl_ref, br_ref,
                    x1_ref, x1b_ref, ri_ref, rw_ref, cnt_ref, *, alpha, tb, sub):
    tm = x_ref.shape[0]
    ones = jnp.ones((SUBLANES, sub), BF16)
    groups = [slice(h * sub, (h + 1) * sub) for h in range(tm // sub)]
    mixes = [(_dot(a_ref[rs, :], w_ref[0:A_WIDTH, :])
              + _dot(r_ref[rs, :], w_ref[A_WIDTH:A_WIDTH + R_WIDTH, :])
              + _dot(c_ref[rs, :], w_ref[A_WIDTH + R_WIDTH:, :])) for rs in groups]
    logits = []
    for rs, mix in zip(groups, mixes):
        x1 = _layer_norm(alpha * x_ref[rs, :] + mix, g_ref[...], b_ref[...])
        x_hi = x1.astype(BF16)
        x1_ref[rs, :] = x1
        x1b_ref[rs, :] = x_hi
        x_lo = (x1 - x_hi.astype(F32)).astype(BF16)
        logits.append(_dot(x_hi, wrh_ref[...]) + _dot(x_lo, wrh_ref[...]) + _dot(x_hi, wrl_ref[...]))
    counts = []
    for rs, lg in zip(groups, logits):
        e1, e2, w1, w2 = _route(lg.T[0:N_EXPERTS, :] + br_ref[...])
        ri_ref[:, rs] = jnp.concatenate([e1, e2, jnp.zeros((SUBLANES - 2, sub), jnp.int32)], axis=0)
        rw_ref[:, rs] = jnp.concatenate([w1, w2, jnp.zeros((SUBLANES - 2, sub), F32)], axis=0)
        counts.append(_dot_nt(ones, _expert_onehot(e1, e2).astype(BF16)))
    per = tb // sub
    for k in range(tm // tb):
        c = functools.reduce(jnp.add, counts[k * per:(k + 1) * per])
        c = jnp.concatenate([c, jnp.zeros((SUBLANES, LANES - N_EXPERTS), F32)], axis=1)
        cnt_ref[k * SUBLANES:(k + 1) * SUBLANES, :] = c.astype(jnp.int32)


def _outproj(attn, ret, conv, x, w_bf, ln_g, ln_b, w_router, br, alpha, tm, tb):
    n = x.shape[0]
    wr = jnp.pad(w_router, ((0, 0), (0, LANES - N_EXPERTS)))
    wr_hi = wr.astype(BF16)
    wr_lo = (wr - wr_hi.astype(F32)).astype(BF16)
    row = lambda i: (i, 0)
    col = lambda i: (0, i)
    const = lambda i: (0, 0)
    nsub = tm // tb
    return pl.pallas_call(
        functools.partial(_outproj_kernel, alpha=alpha, tb=tb, sub=min(tb, 2 * LANES)),
        grid=(n // tm,),
        in_specs=[pl.BlockSpec((tm, A_WIDTH), row), pl.BlockSpec((tm, R_WIDTH), row),
                  pl.BlockSpec((tm, C_WIDTH), row), pl.BlockSpec((tm, D_MODEL), row),
                  pl.BlockSpec((D_MODEL, D_MODEL), const),
                  pl.BlockSpec((1, D_MODEL), const), pl.BlockSpec((1, D_MODEL), const),
                  pl.BlockSpec((D_MODEL, LANES), const), pl.BlockSpec((D_MODEL, LANES), const),
                  pl.BlockSpec((N_EXPERTS, 1), const)],
        out_specs=[pl.BlockSpec((tm, D_MODEL), row), pl.BlockSpec((tm, D_MODEL), row),
                   pl.BlockSpec((SUBLANES, tm), col), pl.BlockSpec((SUBLANES, tm), col),
                   pl.BlockSpec((nsub * SUBLANES, LANES), row)],
        out_shape=[jax.ShapeDtypeStruct((n, D_MODEL), F32), jax.ShapeDtypeStruct((n, D_MODEL), BF16),
                   jax.ShapeDtypeStruct((SUBLANES, n), jnp.int32),
                   jax.ShapeDtypeStruct((SUBLANES, n), F32),
                   jax.ShapeDtypeStruct((n // tb * SUBLANES, LANES), jnp.int32)],
        compiler_params=_params(("arbitrary",), 40),
        name="outproj",
    )(attn, ret, conv, x, w_bf, ln_g.reshape(1, D_MODEL), ln_b.reshape(1, D_MODEL), wr_hi, wr_lo, br)


_UNIT = 16
_XS_WIDTH = D_MODEL + LANES
_FFN_ROWS = 512
_DISPATCH_TOKENS = 512


def _sorted_positions(e1, e2, lo_ref, base, tb):
    onehot = _expert_onehot(e1, e2).astype(BF16)
    r = lax.broadcasted_iota(jnp.int32, (tb, tb), 0)
    c = lax.broadcasted_iota(jnp.int32, (tb, tb), 1)
    earlier = jnp.where(r < c, 1.0, 0.0).astype(BF16)
    rank = _dot(onehot, earlier).astype(jnp.int32)
    pos1 = jnp.zeros(e1.shape, jnp.int32)
    pos2 = jnp.zeros(e1.shape, jnp.int32)
    for e in range(N_EXPERTS):
        p = rank[e:e + 1, :] + lo_ref[base + e]
        pos1 = jnp.where(e1 == e, p, pos1)
        pos2 = jnp.where(e2 == e, p, pos2)
    return pos1, pos2


def _permutation(pos1, pos2, rows):
    j = lax.broadcasted_iota(jnp.int32, (rows, pos1.shape[1]), 0)
    return jnp.where(j == pos1, 1.0, jnp.where(j == pos2, 1.0, 0.0)).astype(BF16)


def _dispatch_kernel(lo_ref, urow_ref, ri_ref, rw_ref, x_ref, xs_hbm, pos_ref,
                     stage, sems, *, nblk, tb, cap, spare_row):
    i = pl.program_id(0)
    slot = i % 2
    units = cap // _UNIT

    def wait_slot(s):
        pltpu.make_async_copy(stage.at[s], xs_hbm.at[pl.ds(0, cap), :], sems.at[s]).wait()

    @pl.when(i >= 2)
    def _():
        wait_slot(slot)

    e1 = ri_ref[0:1, :]
    e2 = ri_ref[1:2, :]
    w1 = rw_ref[0:1, :]
    w2 = rw_ref[1:2, :]
    pos1, pos2 = _sorted_positions(e1, e2, lo_ref, i * N_EXPERTS, tb)
    pos_ref[...] = jnp.concatenate([pos1, pos2, jnp.zeros((SUBLANES - 2, tb), jnp.int32)], axis=0)
    perm = _permutation(pos1, pos2, cap)
    stage[slot, :, 0:D_MODEL] = _dot(perm, x_ref[...]).astype(BF16)
    j = lax.broadcasted_iota(jnp.int32, (cap, tb), 0)
    gate = jnp.sum(jnp.where(j == pos1, w1, jnp.where(j == pos2, w2, 0.0)), axis=1, keepdims=True)
    g0 = gate.astype(BF16).astype(F32)
    g1 = (gate - g0).astype(BF16).astype(F32)
    g2 = gate - g0 - g1
    lane = lax.broadcasted_iota(jnp.int32, (cap, LANES), 1)
    pieces = jnp.where(lane == 0, g0, jnp.where(lane == 1, g1, jnp.where(lane == 2, g2, 0.0)))
    stage[slot, :, D_MODEL:_XS_WIDTH] = pieces.astype(BF16)

    def body(u, c):
        t = urow_ref[i * units + u]
        s = pl.multiple_of(u * _UNIT, _UNIT)
        d = pl.multiple_of(jnp.where(t >= 0, t, spare_row + slot * cap + s), _UNIT)
        pltpu.make_async_copy(stage.at[slot, pl.ds(s, _UNIT), :], xs_hbm.at[pl.ds(d, _UNIT), :],
                              sems.at[slot]).start()
        return c

    lax.fori_loop(0, units, body, 0)

    @pl.when(i == nblk - 1)
    def _():
        wait_slot(slot)
        if nblk > 1:
            wait_slot(1 - slot)


def _ffn_kernel(te_ref, tv_ref, nt_ref, xs_ref, wg_ref, wu_ref, wd_ref, ys_ref, wg_b, wu_b, wd_b):
    j = pl.program_id(0)
    jm = jnp.maximum(j, 1)

    @pl.when((j == 0) | (te_ref[j] != te_ref[jm - 1]))
    def _():
        wg_b[...] = wg_ref[...].astype(BF16)
        wu_b[...] = wu_ref[...].astype(BF16)
        wd_b[...] = wd_ref[...].astype(BF16)

    @pl.when(j < nt_ref[0])
    def _():
        rows = xs_ref.shape[0]
        valid = lax.broadcasted_iota(jnp.int32, (rows, LANES), 0) < tv_ref[j]
        zero = jnp.zeros((rows, LANES), BF16)
        x = jnp.concatenate([jnp.where(valid, xs_ref[:, c * LANES:(c + 1) * LANES], zero)
                             for c in range(D_MODEL // LANES)], axis=1)
        gp = jnp.where(valid, xs_ref[:, D_MODEL:_XS_WIDTH], zero).astype(F32)
        g = gp[:, 0:1] + gp[:, 1:2] + gp[:, 2:3]
        hg = _dot(x, wg_b[...])
        hu = _dot(x, wu_b[...])
        h = hg * _sigmoid(hg) * hu * g
        ys_ref[...] = _dot(h.astype(BF16), wd_b[...]).astype(BF16)


def _combine_kernel(urow_ref, pos_ref, x1_ref, p_ref, g_ref, b_ref, wpg_ref, bpg_ref, wpp_ref,
                    ys_hbm, out_ref, stage, sems, *, nblk, tb, cap, alpha):
    i = pl.program_id(0)
    slot = i % 2
    units = cap // _UNIT

    def fetch(blk, s):
        def body(u, c):
            sr = pl.multiple_of(jnp.maximum(urow_ref[blk * units + u], 0), _UNIT)
            ds = pl.multiple_of(u * _UNIT, _UNIT)
            pltpu.make_async_copy(ys_hbm.at[pl.ds(sr, _UNIT), :], stage.at[s, pl.ds(ds, _UNIT), :],
                                  sems.at[s]).start()
            return c

        lax.fori_loop(0, units, body, 0)

    @pl.when(i == 0)
    def _():
        fetch(0, 0)

    @pl.when(i + 1 < nblk)
    def _():
        fetch(i + 1, 1 - slot)

    pltpu.make_async_copy(ys_hbm.at[pl.ds(0, cap), :], stage.at[slot], sems.at[slot]).wait()
    sub = min(tb, 2 * LANES)
    groups = [slice(h * sub, (h + 1) * sub) for h in range(tb // sub)]
    ys = [_dot_tn(_permutation(pos_ref[0:1, rs], pos_ref[1:2, rs], cap), stage[slot]) for rs in groups]
    for rs, y in zip(groups, ys):
        x2 = _layer_norm(alpha * x1_ref[rs, :] + y, g_ref[...], b_ref[...])
        gate = _sigmoid(_dot(x2.astype(BF16), wpg_ref[...]) + bpg_ref[...])
        out_ref[rs, :] = x2 + gate * _dot(p_ref[rs, :].astype(BF16), wpp_ref[...])


def _moe_plan(cnt, nblk, tb, ffn_rows):
    cnt = cnt.reshape(nblk, SUBLANES, LANES)[:, 0, :N_EXPERTS]
    pc = (cnt + (_UNIT - 1)) // _UNIT * _UNIT
    lo = jnp.cumsum(pc, axis=1) - pc
    tot = jnp.sum(pc, axis=0)
    reg = (tot + (ffn_rows - 1)) // ffn_rows * ffn_rows
    rstart = jnp.cumsum(reg) - reg
    gs = rstart[None, :] + jnp.cumsum(pc, axis=0) - pc
    tiles_e = reg // ffn_rows
    tile_end = jnp.cumsum(tiles_e)
    max_rows = 2 * nblk * tb + nblk * N_EXPERTS * _UNIT + N_EXPERTS * ffn_rows
    max_tiles = -(-max_rows // ffn_rows)
    tj =jnp.arange(max_tiles, dtype=jnp.int32)
    te = jnp.minimum(jnp.sum((tj[:, None] >= tile_end[None, :]).astype(jnp.int32), axis=1), N_EXPERTS - 1)
    tv = jnp.clip(tot[te] - (tj - (tile_end - tiles_e)[te]) * ffn_rows, 0, ffn_rows)
    nt = tile_end[-1:].astype(jnp.int32)
    cap = 2 * tb + N_EXPERTS * _UNIT
    urow0 = jnp.arange(cap // _UNIT, dtype=jnp.int32) * _UNIT
    run = jnp.sum((urow0[None, :, None] >= (lo + pc)[:, None, :]).astype(jnp.int32), axis=2)
    pick = run[:, :, None] == jnp.arange(N_EXPERTS, dtype=jnp.int32)[None, None, :]
    urow = jnp.sum(jnp.where(pick, (gs - lo)[:, None, :], 0), axis=2) + urow0[None, :]
    urow = jnp.where(urow0[None, :] < jnp.sum(pc, axis=1, keepdims=True), urow, -1)
    flat = lambda a: a.reshape(-1).astype(jnp.int32)
    return flat(lo), flat(urow), te.astype(jnp.int32), tv.astype(jnp.int32), nt, max_tiles, cap


def _moe(x1b, x1, ri, rw, cnt, p, wg, wu, wd, layer, ln_g, ln_b, wpg, bpg, wpp, alpha, tb):
    n = x1.shape[0]
    nblk = n // tb
    ffn_rows = _FFN_ROWS if 2 * n >= N_EXPERTS * _FFN_ROWS else LANES
    lo, urow, te, tv, nt, max_tiles, cap = _moe_plan(cnt, nblk, tb, ffn_rows)
    max_rows = max_tiles * ffn_rows

    xs, pos = pl.pallas_call(
        functools.partial(_dispatch_kernel, nblk=nblk, tb=tb, cap=cap, spare_row=max_rows),
        grid_spec=pltpu.PrefetchScalarGridSpec(
            num_scalar_prefetch=2,
            grid=(nblk,),
            in_specs=[pl.BlockSpec((SUBLANES, tb), lambda i, *_: (0, i)),
                      pl.BlockSpec((SUBLANES, tb), lambda i, *_: (0, i)),
                      pl.BlockSpec((tb, D_MODEL), lambda i, *_: (i, 0))],
            out_specs=[pl.BlockSpec(memory_space=pl.ANY),
                       pl.BlockSpec((SUBLANES, tb), lambda i, *_: (0, i))],
            scratch_shapes=[pltpu.VMEM((2, cap, _XS_WIDTH), BF16),
                            pltpu.SemaphoreType.DMA((2,))]),
        out_shape=[jax.ShapeDtypeStruct((max_rows + 2 * cap, _XS_WIDTH), BF16),
                   jax.ShapeDtypeStruct((SUBLANES, n), jnp.int32)],
        compiler_params=_params(("arbitrary",), 32),
        name="moe_dispatch",
    )(lo, urow, ri, rw, x1b)

    def tile(j, te_ref, tv_ref, nt_ref):
        return jnp.minimum(j, nt_ref[0] - 1)

    ys = pl.pallas_call(
        _ffn_kernel,
        grid_spec=pltpu.PrefetchScalarGridSpec(
            num_scalar_prefetch=3,
            grid=(max_tiles,),
            in_specs=[pl.BlockSpec((ffn_rows, _XS_WIDTH), lambda j, *s: (tile(j, *s), 0)),
                      pl.BlockSpec((None, None, D_MODEL, D_EXPERT),
                                   lambda j, *s: (layer, s[0][tile(j, *s)], 0, 0)),
                      pl.BlockSpec((None, None, D_MODEL, D_EXPERT),
                                   lambda j, *s: (layer, s[0][tile(j, *s)], 0, 0)),
                      pl.BlockSpec((None, None, D_EXPERT, D_MODEL),
                                   lambda j, *s: (layer, s[0][tile(j, *s)], 0, 0))],
            out_specs=pl.BlockSpec((ffn_rows, D_MODEL), lambda j, *s: (tile(j, *s), 0)),
            scratch_shapes=[pltpu.VMEM((D_MODEL, D_EXPERT), BF16), pltpu.VMEM((D_MODEL, D_EXPERT), BF16),
                            pltpu.VMEM((D_EXPERT, D_MODEL), BF16)]),
        out_shape=jax.ShapeDtypeStruct((max_rows, D_MODEL), BF16),
        compiler_params=_params(("arbitrary",), 40),
        name="moe_ffn",
    )(te, tv, nt, xs, wg, wu, wd)

    const = lambda i, *_: (0, 0)
    return pl.pallas_call(
        functools.partial(_combine_kernel, nblk=nblk, tb=tb, cap=cap, alpha=alpha),
        grid_spec=pltpu.PrefetchScalarGridSpec(
            num_scalar_prefetch=1,
            grid=(nblk,),
            in_specs=[pl.BlockSpec((SUBLANES, tb), lambda i, *_: (0, i)),
                      pl.BlockSpec((tb, D_MODEL), lambda i, *_: (i, 0)),
                      pl.BlockSpec((None, tb, PLE_DIM), lambda i, *_: (layer, i, 0)),
                      pl.BlockSpec((1, D_MODEL), const), pl.BlockSpec((1, D_MODEL), const),
                      pl.BlockSpec((D_MODEL, D_MODEL), const), pl.BlockSpec((1, D_MODEL), const),
                      pl.BlockSpec((PLE_DIM, D_MODEL), const),
                      pl.BlockSpec(memory_space=pl.ANY)],
            out_specs=pl.BlockSpec((tb, D_MODEL), lambda i, *_: (i, 0)),
            scratch_shapes=[pltpu.VMEM((2, cap, D_MODEL), BF16),
                            pltpu.SemaphoreType.DMA((2,))]),
        out_shape=jax.ShapeDtypeStruct((n, D_MODEL), F32),
        compiler_params=_params(("arbitrary",), 40),
        name="moe_combine",
    )(urow, pos, x1, p, ln_g.reshape(1, D_MODEL), ln_b.reshape(1, D_MODEL),
      wpg, bpg.reshape(1, D_MODEL), wpp, ys)


def _rope_tables(pos, rows):
    half = R_KEY_DIM // 2
    inv_freq = 1.0 / (ROPE_BASE ** jnp.linspace(0.0, 1.0, half, dtype=jnp.float32))
    ang = pos.astype(jnp.float32)[:, None] * inv_freq[None, :]
    cos = jnp.cos(ang)
    sin = jnp.sin(ang)
    cos_t = jnp.tile(cos, (rows // pos.shape[0], 4))
    sin_t = jnp.tile(jnp.concatenate([-sin, sin], axis=1), (rows // pos.shape[0], 2))
    return cos_t, sin_t


def kernel(x_prompt, x_sample, p_prompt, p_sample, cache_k, cache_v, state_ret, state_conv, ln_emb_g, ln_emb_b, rel_bias, w_router, b_router, w_in, lam_q1, lam_k1, lam_q2, lam_k2, subln_g, dw_w, dw_b, conv_ln_g, conv_ln_b, w_out, ln1_g, ln1_b, w_exp_gate, w_exp_up, w_exp_down, ln2_g, ln2_b, w_ple_gate, b_ple_gate, w_ple_proj):
    B, T, D = x_prompt.shape
    Bs, Ts, _ = x_sample.shape
    depth = w_in.shape[0]
    past = cache_k.shape[2]
    n_p, n_s = B * T, Bs * Ts
    alpha = (2 * depth) ** 0.25

    tm_p = 512 if n_p % 512 == 0 else n_p
    tm_s = n_s
    TB = 2 * LANES
    c_p = min(T, 2 * LANES)

    pos_p = jnp.arange(T, dtype=jnp.int32)
    pos_s = past + jnp.arange(Ts, dtype=jnp.int32)
    rope_p = _rope_tables(pos_p, max(T, tm_p))
    rope_s = _rope_tables(pos_s, max(Ts, tm_s))
    rel_p = (jnp.arange(2 * TB, dtype=jnp.int32)[None, :] - TB) - jnp.arange(TB, dtype=jnp.int32)[:, None]
    padk = -(-(past + Ts) // LANES) * LANES
    rel_s = jnp.arange(padk, dtype=jnp.int32)[None, :] - pos_s[:, None]
    bias_p = _bias_table(rel_bias, _t5_bucket(rel_p))
    bias_s = _bias_table(rel_bias, _t5_bucket(rel_s))
    tabs_p = _retention_tables(c_p)
    tabs_s = _retention_tables(Ts)

    br = b_router.reshape(N_EXPERTS, 1)
    ret0_p = jnp.zeros((B, R_HEADS, R_KEY_DIM, R_V_DIM), F32)
    conv0_p = jnp.zeros((B, CONV_WIDTH - 1, C_WIDTH), F32)

    xp = x_prompt.reshape(n_p, D)
    xs = x_sample.reshape(n_s, D)
    outs = {k: [] for k in ("rp", "cp", "rs", "cs")}
    kv_p = kv_s = None
    for l in range(depth):
        lam_init = 0.8 - 0.6 * math.exp(-0.3 * l)
        lam4 = jnp.stack([lam_q1[l], lam_k1[l], lam_q2[l], lam_k2[l]])
        g_sub = subln_g[l].reshape(1, A_V_DIM)
        w_in_b = w_in[l].astype(BF16)
        w_out_b = w_out[l].astype(BF16)
        wpg, wpp = w_ple_gate[l].astype(BF16), w_ple_proj[l].astype(BF16)
        ln = (ln_emb_g, ln_emb_b) if l == 0 else None

        def channel(x, attn, ret, conv, p_l, tm):
            tb = min(tm, _DISPATCH_TOKENS)
            tm_o = 2 * tm if x.shape[0] % (2 * tm) == 0 else tm
            x1, x1b, ri, rw, cnt = _outproj(attn, ret, conv, x, w_out_b, ln1_g[l], ln1_b[l], w_router, br,
                                            alpha, tm_o, tb)
            return _moe(x1b, x1, ri, rw, cnt, p_l, w_exp_gate, w_exp_up, w_exp_down, l, ln2_g[l], ln2_b[l],
                        wpg, b_ple_gate[l], wpp, alpha, tb)

        xp, q, kb, vb, rq, rk, rv, rg, u, k5, v5 = _inproj(xp, w_in_b, rope_p[0], rope_p[1], ln, tm_p,
                                                           kv_p, l, depth, B, T)
        kv_p = (k5, v5)
        attn = _attn_prompt(lam4, g_sub, bias_p, q, kb, vb, B, T, lam_init)
        ret, rstate = _retention(rq, rk, rv, rg, ret0_p, tabs_p, B, T, c_p)
        conv, ctail = _conv(u, conv0_p, dw_w[l], dw_b[l], conv_ln_g[l], conv_ln_b[l], B, T)
        xp = channel(xp, attn, ret, conv, p_prompt.reshape(depth, n_p, PLE_DIM), tm_p)
        outs["rp"].append(rstate)
        outs["cp"].append(ctail)

        xs, q, kb, vb, rq, rk, rv, rg, u, k5, v5 = _inproj(xs, w_in_b, rope_s[0], rope_s[1], ln, tm_s,
                                                           kv_s, l, depth, Bs, Ts)
        kv_s = (k5, v5)
        attn = _attn_decode(lam4, g_sub, bias_s, q, cache_k, cache_v, kb, vb, l, Bs, Ts, lam_init)
        ret, rstate = _retention(rq, rk, rv, rg, state_ret[l], tabs_s, Bs, Ts, Ts)
        conv, ctail = _conv(u, state_conv[l], dw_w[l], dw_b[l], conv_ln_g[l], conv_ln_b[l], Bs, Ts)
        xs = channel(xs, attn, ret, conv, p_sample.reshape(depth, n_s, PLE_DIM), tm_s)
        outs["rs"].append(rstate)
        outs["cs"].append(ctail)

    return (xp.reshape(B, T, D), xs.reshape(Bs, Ts, D),
            kv_p[0], kv_p[1], jnp.stack(outs["rp"]), jnp.stack(outs["cp"]),
            kv_s[0], kv_s[1], jnp.stack(outs["rs"]), jnp.stack(outs["cs"]))
```

```python
import functools
import math

import jax
import jax.numpy as jnp
from jax import lax
from jax.experimental import pallas as pl
from jax.experimental.pallas import tpu as pltpu

F32 = jnp.float32
BF16 = jnp.bfloat16

D_MODEL = 1024
CHUNK = 64
HEAD_DIM = 64
A_HEADS = 4
A_V_DIM = 128
A_WIDTH = 512
R_HEADS = 4
R_KEY_DIM = 64
R_V_DIM = 64
R_WIDTH = 256
C_WIDTH = 256
CONV_WIDTH = 31
IN_WIDTH = 3072
N_BUCKETS = 32
MAX_DISTANCE = 128
ROPE_BASE = 10000.0
N_EXPERTS = 16
N_GROUPS = 4
EXPERTS_PER_GROUP = 4
D_EXPERT = 512
PLE_DIM = 256
LN_EPS = 1e-5
NEG_INF = -1e30
LOG2E = 1.4426950408889634

LANES = 128
SUBLANES = 8
_LOG2_CHUNK = 6
assert CHUNK == HEAD_DIM == R_KEY_DIM == R_V_DIM == 1 << _LOG2_CHUNK
MIB = 1024 * 1024

_OFF_Q, _OFF_K, _OFF_V = 0, 512, 1024
_OFF_RQ, _OFF_RK, _OFF_RV, _OFF_RG, _OFF_C = 1536, 1792, 2048, 2304, 2560


def _params(sem, vmem_mib):
    return pltpu.CompilerParams(dimension_semantics=sem, vmem_limit_bytes=vmem_mib * MIB)


def _layer_norm(x, g, b):
    mu = jnp.mean(x, axis=-1, keepdims=True)
    xc = x - mu
    var = jnp.mean(xc * xc, axis=-1, keepdims=True)
    return xc * lax.rsqrt(var + LN_EPS) * g + b


def _sigmoid(x):
    return 1.0 / (1.0 + jnp.exp(-x))


def _dot(a, b):
    return jnp.dot(a, b, preferred_element_type=F32)


def _dot_nt(a, b):
    return lax.dot_general(a, b, (((1,), (1,)), ((), ())), preferred_element_type=F32)


def _dot_tn(a, b):
    return lax.dot_general(a, b, (((0,), (0,)), ((), ())), preferred_element_type=F32)


def _bias_kernel(relb_ref, idx_ref, out_ref):
    idx = idx_ref[...]
    for h in range(A_HEADS):
        acc = jnp.zeros(idx.shape, F32)
        for b in range(N_BUCKETS):
            acc = jnp.where(idx == b, relb_ref[b, h], acc)
        out_ref[h] = acc


def _bias_table(rel_bias, idx):
    r, c = idx.shape
    return pl.pallas_call(
        _bias_kernel,
        out_shape=jax.ShapeDtypeStruct((A_HEADS, r, c), F32),
        in_specs=[pl.BlockSpec(memory_space=pltpu.SMEM),
                  pl.BlockSpec(memory_space=pltpu.VMEM)],
        out_specs=pl.BlockSpec(memory_space=pltpu.VMEM),
        name="bias_table",
    )(rel_bias, idx)


def _t5_bucket(rel):
    nb = N_BUCKETS // 2
    max_exact = nb // 2
    n = jnp.abs(rel)
    nf = jnp.maximum(n, 1).astype(jnp.float32)
    large = max_exact + (jnp.log(nf / max_exact) / math.log(MAX_DISTANCE / max_exact)
                         * (nb - max_exact)).astype(jnp.int32)
    large = jnp.minimum(large, nb - 1)
    return jnp.where(rel > 0, nb, 0) + jnp.where(n < max_exact, n, large)


def _rotary128(x, cos, sin_signed, lo32):
    partner = jnp.where(lo32, pltpu.roll(x, 96, 1), pltpu.roll(x, 32, 1))
    return x * cos + partner * sin_signed


def _inproj_kernel(*refs, apply_ln, has_prev, layer, tm, T, nsteps):
    refs = list(refs)
    x_ref = refs.pop(0)
    if apply_ln:
        g_ref, b_ref = refs.pop(0), refs.pop(0)
    w_ref, cos_ref, sin_ref = refs.pop(0), refs.pop(0), refs.pop(0)
    if has_prev:
        refs.pop(0), refs.pop(0)
    if apply_ln:
        xn_ref = refs.pop(0)
    (q_ref, kb_ref, vb_ref, rq_ref, rk_ref, rv_ref, rg_ref, u_ref, k5_hbm, v5_hbm,
     kbuf, vbuf, sems) = refs

    i = pl.program_id(0)
    slot = i % 2
    rows = min(tm, T)

    def kv_copies(s, step):
        cps = []
        for bb in range(tm // rows):
            r0 = step * tm + bb * rows
            b = r0 // T
            t0 = pl.multiple_of(r0 % T, SUBLANES)
            for buf, out in ((kbuf, k5_hbm), (vbuf, v5_hbm)):
                for h in range(A_HEADS):
                    cps.append(pltpu.make_async_copy(
                        buf.at[s, pl.ds(bb * rows, rows), pl.ds(h * A_V_DIM, A_V_DIM)],
                        out.at[layer, b, pl.ds(t0, rows), h, :], sems.at[s]))
        return cps

    @pl.when(i >= 2)
    def _():
        for cp in kv_copies(slot, i - 2):
            cp.wait()

    x = x_ref[...]
    if apply_ln:
        x = _layer_norm(x, g_ref[...], b_ref[...])
        xn_ref[...] = x
    xb = x.astype(BF16)

    def mm(c0, c1):
        return _dot(xb, w_ref[:, c0:c1])

    q_ref[...] = (mm(_OFF_Q, _OFF_K) * (LOG2E * HEAD_DIM ** -0.5)).astype(BF16)
    a = mm(_OFF_K, _OFF_V)
    kbuf[slot] = a
    kb_ref[...] = a.astype(BF16)
    a = mm(_OFF_V, _OFF_RQ)
    vbuf[slot] = a
    vb_ref[...] = a.astype(BF16)
    for cp in kv_copies(slot, i):
        cp.start()

    cos = cos_ref[...]
    sin = sin_ref[...]
    lane = lax.broadcasted_iota(jnp.int32, cos.shape, 1)
    lo32 = (lane & 63) < 32

    def rot(a, scale):
        parts = [_rotary128(a[:, c * LANES:(c + 1) * LANES], cos, sin, lo32) for c in range(2)]
        r = jnp.concatenate(parts, axis=1)
        if scale != 1.0:
            r = r * scale
        return r.astype(BF16)

    rq_ref[...] = rot(mm(_OFF_RQ, _OFF_RK), 1.0)
    rk_ref[...] = rot(mm(_OFF_RK, _OFF_RV), R_KEY_DIM ** -0.5)
    rv_ref[...] = mm(_OFF_RV, _OFF_RG).astype(BF16)
    a = mm(_OFF_RG, _OFF_C)
    rg_ref[...] = a * _sigmoid(a)
    a = mm(_OFF_C, IN_WIDTH)
    u_ref[...] = a[:, :C_WIDTH] * _sigmoid(a[:, C_WIDTH:])

    @pl.when(i == nsteps - 1)
    def _():
        for cp in kv_copies(slot, i):
            cp.wait()
        if nsteps > 1:
            for cp in kv_copies(1 - slot, i - 1):
                cp.wait()


def _inproj(x, w_bf, cos_t, sin_t, ln, tm, kv_prev, layer, depth, B, T):
    n = x.shape[0]
    assert (tm % T == 0 or T % tm == 0) and n % tm == 0
    nblk = cos_t.shape[0] // tm
    nsteps = n // tm
    row = lambda i: (i, 0)
    const = lambda i: (0, 0)
    tab = lambda i: (i % nblk, 0)
    in_specs = [pl.BlockSpec((tm, D_MODEL), row)]
    args = [x]
    if ln is not None:
        in_specs += [pl.BlockSpec((1, D_MODEL), const)] * 2
        args += [ln[0].reshape(1, D_MODEL), ln[1].reshape(1, D_MODEL)]
    in_specs += [pl.BlockSpec((D_MODEL, IN_WIDTH), const),
                 pl.BlockSpec((tm, LANES), tab), pl.BlockSpec((tm, LANES), tab)]
    args += [w_bf, cos_t, sin_t]
    aliases = {}
    if kv_prev is not None:
        k5_index = (1 if ln is not None else 0) + 8
        aliases = {len(args): k5_index, len(args) + 1: k5_index + 1}
        in_specs += [pl.BlockSpec(memory_space=pl.ANY)] * 2
        args += list(kv_prev)

    def o(width, dt):
        return jax.ShapeDtypeStruct((n, width), dt), pl.BlockSpec((tm, width), row)

    outs = []
    if ln is not None:
        outs.append(o(D_MODEL, F32))
    outs += [o(512, BF16), o(512, BF16), o(512, BF16),
             o(256, BF16), o(256, BF16), o(256, BF16), o(256, F32), o(256, F32)]
    kv5 = jax.ShapeDtypeStruct((depth, B, T, A_HEADS, A_V_DIM), F32)
    outs += [(kv5, pl.BlockSpec(memory_space=pl.ANY))] * 2
    res = pl.pallas_call(
        functools.partial(_inproj_kernel, apply_ln=ln is not None, has_prev=kv_prev is not None,
                          layer=layer, tm=tm, T=T, nsteps=nsteps),
        grid=(nsteps,),
        in_specs=in_specs,
        out_specs=[s for _, s in outs],
        out_shape=[s for s, _ in outs],
        scratch_shapes=[pltpu.VMEM((2, tm, A_WIDTH), F32), pltpu.VMEM((2, tm, A_WIDTH), F32),
                        pltpu.SemaphoreType.DMA((2,))],
        input_output_aliases=aliases,
        compiler_params=_params(("arbitrary",), 52),
        name="inproj",
    )(*args)
    if ln is None:
        res = [x] + list(res)
    return res


def _lambda(lam_ref, lam_init):
    lv = lam_ref[...]
    s1 = jnp.sum(lv[0:1] * lv[1:2], axis=-1, keepdims=True)
    s2 = jnp.sum(lv[2:3] * lv[3:4], axis=-1, keepdims=True)
    return jnp.exp(s1) - jnp.exp(s2) + lam_init


def _stack_maps(q):
    lane = lax.broadcasted_iota(jnp.int32, q.shape, 1)
    lo = lane < HEAD_DIM
    z = jnp.zeros_like(q)
    return jnp.concatenate([jnp.where(lo, q, z), jnp.where(lo, z, q)], axis=0)


def _diff_finish(o2, t, lam, g, lam_init):
    o = o2[:t] - lam * o2[t:]
    ms = jnp.mean(o * o, axis=-1, keepdims=True)
    return (o * lax.rsqrt(ms + LN_EPS) * g * (1.0 - lam_init)).astype(BF16)


_ATTN_AHEAD = 1


def _attn_prompt_kernel(lam_ref, g_ref, bias_ref, q_ref, k_ref, v_ref, o_ref,
                        s_scr, m_scr, p_scr, vo_scr, bp_scr, bd_scr, *, T, TB, lam_init):
    lam = _lambda(lam_ref, lam_init)
    g = g_ref[...]
    bt = bias_ref[...]
    bt = (bt - bt[0:1, 0:1]) * LOG2E
    bp_scr[...] = bt[:, :TB]
    bd_scr[...] = bt[:, TB:]
    half = TB // 2
    vo_scr[:, :A_V_DIM] = v_ref[...]
    vo_scr[:, A_V_DIM:] = jnp.ones((T, A_V_DIM), BF16)

    def pass1(qi):
        r0 = qi * TB
        buf = qi % (_ATTN_AHEAD + 1)
        q = q_ref[r0:r0 + TB, :]
        lo = lax.broadcasted_iota(jnp.int32, q.shape, 1) < HEAD_DIM
        zq = jnp.zeros_like(q)
        maps = (jnp.where(lo, q, zq), jnp.where(lo, zq, q))
        for mp in range(2):
            m = None
            for ki in range(qi + 1):
                s = _dot_nt(maps[mp], k_ref[ki * TB:(ki + 1) * TB, :])
                if ki == qi - 1:
                    s = s + bp_scr[...]
                elif ki == qi:
                    row = lax.broadcasted_iota(jnp.int32, (TB, TB), 0)
                    col = lax.broadcasted_iota(jnp.int32, (TB, TB), 1)
                    vis = (col >> _LOG2_CHUNK) <= (row >> _LOG2_CHUNK)
                    s = jnp.where(vis, s + bd_scr[...], NEG_INF)
                s_scr[buf, ki, mp * TB:(mp + 1) * TB, :] = s
                mt = jnp.maximum(s[:, :half], s[:, half:])
                m = mt if m is None else jnp.maximum(m, mt)
            m_scr[buf, mp * TB:(mp + 1) * TB, :] = jnp.broadcast_to(jnp.max(m, axis=1, keepdims=True),
                                                                    (TB, LANES))

    def pass2(qi):
        r0 = qi * TB
        buf = qi % (_ATTN_AHEAD + 1)
        mb = m_scr[buf]
        mb2 = jnp.concatenate([mb, mb], axis=1)
        for ki in range(qi + 1):
            p_scr[buf, :, ki * TB:(ki + 1) * TB] = jnp.exp2(s_scr[buf, ki] - mb2).astype(BF16)
        kk = (qi + 1) * TB
        out = _dot(p_scr[buf, :, 0:kk], vo_scr[0:kk, :])
        o_ref[r0:r0 + TB, :] = _diff_finish(out[:, :A_V_DIM] / out[:, A_V_DIM:], TB, lam, g, lam_init)

    nq = T // TB
    for qi in range(min(_ATTN_AHEAD, nq)):
        pass1(qi)
    for qi in range(nq):
        if qi + _ATTN_AHEAD < nq:
            pass1(qi + _ATTN_AHEAD)
        pass2(qi)


def _attn_prompt(lam4, g, bias, q, k, v, B, T, lam_init):
    TB = 2 * LANES
    assert T % TB == 0 and TB % CHUNK == 0
    n = B * T
    blk = pl.BlockSpec((T, A_V_DIM), lambda b, h: (b, h))
    return pl.pallas_call(
        functools.partial(_attn_prompt_kernel, T=T, TB=TB, lam_init=lam_init),
        grid=(B, A_HEADS),
        in_specs=[pl.BlockSpec((4, HEAD_DIM), lambda b, h: (0, 0)),
                  pl.BlockSpec((1, A_V_DIM), lambda b, h: (0, 0)),
                  pl.BlockSpec((None, TB, 2 * TB), lambda b, h: (h, 0, 0)),
                  blk, blk, blk],
        out_specs=blk,
        out_shape=jax.ShapeDtypeStruct((n, A_WIDTH), BF16),
        scratch_shapes=[pltpu.VMEM((_ATTN_AHEAD + 1, T // TB, 2 * TB, TB), F32),
                        pltpu.VMEM((_ATTN_AHEAD + 1, 2 * TB, LANES), F32),
                        pltpu.VMEM((_ATTN_AHEAD + 1, 2 * TB, T), BF16),
                        pltpu.VMEM((T, 2 * A_V_DIM), BF16),
                        pltpu.VMEM((TB, TB), F32),
                        pltpu.VMEM((TB, TB), F32)],
        compiler_params=_params(("arbitrary", "arbitrary"), 32),
        name="attn_prompt",
    )(lam4, g, bias, q, k, v)


def _attn_decode_kernel(lam_ref, g_ref, bias_ref, q_ref, kn_ref, vn_ref, ck_hbm, cv_hbm, o_ref,
                        kc_buf, vc_buf, sems, *, Ts, past, layer, lam_init):
    b = pl.program_id(0)
    h = pl.program_id(1)
    step = b * A_HEADS + h
    nsteps = pl.num_programs(0) * A_HEADS
    slot = step % 2

    def cache_copies(st, s):
        bb = st // A_HEADS
        hh = st % A_HEADS
        return [pltpu.make_async_copy(src.at[layer, bb, :, hh, :], dst.at[s], sems.at[s, j])
                for j, (src, dst) in enumerate(((ck_hbm, kc_buf), (cv_hbm, vc_buf)))]

    @pl.when(step == 0)
    def _():
        for cp in cache_copies(0, 0):
            cp.start()

    @pl.when(step + 1 < nsteps)
    def _():
        for cp in cache_copies(step + 1, 1 - slot):
            cp.start()

    for cp in cache_copies(step, slot):
        cp.wait()
    kc_ref = kc_buf.at[slot]
    vc_ref = vc_buf.at[slot]

    lam = _lambda(lam_ref, lam_init)
    q2 = _stack_maps(q_ref[...])
    bias = bias_ref[...] * LOG2E
    s_p = _dot_nt(q2, kc_ref[...].astype(BF16)).reshape(2, Ts, past) + bias[:, :past][None]
    s_n = _dot_nt(q2, kn_ref[...]).reshape(2, Ts, Ts) + bias[:, past:past + Ts][None]
    s_p = s_p.reshape(2 * Ts, past)
    s_n = s_n.reshape(2 * Ts, Ts)
    m = jnp.maximum(jnp.max(s_p, axis=1, keepdims=True), jnp.max(s_n, axis=1, keepdims=True))
    p_p = jnp.exp2(s_p - m)
    p_n = jnp.exp2(s_n - m)
    l = jnp.sum(p_p, axis=1, keepdims=True) + jnp.sum(p_n, axis=1, keepdims=True)
    acc = _dot(p_p.astype(BF16), vc_ref[...].astype(BF16)) + _dot(p_n.astype(BF16), vn_ref[...])
    o_ref[...] = _diff_finish(acc / l, Ts, lam, g_ref[...], lam_init)


def _attn_decode(lam4, g, bias, q, cache_k, cache_v, kn, vn, layer, Bs, Ts, lam_init):
    past = cache_k.shape[2]
    assert past % CHUNK == 0 and Ts <= CHUNK
    padk = bias.shape[2]
    new = pl.BlockSpec((Ts, A_V_DIM), lambda b, h: (b, h))
    cache = pl.BlockSpec(memory_space=pl.ANY)
    return pl.pallas_call(
        functools.partial(_attn_decode_kernel, Ts=Ts, past=past, layer=layer, lam_init=lam_init),
        grid=(Bs, A_HEADS),
        in_specs=[pl.BlockSpec((4, HEAD_DIM), lambda b, h: (0, 0)),
                  pl.BlockSpec((1, A_V_DIM), lambda b, h: (0, 0)),
                  pl.BlockSpec((None, Ts, padk), lambda b, h: (h, 0, 0)),
                  new, new, new, cache, cache],
        out_specs=new,
        out_shape=jax.ShapeDtypeStruct((Bs * Ts, A_WIDTH), BF16),
        scratch_shapes=[pltpu.VMEM((2, past, A_V_DIM), F32), pltpu.VMEM((2, past, A_V_DIM), F32),
                        pltpu.SemaphoreType.DMA((2, 2))],
        compiler_params=_params(("arbitrary", "arbitrary"), 40),
        name="attn_decode",
    )(lam4, g, bias, q, kn, vn, cache_k, cache_v)


def _ret_kernel(q_ref, k_ref, v_ref, g_ref, s0_ref, dm_ref, qd_ref, kd_ref, cd_ref,
                o_ref, sn_ref, *, T, C):
    z = jnp.zeros((R_KEY_DIM, R_V_DIM), F32)
    state = jnp.concatenate([jnp.concatenate([s0_ref[0], z], axis=1),
                             jnp.concatenate([z, s0_ref[1]], axis=1)], axis=0)
    r = lax.broadcasted_iota(jnp.int32, (LANES, LANES), 0)
    c = lax.broadcasted_iota(jnp.int32, (LANES, LANES), 1)
    same_head = (r >> _LOG2_CHUNK) == (c >> _LOG2_CHUNK)
    ones_bd = jnp.where(same_head, 1.0, 0.0).astype(BF16)
    lo = lax.broadcasted_iota(jnp.int32, (C, LANES), 1) < R_V_DIM
    cd = cd_ref[...]

    def within(n):
        r0 = n * C
        q = q_ref[pl.ds(r0, C), :]
        k = k_ref[pl.ds(r0, C), :]
        v = v_ref[pl.ds(r0, C), :]
        p = (_dot_nt(_stack_maps(q), k) * dm_ref[...]).astype(BF16)
        o2 = _dot(p, v)
        inner = jnp.where(lo, o2[:C], o2[C:])
        qd = (q.astype(F32) * qd_ref[...]).astype(BF16)
        vk = (v.astype(F32) * kd_ref[...]).astype(BF16)
        return inner, qd, jnp.where(same_head, _dot_tn(k, vk), 0.0)

    def finish(n, parts, state):
        r0 = n * C
        inner, qd, kv = parts
        o = inner + _dot(qd, state.astype(BF16))
        new_state = cd * state + kv
        oo = o * o
        hi = oo.astype(BF16)
        lo_part = (oo - hi.astype(F32)).astype(BF16)
        ss = _dot(hi, ones_bd) + _dot(lo_part, ones_bd)
        out = o * lax.rsqrt(ss * (1.0 / R_V_DIM) + LN_EPS) * g_ref[pl.ds(r0, C), :]
        o_ref[pl.ds(r0, C), :] = out.astype(BF16)
        return new_state

    nchunks = T // C
    ahead = within(0)
    for n in range(nchunks):
        parts = ahead
        if n + 1 < nchunks:
            ahead = within(n + 1)
        state = finish(n, parts, state)
    sn_ref[0] = state[:R_KEY_DIM, :R_V_DIM]
    sn_ref[1] = state[R_KEY_DIM:, R_V_DIM:]


def _retention(rq, rk, rv, rg, state0, tabs, B, T, C):
    dm, qd, kd, cd = tabs
    blk = pl.BlockSpec((T, LANES), lambda b, hp: (b, hp))
    st = pl.BlockSpec((None, 2, R_KEY_DIM, R_V_DIM), lambda b, hp: (b, hp, 0, 0))
    return pl.pallas_call(
        functools.partial(_ret_kernel, T=T, C=C),
        grid=(B, 2),
        in_specs=[blk, blk, blk, blk, st,
                  pl.BlockSpec((None, 2 * C, C), lambda b, hp: (hp, 0, 0)),
                  pl.BlockSpec((None, C, LANES), lambda b, hp: (hp, 0, 0)),
                  pl.BlockSpec((None, C, LANES), lambda b, hp: (hp, 0, 0)),
                  pl.BlockSpec((None, 1, LANES), lambda b, hp: (hp, 0, 0))],
        out_specs=[blk, st],
        out_shape=[jax.ShapeDtypeStruct((B * T, R_WIDTH), BF16),
                   jax.ShapeDtypeStruct((B, R_HEADS, R_KEY_DIM, R_V_DIM), F32)],
        compiler_params=_params(("arbitrary", "arbitrary"), 32),
        name="retention",
    )(rq, rk, rv, rg, state0, dm, qd, kd, cd)


def _retention_tables(C):
    h = jnp.arange(R_HEADS, dtype=F32)
    log_g = jnp.log1p(-jnp.exp2(-5.0 - h))
    i = jnp.arange(C, dtype=F32)
    diff = i[:, None] - i[None, :]
    dmask = jnp.where(diff >= 0, jnp.exp(jnp.maximum(diff, 0.0)[None] * log_g[:, None, None]), 0.0)
    dm = dmask.reshape(2, 2 * C, C)
    q_dec = jnp.exp((i + 1.0)[None, :] * log_g[:, None])
    k_dec = jnp.exp((C - 1 - i)[None, :] * log_g[:, None])
    c_dec = jnp.exp(C * log_g)

    def lanes(t):
        t = jnp.repeat(t[:, :, None], R_V_DIM, axis=2).reshape(2, 2, C, R_V_DIM)
        return jnp.concatenate([t[:, 0], t[:, 1]], axis=-1)

    cd = jnp.repeat(c_dec[:, None], R_V_DIM, axis=1).reshape(2, 1, LANES)
    return dm, lanes(q_dec), lanes(k_dec), cd


_CONV_PAD = 32


def _conv_kernel(u_ref, h_ref, w_ref, b_ref, g_ref, be_ref, o_ref, t_ref, up_scr, *, T, RT):
    hist = CONV_WIDTH - 1
    off = _CONV_PAD - hist
    up_scr[off:_CONV_PAD, :] = h_ref[...]
    up_scr[_CONV_PAD:_CONV_PAD + T, :] = u_ref[...]
    bias = b_ref[...]
    g = g_ref[...]
    be = be_ref[...]
    win_rows = RT + _CONV_PAD
    for t0 in range(0, T, RT):
        win = up_scr[t0:t0 + win_rows, :]
        acc = jnp.zeros((RT, C_WIDTH), F32)
        for s in range(SUBLANES):
            rolled = win if s == 0 else pltpu.roll(win, win_rows - s, 0)
            for a in range(_CONV_PAD // SUBLANES + 1):
                j = a * SUBLANES + s - off
                if 0 <= j < CONV_WIDTH:
                    acc = acc + rolled[a * SUBLANES:a * SUBLANES + RT, :] * w_ref[j:j + 1, :]
        y = _layer_norm(acc + bias, g, be)
        o_ref[t0:t0 + RT, :] = (y * _sigmoid(y)).astype(BF16)
    t_ref[...] = up_scr[T + off:T + _CONV_PAD, :]


def _conv(u, hist, dw_w, dw_b, ln_g, ln_b, B, T):
    RT = min(T, LANES)
    assert T % RT == 0
    hl = CONV_WIDTH - 1
    vec = pl.BlockSpec((1, C_WIDTH), lambda b: (0, 0))
    hb = pl.BlockSpec((None, hl, C_WIDTH), lambda b: (b, 0, 0))
    return pl.pallas_call(
        functools.partial(_conv_kernel, T=T, RT=RT),
        grid=(B,),
        in_specs=[pl.BlockSpec((T, C_WIDTH), lambda b: (b, 0)), hb,
                  pl.BlockSpec((CONV_WIDTH, C_WIDTH), lambda b: (0, 0)), vec, vec, vec],
        out_specs=[pl.BlockSpec((T, C_WIDTH), lambda b: (b, 0)), hb],
        out_shape=[jax.ShapeDtypeStruct((B * T, C_WIDTH), BF16),
                   jax.ShapeDtypeStruct((B, hl, C_WIDTH), F32)],
        scratch_shapes=[pltpu.VMEM((T + _CONV_PAD, C_WIDTH), F32)],
        compiler_params=_params(("arbitrary",), 32),
        name="conv",
    )(u, hist, dw_w, dw_b.reshape(1, C_WIDTH), ln_g.reshape(1, C_WIDTH), ln_b.reshape(1, C_WIDTH))


def _first_index(vals, target):
    idx = jnp.full(target.shape, len(vals) - 1, jnp.int32)
    for j in range(len(vals) - 2, -1, -1):
        idx = jnp.where(vals[j] == target, j, idx)
    return idx


def _select(idx, vals):
    out = vals[-1]
    for j in range(len(vals) - 2, -1, -1):
        out = jnp.where(idx == j, vals[j], out)
    return out


def _route(logits_t):
    rows = [logits_t[e:e + 1, :] for e in range(N_EXPERTS)]
    m = functools.reduce(jnp.maximum, rows)
    ex = [jnp.exp(r - m) for r in rows]
    z = functools.reduce(jnp.add, ex)
    sc = [e / z for e in ex]
    v1s, v2s, i1s, i2s, gss = [], [], [], [], []
    for g in range(N_GROUPS):
        a = sc[g * EXPERTS_PER_GROUP:(g + 1) * EXPERTS_PER_GROUP]
        v1 = functools.reduce(jnp.maximum, a)
        i1 = _first_index(a, v1)
        rest = [jnp.where(i1 == j, -1.0, a[j]) for j in range(EXPERTS_PER_GROUP)]
        v2 = functools.reduce(jnp.maximum, rest)
        i2 = _first_index(rest, v2)
        v1s.append(v1); v2s.append(v2); i1s.append(i1); i2s.append(i2); gss.append(v1 + v2)
    grp = _first_index(gss, functools.reduce(jnp.maximum, gss))
    v1 = _select(grp, v1s)
    v2 = _select(grp, v2s)
    e1 = _select(grp, i1s) + grp * EXPERTS_PER_GROUP
    e2 = _select(grp, i2s) + grp * EXPERTS_PER_GROUP
    den = v1 + v2
    return e1, e2, v1 / den, v2 / den


def _expert_onehot(e1, e2):
    rows = [jnp.where(e1 == e, 1.0, 0.0) + jnp.where(e2 == e, 1.0, 0.0) for e in range(N_EXPERTS)]
    return jnp.concatenate(rows, axis=0)


def _outproj_kernel(a_ref, r_ref, c_ref, x_ref, w_ref, g_ref, b_ref, wrh_ref, wrl_ref, br_ref,
                    x1_ref, x1b_ref, ri_ref, rw_ref, cnt_ref, *, alpha, tb, sub):
    tm = x_ref.shape[0]
    ones = jnp.ones((SUBLANES, sub), BF16)
    groups = [slice(h * sub, (h + 1) * sub) for h in range(tm // sub)]
    mixes = [(_dot(a_ref[rs, :], w_ref[0:A_WIDTH, :])
              + _dot(r_ref[rs, :], w_ref[A_WIDTH:A_WIDTH + R_WIDTH, :])
              + _dot(c_ref[rs, :], w_ref[A_WIDTH + R_WIDTH:, :])) for rs in groups]
    logits = []
    for rs, mix in zip(groups, mixes):
        x1 = _layer_norm(alpha * x_ref[rs, :] + mix, g_ref[...], b_ref[...])
        x_hi = x1.astype(BF16)
        x1_ref[rs, :] = x1
        x1b_ref[rs, :] = x_hi
        x_lo = (x1 - x_hi.astype(F32)).astype(BF16)
        logits.append(_dot(x_hi, wrh_ref[...]) + _dot(x_lo, wrh_ref[...]) + _dot(x_hi, wrl_ref[...]))
    counts = []
    for rs, lg in zip(groups, logits):
        e1, e2, w1, w2 = _route(lg.T[0:N_EXPERTS, :] + br_ref[...])
        ri_ref[:, rs] = jnp.concatenate([e1, e2, jnp.zeros((SUBLANES - 2, sub), jnp.int32)], axis=0)
        rw_ref[:, rs] = jnp.concatenate([w1, w2, jnp.zeros((SUBLANES - 2, sub), F32)], axis=0)
        counts.append(_dot_nt(ones, _expert_onehot(e1, e2).astype(BF16)))
    per = tb // sub
    for k in range(tm // tb):
        c = functools.reduce(jnp.add, counts[k * per:(k + 1) * per])
        c = jnp.concatenate([c, jnp.zeros((SUBLANES, LANES - N_EXPERTS), F32)], axis=1)
        cnt_ref[k * SUBLANES:(k + 1) * SUBLANES, :] = c.astype(jnp.int32)


def _outproj(attn, ret, conv, x, w_bf, ln_g, ln_b, w_router, br, alpha, tm, tb):
    n = x.shape[0]
    wr = jnp.pad(w_router, ((0, 0), (0, LANES - N_EXPERTS)))
    wr_hi = wr.astype(BF16)
    wr_lo = (wr - wr_hi.astype(F32)).astype(BF16)
    row = lambda i: (i, 0)
    col = lambda i: (0, i)
    const = lambda i: (0, 0)
    nsub = tm // tb
    return pl.pallas_call(
        functools.partial(_outproj_kernel, alpha=alpha, tb=tb, sub=min(tb, 2 * LANES)),
        grid=(n // tm,),
        in_specs=[pl.BlockSpec((tm, A_WIDTH), row), pl.BlockSpec((tm, R_WIDTH), row),
                  pl.BlockSpec((tm, C_WIDTH), row), pl.BlockSpec((tm, D_MODEL), row),
                  pl.BlockSpec((D_MODEL, D_MODEL), const),
                  pl.BlockSpec((1, D_MODEL), const), pl.BlockSpec((1, D_MODEL), const),
                  pl.BlockSpec((D_MODEL, LANES), const), pl.BlockSpec((D_MODEL, LANES), const),
                  pl.BlockSpec((N_EXPERTS, 1), const)],
        out_specs=[pl.BlockSpec((tm, D_MODEL), row), pl.BlockSpec((tm, D_MODEL), row),
                   pl.BlockSpec((SUBLANES, tm), col), pl.BlockSpec((SUBLANES, tm), col),
                   pl.BlockSpec((nsub * SUBLANES, LANES), row)],
        out_shape=[jax.ShapeDtypeStruct((n, D_MODEL), F32), jax.ShapeDtypeStruct((n, D_MODEL), BF16),
                   jax.ShapeDtypeStruct((SUBLANES, n), jnp.int32),
                   jax.ShapeDtypeStruct((SUBLANES, n), F32),
                   jax.ShapeDtypeStruct((n // tb * SUBLANES, LANES), jnp.int32)],
        compiler_params=_params(("arbitrary",), 40),
        name="outproj",
    )(attn, ret, conv, x, w_bf, ln_g.reshape(1, D_MODEL), ln_b.reshape(1, D_MODEL), wr_hi, wr_lo, br)


_UNIT = 16
_XS_WIDTH = D_MODEL + LANES
_FFN_ROWS = 512
_DISPATCH_TOKENS = 512


def _sorted_positions(e1, e2, lo_ref, base, tb):
    onehot = _expert_onehot(e1, e2).astype(BF16)
    r = lax.broadcasted_iota(jnp.int32, (tb, tb), 0)
    c = lax.broadcasted_iota(jnp.int32, (tb, tb), 1)
    earlier = jnp.where(r < c, 1.0, 0.0).astype(BF16)
    rank = _dot(onehot, earlier).astype(jnp.int32)
    pos1 = jnp.zeros(e1.shape, jnp.int32)
    pos2 = jnp.zeros(e1.shape, jnp.int32)
    for e in range(N_EXPERTS):
        p = rank[e:e + 1, :] + lo_ref[base + e]
        pos1 = jnp.where(e1 == e, p, pos1)
        pos2 = jnp.where(e2 == e, p, pos2)
    return pos1, pos2


def _permutation(pos1, pos2, rows):
    j = lax.broadcasted_iota(jnp.int32, (rows, pos1.shape[1]), 0)
    return jnp.where(j == pos1, 1.0, jnp.where(j == pos2, 1.0, 0.0)).astype(BF16)


def _dispatch_kernel(lo_ref, urow_ref, ri_ref, rw_ref, x_ref, xs_hbm, pos_ref,
                     stage, sems, *, nblk, tb, cap, spare_row):
    i = pl.program_id(0)
    slot = i % 2
    units = cap // _UNIT

    def wait_slot(s):
        pltpu.make_async_copy(stage.at[s], xs_hbm.at[pl.ds(0, cap), :], sems.at[s]).wait()

    @pl.when(i >= 2)
    def _():
        wait_slot(slot)

    e1 = ri_ref[0:1, :]
    e2 = ri_ref[1:2, :]
    w1 = rw_ref[0:1, :]
    w2 = rw_ref[1:2, :]
    pos1, pos2 = _sorted_positions(e1, e2, lo_ref, i * N_EXPERTS, tb)
    pos_ref[...] = jnp.concatenate([pos1, pos2, jnp.zeros((SUBLANES - 2, tb), jnp.int32)], axis=0)
    perm = _permutation(pos1, pos2, cap)
    stage[slot, :, 0:D_MODEL] = _dot(perm, x_ref[...]).astype(BF16)
    j = lax.broadcasted_iota(jnp.int32, (cap, tb), 0)
    gate = jnp.sum(jnp.where(j == pos1, w1, jnp.where(j == pos2, w2, 0.0)), axis=1, keepdims=True)
    g0 = gate.astype(BF16).astype(F32)
    g1 = (gate - g0).astype(BF16).astype(F32)
    g2 = gate - g0 - g1
    lane = lax.broadcasted_iota(jnp.int32, (cap, LANES), 1)
    pieces = jnp.where(lane == 0, g0, jnp.where(lane == 1, g1, jnp.where(lane == 2, g2, 0.0)))
    stage[slot, :, D_MODEL:_XS_WIDTH] = pieces.astype(BF16)

    def body(u, c):
        t = urow_ref[i * units + u]
        s = pl.multiple_of(u * _UNIT, _UNIT)
        d = pl.multiple_of(jnp.where(t >= 0, t, spare_row + slot * cap + s), _UNIT)
        pltpu.make_async_copy(stage.at[slot, pl.ds(s, _UNIT), :], xs_hbm.at[pl.ds(d, _UNIT), :],
                              sems.at[slot]).start()
        return c

    lax.fori_loop(0, units, body, 0)

    @pl.when(i == nblk - 1)
    def _():
        wait_slot(slot)
        if nblk > 1:
            wait_slot(1 - slot)


def _ffn_kernel(te_ref, tv_ref, nt_ref, xs_ref, wg_ref, wu_ref, wd_ref, ys_ref, wg_b, wu_b, wd_b):
    j = pl.program_id(0)
    jm = jnp.maximum(j, 1)

    @pl.when((j == 0) | (te_ref[j] != te_ref[jm - 1]))
    def _():
        wg_b[...] = wg_ref[...].astype(BF16)
        wu_b[...] = wu_ref[...].astype(BF16)
        wd_b[...] = wd_ref[...].astype(BF16)

    @pl.when(j < nt_ref[0])
    def _():
        rows = xs_ref.shape[0]
        valid = lax.broadcasted_iota(jnp.int32, (rows, LANES), 0) < tv_ref[j]
        zero = jnp.zeros((rows, LANES), BF16)
        x = jnp.concatenate([jnp.where(valid, xs_ref[:, c * LANES:(c + 1) * LANES], zero)
                             for c in range(D_MODEL // LANES)], axis=1)
        gp = jnp.where(valid, xs_ref[:, D_MODEL:_XS_WIDTH], zero).astype(F32)
        g = gp[:, 0:1] + gp[:, 1:2] + gp[:, 2:3]
        hg = _dot(x, wg_b[...])
        hu = _dot(x, wu_b[...])
        h = hg * _sigmoid(hg) * hu * g
        ys_ref[...] = _dot(h.astype(BF16), wd_b[...]).astype(BF16)


def _combine_kernel(urow_ref, pos_ref, x1_ref, p_ref, g_ref, b_ref, wpg_ref, bpg_ref, wpp_ref,
                    ys_hbm, out_ref, stage, sems, *, nblk, tb, cap, alpha):
    i = pl.program_id(0)
    slot = i % 2
    units = cap // _UNIT

    def fetch(blk, s):
        def body(u, c):
            sr = pl.multiple_of(jnp.maximum(urow_ref[blk * units + u], 0), _UNIT)
            ds = pl.multiple_of(u * _UNIT, _UNIT)
            pltpu.make_async_copy(ys_hbm.at[pl.ds(sr, _UNIT), :], stage.at[s, pl.ds(ds, _UNIT), :],
                                  sems.at[s]).start()
            return c

        lax.fori_loop(0, units, body, 0)

    @pl.when(i == 0)
    def _():
        fetch(0, 0)

    @pl.when(i + 1 < nblk)
    def _():
        fetch(i + 1, 1 - slot)

    pltpu.make_async_copy(ys_hbm.at[pl.ds(0, cap), :], stage.at[slot], sems.at[slot]).wait()
    sub = min(tb, 2 * LANES)
    groups = [slice(h * sub, (h + 1) * sub) for h in range(tb // sub)]
    ys = [_dot_tn(_permutation(pos_ref[0:1, rs], pos_ref[1:2, rs], cap), stage[slot]) for rs in groups]
    for rs, y in zip(groups, ys):
        x2 = _layer_norm(alpha * x1_ref[rs, :] + y, g_ref[...], b_ref[...])
        gate = _sigmoid(_dot(x2.astype(BF16), wpg_ref[...]) + bpg_ref[...])
        out_ref[rs, :] = x2 + gate * _dot(p_ref[rs, :].astype(BF16), wpp_ref[...])


def _moe_plan(cnt, nblk, tb, ffn_rows):
    cnt = cnt.reshape(nblk, SUBLANES, LANES)[:, 0, :N_EXPERTS]
    pc = (cnt + (_UNIT - 1)) // _UNIT * _UNIT
    lo = jnp.cumsum(pc, axis=1) - pc
    tot = jnp.sum(pc, axis=0)
    reg = (tot + (ffn_rows - 1)) // ffn_rows * ffn_rows
    rstart = jnp.cumsum(reg) - reg
    gs = rstart[None, :] + jnp.cumsum(pc, axis=0) - pc
    tiles_e = reg // ffn_rows
    tile_end = jnp.cumsum(tiles_e)
    max_rows = 2 * nblk * tb + nblk * N_EXPERTS * _UNIT + N_EXPERTS * ffn_rows
    max_tiles = -(-max_rows // ffn_rows)
    tj =jnp.arange(max_tiles, dtype=jnp.int32)
    te = jnp.minimum(jnp.sum((tj[:, None] >= tile_end[None, :]).astype(jnp.int32), axis=1), N_EXPERTS - 1)
    tv = jnp.clip(tot[te] - (tj - (tile_end - tiles_e)[te]) * ffn_rows, 0, ffn_rows)
    nt = tile_end[-1:].astype(jnp.int32)
    cap = 2 * tb + N_EXPERTS * _UNIT
    urow0 = jnp.arange(cap // _UNIT, dtype=jnp.int32) * _UNIT
    run = jnp.sum((urow0[None, :, None] >= (lo + pc)[:, None, :]).astype(jnp.int32), axis=2)
    pick = run[:, :, None] == jnp.arange(N_EXPERTS, dtype=jnp.int32)[None, None, :]
    urow = jnp.sum(jnp.where(pick, (gs - lo)[:, None, :], 0), axis=2) + urow0[None, :]
    urow = jnp.where(urow0[None, :] < jnp.sum(pc, axis=1, keepdims=True), urow, -1)
    flat = lambda a: a.reshape(-1).astype(jnp.int32)
    return flat(lo), flat(urow), te.astype(jnp.int32), tv.astype(jnp.int32), nt, max_tiles, cap


def _moe(x1b, x1, ri, rw, cnt, p, wg, wu, wd, layer, ln_g, ln_b, wpg, bpg, wpp, alpha, tb):
    n = x1.shape[0]
    nblk = n // tb
    ffn_rows = _FFN_ROWS if 2 * n >= N_EXPERTS * _FFN_ROWS else LANES
    lo, urow, te, tv, nt, max_tiles, cap = _moe_plan(cnt, nblk, tb, ffn_rows)
    max_rows = max_tiles * ffn_rows

    xs, pos = pl.pallas_call(
        functools.partial(_dispatch_kernel, nblk=nblk, tb=tb, cap=cap, spare_row=max_rows),
        grid_spec=pltpu.PrefetchScalarGridSpec(
            num_scalar_prefetch=2,
            grid=(nblk,),
            in_specs=[pl.BlockSpec((SUBLANES, tb), lambda i, *_: (0, i)),
                      pl.BlockSpec((SUBLANES, tb), lambda i, *_: (0, i)),
                      pl.BlockSpec((tb, D_MODEL), lambda i, *_: (i, 0))],
            out_specs=[pl.BlockSpec(memory_space=pl.ANY),
                       pl.BlockSpec((SUBLANES, tb), lambda i, *_: (0, i))],
            scratch_shapes=[pltpu.VMEM((2, cap, _XS_WIDTH), BF16),
                            pltpu.SemaphoreType.DMA((2,))]),
        out_shape=[jax.ShapeDtypeStruct((max_rows + 2 * cap, _XS_WIDTH), BF16),
                   jax.ShapeDtypeStruct((SUBLANES, n), jnp.int32)],
        compiler_params=_params(("arbitrary",), 32),
        name="moe_dispatch",
    )(lo, urow, ri, rw, x1b)

    def tile(j, te_ref, tv_ref, nt_ref):
        return jnp.minimum(j, nt_ref[0] - 1)

    ys = pl.pallas_call(
        _ffn_kernel,
        grid_spec=pltpu.PrefetchScalarGridSpec(
            num_scalar_prefetch=3,
            grid=(max_tiles,),
            in_specs=[pl.BlockSpec((ffn_rows, _XS_WIDTH), lambda j, *s: (tile(j, *s), 0)),
                      pl.BlockSpec((None, None, D_MODEL, D_EXPERT),
                                   lambda j, *s: (layer, s[0][tile(j, *s)], 0, 0)),
                      pl.BlockSpec((None, None, D_MODEL, D_EXPERT),
                                   lambda j, *s: (layer, s[0][tile(j, *s)], 0, 0)),
                      pl.BlockSpec((None, None, D_EXPERT, D_MODEL),
                                   lambda j, *s: (layer, s[0][tile(j, *s)], 0, 0))],
            out_specs=pl.BlockSpec((ffn_rows, D_MODEL), lambda j, *s: (tile(j, *s), 0)),
            scratch_shapes=[pltpu.VMEM((D_MODEL, D_EXPERT), BF16), pltpu.VMEM((D_MODEL, D_EXPERT), BF16),
                            pltpu.VMEM((D_EXPERT, D_MODEL), BF16)]),
        out_shape=jax.ShapeDtypeStruct((max_rows, D_MODEL), BF16),
        compiler_params=_params(("arbitrary",), 40),
        name="moe_ffn",
    )(te, tv, nt, xs, wg, wu, wd)

    const = lambda i, *_: (0, 0)
    return pl.pallas_call(
        functools.partial(_combine_kernel, nblk=nblk, tb=tb, cap=cap, alpha=alpha),
        grid_spec=pltpu.PrefetchScalarGridSpec(
            num_scalar_prefetch=1,
            grid=(nblk,),
            in_specs=[pl.BlockSpec((SUBLANES, tb), lambda i, *_: (0, i)),
                      pl.BlockSpec((tb, D_MODEL), lambda i, *_: (i, 0)),
                      pl.BlockSpec((None, tb, PLE_DIM), lambda i, *_: (layer, i, 0)),
                      pl.BlockSpec((1, D_MODEL), const), pl.BlockSpec((1, D_MODEL), const),
                      pl.BlockSpec((D_MODEL, D_MODEL), const), pl.BlockSpec((1, D_MODEL), const),
                      pl.BlockSpec((PLE_DIM, D_MODEL), const),
                      pl.BlockSpec(memory_space=pl.ANY)],
            out_specs=pl.BlockSpec((tb, D_MODEL), lambda i, *_: (i, 0)),
            scratch_shapes=[pltpu.VMEM((2, cap, D_MODEL), BF16),
                            pltpu.SemaphoreType.DMA((2,))]),
        out_shape=jax.ShapeDtypeStruct((n, D_MODEL), F32),
        compiler_params=_params(("arbitrary",), 40),
        name="moe_combine",
    )(urow, pos, x1, p, ln_g.reshape(1, D_MODEL), ln_b.reshape(1, D_MODEL),
      wpg, bpg.reshape(1, D_MODEL), wpp, ys)


def _rope_tables(pos, rows):
    half = R_KEY_DIM // 2
    inv_freq = 1.0 / (ROPE_BASE ** jnp.linspace(0.0, 1.0, half, dtype=jnp.float32))
    ang = pos.astype(jnp.float32)[:, None] * inv_freq[None, :]
    cos = jnp.cos(ang)
    sin = jnp.sin(ang)
    cos_t = jnp.tile(cos, (rows // pos.shape[0], 4))
    sin_t = jnp.tile(jnp.concatenate([-sin, sin], axis=1), (rows // pos.shape[0], 2))
    return cos_t, sin_t


def kernel(x_prompt, x_sample, p_prompt, p_sample, cache_k, cache_v, state_ret, state_conv, ln_emb_g, ln_emb_b, rel_bias, w_router, b_router, w_in, lam_q1, lam_k1, lam_q2, lam_k2, subln_g, dw_w, dw_b, conv_ln_g, conv_ln_b, w_out, ln1_g, ln1_b, w_exp_gate, w_exp_up, w_exp_down, ln2_g, ln2_b, w_ple_gate, b_ple_gate, w_ple_proj):
    B, T, D = x_prompt.shape
    Bs, Ts, _ = x_sample.shape
    depth = w_in.shape[0]
    past = cache_k.shape[2]
    n_p, n_s = B * T, Bs * Ts
    alpha = (2 * depth) ** 0.25

    tm_p = 512 if n_p % 512 == 0 else n_p
    tm_s = n_s
    TB = 2 * LANES
    c_p = min(T, 2 * LANES)

    pos_p = jnp.arange(T, dtype=jnp.int32)
    pos_s = past + jnp.arange(Ts, dtype=jnp.int32)
    rope_p = _rope_tables(pos_p, max(T, tm_p))
    rope_s = _rope_tables(pos_s, max(Ts, tm_s))
    rel_p = (jnp.arange(2 * TB, dtype=jnp.int32)[None, :] - TB) - jnp.arange(TB, dtype=jnp.int32)[:, None]
    padk = -(-(past + Ts) // LANES) * LANES
    rel_s = jnp.arange(padk, dtype=jnp.int32)[None, :] - pos_s[:, None]
    bias_p = _bias_table(rel_bias, _t5_bucket(rel_p))
    bias_s = _bias_table(rel_bias, _t5_bucket(rel_s))
    tabs_p = _retention_tables(c_p)
    tabs_s = _retention_tables(Ts)

    br = b_router.reshape(N_EXPERTS, 1)
    ret0_p = jnp.zeros((B, R_HEADS, R_KEY_DIM, R_V_DIM), F32)
    conv0_p = jnp.zeros((B, CONV_WIDTH - 1, C_WIDTH), F32)

    xp = x_prompt.reshape(n_p, D)
    xs = x_sample.reshape(n_s, D)
    outs = {k: [] for k in ("rp", "cp", "rs", "cs")}
    kv_p = kv_s = None
    for l in range(depth):
        lam_init = 0.8 - 0.6 * math.exp(-0.3 * l)
        lam4 = jnp.stack([lam_q1[l], lam_k1[l], lam_q2[l], lam_k2[l]])
        g_sub = subln_g[l].reshape(1, A_V_DIM)
        w_in_b = w_in[l].astype(BF16)
        w_out_b = w_out[l].astype(BF16)
        wpg, wpp = w_ple_gate[l].astype(BF16), w_ple_proj[l].astype(BF16)
        ln = (ln_emb_g, ln_emb_b) if l == 0 else None

        def channel(x, attn, ret, conv, p_l, tm):
            tb = min(tm, _DISPATCH_TOKENS)
            tm_o = 2 * tm if x.shape[0] % (2 * tm) == 0 else tm
            x1, x1b, ri, rw, cnt = _outproj(attn, ret, conv, x, w_out_b, ln1_g[l], ln1_b[l], w_router, br,
                                            alpha, tm_o, tb)
            return _moe(x1b, x1, ri, rw, cnt, p_l, w_exp_gate, w_exp_up, w_exp_down, l, ln2_g[l], ln2_b[l],
                        wpg, b_ple_gate[l], wpp, alpha, tb)

        xp, q, kb, vb, rq, rk, rv, rg, u, k5, v5 = _inproj(xp, w_in_b, rope_p[0], rope_p[1], ln, tm_p,
                                                           kv_p, l, depth, B, T)
        kv_p = (k5, v5)
        attn = _attn_prompt(lam4, g_sub, bias_p, q, kb, vb, B, T, lam_init)
        ret, rstate = _retention(rq, rk, rv, rg, ret0_p, tabs_p, B, T, c_p)
        conv, ctail = _conv(u, conv0_p, dw_w[l], dw_b[l], conv_ln_g[l], conv_ln_b[l], B, T)
        xp = channel(xp, attn, ret, conv, p_prompt.reshape(depth, n_p, PLE_DIM), tm_p)
        outs["rp"].append(rstate)
        outs["cp"].append(ctail)

        xs, q, kb, vb, rq, rk, rv, rg, u, k5, v5 = _inproj(xs, w_in_b, rope_s[0], rope_s[1], ln, tm_s,
                                                           kv_s, l, depth, Bs, Ts)
        kv_s = (k5, v5)
        attn = _attn_decode(lam4, g_sub, bias_s, q, cache_k, cache_v, kb, vb, l, Bs, Ts, lam_init)
        ret, rstate = _retention(rq, rk, rv, rg, state_ret[l], tabs_s, Bs, Ts, Ts)
        conv, ctail = _conv(u, state_conv[l], dw_w[l], dw_b[l], conv_ln_g[l], conv_ln_b[l], Bs, Ts)
        xs = channel(xs, attn, ret, conv, p_sample.reshape(depth, n_s, PLE_DIM), tm_s)
        outs["rs"].append(rstate)
        outs["cs"].append(ctail)

    return (xp.reshape(B, T, D), xs.reshape(Bs, Ts, D),
            kv_p[0], kv_p[1], jnp.stack(outs["rp"]), jnp.stack(outs["cp"]),
            kv_s[0], kv_s[1], jnp.stack(outs["rs"]), jnp.stack(outs["cs"]))
```
